```python
import jax, jax.numpy as jnp
from jax import lax
import numpy as np

D_MODEL = 2048
BATCH = 1
SEQ = 8192
DEPTH = 2
DEC_BATCH = 32
DEC_SEQ = 32
PAST_LEN = 2048

CHUNK = 64
QBLOCK = 128
HD = 64
SB_H = 8
DSA_H = 8
IDX_H = 8
IDX_D = 64
TOPK_MAX = 256
GLA_H = 4
GLA_DK = 64
GLA_DV = 128
GLA_LOWRANK = 16
GLA_GATE_NORM = 16.0
RW_H = 8
RW_N = 64
RW_DECAY_LORA = 64
RW_AAA_LORA = 64
RW_GATE_LORA = 128
N_BRANCH = 4
BRANCH_W = 512
D_FF = 5504
NORM_EPS = 1e-6
RW_LN_EPS = 64e-5

SB_COLS = 3 * SB_H * HD
DSA_COLS = 3 * DSA_H * HD + IDX_H * IDX_D + IDX_D + IDX_H
GLA_COLS = 2 * GLA_H * GLA_DK + 2 * GLA_H * GLA_DV + GLA_LOWRANK
RW_COLS = 3 * RW_H * RW_N + RW_DECAY_LORA + RW_AAA_LORA + RW_GATE_LORA
GATE_COLS = N_BRANCH * D_MODEL
IN_WIDTH = SB_COLS + DSA_COLS + GLA_COLS + RW_COLS + GATE_COLS

kernel_name = 'hybrid_streaming_encoder_step'


def rmsnorm(x, g):
    xf = x.astype(jnp.float32)
    y = xf * lax.rsqrt(jnp.mean(xf * xf, axis=-1, keepdims=True) + NORM_EPS)
    return (y * g.astype(jnp.float32)).astype(x.dtype)


def take_cols(z, widths):
    outs = []
    off = 0
    for w in widths:
        outs.append(z[..., off:off + w])
        off += w
    return outs


def swiglu_ffn(x, norm_g, w_up, w_down):
    gu = rmsnorm(x, norm_g) @ w_up
    return (jax.nn.silu(gu[..., :D_FF]) * gu[..., D_FF:]) @ w_down


def sweep_query_blocks(fn, q_args, q_pos):
    Lq = q_pos.shape[0]
    if Lq <= QBLOCK:
        return fn(*q_args, q_pos)
    nb = Lq // QBLOCK
    def body(i):
        s = i * QBLOCK
        blk = [lax.dynamic_slice_in_dim(a, s, QBLOCK, axis=1) for a in q_args]
        return fn(*blk, lax.dynamic_slice_in_dim(q_pos, s, QBLOCK, axis=0))
    out = lax.map(body, jnp.arange(nb))
    B = q_args[0].shape[0]
    return jnp.moveaxis(out, 0, 1).reshape((B, Lq) + out.shape[3:])


def stick_breaking_attention(q, k, v, q_pos, k_pos):
    scale = HD ** -0.5
    def blk(qb, qp):
        z = jnp.einsum('bqhd,bkhd->bhqk', qb, k).astype(jnp.float32) * scale
        valid = (k_pos[None, :] < qp[:, None])[None, None]
        log_1mb = jnp.where(valid, jax.nn.log_sigmoid(-z), 0.0)
        surv = lax.cumsum(log_1mb, axis=3, reverse=True) - log_1mb
        A = jnp.where(valid, jnp.exp(jax.nn.log_sigmoid(z) + surv), 0.0)
        return jnp.einsum('bhqk,bkhd->bqhd', A.astype(v.dtype), v)
    return sweep_query_blocks(blk, [q], q_pos)


def dsa_attention(q, qi, wi, k, v, kidx, q_pos, k_pos):
    topk = min(TOPK_MAX, k.shape[1] // 4)
    scale = HD ** -0.5
    k_chunk = k_pos // CHUNK
    gather = jax.vmap(lambda kb, ib: kb[ib])
    def blk(qb, qib, wib, qp):
        dots = jnp.einsum('bqhd,bkd->bqhk', qib, kidx).astype(jnp.float32) * (IDX_D ** -0.5)
        score = jnp.einsum('bqh,bqhk->bqk', wib.astype(jnp.float32) * (IDX_H ** -0.5), jax.nn.relu(dots))
        q_chunk = qp // CHUNK
        adm = k_chunk[None, :] <= q_chunk[:, None]
        score = jnp.where(adm[None], score, -jnp.inf)
        _, idx = lax.top_k(score, topk)
        kg = gather(k, idx)
        vg = gather(v, idx)
        valid = k_chunk[idx] <= q_chunk[None, :, None]
        s = jnp.einsum('bqhd,bqkhd->bhqk', qb, kg).astype(jnp.float32) * scale
        s = jnp.where(valid[:, None], s, -jnp.inf)
        p = jax.nn.softmax(s, axis=-1)
        return jnp.einsum('bhqk,bqkhd->bqhd', p.astype(v.dtype), vg)
    return sweep_query_blocks(blk, [q, qi, wi], q_pos)


def gla_chunked(q, k, v, glog, S0):
    B, L, H, DK = q.shape
    DV = v.shape[-1]
    C = min(CHUNK, L)
    n = L // C
    def to_chunks(t):
        return jnp.transpose(t.astype(jnp.float32).reshape(B, n, C, H, t.shape[-1]), (1, 0, 3, 2, 4))
    xs = (to_chunks(q), to_chunks(k), to_chunks(v), to_chunks(glog))
    tri = jnp.tril(jnp.ones((C, C), dtype=bool))[None, None, :, :, None]
    def step(S, inp):
        qc, kc, vc, gc = inp
        b = jnp.cumsum(gc, axis=2)
        diff = b[:, :, :, None, :] - b[:, :, None, :, :]
        dec = jnp.exp(jnp.where(tri, diff, -jnp.inf))
        A = jnp.einsum('bhtd,bhsd,bhtsd->bhts', qc, kc, dec)
        o = jnp.einsum('bhts,bhse->bhte', A, vc) + jnp.einsum('bhtd,bhde->bhte', qc * jnp.exp(b), S)
        b_last = b[:, :, -1:, :]
        S = jnp.exp(b_last[:, :, 0, :])[..., None] * S + jnp.einsum('bhsd,bhse->bhde', kc * jnp.exp(b_last - b), vc)
        return S, o
    S, o = lax.scan(step, S0.astype(jnp.float32), xs)
    o = jnp.transpose(o, (1, 0, 3, 2, 4)).reshape(B, L, H, DV)
    return o.astype(v.dtype), S.astype(S0.dtype)


def rwkv7_recurrence(r, w, k, v, kk, a, S0):
    f = lambda t: jnp.moveaxis(t.astype(jnp.float32), 1, 0)
    def step(S, inp):
        rt, wt, kt, vt, kkt, at = inp
        sa = jnp.einsum('bhvk,bhk->bhv', S, -kkt)
        S = S * wt[:, :, None, :] + sa[..., None] * (kkt * at)[:, :, None, :] + vt[..., None] * kt[:, :, None, :]
        return S, jnp.einsum('bhvk,bhk->bhv', S, rt)
    S, o = lax.scan(step, S0.astype(jnp.float32), (f(r), f(w), f(k), f(v), f(kk), f(a)))
    return jnp.moveaxis(o, 0, 1), S.astype(S0.dtype)


def trunk_layer(x, past, ffn1_norm, ffn1_up, ffn1_down, mix_norm, w_in, sb_qnorm, sb_knorm,
                dsa_qnorm, dsa_knorm, gla_a2, gla_ab, gla_onorm, rwkv_mu, rwkv_w0, rwkv_w2,
                rwkv_a0, rwkv_a2, rwkv_g2, rwkv_kk, rwkv_ka, rwkv_rk, rwkv_lnw, rwkv_lnb,
                w_branch, w_out, ffn2_norm, ffn2_up, ffn2_down):
    p_sb_k, p_sb_v, p_dsa_k, p_dsa_v, p_kidx, p_gla, p_rwkv, p_shift = past
    B, L, _ = x.shape
    P = p_sb_k.shape[1]
    q_pos = P + jnp.arange(L)
    k_pos = jnp.arange(P + L)

    x = x + 0.5 * swiglu_ffn(x, ffn1_norm, ffn1_up, ffn1_down)
    h = rmsnorm(x, mix_norm)
    z = h @ w_in
    z_sb, z_dsa, z_gla, z_rw, z_gate = take_cols(z, (SB_COLS, DSA_COLS, GLA_COLS, RW_COLS, GATE_COLS))

    sq, sk, sv = take_cols(z_sb, (SB_H * HD, SB_H * HD, SB_H * HD))
    sq = rmsnorm(sq.reshape(B, L, SB_H, HD), sb_qnorm)
    sk = rmsnorm(sk.reshape(B, L, SB_H, HD), sb_knorm)
    sv = sv.reshape(B, L, SB_H, HD)
    o_sb = stick_breaking_attention(sq, jnp.concatenate([p_sb_k, sk], axis=1),
                                    jnp.concatenate([p_sb_v, sv], axis=1), q_pos, k_pos).reshape(B, L, BRANCH_W)

    dq, dk, dv, qi, ki, wi = take_cols(z_dsa, (DSA_H * HD, DSA_H * HD, DSA_H * HD, IDX_H * IDX_D, IDX_D, IDX_H))
    dq = rmsnorm(dq.reshape(B, L, DSA_H, HD), dsa_qnorm)
    dk = rmsnorm(dk.reshape(B, L, DSA_H, HD), dsa_knorm)
    dv = dv.reshape(B, L, DSA_H, HD)
    o_dsa = dsa_attention(dq, qi.reshape(B, L, IDX_H, IDX_D), wi,
                          jnp.concatenate([p_dsa_k, dk], axis=1), jnp.concatenate([p_dsa_v, dv], axis=1),
                          jnp.concatenate([p_kidx, ki], axis=1), q_pos, k_pos).reshape(B, L, BRANCH_W)

    gq, gk, gv, gr, ga = take_cols(z_gla, (GLA_H * GLA_DK, GLA_H * GLA_DK, GLA_H * GLA_DV, GLA_H * GLA_DV, GLA_LOWRANK))
    glog = jax.nn.log_sigmoid((ga @ gla_a2 + gla_ab).astype(jnp.float32)) / GLA_GATE_NORM
    o_g, new_gla = gla_chunked(gq.reshape(B, L, GLA_H, GLA_DK) * (GLA_DK ** -0.5), gk.reshape(B, L, GLA_H, GLA_DK),
                               gv.reshape(B, L, GLA_H, GLA_DV), glog.reshape(B, L, GLA_H, GLA_DK), p_gla)
    o_g = rmsnorm(o_g, gla_onorm).reshape(B, L, BRANCH_W) * jax.nn.silu(gr)

    prev = jnp.concatenate([p_shift, z_rw[:, :-1]], axis=1)
    zm = z_rw + rwkv_mu * (prev - z_rw)
    rr, wl, rk, rv, al, gl = take_cols(zm, (RW_H * RW_N, RW_DECAY_LORA, RW_H * RW_N, RW_H * RW_N, RW_AAA_LORA, RW_GATE_LORA))
    wlog = -jax.nn.softplus(-(rwkv_w0 + jnp.tanh(wl) @ rwkv_w2).astype(jnp.float32)) - 0.5
    decay = jnp.exp(-jnp.exp(wlog))
    a = jax.nn.sigmoid(rwkv_a0 + al @ rwkv_a2)
    gate = jax.nn.sigmoid(gl) @ rwkv_g2
    hs = lambda t: t.reshape(B, L, RW_H, RW_N)
    kkf = hs(rk * rwkv_kk).astype(jnp.float32)
    kk = kkf / jnp.maximum(jnp.sqrt(jnp.sum(kkf * kkf, axis=-1, keepdims=True)), 1e-12)
    kmod = rk * (1.0 + (a - 1.0) * rwkv_ka)
    o_r, new_rwkv = rwkv7_recurrence(hs(rr), hs(decay), hs(kmod), hs(rv), kk, hs(a), p_rwkv)
    mu = jnp.mean(o_r, axis=-1, keepdims=True)
    var = jnp.mean(jnp.square(o_r - mu), axis=-1, keepdims=True)
    o_n = ((o_r - mu) * lax.rsqrt(var + RW_LN_EPS)).reshape(B, L, BRANCH_W)
    o_n = (o_n * rwkv_lnw.astype(jnp.float32) + rwkv_lnb.astype(jnp.float32)).astype(x.dtype)
    bonus = (jnp.sum(hs(rr * kmod * rwkv_rk), axis=-1, keepdims=True) * hs(rv)).reshape(B, L, BRANCH_W)
    o_rw = (o_n + bonus) * gate

    gates = jax.nn.sigmoid(z_gate.astype(jnp.float32)).reshape(B, L, N_BRANCH, D_MODEL).astype(x.dtype)
    branches = jnp.stack([o_sb, o_dsa, o_g, o_rw], axis=2)
    proj = jnp.einsum('blnc,ncd->blnd', branches, w_branch)
    x = x + jnp.sum(gates * proj, axis=2) @ w_out

    x = x + 0.5 * swiglu_ffn(x, ffn2_norm, ffn2_up, ffn2_down)
    new_state = (sk, sv, dk, dv, ki, new_gla, new_rwkv, z_rw[:, -1:])
    return x, new_state


def setup_inputs(seed: int = 0) -> dict:
    key = jax.random.key(seed)
    ks = iter(jax.random.split(key, 48))
    nrm = lambda shape, s: jax.random.normal(next(ks), shape, jnp.float32) * s
    gain = lambda shape: 1.0 + 0.05 * jax.random.normal(next(ks), shape, jnp.float32)
    D = D_MODEL
    return {
        'x_prompt': nrm((BATCH, SEQ, D), 1.0),
        'x_sample': nrm((DEC_BATCH, DEC_SEQ, D), 1.0),
        'cache_sb_k': nrm((DEPTH, DEC_BATCH, PAST_LEN, SB_H, HD), 1.0),
        'cache_sb_v': nrm((DEPTH, DEC_BATCH, PAST_LEN, SB_H, HD), 1.0),
        'cache_dsa_k': nrm((DEPTH, DEC_BATCH, PAST_LEN, DSA_H, HD), 1.0),
        'cache_dsa_v': nrm((DEPTH, DEC_BATCH, PAST_LEN, DSA_H, HD), 1.0),
        'cache_dsa_kidx': nrm((DEPTH, DEC_BATCH, PAST_LEN, IDX_D), 1.0),
        'state_gla': nrm((DEPTH, DEC_BATCH, GLA_H, GLA_DK, GLA_DV), 0.1),
        'state_rwkv': nrm((DEPTH, DEC_BATCH, RW_H, RW_N, RW_N), 0.1),
        'state_rwkv_shift': nrm((DEPTH, DEC_BATCH, 1, RW_COLS), 1.0),
        'ffn1_norm': gain((DEPTH, D)),
        'ffn1_up': nrm((DEPTH, D, 2 * D_FF), D ** -0.5),
        'ffn1_down': nrm((DEPTH, D_FF, D), D_FF ** -0.5),
        'mix_norm': gain((DEPTH, D)),
        'w_in': nrm((DEPTH, D, IN_WIDTH), D ** -0.5),
        'sb_qnorm': gain((DEPTH, HD)),
        'sb_knorm': gain((DEPTH, HD)),
        'dsa_qnorm': gain((DEPTH, HD)),
        'dsa_knorm': gain((DEPTH, HD)),
        'gla_a2': nrm((DEPTH, GLA_LOWRANK, GLA_H * GLA_DK), GLA_LOWRANK ** -0.5),
        'gla_ab': nrm((DEPTH, GLA_H * GLA_DK), 0.01),
        'gla_onorm': gain((DEPTH, GLA_DV)),
        'rwkv_mu': jax.random.uniform(next(ks), (DEPTH, RW_COLS), jnp.float32),
        'rwkv_w0': nrm((DEPTH, RW_H * RW_N), 0.5),
        'rwkv_w2': nrm((DEPTH, RW_DECAY_LORA, RW_H * RW_N), 0.1),
        'rwkv_a0': nrm((DEPTH, RW_H * RW_N), 0.1),
        'rwkv_a2': nrm((DEPTH, RW_AAA_LORA, RW_H * RW_N), 0.5 * RW_AAA_LORA ** -0.5),
        'rwkv_g2': nrm((DEPTH, RW_GATE_LORA, RW_H * RW_N), RW_GATE_LORA ** -0.5),
        'rwkv_kk': gain((DEPTH, RW_H * RW_N)),
        'rwkv_ka': gain((DEPTH, RW_H * RW_N)),
        'rwkv_rk': nrm((DEPTH, RW_H * RW_N), 0.1),
        'rwkv_lnw': gain((DEPTH, RW_H * RW_N)),
        'rwkv_lnb': nrm((DEPTH, RW_H * RW_N), 0.01),
        'w_branch': nrm((DEPTH, N_BRANCH, BRANCH_W, D), BRANCH_W ** -0.5),
        'w_out': nrm((DEPTH, D, D), D ** -0.5),
        'ffn2_norm': gain((DEPTH, D)),
        'ffn2_up': nrm((DEPTH, D, 2 * D_FF), D ** -0.5),
        'ffn2_down': nrm((DEPTH, D_FF, D), D_FF ** -0.5),
    }


def reference(x_prompt, x_sample, cache_sb_k, cache_sb_v, cache_dsa_k, cache_dsa_v, cache_dsa_kidx,
              state_gla, state_rwkv, state_rwkv_shift, ffn1_norm, ffn1_up, ffn1_down, mix_norm, w_in,
              sb_qnorm, sb_knorm, dsa_qnorm, dsa_knorm, gla_a2, gla_ab, gla_onorm, rwkv_mu, rwkv_w0,
              rwkv_w2, rwkv_a0, rwkv_a2, rwkv_g2, rwkv_kk, rwkv_ka, rwkv_rk, rwkv_lnw, rwkv_lnb,
              w_branch, w_out, ffn2_norm, ffn2_up, ffn2_down):
    weights = (ffn1_norm, ffn1_up, ffn1_down, mix_norm, w_in, sb_qnorm, sb_knorm, dsa_qnorm, dsa_knorm,
               gla_a2, gla_ab, gla_onorm, rwkv_mu, rwkv_w0, rwkv_w2, rwkv_a0, rwkv_a2, rwkv_g2,
               rwkv_kk, rwkv_ka, rwkv_rk, rwkv_lnw, rwkv_lnb, w_branch, w_out, ffn2_norm, ffn2_up, ffn2_down)
    B = x_prompt.shape[0]
    dt = x_prompt.dtype
    y_p = x_prompt
    y_s = x_sample
    new_p = [[] for _ in range(8)]
    new_s = [[] for _ in range(8)]
    for i in range(DEPTH):
        w_i = [w[i] for w in weights]
        past_p = (jnp.zeros((B, 0, SB_H, HD), dt), jnp.zeros((B, 0, SB_H, HD), dt),
                  jnp.zeros((B, 0, DSA_H, HD), dt), jnp.zeros((B, 0, DSA_H, HD), dt),
                  jnp.zeros((B, 0, IDX_D), dt), jnp.zeros((B, GLA_H, GLA_DK, GLA_DV), dt),
                  jnp.zeros((B, RW_H, RW_N, RW_N), dt), jnp.zeros((B, 1, RW_COLS), dt))
        past_s = (cache_sb_k[i], cache_sb_v[i], cache_dsa_k[i], cache_dsa_v[i], cache_dsa_kidx[i],
                  state_gla[i], state_rwkv[i], state_rwkv_shift[i])
        y_p, st_p = trunk_layer(y_p, past_p, *w_i)
        y_s, st_s = trunk_layer(y_s, past_s, *w_i)
        for j in range(8):
            new_p[j].append(st_p[j])
            new_s[j].append(st_s[j])
    p_sb_k, p_sb_v, p_dsa_k, p_dsa_v, p_dsa_kidx, p_gla, p_rwkv, p_shift = [jnp.stack(l, axis=0) for l in new_p]
    s_sb_k, s_sb_v, s_dsa_k, s_dsa_v, s_dsa_kidx, s_gla, s_rwkv, s_shift = [jnp.stack(l, axis=0) for l in new_s]
    return (y_p, y_s, p_sb_k, p_sb_v, p_dsa_k, p_dsa_v, p_dsa_kidx, p_gla, p_rwkv, p_shift,
            s_sb_k, s_sb_v, s_dsa_k, s_dsa_v, s_dsa_kidx, s_gla, s_rwkv, s_shift)
```

```python
import functools
import math

import jax
import jax.numpy as jnp
import numpy as np
from jax import lax
from jax.experimental import pallas as pl
from jax.experimental.pallas import tpu as pltpu

F32 = jnp.float32
BF16 = jnp.bfloat16

LANES = 128
SUBLANES = 8
HD = 64
NH = 8
BW = NH * HD
GLA_H = 4
GLA_DV = 128
GLA_LOWRANK = 16
GLA_GATE_NORM = 16.0
CHUNK = 64
TOPK_MAX = 256
RW_LORA = 64
RW_GATE_LORA = 128
RW_COLS = 3 * BW + 2 * RW_LORA + RW_GATE_LORA
NORM_EPS = 1e-6
RW_LN_EPS = 64e-5
NEG_BIG = -1e30
VMEM_LIMIT = 56 * 1024 * 1024

OFF_RW = 0
OFF_SQ = 2048
OFF_SK = 3072
OFF_SV = 3584
OFF_DQ = 4096
OFF_DK = 5120
OFF_DV = 5632
OFF_QI = 6144
OFF_GQ = 7168
OFF_GK = 7680
OFF_GV = 8192
OFF_GR = 8704
OFF_KI = 9216
OFF_WI = 9344
OFF_GA = 9472
OFF_GATE = 9728


def _cparams(sem):
    return pltpu.CompilerParams(dimension_semantics=sem, vmem_limit_bytes=VMEM_LIMIT)


def _pick(n, prefs):
    for p in prefs:
        if n % p == 0:
            return p
    raise ValueError(f"no tile for {n} in {prefs}")


def _split_dot(a, b):
    hi = a.astype(BF16)
    lo = (a - hi.astype(F32)).astype(BF16)
    return (jnp.dot(hi, b, preferred_element_type=F32)
            + jnp.dot(lo, b, preferred_element_type=F32))


def _split_dot_left(a, b):
    hi = b.astype(BF16)
    lo = (b - hi.astype(F32)).astype(BF16)
    return (jnp.dot(a, hi, preferred_element_type=F32)
            + jnp.dot(a, lo, preferred_element_type=F32))


def _dot_nt(a, b):
    return lax.dot_general(a, b, (((1,), (1,)), ((), ())), preferred_element_type=F32)


def _dot_tn(a, b):
    return lax.dot_general(a, b, (((0,), (0,)), ((), ())), preferred_element_type=F32)


def _neg_softplus(z):
    return -(jnp.maximum(z, 0.0) + jnp.log(1.0 + jnp.exp(-jnp.abs(z))))


def _ffn_body(x_ref, g_ref, wup_ref, wdn_ref, o_ref, xn_ref, *, tf):
    @pl.when(pl.program_id(1) == 0)
    def _():
        x = x_ref[...]
        ms = jnp.mean(x * x, axis=-1, keepdims=True)
        xn_ref[...] = (x * lax.rsqrt(ms + NORM_EPS) * g_ref[...]).astype(BF16)
        o_ref[...] = x

    gu = jnp.dot(xn_ref[...], wup_ref[...], preferred_element_type=F32)
    gate = gu[:, :tf]
    act = (gate * jax.nn.sigmoid(gate) * gu[:, tf:]).astype(BF16)
    o_ref[...] += 0.5 * jnp.dot(act, wdn_ref[...], preferred_element_type=F32)


def _ffn(x, g, wup, wdn, tf):
    T, D = x.shape
    fp = wdn.shape[0]
    tm = _pick(T, (512, 256, 128))
    return pl.pallas_call(
        functools.partial(_ffn_body, tf=tf),
        grid=(T // tm, fp // tf),
        in_specs=[pl.BlockSpec((tm, D), lambda i, j: (i, 0)),
                  pl.BlockSpec((1, D), lambda i, j: (0, 0)),
                  pl.BlockSpec((D, 2 * tf), lambda i, j: (0, j)),
                  pl.BlockSpec((tf, D), lambda i, j: (j, 0))],
        out_specs=pl.BlockSpec((tm, D), lambda i, j: (i, 0)),
        out_shape=jax.ShapeDtypeStruct((T, D), F32),
        scratch_shapes=[pltpu.VMEM((tm, D), BF16)],
        compiler_params=_cparams(("parallel", "arbitrary")),
        name="ffn",
    )(x, g, wup, wdn)


def _proj_body(x_ref, g_ref, w_ref, o_ref, xn_ref):
    @pl.when(pl.program_id(1) == 0)
    def _():
        x = x_ref[...]
        ms = jnp.mean(x * x, axis=-1, keepdims=True)
        xn_ref[...] = (x * lax.rsqrt(ms + NORM_EPS) * g_ref[...]).astype(BF16)

    o_ref[...] = jnp.dot(xn_ref[...], w_ref[...], preferred_element_type=F32)


def _proj(x, g, w):
    T, D = x.shape
    n = w.shape[1]
    tm = _pick(T, (1024, 512, 256, 128))
    tn = 512
    return pl.pallas_call(
        _proj_body,
        grid=(T // tm, n // tn),
        in_specs=[pl.BlockSpec((tm, D), lambda i, j: (i, 0)),
                  pl.BlockSpec((1, D), lambda i, j: (0, 0)),
                  pl.BlockSpec((D, tn), lambda i, j: (0, j))],
        out_specs=pl.BlockSpec((tm, tn), lambda i, j: (i, j)),
        out_shape=jax.ShapeDtypeStruct((T, n), F32),
        scratch_shapes=[pltpu.VMEM((tm, D), BF16)],
        compiler_params=_cparams(("parallel", "arbitrary")),
        name="in_proj",
    )(x, g, w)


def _merge_body(osb_ref, ods_ref, ogl_ref, orw_ref, g0_ref, g1_ref, g2_ref, g3_ref,
                wb_ref, wo_ref, x_ref, o_ref):
    @pl.when(pl.program_id(1) == 0)
    def _():
        o_ref[...] = x_ref[...]

    acc = None
    for n, (b_ref, g_ref) in enumerate(((osb_ref, g0_ref), (ods_ref, g1_ref),
                                        (ogl_ref, g2_ref), (orw_ref, g3_ref))):
        p = jnp.dot(b_ref[...].astype(BF16), wb_ref[n], preferred_element_type=F32)
        t = jax.nn.sigmoid(g_ref[...]) * p
        acc = t if acc is None else acc + t
    o_ref[...] += jnp.dot(acc.astype(BF16), wo_ref[...], preferred_element_type=F32)


def _merge(x, z, o_sb, o_dsa, o_gla, o_rw, wb, wo):
    T, D = x.shape
    tm = _pick(T, (512, 256, 128))
    tc = 512
    nc = D // tc
    gate_blk = OFF_GATE // tc
    bspec = pl.BlockSpec((tm, BW), lambda i, c: (i, 0))
    gspecs = [pl.BlockSpec((tm, tc), functools.partial(lambda i, c, n: (i, gate_blk + n * nc + c), n=n))
              for n in range(4)]
    return pl.pallas_call(
        _merge_body,
        grid=(T // tm, nc),
        in_specs=[bspec, bspec, bspec, bspec] + gspecs + [
            pl.BlockSpec((4, BW, tc), lambda i, c: (0, 0, c)),
            pl.BlockSpec((tc, D), lambda i, c: (c, 0)),
            pl.BlockSpec((tm, D), lambda i, c: (i, 0))],
        out_specs=pl.BlockSpec((tm, D), lambda i, c: (i, 0)),
        out_shape=jax.ShapeDtypeStruct((T, D), F32),
        compiler_params=_cparams(("parallel", "arbitrary")),
        name="merge",
    )(o_sb, o_dsa, o_gla, o_rw, z, z, z, z, wb, wo, x)


def _qknorm_body(sq_ref, sk_ref, dq_ref, dk_ref, gsq_ref, gsk_ref, gdq_ref, gdk_ref, bd_ref,
                 osq_ref, osk_ref, odq_ref, odk_ref):
    def norm(x, g, n):
        ms = _split_dot(x * x, bd_ref[:n, :n])
        return x * lax.rsqrt(ms + NORM_EPS) * g

    scale = HD ** -0.5
    osq_ref[...] = (norm(sq_ref[...], gsq_ref[...], 2 * BW) * scale).astype(BF16)
    osk_ref[...] = norm(sk_ref[...], gsk_ref[...], BW)
    odq_ref[...] = (norm(dq_ref[...], gdq_ref[...], 2 * BW) * scale).astype(BF16)
    odk_ref[...] = norm(dk_ref[...], gdk_ref[...], BW)


def _qknorm(z, gsq, gsk, gdq, gdk, bd):
    T = z.shape[0]
    tm = _pick(T, (512, 256, 128))
    wq, wk = 2 * BW, BW
    row = lambda w, off: pl.BlockSpec((tm, w), lambda i: (i, off // w))
    par = lambda w: pl.BlockSpec((1, w), lambda i: (0, 0))
    out = lambda w: pl.BlockSpec((tm, w), lambda i: (i, 0))
    return pl.pallas_call(
        _qknorm_body,
        grid=(T // tm,),
        in_specs=[row(wq, OFF_SQ), row(wk, OFF_SK), row(wq, OFF_DQ), row(wk, OFF_DK),
                  par(wq), par(wk), par(wq), par(wk),
                  pl.BlockSpec((wq, wq), lambda i: (0, 0))],
        out_specs=[out(wq), out(wk), out(wq), out(wk)],
        out_shape=[jax.ShapeDtypeStruct((T, wq), BF16), jax.ShapeDtypeStruct((T, wk), F32),
                   jax.ShapeDtypeStruct((T, wq), BF16), jax.ShapeDtypeStruct((T, wk), F32)],
        compiler_params=_cparams(("parallel",)),
        name="qk_norm",
    )(z, z, z, z, gsq, gsk, gdq, gdk, bd)


def _sb_body(*refs, tq, n_cache, tkc):
    if n_cache:
        q_ref, kn_ref, vn_ref, kc_ref, vc_ref, o_ref = refs
    else:
        q_ref, kn_ref, vn_ref, o_ref = refs
    qb = pl.program_id(1) if not n_cache else 0
    qs = [q_ref[:, e * LANES:(e + 1) * LANES] for e in range(2)]

    def tri(n):
        r = lax.broadcasted_iota(jnp.int32, (n, n), 0)
        c = lax.broadcasted_iota(jnp.int32, (n, n), 1)
        return (r > c).astype(BF16)

    def visit(state, kblk, vblk, u, valid):
        kb16 = kblk.astype(BF16)
        vb16 = vblk.astype(BF16)
        new = []
        for e in range(2):
            carry, acc = state[e]
            z = _dot_nt(qs[e], kb16)
            lg = _neg_softplus(z)
            if valid is not None:
                lg = jnp.where(valid, lg, 0.0)
            inner = _split_dot(lg, u)
            a = jnp.exp(z + lg + inner + carry)
            if valid is not None:
                a = jnp.where(valid, a, 0.0)
            acc = acc + jnp.dot(a.astype(BF16), vb16, preferred_element_type=F32)
            carry = carry + inner[:, :1] + lg[:, :1]
            new.append((carry, acc))
        return tuple(new)

    zero = (jnp.zeros((tq, 1), F32), jnp.zeros((tq, LANES), F32))
    state = (zero, zero)

    r = lax.broadcasted_iota(jnp.int32, (tq, tq), 0)
    c = lax.broadcasted_iota(jnp.int32, (tq, tq), 1)
    u_new = tri(tq)
    if n_cache:
        state = visit(state, kn_ref[...], vn_ref[...], u_new, c < r)
    else:
        row0 = pl.multiple_of(qb * tq, tq)
        state = visit(state, kn_ref[pl.ds(row0, tq), :], vn_ref[pl.ds(row0, tq), :], u_new, c < r)

    if not n_cache:
        def step(i, st):
            r0 = pl.multiple_of((qb - 1 - i) * tq, tq)
            return visit(st, kn_ref[pl.ds(r0, tq), :], vn_ref[pl.ds(r0, tq), :], u_new, None)
        state = lax.fori_loop(0, qb, step, state)
    else:
        u_c = tri(tkc)

        def step(i, st):
            r0 = pl.multiple_of((n_cache - 1 - i) * tkc, tkc)
            return visit(st, kc_ref[0, 0, pl.ds(r0, tkc), :], vc_ref[0, 0, pl.ds(r0, tkc), :], u_c, None)
        state = lax.fori_loop(0, n_cache, step, state)

    lane = lax.broadcasted_iota(jnp.int32, (tq, LANES), 1)
    o_ref[...] = jnp.where(lane < HD, state[0][1], state[1][1])


def _sb_prompt(qn, kn, z, n_rows):
    tq = 128
    vblk = OFF_SV // LANES
    return pl.pallas_call(
        functools.partial(_sb_body, tq=tq, n_cache=0, tkc=0),
        grid=(NH // 2, n_rows // tq),
        in_specs=[pl.BlockSpec((tq, 2 * LANES), lambda p, i: (i, p)),
                  pl.BlockSpec((n_rows, LANES), lambda p, i: (0, p)),
                  pl.BlockSpec((n_rows, LANES), lambda p, i: (0, vblk + p))],
        out_specs=pl.BlockSpec((tq, LANES), lambda p, i: (i, p)),
        out_shape=jax.ShapeDtypeStruct((n_rows, BW), F32),
        compiler_params=_cparams(("parallel", "arbitrary")),
        name="sb_prompt",
    )(qn, kn, z)


def _sb_sample(qn, kn, z, cache_k, cache_v, layer, row_off, nb, L):
    P = cache_k.shape[2]
    tkc = _pick(P, (256, 128))
    vblk = OFF_SV // LANES
    rb = row_off // L
    return pl.pallas_call(
        functools.partial(_sb_body, tq=L, n_cache=P // tkc, tkc=tkc),
        grid=(nb, NH // 2),
        in_specs=[pl.BlockSpec((L, 2 * LANES), lambda b, p: (rb + b, p)),
                  pl.BlockSpec((L, LANES), lambda b, p: (rb + b, p)),
                  pl.BlockSpec((L, LANES), lambda b, p: (rb + b, vblk + p)),
                  pl.BlockSpec((1, 1, P, LANES), lambda b, p: (layer, b, 0, p)),
                  pl.BlockSpec((1, 1, P, LANES), lambda b, p: (layer, b, 0, p))],
        out_specs=pl.BlockSpec((L, LANES), lambda b, p: (b, p)),
        out_shape=jax.ShapeDtypeStruct((nb * L, BW), F32),
        compiler_params=_cparams(("parallel", "parallel")),
        name="sb_sample",
    )(qn, kn, z, cache_k, cache_v)


def _sortable(x):
    b = pltpu.bitcast(x + 0.0, jnp.int32)
    return jnp.where(b < 0, b ^ jnp.int32(0x7FFFFFFF), b)


def _select_topk(key_ref, n_blk, tkb, topk, rows):
    def count(pred):
        def body(i, acc):
            c0 = pl.multiple_of(i * tkb, tkb)
            blk = key_ref[:, pl.ds(c0, tkb)]
            col = c0 + lax.broadcasted_iota(jnp.int32, (rows, tkb), 1)
            return acc + jnp.sum(jnp.where(pred(blk, col), 1.0, 0.0), axis=1, keepdims=True)
        return lax.fori_loop(0, n_blk, body, jnp.zeros((rows, 1), F32))

    def bit_step(i, thr):
        cand = thr + (jnp.int32(1) << (31 - i))
        n_ge = count(lambda blk, col: blk >= cand)
        return jnp.where(n_ge >= topk, cand, thr)

    thr = lax.fori_loop(0, 32, bit_step, jnp.full((rows, 1), jnp.iinfo(jnp.int32).min, jnp.int32))
    n_gt = count(lambda blk, col: blk > thr)
    n_ge = count(lambda blk, col: blk >= thr)
    need = topk - n_gt
    n_col_bits = max(1, int(math.ceil(math.log2(key_ref.shape[1] + 1))))

    def tie_search():
        def col_step(i, j):
            cand = j + (jnp.int32(1) << (n_col_bits - 1 - i))
            n = count(lambda blk, col: (blk == thr) & (col < cand))
            return jnp.where(n < need, cand, j)
        return lax.fori_loop(0, n_col_bits, col_step, jnp.zeros((rows, 1), jnp.int32))

    any_tie = jnp.max(jnp.abs(n_ge - topk)) > 0
    j_hi = lax.cond(any_tie, tie_search,
                    lambda: jnp.full((rows, 1), jnp.iinfo(jnp.int32).max, jnp.int32))
    return thr, j_hi


def _idx_scores(qi, wi, kblk16, k_real):
    acc = None
    for h in range(NH):
        qh = qi[:, h * LANES:h * LANES + k_real].astype(BF16)
        d = jnp.maximum(_dot_nt(qh, kblk16), 0.0) * (HD ** -0.5)
        t = (wi[:, h:h + 1] * (NH ** -0.5)) * d
        acc = t if acc is None else acc + t
    return acc


def _dsa_sel_prompt_body(qi_ref, wi_ref, ki_ref, m_ref, key_ref, *, tq, tkb, topk, n_keys):
    qb = pl.program_id(0)
    n_blk = ((qb + 1) * tq + tkb - 1) // tkb
    qi = qi_ref[...]
    wi = wi_ref[...]
    q_chunk = (qb * tq + lax.broadcasted_iota(jnp.int32, (tq, tkb), 0)) // CHUNK

    def score_blk(i, _):
        c0 = pl.multiple_of(i * tkb, tkb)
        s = _idx_scores(qi, wi, ki_ref[pl.ds(c0, tkb), :].astype(BF16), LANES)
        k_chunk = (c0 + lax.broadcasted_iota(jnp.int32, (tq, tkb), 1)) // CHUNK
        s = jnp.where(k_chunk <= q_chunk, s, -jnp.inf)
        key_ref[:, pl.ds(c0, tkb)] = _sortable(s)
        return 0
    lax.fori_loop(0, n_blk, score_blk, 0)

    thr, j_hi = _select_topk(key_ref, n_blk, tkb, topk, tq)
    neg_inf_key = _sortable(jnp.full((1, 1), -jnp.inf, F32))

    def write_blk(i, _):
        c0 = pl.multiple_of(i * tkb, tkb)
        blk = key_ref[:, pl.ds(c0, tkb)]
        col = c0 + lax.broadcasted_iota(jnp.int32, (tq, tkb), 1)
        sel = ((blk > thr) | ((blk == thr) & (col <= j_hi))) & (blk > neg_inf_key)
        m_ref[:, pl.ds(c0, tkb)] = jnp.where(sel, 1.0, 0.0).astype(BF16)
        return 0
    lax.fori_loop(0, n_blk, write_blk, 0)

    def zero_blk(i, _):
        m_ref[:, pl.ds(pl.multiple_of(i * tkb, tkb), tkb)] = jnp.zeros((tq, tkb), BF16)
        return 0
    lax.fori_loop(n_blk, n_keys // tkb, zero_blk, 0)


def _dsa_sel_prompt(z, n_rows, topk):
    tq, tkb = 128, 512
    return pl.pallas_call(
        functools.partial(_dsa_sel_prompt_body, tq=tq, tkb=tkb, topk=topk, n_keys=n_rows),
        grid=(n_rows // tq,),
        in_specs=[pl.BlockSpec((tq, NH * LANES), lambda i: (i, OFF_QI // (NH * LANES))),
                  pl.BlockSpec((tq, LANES), lambda i: (i, OFF_WI // LANES)),
                  pl.BlockSpec((n_rows, LANES), lambda i: (0, OFF_KI // LANES))],
        out_specs=pl.BlockSpec((tq, n_rows), lambda i: (i, 0)),
        out_shape=jax.ShapeDtypeStruct((n_rows, n_rows), BF16),
        scratch_shapes=[pltpu.VMEM((tq, n_rows), jnp.int32)],
        compiler_params=_cparams(("parallel",)),
        name="dsa_select_prompt",
    )(z, z, z)


def _dsa_sel_sample_body(qi_ref, wi_ref, kin_ref, kic_ref, m_ref, key_ref, *, L, P, tkb, topk):
    qi = qi_ref[...]
    wi = wi_ref[...]
    n_c = P // tkb

    def score_blk(i, _):
        c0 = pl.multiple_of(i * tkb, tkb)
        s = _idx_scores(qi, wi, kic_ref[0, 0, pl.ds(c0, tkb), :].astype(BF16), HD)
        key_ref[:, pl.ds(c0, tkb)] = _sortable(s)
        return 0
    lax.fori_loop(0, n_c, score_blk, 0)

    knew = jnp.concatenate([kin_ref[...], jnp.zeros((tkb - L, LANES), F32)], axis=0).astype(BF16)
    s = _idx_scores(qi, wi, knew, LANES)
    col = lax.broadcasted_iota(jnp.int32, (L, tkb), 1)
    key_ref[:, pl.ds(P, tkb)] = _sortable(jnp.where(col < L, s, -jnp.inf))

    thr, j_hi = _select_topk(key_ref, n_c + 1, tkb, topk, L)
    neg_inf_key = _sortable(jnp.full((1, 1), -jnp.inf, F32))

    def write_blk(i, _):
        c0 = pl.multiple_of(i * tkb, tkb)
        blk = key_ref[:, pl.ds(c0, tkb)]
        cc = c0 + lax.broadcasted_iota(jnp.int32, (L, tkb), 1)
        sel = ((blk > thr) | ((blk == thr) & (cc <= j_hi))) & (blk > neg_inf_key)
        m_ref[:, pl.ds(c0, tkb)] = jnp.where(sel, 1.0, 0.0).astype(BF16)
        return 0
    lax.fori_loop(0, n_c + 1, write_blk, 0)


def _dsa_sel_sample(z, cache_ki, layer, row_off, nb, L, topk):
    P = cache_ki.shape[2]
    tkb = _pick(P, (512, 256, 128))
    rb = row_off // L
    return pl.pallas_call(
        functools.partial(_dsa_sel_sample_body, L=L, P=P, tkb=tkb, topk=topk),
        grid=(nb,),
        in_specs=[pl.BlockSpec((L, NH * LANES), lambda b: (rb + b, OFF_QI // (NH * LANES))),
                  pl.BlockSpec((L, LANES), lambda b: (rb + b, OFF_WI // LANES)),
                  pl.BlockSpec((L, LANES), lambda b: (rb + b, OFF_KI // LANES)),
                  pl.BlockSpec((1, 1, P, HD), lambda b: (layer, b, 0, 0))],
        out_specs=pl.BlockSpec((L, P + tkb), lambda b: (b, 0)),
        out_shape=jax.ShapeDtypeStruct((nb * L, P + tkb), BF16),
        scratch_shapes=[pltpu.VMEM((L, P + tkb), jnp.int32)],
        compiler_params=_cparams(("parallel",)),
        name="dsa_select_sample",
    )(z, z, z, cache_ki)


def _dsa_attn_body(*refs, tq, tkb, n_cache, prompt):
    if prompt:
        q_ref, m_ref, kn_ref, vn_ref, o_ref = refs
        qb = pl.program_id(1)
        n_blk = ((qb + 1) * tq + tkb - 1) // tkb
    else:
        q_ref, m_ref, kn_ref, vn_ref, kc_ref, vc_ref, o_ref = refs
        n_blk = n_cache
    qs = [q_ref[:, e * LANES:(e + 1) * LANES] for e in range(2)]

    def visit(state, kblk, vblk, mblk):
        kb16 = kblk.astype(BF16)
        vb16 = vblk.astype(BF16)
        sel = mblk > 0
        new = []
        for e in range(2):
            m, l, acc = state[e]
            s = jnp.where(sel, _dot_nt(qs[e], kb16), NEG_BIG)
            m_new = jnp.maximum(m, jnp.max(s, axis=1, keepdims=True))
            p = jnp.where(sel, jnp.exp(s - m_new), 0.0)
            alpha = jnp.exp(m - m_new)
            l = alpha * l + jnp.sum(p, axis=1, keepdims=True)
            acc = alpha * acc + jnp.dot(p.astype(BF16), vb16, preferred_element_type=F32)
            new.append((m_new, l, acc))
        return tuple(new)

    init = (jnp.full((tq, 1), NEG_BIG, F32), jnp.zeros((tq, 1), F32), jnp.zeros((tq, LANES), F32))
    state = (init, init)

    def step(i, st):
        c0 = pl.multiple_of(i * tkb, tkb)
        if prompt:
            return visit(st, kn_ref[pl.ds(c0, tkb), :], vn_ref[pl.ds(c0, tkb), :], m_ref[:, pl.ds(c0, tkb)])
        return visit(st, kc_ref[0, 0, pl.ds(c0, tkb), :], vc_ref[0, 0, pl.ds(c0, tkb), :],
                     m_ref[:, pl.ds(c0, tkb)])
    state = lax.fori_loop(0, n_blk, step, state)

    if not prompt:
        pad = jnp.zeros((tkb - tq, LANES), F32)
        knew = jnp.concatenate([kn_ref[...], pad], axis=0)
        vnew = jnp.concatenate([vn_ref[...], pad], axis=0)
        state = visit(state, knew, vnew, m_ref[:, pl.ds(n_cache * tkb, tkb)])

    lane = lax.broadcasted_iota(jnp.int32, (tq, LANES), 1)
    o0 = state[0][2] / state[0][1]
    o1 = state[1][2] / state[1][1]
    o_ref[...] = jnp.where(lane < HD, o0, o1)


def _dsa_attn_prompt(qn, kn, z, mask, n_rows):
    tq, tkb = 128, 512
    vblk = OFF_DV // LANES
    return pl.pallas_call(
        functools.partial(_dsa_attn_body, tq=tq, tkb=tkb, n_cache=0, prompt=True),
        grid=(NH // 2, n_rows // tq),
        in_specs=[pl.BlockSpec((tq, 2 * LANES), lambda p, i: (i, p)),
                  pl.BlockSpec((tq, n_rows), lambda p, i: (i, 0)),
                  pl.BlockSpec((n_rows, LANES), lambda p, i: (0, p)),
                  pl.BlockSpec((n_rows, LANES), lambda p, i: (0, vblk + p))],
        out_specs=pl.BlockSpec((tq, LANES), lambda p, i: (i, p)),
        out_shape=jax.ShapeDtypeStruct((n_rows, BW), F32),
        compiler_params=_cparams(("parallel", "arbitrary")),
        name="dsa_attn_prompt",
    )(qn, mask, kn, z)


def _dsa_attn_sample(qn, kn, z, mask, cache_k, cache_v, layer, row_off, nb, L):
    P = cache_k.shape[2]
    tkb = mask.shape[1] - P
    vblk = OFF_DV // LANES
    rb = row_off // L
    return pl.pallas_call(
        functools.partial(_dsa_attn_body, tq=L, tkb=tkb, n_cache=P // tkb, prompt=False),
        grid=(nb, NH // 2),
        in_specs=[pl.BlockSpec((L, 2 * LANES), lambda b, p: (rb + b, p)),
                  pl.BlockSpec((L, P + tkb), lambda b, p: (b, 0)),
                  pl.BlockSpec((L, LANES), lambda b, p: (rb + b, p)),
                  pl.BlockSpec((L, LANES), lambda b, p: (rb + b, vblk + p)),
                  pl.BlockSpec((1, 1, P, LANES), lambda b, p: (layer, b, 0, p)),
                  pl.BlockSpec((1, 1, P, LANES), lambda b, p: (layer, b, 0, p))],
        out_specs=pl.BlockSpec((L, LANES), lambda b, p: (b, p)),
        out_shape=jax.ShapeDtypeStruct((nb * L, BW), F32),
        compiler_params=_cparams(("parallel", "parallel")),
        name="dsa_attn_sample",
    )(qn, mask, kn, z, cache_k, cache_v)


def _gla_consts(C):
    nlev = int(math.log2(C))
    t = np.arange(C)
    cum = (t[:, None] >= t[None, :]).astype(np.float32)
    sel = np.zeros((nlev * C, C), np.float32)
    for l in range(nlev):
        m = 1 << l
        mid = (t // (2 * m)) * 2 * m + m
        sel[l * C + t, mid - 1] = 1.0
    return jnp.asarray(cum, BF16), jnp.asarray(sel, BF16)


def _gla_body(gq_ref, gk_ref, gv_ref, gr_ref, ga_ref, a2_ref, ab_ref, on_ref, cum_ref, sel_ref,
              s0_ref, o_ref, sout_ref, st_ref, *, C, n_chunks):
    ci = pl.program_id(1)
    nlev = int(math.log2(C))

    @pl.when(ci == 0)
    def _():
        st_ref[...] = s0_ref[0]

    x = jnp.dot(ga_ref[...].astype(BF16), a2_ref[...], preferred_element_type=F32) + ab_ref[...]
    g = _neg_softplus(-x) / GLA_GATE_NORM
    b = _split_dot_left(cum_ref[...], g)
    c_all = _split_dot_left(sel_ref[...], b)

    row = lax.broadcasted_iota(jnp.int32, (C, C), 0)
    col = lax.broadcasted_iota(jnp.int32, (C, C), 1)
    rowl = lax.broadcasted_iota(jnp.int32, (C, LANES), 0)

    for h in range(GLA_H):
        sl = slice(h * LANES, (h + 1) * LANES)
        q = gq_ref[:, sl] * (HD ** -0.5)
        k = gk_ref[:, sl]
        v = gv_ref[:, sl]
        bh = b[:, sl]
        a = jnp.where(row == col, jnp.sum(q * k, axis=1, keepdims=True), 0.0)
        for l in range(nlev):
            m = 1 << l
            later = (rowl & m) != 0
            ch = c_all[l * C:(l + 1) * C, sl]
            qe = jnp.where(later, q * jnp.exp(jnp.where(later, bh - ch, 0.0)), 0.0)
            ke = jnp.where(later, 0.0, k * jnp.exp(jnp.where(later, 0.0, ch - bh)))
            al = _dot_nt(qe.astype(BF16), ke.astype(BF16))
            a = a + jnp.where((row // (2 * m)) == (col // (2 * m)), al, 0.0)
        st = st_ref[h]
        o = jnp.dot(a.astype(BF16), v.astype(BF16), preferred_element_type=F32)
        o = o + _dot_nt((q * jnp.exp(bh)).astype(BF16), st.astype(BF16))
        b_last = bh[C - 1:C, :]
        kx = k * jnp.exp(b_last - bh)
        st_ref[h] = st * jnp.exp(b_last) + _dot_tn(v.astype(BF16), kx.astype(BF16))
        on = o * lax.rsqrt(jnp.mean(o * o, axis=-1, keepdims=True) + NORM_EPS) * on_ref[...]
        gr = gr_ref[:, sl]
        o_ref[:, sl] = on * (gr * jax.nn.sigmoid(gr))

    @pl.when(ci == n_chunks - 1)
    def _():
        sout_ref[0] = st_ref[...]


def _gla(z, a2p, abp, onorm, s0t, row_off, n_seq, L):
    C = min(CHUNK, L)
    n_chunks = L // C
    rb = row_off // C
    cum, sel = _gla_consts(C)
    blk = lambda off: pl.BlockSpec((C, BW), lambda s, c: (rb + s * n_chunks + c, off // BW))
    full = lambda a: pl.BlockSpec(a.shape, lambda s, c: (0,) * a.ndim)
    return pl.pallas_call(
        functools.partial(_gla_body, C=C, n_chunks=n_chunks),
        grid=(n_seq, n_chunks),
        in_specs=[blk(OFF_GQ), blk(OFF_GK), blk(OFF_GV), blk(OFF_GR),
                  pl.BlockSpec((C, LANES), lambda s, c: (rb + s * n_chunks + c, OFF_GA // LANES)),
                  full(a2p), full(abp), full(onorm), full(cum), full(sel),
                  pl.BlockSpec((1, GLA_H, GLA_DV, LANES), lambda s, c: (s, 0, 0, 0))],
        out_specs=[pl.BlockSpec((C, BW), lambda s, c: (s * n_chunks + c, 0)),
                   pl.BlockSpec((1, GLA_H, GLA_DV, LANES), lambda s, c: (s, 0, 0, 0))],
        out_shape=[jax.ShapeDtypeStruct((n_seq * L, BW), F32),
                   jax.ShapeDtypeStruct((n_seq, GLA_H, GLA_DV, LANES), F32)],
        scratch_shapes=[pltpu.VMEM((GLA_H, GLA_DV, LANES), F32)],
        compiler_params=_cparams(("parallel", "arbitrary")),
        name="gla",
    )(z, z, z, z, z, a2p, abp, onorm, cum, sel, s0t)


def _rw_prep_body(z_ref, prev_ref, mu_ref, w0_ref, a0_ref, kkp_ref, kap_ref, rkp_ref,
                  w2_ref, a2_ref, g2_ref, seg_ref,
                  r_ref, w_ref, k_ref, v_ref, nkk_ref, kka_ref, gate_ref, bonus_ref):
    zz = z_ref[...]
    zm = zz + mu_ref[...] * (prev_ref[...] - zz)
    rr = zm[:, 0:BW]
    rk = zm[:, BW:2 * BW]
    rv = zm[:, 2 * BW:3 * BW]
    lora = zm[:, 3 * BW:3 * BW + LANES]
    gl = zm[:, 3 * BW + LANES:3 * BW + 2 * LANES]
    xw = w0_ref[...] + jnp.dot(jnp.tanh(lora).astype(BF16), w2_ref[...], preferred_element_type=F32)
    wlog = _neg_softplus(-xw) - 0.5
    decay = jnp.exp(-jnp.exp(wlog))
    a = jax.nn.sigmoid(a0_ref[...] + jnp.dot(lora.astype(BF16), a2_ref[...], preferred_element_type=F32))
    gate = jnp.dot(jax.nn.sigmoid(gl).astype(BF16), g2_ref[...], preferred_element_type=F32)
    kkf = rk * kkp_ref[...]
    nrm = jnp.sqrt(_split_dot(kkf * kkf, seg_ref[...]))
    kk = kkf / jnp.maximum(nrm, 1e-12)
    kmod = rk * (1.0 + (a - 1.0) * kap_ref[...])
    coef = _split_dot(rr * kmod * rkp_ref[...], seg_ref[...])
    r_ref[...] = rr
    w_ref[...] = decay
    k_ref[...] = kmod
    v_ref[...] = rv
    nkk_ref[...] = -kk
    kka_ref[...] = kk * a
    gate_ref[...] = gate
    bonus_ref[...] = coef * rv


def _rw_prep(z, prev, mu, w0, a0, kkp, kap, rkp, w2p, a2p, g2, seg):
    T = z.shape[0]
    tm = _pick(T, (256, 128))
    W = 4 * BW
    par = lambda a: pl.BlockSpec(a.shape, lambda i: (0,) * a.ndim)
    out = pl.BlockSpec((tm, BW), lambda i: (i, 0))
    return pl.pallas_call(
        _rw_prep_body,
        grid=(T // tm,),
        in_specs=[pl.BlockSpec((tm, W), lambda i: (i, 0)), pl.BlockSpec((tm, W), lambda i: (i, 0)),
                  par(mu), par(w0), par(a0), par(kkp), par(kap), par(rkp),
                  par(w2p), par(a2p), par(g2), par(seg)],
        out_specs=[out] * 8,
        out_shape=[jax.ShapeDtypeStruct((T, BW), F32)] * 8,
        compiler_params=_cparams(("parallel",)),
        name="rwkv_prep",
    )(z, prev, mu, w0, a0, kkp, kap, rkp, w2p, a2p, g2, seg)


def _rw_rec_body(r_ref, w_ref, k_ref, v_ref, nkk_ref, kka_ref, g_ref, s0_ref,
                 ot_ref, sout_ref, s_ref, *, TT, L, n_tiles):
    ti = pl.program_id(0)
    npair = NH // 2
    gm = g_ref[...]
    lane = lax.broadcasted_iota(jnp.int32, (HD, LANES), 1)
    sub = lax.broadcasted_iota(jnp.int32, (HD, LANES), 0)
    eye2 = (lane % HD) == sub
    lo_half = lane < HD
    ones_lo = jnp.where(lax.broadcasted_iota(jnp.int32, (LANES, LANES), 0) < HD, 1.0, 0.0).astype(BF16)
    ones_hi = jnp.where(lax.broadcasted_iota(jnp.int32, (LANES, LANES), 0) >= HD, 1.0, 0.0).astype(BF16)

    def token_group(gi, _):
        t0 = pl.multiple_of(gi * SUBLANES, SUBLANES)
        for p in range(npair):
            sl = slice(p * LANES, (p + 1) * LANES)
            rows = [ref[pl.ds(t0, SUBLANES), sl]
                    for ref in (r_ref, w_ref, k_ref, v_ref, nkk_ref, kka_ref)]
            r0 = 2 * p * HD
            s = s_ref[p]
            o_lo_acc = ot_ref[r0:r0 + HD, :]
            o_hi_acc = ot_ref[r0 + HD:r0 + 2 * HD, :]
            for j in range(SUBLANES):
                rr, ww, kk, vv, nk, ka = [x[j:j + 1, :] for x in rows]
                sa = _split_dot(s * nk, gm)
                vb = jnp.dot(jnp.where(eye2, vv, 0.0).astype(BF16), gm, preferred_element_type=F32)
                s = s * ww + sa * ka + vb * kk
                pr = (s * rr).astype(BF16)
                hit = lane == t0 + j
                o_lo_acc = jnp.where(hit, jnp.dot(pr, ones_lo, preferred_element_type=F32), o_lo_acc)
                o_hi_acc = jnp.where(hit, jnp.dot(pr, ones_hi, preferred_element_type=F32), o_hi_acc)
            s_ref[p] = s
            ot_ref[r0:r0 + HD, :] = o_lo_acc
            ot_ref[r0 + HD:r0 + 2 * HD, :] = o_hi_acc
        return 0

    ot_ref[...] = jnp.zeros((BW, TT), F32)
    if L >= TT:
        @pl.when(ti == 0)
        def _():
            s_ref[...] = s0_ref[0]
        lax.fori_loop(0, TT // SUBLANES, token_group, 0)

        @pl.when(ti == n_tiles - 1)
        def _():
            sout_ref[0] = s_ref[...]
    else:
        for q in range(TT // L):
            s_ref[...] = s0_ref[q]
            lax.fori_loop(q * L // SUBLANES, (q + 1) * L // SUBLANES, token_group, 0)
            sout_ref[q] = s_ref[...]


def _rw_rec(r, w, k, v, nkk, kka, gmat, s0, row_off, n_seq, L):
    TT = LANES
    n_tok = n_seq * L
    n_tiles = n_tok // TT
    rb = row_off // TT
    spt = max(1, TT // L)
    tok = pl.BlockSpec((TT, BW), lambda i: (rb + i, 0))
    if L >= TT:
        sspec = pl.BlockSpec((1, NH // 2, HD, LANES), lambda i: (0, 0, 0, 0))
    else:
        sspec = pl.BlockSpec((spt, NH // 2, HD, LANES), lambda i: (i, 0, 0, 0))
    return pl.pallas_call(
        functools.partial(_rw_rec_body, TT=TT, L=L, n_tiles=n_tiles),
        grid=(n_tiles,),
        in_specs=[tok] * 6 + [pl.BlockSpec((LANES, LANES), lambda i: (0, 0)), sspec],
        out_specs=[pl.BlockSpec((BW, TT), lambda i: (0, i)), sspec],
        out_shape=[jax.ShapeDtypeStruct((BW, n_tok), F32),
                   jax.ShapeDtypeStruct(s0.shape, F32)],
        scratch_shapes=[pltpu.VMEM((NH // 2, HD, LANES), F32)],
        compiler_params=_cparams(("arbitrary",)),
        name="rwkv_recurrence",
    )(r, w, k, v, nkk, kka, gmat, s0)


def _rw_post_body(o_ref, bonus_ref, gate_ref, lnw_ref, lnb_ref, seg_ref, out_ref):
    o = o_ref[...]
    mu = _split_dot(o, seg_ref[...]) * (1.0 / HD)
    d = o - mu
    var = _split_dot(d * d, seg_ref[...]) * (1.0 / HD)
    on = d * lax.rsqrt(var + RW_LN_EPS) * lnw_ref[...] + lnb_ref[...]
    out_ref[...] = (on + bonus_ref[...]) * gate_ref[...]


def _rw_post(o, bonus, gate, lnw, lnb, seg):
    T = o.shape[0]
    tm = _pick(T, (512, 256, 128))
    blk = pl.BlockSpec((tm, BW), lambda i: (i, 0))
    par = lambda a: pl.BlockSpec(a.shape, lambda i: (0,) * a.ndim)
    return pl.pallas_call(
        _rw_post_body,
        grid=(T // tm,),
        in_specs=[blk, blk, blk, par(lnw), par(lnb), par(seg)],
        out_specs=blk,
        out_shape=jax.ShapeDtypeStruct((T, BW), F32),
        compiler_params=_cparams(("parallel",)),
        name="rwkv_post",
    )(o, bonus, gate, lnw, lnb, seg)


def _heads_alt(w):
    d = w.shape[0]
    w4 = w.reshape(d, NH // 2, 2, HD)
    zero = jnp.zeros((d, NH // 2, HD), w.dtype)
    ev = jnp.concatenate([w4[:, :, 0], zero], -1)
    od = jnp.concatenate([zero, w4[:, :, 1]], -1)
    return jnp.stack([ev, od], 2).reshape(d, NH * LANES)


def _heads_lo(w, nh):
    d = w.shape[0]
    w3 = w.reshape(d, nh, HD)
    return jnp.concatenate([w3, jnp.zeros_like(w3)], -1).reshape(d, nh * LANES)


def _padc(w, n):
    return jnp.pad(w, ((0, 0), (0, n - w.shape[1])))


def _rw_reorder(a):
    o = 0
    parts = {}
    for name, wd in (("rr", BW), ("wl", RW_LORA), ("rk", BW), ("rv", BW), ("al", RW_LORA), ("gl", RW_GATE_LORA)):
        parts[name] = a[..., o:o + wd]
        o += wd
    return jnp.concatenate([parts[n] for n in ("rr", "rk", "rv", "wl", "al", "gl")], axis=-1)


def _rw_unorder(a):
    rr, rk, rv = a[..., 0:BW], a[..., BW:2 * BW], a[..., 2 * BW:3 * BW]
    wl = a[..., 3 * BW:3 * BW + RW_LORA]
    al = a[..., 3 * BW + RW_LORA:3 * BW + 2 * RW_LORA]
    gl = a[..., 3 * BW + 2 * RW_LORA:RW_COLS]
    return jnp.concatenate([rr, wl, rk, rv, al, gl], axis=-1)


def _relayout_w_in(w, d_model):
    w = w.astype(BF16)
    o = 0

    def take(n):
        nonlocal o
        s = w[:, o:o + n]
        o += n
        return s
    sq, sk, sv = take(BW), take(BW), take(BW)
    dq, dk, dv, qi, ki, wi = take(BW), take(BW), take(BW), take(BW), take(HD), take(NH)
    gq, gk = take(GLA_H * HD), take(GLA_H * HD)
    gv, gr, ga = take(BW), take(BW), take(GLA_LOWRANK)
    rw = take(RW_COLS)
    gate = take(4 * d_model)
    cols = [_padc(_rw_reorder(rw), 4 * BW), _heads_alt(sq), sk, sv, _heads_alt(dq), dk, dv,
            _heads_lo(qi, NH), _heads_lo(gq, GLA_H), _heads_lo(gk, GLA_H), gv, gr,
            _padc(ki, LANES), _padc(wi, LANES), _padc(ga, 2 * LANES), gate]
    return jnp.concatenate(cols, axis=1)


def _relayout_ffn(w_up, w_down, tf):
    d, f2 = w_up.shape
    f = f2 // 2
    fp = -(-f // tf) * tf
    g = _padc(w_up[:, :f].astype(BF16), fp).reshape(d, fp // tf, tf)
    u = _padc(w_up[:, f:].astype(BF16), fp).reshape(d, fp // tf, tf)
    wup = jnp.concatenate([g, u], axis=2).reshape(d, 2 * fp)
    wdn = jnp.pad(w_down.astype(BF16), ((0, fp - f), (0, 0)))
    return wup, wdn


def _gain_alt(g):
    return jnp.tile(g, 2 * NH)[None, :]


def kernel(x_prompt, x_sample, cache_sb_k, cache_sb_v, cache_dsa_k, cache_dsa_v, cache_dsa_kidx, state_gla, state_rwkv, state_rwkv_shift, ffn1_norm, ffn1_up, ffn1_down, mix_norm, w_in, sb_qnorm, sb_knorm, dsa_qnorm, dsa_knorm, gla_a2, gla_ab, gla_onorm, rwkv_mu, rwkv_w0, rwkv_w2, rwkv_a0, rwkv_a2, rwkv_g2, rwkv_kk, rwkv_ka, rwkv_rk, rwkv_lnw, rwkv_lnb, w_branch, w_out, ffn2_norm, ffn2_up, ffn2_down):
    depth = w_in.shape[0]
    bp, lp, d_model = x_prompt.shape
    nb, ls, _ = x_sample.shape
    assert bp == 1, "prompt group is one sequence"
    tp = bp * lp
    ts = nb * ls
    past = cache_sb_k.shape[2]
    tf = 512

    x = jnp.concatenate([x_prompt.reshape(tp, d_model), x_sample.reshape(ts, d_model)], axis=0)
    ck_sb = cache_sb_k.reshape(depth, nb, past, BW)
    cv_sb = cache_sb_v.reshape(depth, nb, past, BW)
    ck_dsa = cache_dsa_k.reshape(depth, nb, past, BW)
    cv_dsa = cache_dsa_v.reshape(depth, nb, past, BW)

    bd = jnp.asarray(np.kron(np.eye(2 * NH), np.full((HD, HD), 1.0 / HD)), BF16)
    seg = jnp.asarray(np.kron(np.eye(NH), np.ones((HD, HD))), BF16)
    gmat = jnp.asarray(np.kron(np.eye(2), np.ones((HD, HD))), BF16)
    topk_p = min(TOPK_MAX, lp // 4)
    topk_s = min(TOPK_MAX, (past + ls) // 4)

    new_p = [[] for _ in range(8)]
    new_s = [[] for _ in range(8)]
    for i in range(depth):
        wup1, wdn1 = _relayout_ffn(ffn1_up[i], ffn1_down[i], tf)
        wup2, wdn2 = _relayout_ffn(ffn2_up[i], ffn2_down[i], tf)
        win = _relayout_w_in(w_in[i], d_model)

        x = _ffn(x, ffn1_norm[i][None], wup1, wdn1, tf)
        z = _proj(x, mix_norm[i][None], win)

        sqn, skn, dqn, dkn = _qknorm(z, _gain_alt(sb_qnorm[i]), jnp.tile(sb_knorm[i], NH)[None],
                                     _gain_alt(dsa_qnorm[i]), jnp.tile(dsa_knorm[i], NH)[None], bd)
        o_sb = jnp.concatenate([
            _sb_prompt(sqn, skn, z, tp),
            _sb_sample(sqn, skn, z, ck_sb, cv_sb, i, tp, nb, ls)], axis=0)
        mask_p = _dsa_sel_prompt(z, tp, topk_p)
        mask_s = _dsa_sel_sample(z, cache_dsa_kidx, i, tp, nb, ls, topk_s)
        o_dsa = jnp.concatenate([
            _dsa_attn_prompt(dqn, dkn, z, mask_p, tp),
            _dsa_attn_sample(dqn, dkn, z, mask_s, ck_dsa, cv_dsa, i, tp, nb, ls)], axis=0)

        a2p = jnp.pad(_heads_lo(gla_a2[i], GLA_H), ((0, LANES - GLA_LOWRANK), (0, 0))).astype(BF16)
        abp = _heads_lo(gla_ab[i][None], GLA_H)
        onorm = gla_onorm[i][None]

        def gla_state_in(s):
            return jnp.pad(jnp.swapaxes(s, 2, 3), ((0, 0), (0, 0), (0, 0), (0, LANES - HD)))

        def gla_state_out(s):
            return jnp.swapaxes(s[..., :HD], 2, 3)
        og_p, sg_p = _gla(z, a2p, abp, onorm, jnp.zeros((bp, GLA_H, GLA_DV, LANES), F32), 0, bp, lp)
        og_s, sg_s = _gla(z, a2p, abp, onorm, gla_state_in(state_gla[i]), tp, nb, ls)
        o_gla = jnp.concatenate([og_p, og_s], axis=0)

        zrw = z[:, :RW_COLS]
        shift_s = _rw_reorder(state_rwkv_shift[i])
        prev_p = jnp.concatenate([jnp.zeros((1, RW_COLS), F32), zrw[:tp - 1]], axis=0)
        prev_s = jnp.concatenate([shift_s, zrw[tp:].reshape(nb, ls, RW_COLS)[:, :-1]], axis=1)
        prev = _padc(jnp.concatenate([prev_p, prev_s.reshape(ts, RW_COLS)], axis=0), 4 * BW)
        mu = _padc(_rw_reorder(rwkv_mu[i])[None], 4 * BW)
        w2p = jnp.concatenate([rwkv_w2[i], jnp.zeros_like(rwkv_a2[i])], axis=0).astype(BF16)
        a2q = jnp.concatenate([jnp.zeros_like(rwkv_w2[i]), rwkv_a2[i]], axis=0).astype(BF16)
        r_, w_, k_, v_, nkk_, kka_, gate_, bonus_ = _rw_prep(
            z, prev, mu, rwkv_w0[i][None], rwkv_a0[i][None], rwkv_kk[i][None], rwkv_ka[i][None],
            rwkv_rk[i][None], w2p, a2q, rwkv_g2[i].astype(BF16), seg)

        def rw_state_in(s):
            n = s.shape[0]
            return s.reshape(n, NH // 2, 2, HD, HD).transpose(0, 1, 3, 2, 4).reshape(n, NH // 2, HD, LANES)

        def rw_state_out(s):
            n = s.shape[0]
            return s.reshape(n, NH // 2, HD, 2, HD).transpose(0, 1, 3, 2, 4).reshape(n, NH, HD, HD)
        ot_p, sr_p = _rw_rec(r_, w_, k_, v_, nkk_, kka_, gmat, jnp.zeros((bp, NH // 2, HD, LANES), F32), 0, bp, lp)
        ot_s, sr_s = _rw_rec(r_, w_, k_, v_, nkk_, kka_, gmat, rw_state_in(state_rwkv[i]), tp, nb, ls)
        o_r = jnp.concatenate([ot_p.T, ot_s.T], axis=0)
        o_rw = _rw_post(o_r, bonus_, gate_, rwkv_lnw[i][None], rwkv_lnb[i][None], seg)

        x = _merge(x, z, o_sb, o_dsa, o_gla, o_rw, w_branch[i].astype(BF16), w_out[i].astype(BF16))
        x = _ffn(x, ffn2_norm[i][None], wup2, wdn2, tf)

        sv = z[:, OFF_SV:OFF_SV + BW]
        dv = z[:, OFF_DV:OFF_DV + BW]
        ki = z[:, OFF_KI:OFF_KI + HD]
        for dst, (lo, n, l) in ((new_p, (0, bp, lp)), (new_s, (tp, nb, ls))):
            hi = lo + n * l
            dst[0].append(skn[lo:hi].reshape(n, l, NH, HD))
            dst[1].append(sv[lo:hi].reshape(n, l, NH, HD))
            dst[2].append(dkn[lo:hi].reshape(n, l, NH, HD))
            dst[3].append(dv[lo:hi].reshape(n, l, NH, HD))
            dst[4].append(ki[lo:hi].reshape(n, l, HD))
            dst[7].append(_rw_unorder(zrw[lo:hi].reshape(n, l, RW_COLS)[:, -1:]))
        new_p[5].append(gla_state_out(sg_p))
        new_s[5].append(gla_state_out(sg_s))
        new_p[6].append(rw_state_out(sr_p))
        new_s[6].append(rw_state_out(sr_s))

    outs_p = [jnp.stack(l, axis=0) for l in new_p]
    outs_s = [jnp.stack(l, axis=0) for l in new_s]
    y_p = x[:tp].reshape(bp, lp, d_model)
    y_s = x[tp:].reshape(nb, ls, d_model)
    return (y_p, y_s, *outs_p, *outs_s)
```

```python
import functools
import math

import jax
import jax.numpy as jnp
import numpy as np
from jax import lax
from jax.experimental import pallas as pl
from jax.experimental.pallas import tpu as pltpu

F32 = jnp.float32
BF16 = jnp.bfloat16

LANES = 128
SUBLANES = 8
HD = 64
NH = 8
BW = NH * HD
GLA_H = 4
GLA_DV = 128
GLA_LOWRANK = 16
GLA_GATE_NORM = 16.0
CHUNK = 64
TOPK_MAX = 256
RW_LORA = 64
RW_GATE_LORA = 128
RW_COLS = 3 * BW + 2 * RW_LORA + RW_GATE_LORA
NORM_EPS = 1e-6
RW_LN_EPS = 64e-5
NEG_BIG = -1e30
SB_UNDERFLOW = -104.0
VMEM_LIMIT = 56 * 1024 * 1024

OFF_RW = 0
OFF_SQ = 2048
OFF_SK = 3072
OFF_SV = 3584
OFF_DQ = 4096
OFF_DK = 5120
OFF_DV = 5632
OFF_QI = 6144
OFF_GQ = 7168
OFF_GK = 7680
OFF_GV = 8192
OFF_GR = 8704
OFF_KI = 9216
OFF_WI = 9344
OFF_GA = 9472
OFF_GATE = 9728


def _cparams(sem):
    return pltpu.CompilerParams(dimension_semantics=sem, vmem_limit_bytes=VMEM_LIMIT)


def _pick(n, prefs):
    for p in prefs:
        if n % p == 0:
            return p
    raise ValueError(f"no tile for {n} in {prefs}")


def _split_dot(a, b):
    hi = a.astype(BF16)
    lo = (a - hi.astype(F32)).astype(BF16)
    return (jnp.dot(hi, b, preferred_element_type=F32)
            + jnp.dot(lo, b, preferred_element_type=F32))


def _split_dot_left(a, b):
    hi = b.astype(BF16)
    lo = (b - hi.astype(F32)).astype(BF16)
    return (jnp.dot(a, hi, preferred_element_type=F32)
            + jnp.dot(a, lo, preferred_element_type=F32))


def _dot_nt(a, b):
    return lax.dot_general(a, b, (((1,), (1,)), ((), ())), preferred_element_type=F32)


def _dot_tn(a, b):
    return lax.dot_general(a, b, (((0,), (0,)), ((), ())), preferred_element_type=F32)


def _neg_softplus(z):
    return -(jnp.maximum(z, 0.0) + jnp.log(1.0 + jnp.exp(-jnp.abs(z))))


def _ffn_body(x_ref, g_ref, wup_ref, wdn_ref, o_ref, xn_ref, *, tf):
    @pl.when(pl.program_id(1) == 0)
    def _():
        x = x_ref[...]
        ms = jnp.mean(x * x, axis=-1, keepdims=True)
        xn_ref[...] = (x * lax.rsqrt(ms + NORM_EPS) * g_ref[...]).astype(BF16)
        o_ref[...] = x

    gu = jnp.dot(xn_ref[...], wup_ref[...], preferred_element_type=F32)
    gate = gu[:, :tf]
    act = (gate * jax.nn.sigmoid(gate) * gu[:, tf:]).astype(BF16)
    o_ref[...] += 0.5 * jnp.dot(act, wdn_ref[...], preferred_element_type=F32)


def _ffn(x, g, wup, wdn, tf):
    T, D = x.shape
    fp = wdn.shape[0]
    tm = _pick(T, (512, 256, 128))
    return pl.pallas_call(
        functools.partial(_ffn_body, tf=tf),
        grid=(T // tm, fp // tf),
        in_specs=[pl.BlockSpec((tm, D), lambda i, j: (i, 0)),
                  pl.BlockSpec((1, D), lambda i, j: (0, 0)),
                  pl.BlockSpec((D, 2 * tf), lambda i, j: (0, j)),
                  pl.BlockSpec((tf, D), lambda i, j: (j, 0))],
        out_specs=pl.BlockSpec((tm, D), lambda i, j: (i, 0)),
        out_shape=jax.ShapeDtypeStruct((T, D), F32),
        scratch_shapes=[pltpu.VMEM((tm, D), BF16)],
        compiler_params=_cparams(("parallel", "arbitrary")),
        name="ffn",
    )(x, g, wup, wdn)


def _proj_body(x_ref, g_ref, w_ref, o_ref, xn_ref):
    @pl.when(pl.program_id(1) == 0)
    def _():
        x = x_ref[...]
        ms = jnp.mean(x * x, axis=-1, keepdims=True)
        xn_ref[...] = (x * lax.rsqrt(ms + NORM_EPS) * g_ref[...]).astype(BF16)

    o_ref[...] = jnp.dot(xn_ref[...], w_ref[...], preferred_element_type=F32)


def _proj(x, g, w):
    T, D = x.shape
    n = w.shape[1]
    tm = _pick(T, (1024, 512, 256, 128))
    tn = 512
    return pl.pallas_call(
        _proj_body,
        grid=(T // tm, n // tn),
        in_specs=[pl.BlockSpec((tm, D), lambda i, j: (i, 0)),
                  pl.BlockSpec((1, D), lambda i, j: (0, 0)),
                  pl.BlockSpec((D, tn), lambda i, j: (0, j))],
        out_specs=pl.BlockSpec((tm, tn), lambda i, j: (i, j)),
        out_shape=jax.ShapeDtypeStruct((T, n), F32),
        scratch_shapes=[pltpu.VMEM((tm, D), BF16)],
        compiler_params=_cparams(("parallel", "arbitrary")),
        name="in_proj",
    )(x, g, w)


def _merge_body(osb_ref, ods_ref, ogl_ref, orw_ref, g0_ref, g1_ref, g2_ref, g3_ref,
                wb_ref, wo_ref, x_ref, o_ref):
    @pl.when(pl.program_id(1) == 0)
    def _():
        o_ref[...] = x_ref[...]

    acc = None
    for n, (b_ref, g_ref) in enumerate(((osb_ref, g0_ref), (ods_ref, g1_ref),
                                        (ogl_ref, g2_ref), (orw_ref, g3_ref))):
        p = jnp.dot(b_ref[...].astype(BF16), wb_ref[n], preferred_element_type=F32)
        t = jax.nn.sigmoid(g_ref[...]) * p
        acc = t if acc is None else acc + t
    o_ref[...] += jnp.dot(acc.astype(BF16), wo_ref[...], preferred_element_type=F32)


def _merge(x, z, o_sb, o_dsa, o_gla, o_rw, wb, wo):
    T, D = x.shape
    tm = _pick(T, (512, 256, 128))
    tc = 512
    nc = D // tc
    gate_blk = OFF_GATE // tc
    bspec = pl.BlockSpec((tm, BW), lambda i, c: (i, 0))
    gspecs = [pl.BlockSpec((tm, tc), functools.partial(lambda i, c, n: (i, gate_blk + n * nc + c), n=n))
              for n in range(4)]
    return pl.pallas_call(
        _merge_body,
        grid=(T // tm, nc),
        in_specs=[bspec, bspec, bspec, bspec] + gspecs + [
            pl.BlockSpec((4, BW, tc), lambda i, c: (0, 0, c)),
            pl.BlockSpec((tc, D), lambda i, c: (c, 0)),
            pl.BlockSpec((tm, D), lambda i, c: (i, 0))],
        out_specs=pl.BlockSpec((tm, D), lambda i, c: (i, 0)),
        out_shape=jax.ShapeDtypeStruct((T, D), F32),
        compiler_params=_cparams(("parallel", "arbitrary")),
        name="merge",
    )(o_sb, o_dsa, o_gla, o_rw, z, z, z, z, wb, wo, x)


def _qknorm_body(sq_ref, sk_ref, dq_ref, dk_ref, gsq_ref, gsk_ref, gdq_ref, gdk_ref, bd_ref,
                 osq_ref, osk_ref, odq_ref, odk_ref):
    def norm(x, g, n):
        ms = _split_dot(x * x, bd_ref[:n, :n])
        return x * lax.rsqrt(ms + NORM_EPS) * g

    scale = HD ** -0.5
    osq_ref[...] = (norm(sq_ref[...], gsq_ref[...], 2 * BW) * scale).astype(BF16)
    osk_ref[...] = norm(sk_ref[...], gsk_ref[...], BW)
    odq_ref[...] = (norm(dq_ref[...], gdq_ref[...], 2 * BW) * scale).astype(BF16)
    odk_ref[...] = norm(dk_ref[...], gdk_ref[...], BW)


def _qknorm(z, gsq, gsk, gdq, gdk, bd):
    T = z.shape[0]
    tm = _pick(T, (512, 256, 128))
    wq, wk = 2 * BW, BW
    row = lambda w, off: pl.BlockSpec((tm, w), lambda i: (i, off // w))
    par = lambda w: pl.BlockSpec((1, w), lambda i: (0, 0))
    out = lambda w: pl.BlockSpec((tm, w), lambda i: (i, 0))
    return pl.pallas_call(
        _qknorm_body,
        grid=(T // tm,),
        in_specs=[row(wq, OFF_SQ), row(wk, OFF_SK), row(wq, OFF_DQ), row(wk, OFF_DK),
                  par(wq), par(wk), par(wq), par(wk),
                  pl.BlockSpec((wq, wq), lambda i: (0, 0))],
        out_specs=[out(wq), out(wk), out(wq), out(wk)],
        out_shape=[jax.ShapeDtypeStruct((T, wq), BF16), jax.ShapeDtypeStruct((T, wk), F32),
                   jax.ShapeDtypeStruct((T, wq), BF16), jax.ShapeDtypeStruct((T, wk), F32)],
        compiler_params=_cparams(("parallel",)),
        name="qk_norm",
    )(z, z, z, z, gsq, gsk, gdq, gdk, bd)


def _sb_body(*refs, tq, n_cache, tkc):
    if n_cache:
        q_ref, kn_ref, vn_ref, kc_ref, vc_ref, o_ref = refs
    else:
        q_ref, kn_ref, vn_ref, o_ref = refs
    qb = pl.program_id(1) if not n_cache else 0
    qs = [q_ref[:, e * LANES:(e + 1) * LANES] for e in range(2)]

    def tri(n):
        r = lax.broadcasted_iota(jnp.int32, (n, n), 0)
        c = lax.broadcasted_iota(jnp.int32, (n, n), 1)
        return (r > c).astype(BF16)

    def visit(state, kblk, vblk, u, valid):
        kb16 = kblk.astype(BF16)
        vb16 = vblk.astype(BF16)
        new = []
        for e in range(2):
            carry, acc = state[e]
            z = _dot_nt(qs[e], kb16)
            lg = _neg_softplus(z)
            if valid is not None:
                lg = jnp.where(valid, lg, 0.0)
            inner = _split_dot(lg, u)
            a = jnp.exp(z + lg + inner + carry)
            if valid is not None:
                a = jnp.where(valid, a, 0.0)
            acc = acc + jnp.dot(a.astype(BF16), vb16, preferred_element_type=F32)
            carry = carry + inner[:, :1] + lg[:, :1]
            new.append((carry, acc))
        return tuple(new)

    zero = (jnp.zeros((tq, 1), F32), jnp.zeros((tq, LANES), F32))
    state = (zero, zero)

    r = lax.broadcasted_iota(jnp.int32, (tq, tq), 0)
    c = lax.broadcasted_iota(jnp.int32, (tq, tq), 1)
    u_new = tri(tq)
    if n_cache:
        state = visit(state, kn_ref[...], vn_ref[...], u_new, c < r)
    else:
        row0 = pl.multiple_of(qb * tq, tq)
        state = visit(state, kn_ref[pl.ds(row0, tq), :], vn_ref[pl.ds(row0, tq), :], u_new, c < r)

    def live(c):
        i, st = c
        return (i < n_old) & (jnp.max(jnp.maximum(st[0][0], st[1][0])) > SB_UNDERFLOW)

    if not n_cache:
        n_old = qb

        def step(c):
            i, st = c
            r0 = pl.multiple_of((qb - 1 - i) * tq, tq)
            return i + 1, visit(st, kn_ref[pl.ds(r0, tq), :], vn_ref[pl.ds(r0, tq), :], u_new, None)
    else:
        n_old = n_cache
        u_c = tri(tkc)

        def step(c):
            i, st = c
            r0 = pl.multiple_of((n_cache - 1 - i) * tkc, tkc)
            return i + 1, visit(st, kc_ref[0, 0, pl.ds(r0, tkc), :], vc_ref[0, 0, pl.ds(r0, tkc), :],
                                u_c, None)
    _, state = lax.while_loop(live, step, (jnp.int32(0), state))

    lane = lax.broadcasted_iota(jnp.int32, (tq, LANES), 1)
    o_ref[...] = jnp.where(lane < HD, state[0][1], state[1][1])


def _sb_prompt(qn, kn, z, n_rows):
    tq = 128
    vblk = OFF_SV // LANES
    return pl.pallas_call(
        functools.partial(_sb_body, tq=tq, n_cache=0, tkc=0),
        grid=(NH // 2, n_rows // tq),
        in_specs=[pl.BlockSpec((tq, 2 * LANES), lambda p, i: (i, p)),
                  pl.BlockSpec((n_rows, LANES), lambda p, i: (0, p)),
                  pl.BlockSpec((n_rows, LANES), lambda p, i: (0, vblk + p))],
        out_specs=pl.BlockSpec((tq, LANES), lambda p, i: (i, p)),
        out_shape=jax.ShapeDtypeStruct((n_rows, BW), F32),
        compiler_params=_cparams(("parallel", "arbitrary")),
        name="sb_prompt",
    )(qn, kn, z)


def _sb_sample(qn, kn, z, cache_k, cache_v, layer, row_off, nb, L):
    P = cache_k.shape[2]
    tkc = _pick(P, (256, 128))
    vblk = OFF_SV // LANES
    rb = row_off // L
    return pl.pallas_call(
        functools.partial(_sb_body, tq=L, n_cache=P // tkc, tkc=tkc),
        grid=(nb, NH // 2),
        in_specs=[pl.BlockSpec((L, 2 * LANES), lambda b, p: (rb + b, p)),
                  pl.BlockSpec((L, LANES), lambda b, p: (rb + b, p)),
                  pl.BlockSpec((L, LANES), lambda b, p: (rb + b, vblk + p)),
                  pl.BlockSpec((1, 1, P, LANES), lambda b, p: (layer, b, 0, p)),
                  pl.BlockSpec((1, 1, P, LANES), lambda b, p: (layer, b, 0, p))],
        out_specs=pl.BlockSpec((L, LANES), lambda b, p: (b, p)),
        out_shape=jax.ShapeDtypeStruct((nb * L, BW), F32),
        compiler_params=_cparams(("parallel", "parallel")),
        name="sb_sample",
    )(qn, kn, z, cache_k, cache_v)


def _sortable(x):
    b = pltpu.bitcast(x + 0.0, jnp.int32)
    return jnp.where(b < 0, b ^ jnp.int32(0x7FFFFFFF), b)


def _select_topk(key_ref, n_blk, tkb, topk, rows):
    def count(pred):
        def body(i, acc):
            c0 = pl.multiple_of(i * tkb, tkb)
            for c in range(tkb // LANES):
                blk = key_ref[:, pl.ds(c0 + c * LANES, LANES)]
                col = c0 + c * LANES + lax.broadcasted_iota(jnp.int32, (rows, LANES), 1)
                acc = acc + jnp.where(pred(blk, col), 1.0, 0.0)
            return acc
        acc = lax.fori_loop(0, n_blk, body, jnp.zeros((rows, LANES), F32))
        return jnp.sum(acc, axis=1, keepdims=True)

    def bit_step(i, thr):
        cand = thr + (jnp.int32(1) << (31 - i))
        n_ge = count(lambda blk, col: blk >= cand)
        return jnp.where(n_ge >= topk, cand, thr)

    thr = lax.fori_loop(0, 32, bit_step, jnp.full((rows, 1), jnp.iinfo(jnp.int32).min, jnp.int32))
    n_gt = count(lambda blk, col: blk > thr)
    n_ge = count(lambda blk, col: blk >= thr)
    need = topk - n_gt
    n_col_bits = max(1, int(math.ceil(math.log2(key_ref.shape[1] + 1))))

    def tie_search():
        def col_step(i, j):
            cand = j + (jnp.int32(1) << (n_col_bits - 1 - i))
            n = count(lambda blk, col: (blk == thr) & (col < cand))
            return jnp.where(n < need, cand, j)
        return lax.fori_loop(0, n_col_bits, col_step, jnp.zeros((rows, 1), jnp.int32))

    any_tie = jnp.max(jnp.abs(n_ge - topk)) > 0
    j_hi = lax.cond(any_tie, tie_search,
                    lambda: jnp.full((rows, 1), jnp.iinfo(jnp.int32).max, jnp.int32))
    return thr, j_hi


def _idx_scores(qi, wi, kblk16, k_real):
    acc = None
    for h in range(NH):
        qh = qi[:, h * LANES:h * LANES + k_real].astype(BF16)
        d = jnp.maximum(_dot_nt(qh, kblk16), 0.0) * (HD ** -0.5)
        t = (wi[:, h:h + 1] * (NH ** -0.5)) * d
        acc = t if acc is None else acc + t
    return acc


def _dsa_sel_prompt_body(qi_ref, wi_ref, ki_ref, m_ref, key_ref, *, tq, tkb, topk, n_keys):
    qb = pl.program_id(0)
    n_blk = ((qb + 1) * tq + tkb - 1) // tkb
    qi = qi_ref[...]
    wi = wi_ref[...]
    q_chunk = (qb * tq + lax.broadcasted_iota(jnp.int32, (tq, tkb), 0)) // CHUNK

    def score_blk(i, _):
        c0 = pl.multiple_of(i * tkb, tkb)
        s = _idx_scores(qi, wi, ki_ref[pl.ds(c0, tkb), :].astype(BF16), LANES)
        k_chunk = (c0 + lax.broadcasted_iota(jnp.int32, (tq, tkb), 1)) // CHUNK
        s = jnp.where(k_chunk <= q_chunk, s, -jnp.inf)
        key_ref[:, pl.ds(c0, tkb)] = _sortable(s)
        return 0
    lax.fori_loop(0, n_blk, score_blk, 0)

    thr, j_hi = _select_topk(key_ref, n_blk, tkb, topk, tq)
    neg_inf_key = _sortable(jnp.full((1, 1), -jnp.inf, F32))

    def write_blk(i, _):
        c0 = pl.multiple_of(i * tkb, tkb)
        blk = key_ref[:, pl.ds(c0, tkb)]
        col = c0 + lax.broadcasted_iota(jnp.int32, (tq, tkb), 1)
        sel = ((blk > thr) | ((blk == thr) & (col <= j_hi))) & (blk > neg_inf_key)
        m_ref[:, pl.ds(c0, tkb)] = jnp.where(sel, 1.0, 0.0).astype(BF16)
        return 0
    lax.fori_loop(0, n_blk, write_blk, 0)

    def zero_blk(i, _):
        m_ref[:, pl.ds(pl.multiple_of(i * tkb, tkb), tkb)] = jnp.zeros((tq, tkb), BF16)
        return 0
    lax.fori_loop(n_blk, n_keys // tkb, zero_blk, 0)


def _dsa_sel_prompt(z, n_rows, topk):
    tq, tkb = 128, 512
    return pl.pallas_call(
        functools.partial(_dsa_sel_prompt_body, tq=tq, tkb=tkb, topk=topk, n_keys=n_rows),
        grid=(n_rows // tq,),
        in_specs=[pl.BlockSpec((tq, NH * LANES), lambda i: (i, OFF_QI // (NH * LANES))),
                  pl.BlockSpec((tq, LANES), lambda i: (i, OFF_WI // LANES)),
                  pl.BlockSpec((n_rows, LANES), lambda i: (0, OFF_KI // LANES))],
        out_specs=pl.BlockSpec((tq, n_rows), lambda i: (i, 0)),
        out_shape=jax.ShapeDtypeStruct((n_rows, n_rows), BF16),
        scratch_shapes=[pltpu.VMEM((tq, n_rows), jnp.int32)],
        compiler_params=_cparams(("parallel",)),
        name="dsa_select_prompt",
    )(z, z, z)


def _dsa_sel_sample_body(qi_ref, wi_ref, kin_ref, kic_ref, m_ref, key_ref, *, L, P, tkb, topk):
    qi = qi_ref[...]
    wi = wi_ref[...]
    n_c = P // tkb

    def score_blk(i, _):
        c0 = pl.multiple_of(i * tkb, tkb)
        s = _idx_scores(qi, wi, kic_ref[0, 0, pl.ds(c0, tkb), :].astype(BF16), HD)
        key_ref[:, pl.ds(c0, tkb)] = _sortable(s)
        return 0
    lax.fori_loop(0, n_c, score_blk, 0)

    knew = jnp.concatenate([kin_ref[...], jnp.zeros((tkb - L, LANES), F32)], axis=0).astype(BF16)
    s = _idx_scores(qi, wi, knew, LANES)
    col = lax.broadcasted_iota(jnp.int32, (L, tkb), 1)
    key_ref[:, pl.ds(P, tkb)] = _sortable(jnp.where(col < L, s, -jnp.inf))

    thr, j_hi = _select_topk(key_ref, n_c + 1, tkb, topk, L)
    neg_inf_key = _sortable(jnp.full((1, 1), -jnp.inf, F32))

    def write_blk(i, _):
        c0 = pl.multiple_of(i * tkb, tkb)
        blk = key_ref[:, pl.ds(c0, tkb)]
        cc = c0 + lax.broadcasted_iota(jnp.int32, (L, tkb), 1)
        sel = ((blk > thr) | ((blk == thr) & (cc <= j_hi))) & (blk > neg_inf_key)
        m_ref[:, pl.ds(c0, tkb)] = jnp.where(sel, 1.0, 0.0).astype(BF16)
        return 0
    lax.fori_loop(0, n_c + 1, write_blk, 0)


def _dsa_sel_sample(z, cache_ki, layer, row_off, nb, L, topk):
    P = cache_ki.shape[2]
    tkb = _pick(P, (512, 256, 128))
    rb = row_off // L
    return pl.pallas_call(
        functools.partial(_dsa_sel_sample_body, L=L, P=P, tkb=tkb, topk=topk),
        grid=(nb,),
        in_specs=[pl.BlockSpec((L, NH * LANES), lambda b: (rb + b, OFF_QI // (NH * LANES))),
                  pl.BlockSpec((L, LANES), lambda b: (rb + b, OFF_WI // LANES)),
                  pl.BlockSpec((L, LANES), lambda b: (rb + b, OFF_KI // LANES)),
                  pl.BlockSpec((1, 1, P, HD), lambda b: (layer, b, 0, 0))],
        out_specs=pl.BlockSpec((L, P + tkb), lambda b: (b, 0)),
        out_shape=jax.ShapeDtypeStruct((nb * L, P + tkb), BF16),
        scratch_shapes=[pltpu.VMEM((L, P + tkb), jnp.int32)],
        compiler_params=_cparams(("parallel",)),
        name="dsa_select_sample",
    )(z, z, z, cache_ki)


def _dsa_attn_body(*refs, tq, tkb, n_cache, prompt):
    if prompt:
        q_ref, m_ref, kn_ref, vn_ref, o_ref = refs
        qb = pl.program_id(1)
        n_blk = ((qb + 1) * tq + tkb - 1) // tkb
    else:
        q_ref, m_ref, kn_ref, vn_ref, kc_ref, vc_ref, o_ref = refs
        n_blk = n_cache
    qs = [q_ref[:, e * LANES:(e + 1) * LANES] for e in range(2)]

    def visit(state, kblk, vblk, mblk):
        kb16 = kblk.astype(BF16)
        vb16 = vblk.astype(BF16)
        sel = mblk > 0
        new = []
        for e in range(2):
            m, l, acc = state[e]
            s = jnp.where(sel, _dot_nt(qs[e], kb16), NEG_BIG)
            m_new = jnp.maximum(m, jnp.max(s, axis=1, keepdims=True))
            p = jnp.where(sel, jnp.exp(s - m_new), 0.0)
            alpha = jnp.exp(m - m_new)
            l = alpha * l + jnp.sum(p, axis=1, keepdims=True)
            acc = alpha * acc + jnp.dot(p.astype(BF16), vb16, preferred_element_type=F32)
            new.append((m_new, l, acc))
        return tuple(new)

    init = (jnp.full((tq, 1), NEG_BIG, F32), jnp.zeros((tq, 1), F32), jnp.zeros((tq, LANES), F32))
    state = (init, init)

    def step(i, st):
        c0 = pl.multiple_of(i * tkb, tkb)
        if prompt:
            return visit(st, kn_ref[pl.ds(c0, tkb), :], vn_ref[pl.ds(c0, tkb), :], m_ref[:, pl.ds(c0, tkb)])
        return visit(st, kc_ref[0, 0, pl.ds(c0, tkb), :], vc_ref[0, 0, pl.ds(c0, tkb), :],
                     m_ref[:, pl.ds(c0, tkb)])
    state = lax.fori_loop(0, n_blk, step, state)

    if not prompt:
        pad = jnp.zeros((tkb - tq, LANES), F32)
        knew = jnp.concatenate([kn_ref[...], pad], axis=0)
        vnew = jnp.concatenate([vn_ref[...], pad], axis=0)
        state = visit(state, knew, vnew, m_ref[:, pl.ds(n_cache * tkb, tkb)])

    lane = lax.broadcasted_iota(jnp.int32, (tq, LANES), 1)
    o0 = state[0][2] / state[0][1]
    o1 = state[1][2] / state[1][1]
    o_ref[...] = jnp.where(lane < HD, o0, o1)


def _dsa_attn_prompt(qn, kn, z, mask, n_rows):
    tq, tkb = 128, 512
    vblk = OFF_DV // LANES
    return pl.pallas_call(
        functools.partial(_dsa_attn_body, tq=tq, tkb=tkb, n_cache=0, prompt=True),
        grid=(NH // 2, n_rows // tq),
        in_specs=[pl.BlockSpec((tq, 2 * LANES), lambda p, i: (i, p)),
                  pl.BlockSpec((tq, n_rows), lambda p, i: (i, 0)),
                  pl.BlockSpec((n_rows, LANES), lambda p, i: (0, p)),
                  pl.BlockSpec((n_rows, LANES), lambda p, i: (0, vblk + p))],
        out_specs=pl.BlockSpec((tq, LANES), lambda p, i: (i, p)),
        out_shape=jax.ShapeDtypeStruct((n_rows, BW), F32),
        compiler_params=_cparams(("parallel", "arbitrary")),
        name="dsa_attn_prompt",
    )(qn, mask, kn, z)


def _dsa_attn_sample(qn, kn, z, mask, cache_k, cache_v, layer, row_off, nb, L):
    P = cache_k.shape[2]
    tkb = mask.shape[1] - P
    vblk = OFF_DV // LANES
    rb = row_off // L
    return pl.pallas_call(
        functools.partial(_dsa_attn_body, tq=L, tkb=tkb, n_cache=P // tkb, prompt=False),
        grid=(nb, NH // 2),
        in_specs=[pl.BlockSpec((L, 2 * LANES), lambda b, p: (rb + b, p)),
                  pl.BlockSpec((L, P + tkb), lambda b, p: (b, 0)),
                  pl.BlockSpec((L, LANES), lambda b, p: (rb + b, p)),
                  pl.BlockSpec((L, LANES), lambda b, p: (rb + b, vblk + p)),
                  pl.BlockSpec((1, 1, P, LANES), lambda b, p: (layer, b, 0, p)),
                  pl.BlockSpec((1, 1, P, LANES), lambda b, p: (layer, b, 0, p))],
        out_specs=pl.BlockSpec((L, LANES), lambda b, p: (b, p)),
        out_shape=jax.ShapeDtypeStruct((nb * L, BW), F32),
        compiler_params=_cparams(("parallel", "parallel")),
        name="dsa_attn_sample",
    )(qn, mask, kn, z, cache_k, cache_v)


def _gla_consts(C):
    nlev = int(math.log2(C))
    t = np.arange(C)
    cum = (t[:, None] >= t[None, :]).astype(np.float32)
    sel = np.zeros((nlev * C, C), np.float32)
    for l in range(nlev):
        m = 1 << l
        mid = (t // (2 * m)) * 2 * m + m
        sel[l * C + t, mid - 1] = 1.0
    return jnp.asarray(cum, BF16), jnp.asarray(sel, BF16)


def _gla_body(gq_ref, gk_ref, gv_ref, gr_ref, ga_ref, a2_ref, ab_ref, on_ref, cum_ref, sel_ref,
              s0_ref, o_ref, sout_ref, st_ref, *, C, n_chunks):
    ci = pl.program_id(1)
    nlev = int(math.log2(C))

    @pl.when(ci == 0)
    def _():
        st_ref[...] = s0_ref[0]

    x = jnp.dot(ga_ref[...].astype(BF16), a2_ref[...], preferred_element_type=F32) + ab_ref[...]
    g = _neg_softplus(-x) / GLA_GATE_NORM
    b = _split_dot_left(cum_ref[...], g)
    c_all = _split_dot_left(sel_ref[...], b)

    row = lax.broadcasted_iota(jnp.int32, (C, C), 0)
    col = lax.broadcasted_iota(jnp.int32, (C, C), 1)
    rowl = lax.broadcasted_iota(jnp.int32, (C, LANES), 0)

    for h in range(GLA_H):
        sl = slice(h * LANES, (h + 1) * LANES)
        q = gq_ref[:, sl] * (HD ** -0.5)
        k = gk_ref[:, sl]
        v = gv_ref[:, sl]
        bh = b[:, sl]
        a = jnp.where(row == col, jnp.sum(q * k, axis=1, keepdims=True), 0.0)
        for l in range(nlev):
            m = 1 << l
            later = (rowl & m) != 0
            ch = c_all[l * C:(l + 1) * C, sl]
            qe = jnp.where(later, q * jnp.exp(jnp.where(later, bh - ch, 0.0)), 0.0)
            ke = jnp.where(later, 0.0, k * jnp.exp(jnp.where(later, 0.0, ch - bh)))
            al = _dot_nt(qe.astype(BF16), ke.astype(BF16))
            a = a + jnp.where((row // (2 * m)) == (col // (2 * m)), al, 0.0)
        st = st_ref[h]
        o = jnp.dot(a.astype(BF16), v.astype(BF16), preferred_element_type=F32)
        o = o + _dot_nt((q * jnp.exp(bh)).astype(BF16), st.astype(BF16))
        b_last = bh[C - 1:C, :]
        kx = k * jnp.exp(b_last - bh)
        st_ref[h] = st * jnp.exp(b_last) + _dot_tn(v.astype(BF16), kx.astype(BF16))
        on = o * lax.rsqrt(jnp.mean(o * o, axis=-1, keepdims=True) + NORM_EPS) * on_ref[...]
        gr = gr_ref[:, sl]
        o_ref[:, sl] = on * (gr * jax.nn.sigmoid(gr))

    @pl.when(ci == n_chunks - 1)
    def _():
        sout_ref[0] = st_ref[...]


def _gla(z, a2p, abp, onorm, s0t, row_off, n_seq, L):
    C = min(CHUNK, L)
    n_chunks = L // C
    rb = row_off // C
    cum, sel = _gla_consts(C)
    blk = lambda off: pl.BlockSpec((C, BW), lambda s, c: (rb + s * n_chunks + c, off // BW))
    full = lambda a: pl.BlockSpec(a.shape, lambda s, c: (0,) * a.ndim)
    return pl.pallas_call(
        functools.partial(_gla_body, C=C, n_chunks=n_chunks),
        grid=(n_seq, n_chunks),
        in_specs=[blk(OFF_GQ), blk(OFF_GK), blk(OFF_GV), blk(OFF_GR),
                  pl.BlockSpec((C, LANES), lambda s, c: (rb + s * n_chunks + c, OFF_GA // LANES)),
                  full(a2p), full(abp), full(onorm), full(cum), full(sel),
                  pl.BlockSpec((1, GLA_H, GLA_DV, LANES), lambda s, c: (s, 0, 0, 0))],
        out_specs=[pl.BlockSpec((C, BW), lambda s, c: (s * n_chunks + c, 0)),
                   pl.BlockSpec((1, GLA_H, GLA_DV, LANES), lambda s, c: (s, 0, 0, 0))],
        out_shape=[jax.ShapeDtypeStruct((n_seq * L, BW), F32),
                   jax.ShapeDtypeStruct((n_seq, GLA_H, GLA_DV, LANES), F32)],
        scratch_shapes=[pltpu.VMEM((GLA_H, GLA_DV, LANES), F32)],
        compiler_params=_cparams(("parallel", "arbitrary")),
        name="gla",
    )(z, z, z, z, z, a2p, abp, onorm, cum, sel, s0t)


def _rw_prep_body(z_ref, prev_ref, mu_ref, w0_ref, a0_ref, kkp_ref, kap_ref, rkp_ref,
                  w2_ref, a2_ref, g2_ref, seg_ref,
                  r_ref, w_ref, k_ref, v_ref, nkk_ref, kka_ref, gate_ref, bonus_ref):
    zz = z_ref[...]
    zm = zz + mu_ref[...] * (prev_ref[...] - zz)
    rr = zm[:, 0:BW]
    rk = zm[:, BW:2 * BW]
    rv = zm[:, 2 * BW:3 * BW]
    lora = zm[:, 3 * BW:3 * BW + LANES]
    gl = zm[:, 3 * BW + LANES:3 * BW + 2 * LANES]
    xw = w0_ref[...] + jnp.dot(jnp.tanh(lora).astype(BF16), w2_ref[...], preferred_element_type=F32)
    wlog = _neg_softplus(-xw) - 0.5
    decay = jnp.exp(-jnp.exp(wlog))
    a = jax.nn.sigmoid(a0_ref[...] + jnp.dot(lora.astype(BF16), a2_ref[...], preferred_element_type=F32))
    gate = jnp.dot(jax.nn.sigmoid(gl).astype(BF16), g2_ref[...], preferred_element_type=F32)
    kkf = rk * kkp_ref[...]
    nrm = jnp.sqrt(_split_dot(kkf * kkf, seg_ref[...]))
    kk = kkf / jnp.maximum(nrm, 1e-12)
    kmod = rk * (1.0 + (a - 1.0) * kap_ref[...])
    coef = _split_dot(rr * kmod * rkp_ref[...], seg_ref[...])
    r_ref[...] = rr
    w_ref[...] = decay
    k_ref[...] = kmod
    v_ref[...] = rv
    nkk_ref[...] = -kk
    kka_ref[...] = kk * a
    gate_ref[...] = gate
    bonus_ref[...] = coef * rv


def _rw_prep(z, prev, mu, w0, a0, kkp, kap, rkp, w2p, a2p, g2, seg):
    T = z.shape[0]
    tm = _pick(T, (256, 128))
    W = 4 * BW
    par = lambda a: pl.BlockSpec(a.shape, lambda i: (0,) * a.ndim)
    out = pl.BlockSpec((tm, BW), lambda i: (i, 0))
    return pl.pallas_call(
        _rw_prep_body,
        grid=(T // tm,),
        in_specs=[pl.BlockSpec((tm, W), lambda i: (i, 0)), pl.BlockSpec((tm, W), lambda i: (i, 0)),
                  par(mu), par(w0), par(a0), par(kkp), par(kap), par(rkp),
                  par(w2p), par(a2p), par(g2), par(seg)],
        out_specs=[out] * 8,
        out_shape=[jax.ShapeDtypeStruct((T, BW), F32)] * 8,
        compiler_params=_cparams(("parallel",)),
        name="rwkv_prep",
    )(z, prev, mu, w0, a0, kkp, kap, rkp, w2p, a2p, g2, seg)


def _rw_rec_body(r_ref, w_ref, k_ref, v_ref, nkk_ref, kka_ref, g_ref, s0_ref,
                 ot_ref, sout_ref, s_ref, *, TT, L, NS, n_tiles):
    ti = pl.program_id(0)
    npair = NH // 2
    R = NS * npair * HD
    gm = g_ref[...]
    lane = lax.broadcasted_iota(jnp.int32, (R, LANES), 1)
    rowi = lax.broadcasted_iota(jnp.int32, (R, LANES), 0)
    eye2 = (lane % HD) == (rowi % HD)
    seq_lane0 = (rowi // (npair * HD)) * L
    hs_row = lax.broadcasted_iota(jnp.int32, (LANES, 2 * LANES), 0)
    hs_col = lax.broadcasted_iota(jnp.int32, (LANES, 2 * LANES), 1)
    half_sum = jnp.where((hs_row < HD) == (hs_col < LANES), 1.0, 0.0).astype(BF16)
    refs = (r_ref, w_ref, k_ref, v_ref, nkk_ref, kka_ref)

    def token_group(gi, carry):
        s, o_lo, o_hi = carry
        t0 = pl.multiple_of(gi * SUBLANES, SUBLANES)
        blk = [[ref[pl.ds(q * L + t0, SUBLANES), :] for q in range(NS)] for ref in refs]
        for j in range(SUBLANES):
            def stacked(x):
                return jnp.concatenate(
                    [jnp.broadcast_to(x[q][j:j + 1, p * LANES:(p + 1) * LANES], (HD, LANES))
                     for q in range(NS) for p in range(npair)], axis=0)
            rr, ww, kk, vv, nk, ka = [stacked(x) for x in blk]
            sa = _split_dot(s * nk, gm)
            vb = jnp.dot(jnp.where(eye2, vv, 0.0).astype(BF16), gm, preferred_element_type=F32)
            s = s * ww + sa * ka + vb * kk
            oo = jnp.dot((s * rr).astype(BF16), half_sum, preferred_element_type=F32)
            hit = lane == seq_lane0 + t0 + j
            o_lo = jnp.where(hit, oo[:, :LANES], o_lo)
            o_hi = jnp.where(hit, oo[:, LANES:], o_hi)
        return s, o_lo, o_hi

    if NS == 1:
        @pl.when(ti == 0)
        def _():
            s_ref[...] = s0_ref[0]
        s_in = s_ref[...]
    else:
        s_in = s0_ref[...].reshape(R, LANES)
    zero = jnp.zeros((R, LANES), F32)
    s_out, o_lo, o_hi = lax.fori_loop(0, min(L, TT) // SUBLANES, token_group, (s_in, zero, zero))

    pr = npair * HD
    lo = sum(o_lo[q * pr:(q + 1) * pr] for q in range(NS))
    hi = sum(o_hi[q * pr:(q + 1) * pr] for q in range(NS))
    for p in range(npair):
        ot_ref[2 * p * HD:(2 * p + 1) * HD, :] = lo[p * HD:(p + 1) * HD]
        ot_ref[(2 * p + 1) * HD:(2 * p + 2) * HD, :] = hi[p * HD:(p + 1) * HD]
    if NS == 1:
        s_ref[...] = s_out

        @pl.when(ti == n_tiles - 1)
        def _():
            sout_ref[0] = s_out
    else:
        sout_ref[...] = s_out.reshape(NS, pr, LANES)


def _rw_rec(r, w, k, v, nkk, kka, gmat, s0, row_off, n_seq, L):
    TT = LANES
    n_tok = n_seq * L
    n_tiles = n_tok // TT
    rb = row_off // TT
    ns = max(1, TT // L)
    pr = (NH // 2) * HD
    tok = pl.BlockSpec((TT, BW), lambda i: (rb + i, 0))
    if ns == 1:
        sspec = pl.BlockSpec((1, pr, LANES), lambda i: (0, 0, 0))
    else:
        sspec = pl.BlockSpec((ns, pr, LANES), lambda i: (i, 0, 0))
    return pl.pallas_call(
        functools.partial(_rw_rec_body, TT=TT, L=L, NS=ns, n_tiles=n_tiles),
        grid=(n_tiles,),
        in_specs=[tok] * 6 + [pl.BlockSpec((LANES, LANES), lambda i: (0, 0)), sspec],
        out_specs=[pl.BlockSpec((BW, TT), lambda i: (0, i)), sspec],
        out_shape=[jax.ShapeDtypeStruct((BW, n_tok), F32),
                   jax.ShapeDtypeStruct(s0.shape, F32)],
        scratch_shapes=[pltpu.VMEM((pr, LANES), F32)],
        compiler_params=_cparams(("arbitrary",)),
        name="rwkv_recurrence",
    )(r, w, k, v, nkk, kka, gmat, s0)


def _rw_post_body(o_ref, bonus_ref, gate_ref, lnw_ref, lnb_ref, seg_ref, out_ref):
    o = o_ref[...]
    mu = _split_dot(o, seg_ref[...]) * (1.0 / HD)
    d = o - mu
    var = _split_dot(d * d, seg_ref[...]) * (1.0 / HD)
    on = d * lax.rsqrt(var + RW_LN_EPS) * lnw_ref[...] + lnb_ref[...]
    out_ref[...] = (on + bonus_ref[...]) * gate_ref[...]


def _rw_post(o, bonus, gate, lnw, lnb, seg):
    T = o.shape[0]
    tm = _pick(T, (512, 256, 128))
    blk = pl.BlockSpec((tm, BW), lambda i: (i, 0))
    par = lambda a: pl.BlockSpec(a.shape, lambda i: (0,) * a.ndim)
    return pl.pallas_call(
        _rw_post_body,
        grid=(T // tm,),
        in_specs=[blk, blk, blk, par(lnw), par(lnb), par(seg)],
        out_specs=blk,
        out_shape=jax.ShapeDtypeStruct((T, BW), F32),
        compiler_params=_cparams(("parallel",)),
        name="rwkv_post",
    )(o, bonus, gate, lnw, lnb, seg)


def _heads_alt(w):
    d = w.shape[0]
    w4 = w.reshape(d, NH // 2, 2, HD)
    zero = jnp.zeros((d, NH // 2, HD), w.dtype)
    ev = jnp.concatenate([w4[:, :, 0], zero], -1)
    od = jnp.concatenate([zero, w4[:, :, 1]], -1)
    return jnp.stack([ev, od], 2).reshape(d, NH * LANES)


def _heads_lo(w, nh):
    d = w.shape[0]
    w3 = w.reshape(d, nh, HD)
    return jnp.concatenate([w3, jnp.zeros_like(w3)], -1).reshape(d, nh * LANES)


def _padc(w, n):
    return jnp.pad(w, ((0, 0), (0, n - w.shape[1])))


def _rw_reorder(a):
    o = 0
    parts = {}
    for name, wd in (("rr", BW), ("wl", RW_LORA), ("rk", BW), ("rv", BW), ("al", RW_LORA), ("gl", RW_GATE_LORA)):
        parts[name] = a[..., o:o + wd]
        o += wd
    return jnp.concatenate([parts[n] for n in ("rr", "rk", "rv", "wl", "al", "gl")], axis=-1)


def _rw_unorder(a):
    rr, rk, rv = a[..., 0:BW], a[..., BW:2 * BW], a[..., 2 * BW:3 * BW]
    wl = a[..., 3 * BW:3 * BW + RW_LORA]
    al = a[..., 3 * BW + RW_LORA:3 * BW + 2 * RW_LORA]
    gl = a[..., 3 * BW + 2 * RW_LORA:RW_COLS]
    return jnp.concatenate([rr, wl, rk, rv, al, gl], axis=-1)


def _relayout_w_in(w, d_model):
    w = w.astype(BF16)
    o = 0

    def take(n):
        nonlocal o
        s = w[:, o:o + n]
        o += n
        return s
    sq, sk, sv = take(BW), take(BW), take(BW)
    dq, dk, dv, qi, ki, wi = take(BW), take(BW), take(BW), take(BW), take(HD), take(NH)
    gq, gk = take(GLA_H * HD), take(GLA_H * HD)
    gv, gr, ga = take(BW), take(BW), take(GLA_LOWRANK)
    rw = take(RW_COLS)
    gate = take(4 * d_model)
    cols = [_padc(_rw_reorder(rw), 4 * BW), _heads_alt(sq), sk, sv, _heads_alt(dq), dk, dv,
            _heads_lo(qi, NH), _heads_lo(gq, GLA_H), _heads_lo(gk, GLA_H), gv, gr,
            _padc(ki, LANES), _padc(wi, LANES), _padc(ga, 2 * LANES), gate]
    return jnp.concatenate(cols, axis=1)


def _relayout_ffn(w_up, w_down, tf):
    d, f2 = w_up.shape
    f = f2 // 2
    fp = -(-f // tf) * tf
    g = _padc(w_up[:, :f].astype(BF16), fp).reshape(d, fp // tf, tf)
    u = _padc(w_up[:, f:].astype(BF16), fp).reshape(d, fp // tf, tf)
    wup = jnp.concatenate([g, u], axis=2).reshape(d, 2 * fp)
    wdn = jnp.pad(w_down.astype(BF16), ((0, fp - f), (0, 0)))
    return wup, wdn


def _gain_alt(g):
    return jnp.tile(g, 2 * NH)[None, :]


def kernel(x_prompt, x_sample, cache_sb_k, cache_sb_v, cache_dsa_k, cache_dsa_v, cache_dsa_kidx, state_gla, state_rwkv, state_rwkv_shift, ffn1_norm, ffn1_up, ffn1_down, mix_norm, w_in, sb_qnorm, sb_knorm, dsa_qnorm, dsa_knorm, gla_a2, gla_ab, gla_onorm, rwkv_mu, rwkv_w0, rwkv_w2, rwkv_a0, rwkv_a2, rwkv_g2, rwkv_kk, rwkv_ka, rwkv_rk, rwkv_lnw, rwkv_lnb, w_branch, w_out, ffn2_norm, ffn2_up, ffn2_down):
    depth = w_in.shape[0]
    bp, lp, d_model = x_prompt.shape
    nb, ls, _ = x_sample.shape
    assert bp == 1, "prompt group is one sequence"
    tp = bp * lp
    ts = nb * ls
    past = cache_sb_k.shape[2]
    tf = 512

    x = jnp.concatenate([x_prompt.reshape(tp, d_model), x_sample.reshape(ts, d_model)], axis=0)
    ck_sb = cache_sb_k.reshape(depth, nb, past, BW)
    cv_sb = cache_sb_v.reshape(depth, nb, past, BW)
    ck_dsa = cache_dsa_k.reshape(depth, nb, past, BW)
    cv_dsa = cache_dsa_v.reshape(depth, nb, past, BW)

    bd = jnp.asarray(np.kron(np.eye(2 * NH), np.full((HD, HD), 1.0 / HD)), BF16)
    seg = jnp.asarray(np.kron(np.eye(NH), np.ones((HD, HD))), BF16)
    gmat = jnp.asarray(np.kron(np.eye(2), np.ones((HD, HD))), BF16)
    topk_p = min(TOPK_MAX, lp // 4)
    topk_s = min(TOPK_MAX, (past + ls) // 4)

    new_p = [[] for _ in range(8)]
    new_s = [[] for _ in range(8)]
    for i in range(depth):
        wup1, wdn1 = _relayout_ffn(ffn1_up[i], ffn1_down[i], tf)
        wup2, wdn2 = _relayout_ffn(ffn2_up[i], ffn2_down[i], tf)
        win = _relayout_w_in(w_in[i], d_model)

        x = _ffn(x, ffn1_norm[i][None], wup1, wdn1, tf)
        z = _proj(x, mix_norm[i][None], win)

        sqn, skn, dqn, dkn = _qknorm(z, _gain_alt(sb_qnorm[i]), jnp.tile(sb_knorm[i], NH)[None],
                                     _gain_alt(dsa_qnorm[i]), jnp.tile(dsa_knorm[i], NH)[None], bd)
        o_sb = jnp.concatenate([
            _sb_prompt(sqn, skn, z, tp),
            _sb_sample(sqn, skn, z, ck_sb, cv_sb, i, tp, nb, ls)], axis=0)
        mask_p = _dsa_sel_prompt(z, tp, topk_p)
        mask_s = _dsa_sel_sample(z, cache_dsa_kidx, i, tp, nb, ls, topk_s)
        o_dsa = jnp.concatenate([
            _dsa_attn_prompt(dqn, dkn, z, mask_p, tp),
            _dsa_attn_sample(dqn, dkn, z, mask_s, ck_dsa, cv_dsa, i, tp, nb, ls)], axis=0)

        a2p = jnp.pad(_heads_lo(gla_a2[i], GLA_H), ((0, LANES - GLA_LOWRANK), (0, 0))).astype(BF16)
        abp = _heads_lo(gla_ab[i][None], GLA_H)
        onorm = gla_onorm[i][None]

        def gla_state_in(s):
            return jnp.pad(jnp.swapaxes(s, 2, 3), ((0, 0), (0, 0), (0, 0), (0, LANES - HD)))

        def gla_state_out(s):
            return jnp.swapaxes(s[..., :HD], 2, 3)
        og_p, sg_p = _gla(z, a2p, abp, onorm, jnp.zeros((bp, GLA_H, GLA_DV, LANES), F32), 0, bp, lp)
        og_s, sg_s = _gla(z, a2p, abp, onorm, gla_state_in(state_gla[i]), tp, nb, ls)
        o_gla = jnp.concatenate([og_p, og_s], axis=0)

        zrw = z[:, :RW_COLS]
        shift_s = _rw_reorder(state_rwkv_shift[i])
        prev_p = jnp.concatenate([jnp.zeros((1, RW_COLS), F32), zrw[:tp - 1]], axis=0)
        prev_s = jnp.concatenate([shift_s, zrw[tp:].reshape(nb, ls, RW_COLS)[:, :-1]], axis=1)
        prev = _padc(jnp.concatenate([prev_p, prev_s.reshape(ts, RW_COLS)], axis=0), 4 * BW)
        mu = _padc(_rw_reorder(rwkv_mu[i])[None], 4 * BW)
        w2p = jnp.concatenate([rwkv_w2[i], jnp.zeros_like(rwkv_a2[i])], axis=0).astype(BF16)
        a2q = jnp.concatenate([jnp.zeros_like(rwkv_w2[i]), rwkv_a2[i]], axis=0).astype(BF16)
        r_, w_, k_, v_, nkk_, kka_, gate_, bonus_ = _rw_prep(
            z, prev, mu, rwkv_w0[i][None], rwkv_a0[i][None], rwkv_kk[i][None], rwkv_ka[i][None],
            rwkv_rk[i][None], w2p, a2q, rwkv_g2[i].astype(BF16), seg)

        def rw_state_in(s):
            n = s.shape[0]
            return s.reshape(n, NH // 2, 2, HD, HD).transpose(0, 1, 3, 2, 4).reshape(n, NH // 2 * HD, LANES)

        def rw_state_out(s):
            n = s.shape[0]
            return s.reshape(n, NH // 2, HD, 2, HD).transpose(0, 1, 3, 2, 4).reshape(n, NH, HD, HD)
        ot_p, sr_p = _rw_rec(r_, w_, k_, v_, nkk_, kka_, gmat, jnp.zeros((bp, NH // 2 * HD, LANES), F32), 0, bp, lp)
        ot_s, sr_s = _rw_rec(r_, w_, k_, v_, nkk_, kka_, gmat, rw_state_in(state_rwkv[i]), tp, nb, ls)
        o_r = jnp.concatenate([ot_p.T, ot_s.T], axis=0)
        o_rw = _rw_post(o_r, bonus_, gate_, rwkv_lnw[i][None], rwkv_lnb[i][None], seg)

        x = _merge(x, z, o_sb, o_dsa, o_gla, o_rw, w_branch[i].astype(BF16), w_out[i].astype(BF16))
        x = _ffn(x, ffn2_norm[i][None], wup2, wdn2, tf)

        sv = z[:, OFF_SV:OFF_SV + BW]
        dv = z[:, OFF_DV:OFF_DV + BW]
        ki = z[:, OFF_KI:OFF_KI + HD]
        for dst, (lo, n, l) in ((new_p, (0, bp, lp)), (new_s, (tp, nb, ls))):
            hi = lo + n * l
            dst[0].append(skn[lo:hi].reshape(n, l, NH, HD))
            dst[1].append(sv[lo:hi].reshape(n, l, NH, HD))
            dst[2].append(dkn[lo:hi].reshape(n, l, NH, HD))
            dst[3].append(dv[lo:hi].reshape(n, l, NH, HD))
            dst[4].append(ki[lo:hi].reshape(n, l, HD))
            dst[7].append(_rw_unorder(zrw[lo:hi].reshape(n, l, RW_COLS)[:, -1:]))
        new_p[5].append(gla_state_out(sg_p))
        new_s[5].append(gla_state_out(sg_s))
        new_p[6].append(rw_state_out(sr_p))
        new_s[6].append(rw_state_out(sr_s))

    outs_p = [jnp.stack(l, axis=0) for l in new_p]
    outs_s = [jnp.stack(l, axis=0) for l in new_s]
    y_p = x[:tp].reshape(bp, lp, d_model)
    y_s = x[tp:].reshape(nb, ls, d_model)
    return (y_p, y_s, *outs_p, *outs_s)
```

```python
import functools
import math

import jax
import jax.numpy as jnp
import numpy as np
from jax import lax
from jax.experimental import pallas as pl
from jax.experimental.pallas import tpu as pltpu

F32 = jnp.float32
BF16 = jnp.bfloat16

LANES = 128
SUBLANES = 8
HD = 64
NH = 8
BW = NH * HD
GLA_H = 4
GLA_DV = 128
GLA_LOWRANK = 16
GLA_GATE_NORM = 16.0
CHUNK = 64
TOPK_MAX = 256
RW_LORA = 64
RW_GATE_LORA = 128
RW_COLS = 3 * BW + 2 * RW_LORA + RW_GATE_LORA
NORM_EPS = 1e-6
RW_LN_EPS = 64e-5
NEG_BIG = -1e30
SB_UNDERFLOW = -104.0
VMEM_LIMIT = 56 * 1024 * 1024

OFF_RW = 0
OFF_SQ = 2048
OFF_SK = 3072
OFF_SV = 3584
OFF_DQ = 4096
OFF_DK = 5120
OFF_DV = 5632
OFF_QI = 6144
OFF_GQ = 7168
OFF_GK = 7680
OFF_GV = 8192
OFF_GR = 8704
OFF_KI = 9216
OFF_WI = 9344
OFF_GA = 9472
OFF_GATE = 9728


def _cparams(sem):
    return pltpu.CompilerParams(dimension_semantics=sem, vmem_limit_bytes=VMEM_LIMIT)


def _pick(n, prefs):
    for p in prefs:
        if n % p == 0:
            return p
    raise ValueError(f"no tile for {n} in {prefs}")


def _split_dot(a, b):
    hi = a.astype(BF16)
    lo = (a - hi.astype(F32)).astype(BF16)
    return (jnp.dot(hi, b, preferred_element_type=F32)
            + jnp.dot(lo, b, preferred_element_type=F32))


def _split_dot_left(a, b):
    hi = b.astype(BF16)
    lo = (b - hi.astype(F32)).astype(BF16)
    return (jnp.dot(a, hi, preferred_element_type=F32)
            + jnp.dot(a, lo, preferred_element_type=F32))


def _dot_nt(a, b):
    return lax.dot_general(a, b, (((1,), (1,)), ((), ())), preferred_element_type=F32)


def _dot_tn(a, b):
    return lax.dot_general(a, b, (((0,), (0,)), ((), ())), preferred_element_type=F32)


def _neg_softplus(z):
    return -(jnp.maximum(z, 0.0) + jnp.log(1.0 + jnp.exp(-jnp.abs(z))))


def _ffn_body(x_ref, g_ref, wup_ref, wdn_ref, o_ref, xn_ref, *, tf):
    @pl.when(pl.program_id(1) == 0)
    def _():
        x = x_ref[...]
        ms = jnp.mean(x * x, axis=-1, keepdims=True)
        xn_ref[...] = (x * lax.rsqrt(ms + NORM_EPS) * g_ref[...]).astype(BF16)
        o_ref[...] = x

    gu = jnp.dot(xn_ref[...], wup_ref[...], preferred_element_type=F32)
    gate = gu[:, :tf]
    act = (gate * jax.nn.sigmoid(gate) * gu[:, tf:]).astype(BF16)
    o_ref[...] += 0.5 * jnp.dot(act, wdn_ref[...], preferred_element_type=F32)


def _ffn(x, g, wup, wdn, tf):
    T, D = x.shape
    fp = wdn.shape[0]
    tm = _pick(T, (512, 256, 128))
    return pl.pallas_call(
        functools.partial(_ffn_body, tf=tf),
        grid=(T // tm, fp // tf),
        in_specs=[pl.BlockSpec((tm, D), lambda i, j: (i, 0)),
                  pl.BlockSpec((1, D), lambda i, j: (0, 0)),
                  pl.BlockSpec((D, 2 * tf), lambda i, j: (0, j)),
                  pl.BlockSpec((tf, D), lambda i, j: (j, 0))],
        out_specs=pl.BlockSpec((tm, D), lambda i, j: (i, 0)),
        out_shape=jax.ShapeDtypeStruct((T, D), F32),
        scratch_shapes=[pltpu.VMEM((tm, D), BF16)],
        compiler_params=_cparams(("parallel", "arbitrary")),
        name="ffn",
    )(x, g, wup, wdn)


def _proj_body(x_ref, g_ref, w_ref, o_ref, xn_ref):
    @pl.when(pl.program_id(1) == 0)
    def _():
        x = x_ref[...]
        ms = jnp.mean(x * x, axis=-1, keepdims=True)
        xn_ref[...] = (x * lax.rsqrt(ms + NORM_EPS) * g_ref[...]).astype(BF16)

    o_ref[...] = jnp.dot(xn_ref[...], w_ref[...], preferred_element_type=F32)


def _proj(x, g, w):
    T, D = x.shape
    n = w.shape[1]
    tm = _pick(T, (1024, 512, 256, 128))
    tn = 512
    return pl.pallas_call(
        _proj_body,
        grid=(T // tm, n // tn),
        in_specs=[pl.BlockSpec((tm, D), lambda i, j: (i, 0)),
                  pl.BlockSpec((1, D), lambda i, j: (0, 0)),
                  pl.BlockSpec((D, tn), lambda i, j: (0, j))],
        out_specs=pl.BlockSpec((tm, tn), lambda i, j: (i, j)),
        out_shape=jax.ShapeDtypeStruct((T, n), F32),
        scratch_shapes=[pltpu.VMEM((tm, D), BF16)],
        compiler_params=_cparams(("parallel", "arbitrary")),
        name="in_proj",
    )(x, g, w)


def _merge_body(osb_ref, ods_ref, ogl_ref, orw_ref, g0_ref, g1_ref, g2_ref, g3_ref,
                wb_ref, wo_ref, x_ref, o_ref):
    @pl.when(pl.program_id(1) == 0)
    def _():
        o_ref[...] = x_ref[...]

    acc = None
    for n, (b_ref, g_ref) in enumerate(((osb_ref, g0_ref), (ods_ref, g1_ref),
                                        (ogl_ref, g2_ref), (orw_ref, g3_ref))):
        p = jnp.dot(b_ref[...].astype(BF16), wb_ref[n], preferred_element_type=F32)
        t = jax.nn.sigmoid(g_ref[...]) * p
        acc = t if acc is None else acc + t
    o_ref[...] += jnp.dot(acc.astype(BF16), wo_ref[...], preferred_element_type=F32)


def _merge(x, z, o_sb, o_dsa, o_gla, o_rw, wb, wo):
    T, D = x.shape
    tm = _pick(T, (512, 256, 128))
    tc = 512
    nc = D // tc
    gate_blk = OFF_GATE // tc
    bspec = pl.BlockSpec((tm, BW), lambda i, c: (i, 0))
    gspecs = [pl.BlockSpec((tm, tc), functools.partial(lambda i, c, n: (i, gate_blk + n * nc + c), n=n))
              for n in range(4)]
    return pl.pallas_call(
        _merge_body,
        grid=(T // tm, nc),
        in_specs=[bspec, bspec, bspec, bspec] + gspecs + [
            pl.BlockSpec((4, BW, tc), lambda i, c: (0, 0, c)),
            pl.BlockSpec((tc, D), lambda i, c: (c, 0)),
            pl.BlockSpec((tm, D), lambda i, c: (i, 0))],
        out_specs=pl.BlockSpec((tm, D), lambda i, c: (i, 0)),
        out_shape=jax.ShapeDtypeStruct((T, D), F32),
        compiler_params=_cparams(("parallel", "arbitrary")),
        name="merge",
    )(o_sb, o_dsa, o_gla, o_rw, z, z, z, z, wb, wo, x)


def _qknorm_body(sq_ref, sk_ref, dq_ref, dk_ref, gsq_ref, gsk_ref, gdq_ref, gdk_ref, bd_ref,
                 osq_ref, osk_ref, odq_ref, odk_ref):
    def norm(x, g, n):
        ms = _split_dot(x * x, bd_ref[:n, :n])
        return x * lax.rsqrt(ms + NORM_EPS) * g

    scale = HD ** -0.5
    osq_ref[...] = (norm(sq_ref[...], gsq_ref[...], 2 * BW) * scale).astype(BF16)
    osk_ref[...] = norm(sk_ref[...], gsk_ref[...], BW)
    odq_ref[...] = (norm(dq_ref[...], gdq_ref[...], 2 * BW) * scale).astype(BF16)
    odk_ref[...] = norm(dk_ref[...], gdk_ref[...], BW)


def _qknorm(z, gsq, gsk, gdq, gdk, bd):
    T = z.shape[0]
    tm = _pick(T, (512, 256, 128))
    wq, wk = 2 * BW, BW
    row = lambda w, off: pl.BlockSpec((tm, w), lambda i: (i, off // w))
    par = lambda w: pl.BlockSpec((1, w), lambda i: (0, 0))
    out = lambda w: pl.BlockSpec((tm, w), lambda i: (i, 0))
    return pl.pallas_call(
        _qknorm_body,
        grid=(T // tm,),
        in_specs=[row(wq, OFF_SQ), row(wk, OFF_SK), row(wq, OFF_DQ), row(wk, OFF_DK),
                  par(wq), par(wk), par(wq), par(wk),
                  pl.BlockSpec((wq, wq), lambda i: (0, 0))],
        out_specs=[out(wq), out(wk), out(wq), out(wk)],
        out_shape=[jax.ShapeDtypeStruct((T, wq), BF16), jax.ShapeDtypeStruct((T, wk), F32),
                   jax.ShapeDtypeStruct((T, wq), BF16), jax.ShapeDtypeStruct((T, wk), F32)],
        compiler_params=_cparams(("parallel",)),
        name="qk_norm",
    )(z, z, z, z, gsq, gsk, gdq, gdk, bd)


def _sb_body(*refs, tq, n_cache, tkc):
    if n_cache:
        q_ref, kn_ref, vn_ref, kc_ref, vc_ref, o_ref = refs
    else:
        q_ref, kn_ref, vn_ref, o_ref = refs
    qb = pl.program_id(1) if not n_cache else 0
    qs = [q_ref[:, e * LANES:(e + 1) * LANES] for e in range(2)]

    def tri(n):
        r = lax.broadcasted_iota(jnp.int32, (n, n), 0)
        c = lax.broadcasted_iota(jnp.int32, (n, n), 1)
        return (r > c).astype(BF16)

    def visit(state, kblk, vblk, u, valid):
        kb16 = kblk.astype(BF16)
        vb16 = vblk.astype(BF16)
        new = []
        for e in range(2):
            carry, acc = state[e]
            z = _dot_nt(qs[e], kb16)
            lg = _neg_softplus(z)
            if valid is not None:
                lg = jnp.where(valid, lg, 0.0)
            inner = _split_dot(lg, u)
            a = jnp.exp(z + lg + inner + carry)
            if valid is not None:
                a = jnp.where(valid, a, 0.0)
            acc = acc + jnp.dot(a.astype(BF16), vb16, preferred_element_type=F32)
            carry = carry + inner[:, :1] + lg[:, :1]
            new.append((carry, acc))
        return tuple(new)

    zero = (jnp.zeros((tq, 1), F32), jnp.zeros((tq, LANES), F32))
    state = (zero, zero)

    r = lax.broadcasted_iota(jnp.int32, (tq, tq), 0)
    c = lax.broadcasted_iota(jnp.int32, (tq, tq), 1)
    u_new = tri(tq)
    if n_cache:
        state = visit(state, kn_ref[...], vn_ref[...], u_new, c < r)
    else:
        row0 = pl.multiple_of(qb * tq, tq)
        state = visit(state, kn_ref[pl.ds(row0, tq), :], vn_ref[pl.ds(row0, tq), :], u_new, c < r)

    def live(c):
        i, st = c
        return (i < n_old) & (jnp.max(jnp.maximum(st[0][0], st[1][0])) > SB_UNDERFLOW)

    if not n_cache:
        n_old = qb

        def step(c):
            i, st = c
            r0 = pl.multiple_of((qb - 1 - i) * tq, tq)
            return i + 1, visit(st, kn_ref[pl.ds(r0, tq), :], vn_ref[pl.ds(r0, tq), :], u_new, None)
    else:
        n_old = n_cache
        u_c = tri(tkc)

        def step(c):
            i, st = c
            r0 = pl.multiple_of((n_cache - 1 - i) * tkc, tkc)
            return i + 1, visit(st, kc_ref[0, 0, pl.ds(r0, tkc), :], vc_ref[0, 0, pl.ds(r0, tkc), :],
                                u_c, None)
    _, state = lax.while_loop(live, step, (jnp.int32(0), state))

    lane = lax.broadcasted_iota(jnp.int32, (tq, LANES), 1)
    o_ref[...] = jnp.where(lane < HD, state[0][1], state[1][1])


def _sb_prompt(qn, kn, z, n_rows):
    tq = 128
    vblk = OFF_SV // LANES
    return pl.pallas_call(
        functools.partial(_sb_body, tq=tq, n_cache=0, tkc=0),
        grid=(NH // 2, n_rows // tq),
        in_specs=[pl.BlockSpec((tq, 2 * LANES), lambda p, i: (i, p)),
                  pl.BlockSpec((n_rows, LANES), lambda p, i: (0, p)),
                  pl.BlockSpec((n_rows, LANES), lambda p, i: (0, vblk + p))],
        out_specs=pl.BlockSpec((tq, LANES), lambda p, i: (i, p)),
        out_shape=jax.ShapeDtypeStruct((n_rows, BW), F32),
        compiler_params=_cparams(("parallel", "arbitrary")),
        name="sb_prompt",
    )(qn, kn, z)


def _sb_sample(qn, kn, z, cache_k, cache_v, layer, row_off, nb, L):
    P = cache_k.shape[2]
    tkc = _pick(P, (256, 128))
    vblk = OFF_SV // LANES
    rb = row_off // L
    return pl.pallas_call(
        functools.partial(_sb_body, tq=L, n_cache=P // tkc, tkc=tkc),
        grid=(nb, NH // 2),
        in_specs=[pl.BlockSpec((L, 2 * LANES), lambda b, p: (rb + b, p)),
                  pl.BlockSpec((L, LANES), lambda b, p: (rb + b, p)),
                  pl.BlockSpec((L, LANES), lambda b, p: (rb + b, vblk + p)),
                  pl.BlockSpec((1, 1, P, LANES), lambda b, p: (layer, b, 0, p)),
                  pl.BlockSpec((1, 1, P, LANES), lambda b, p: (layer, b, 0, p))],
        out_specs=pl.BlockSpec((L, LANES), lambda b, p: (b, p)),
        out_shape=jax.ShapeDtypeStruct((nb * L, BW), F32),
        compiler_params=_cparams(("parallel", "parallel")),
        name="sb_sample",
    )(qn, kn, z, cache_k, cache_v)


def _sortable(x):
    b = pltpu.bitcast(x + 0.0, jnp.int32)
    return jnp.where(b < 0, b ^ jnp.int32(0x7FFFFFFF), b)


def _select_topk(key_ref, n_blk, tkb, topk, rows):
    def count(pred):
        def body(i, acc):
            c0 = pl.multiple_of(i * tkb, tkb)
            for c in range(tkb // LANES):
                blk = key_ref[:, pl.ds(c0 + c * LANES, LANES)]
                col = c0 + c * LANES + lax.broadcasted_iota(jnp.int32, (rows, LANES), 1)
                acc = acc + jnp.where(pred(blk, col), 1.0, 0.0)
            return acc
        acc = lax.fori_loop(0, n_blk, body, jnp.zeros((rows, LANES), F32))
        return jnp.sum(acc, axis=1, keepdims=True)

    def unresolved(c):
        i, _, n_ge = c
        return (i < 32) & (jnp.max(jnp.abs(n_ge - topk)) > 0)

    def bit_step(c):
        i, thr, n_ge = c
        cand = thr + (jnp.int32(1) << (31 - i))
        n_cand = count(lambda blk, col: blk >= cand)
        take = n_cand >= topk
        return i + 1, jnp.where(take, cand, thr), jnp.where(take, n_cand, n_ge)

    thr0 = jnp.full((rows, 1), jnp.iinfo(jnp.int32).min, jnp.int32)
    n_all = jnp.full((rows, 1), 1.0, F32) * jnp.asarray(n_blk * tkb).astype(F32)
    _, thr, n_ge = lax.while_loop(unresolved, bit_step, (jnp.int32(0), thr0, n_all))
    n_col_bits = max(1, int(math.ceil(math.log2(key_ref.shape[1] + 1))))

    def tie_search():
        need = topk - count(lambda blk, col: blk > thr)

        def col_step(i, j):
            cand = j + (jnp.int32(1) << (n_col_bits - 1 - i))
            n = count(lambda blk, col: (blk == thr) & (col < cand))
            return jnp.where(n < need, cand, j)
        return lax.fori_loop(0, n_col_bits, col_step, jnp.zeros((rows, 1), jnp.int32))

    any_tie = jnp.max(jnp.abs(n_ge - topk)) > 0
    j_hi = lax.cond(any_tie, tie_search,
                    lambda: jnp.full((rows, 1), jnp.iinfo(jnp.int32).max, jnp.int32))
    return thr, j_hi


def _idx_scores(qi, wi, kblk16, k_real):
    acc = None
    for h in range(NH):
        qh = qi[:, h * LANES:h * LANES + k_real].astype(BF16)
        d = jnp.maximum(_dot_nt(qh, kblk16), 0.0) * (HD ** -0.5)
        t = (wi[:, h:h + 1] * (NH ** -0.5)) * d
        acc = t if acc is None else acc + t
    return acc


def _dsa_sel_prompt_body(qi_ref, wi_ref, ki_ref, m_ref, key_ref, *, tq, tkb, topk, n_keys):
    qb = pl.program_id(0)
    n_blk = ((qb + 1) * tq + tkb - 1) // tkb
    qi = qi_ref[...]
    wi = wi_ref[...]
    q_chunk = (qb * tq + lax.broadcasted_iota(jnp.int32, (tq, tkb), 0)) // CHUNK

    def score_blk(i, _):
        c0 = pl.multiple_of(i * tkb, tkb)
        s = _idx_scores(qi, wi, ki_ref[pl.ds(c0, tkb), :].astype(BF16), LANES)
        k_chunk = (c0 + lax.broadcasted_iota(jnp.int32, (tq, tkb), 1)) // CHUNK
        s = jnp.where(k_chunk <= q_chunk, s, -jnp.inf)
        key_ref[:, pl.ds(c0, tkb)] = _sortable(s)
        return 0
    lax.fori_loop(0, n_blk, score_blk, 0)

    thr, j_hi = _select_topk(key_ref, n_blk, tkb, topk, tq)
    neg_inf_key = _sortable(jnp.full((1, 1), -jnp.inf, F32))

    def write_blk(i, _):
        c0 = pl.multiple_of(i * tkb, tkb)
        blk = key_ref[:, pl.ds(c0, tkb)]
        col = c0 + lax.broadcasted_iota(jnp.int32, (tq, tkb), 1)
        sel = ((blk > thr) | ((blk == thr) & (col <= j_hi))) & (blk > neg_inf_key)
        m_ref[:, pl.ds(c0, tkb)] = jnp.where(sel, 1.0, 0.0).astype(BF16)
        return 0
    lax.fori_loop(0, n_blk, write_blk, 0)

    def zero_blk(i, _):
        m_ref[:, pl.ds(pl.multiple_of(i * tkb, tkb), tkb)] = jnp.zeros((tq, tkb), BF16)
        return 0
    lax.fori_loop(n_blk, n_keys // tkb, zero_blk, 0)


def _dsa_sel_prompt(z, n_rows, topk):
    tq, tkb = 128, 512
    return pl.pallas_call(
        functools.partial(_dsa_sel_prompt_body, tq=tq, tkb=tkb, topk=topk, n_keys=n_rows),
        grid=(n_rows // tq,),
        in_specs=[pl.BlockSpec((tq, NH * LANES), lambda i: (i, OFF_QI // (NH * LANES))),
                  pl.BlockSpec((tq, LANES), lambda i: (i, OFF_WI // LANES)),
                  pl.BlockSpec((n_rows, LANES), lambda i: (0, OFF_KI // LANES))],
        out_specs=pl.BlockSpec((tq, n_rows), lambda i: (i, 0)),
        out_shape=jax.ShapeDtypeStruct((n_rows, n_rows), BF16),
        scratch_shapes=[pltpu.VMEM((tq, n_rows), jnp.int32)],
        compiler_params=_cparams(("parallel",)),
        name="dsa_select_prompt",
    )(z, z, z)


def _dsa_sel_sample_body(qi_ref, wi_ref, kin_ref, kic_ref, m_ref, key_ref, *, L, P, tkb, topk):
    qi = qi_ref[...]
    wi = wi_ref[...]
    n_c = P // tkb

    def score_blk(i, _):
        c0 = pl.multiple_of(i * tkb, tkb)
        s = _idx_scores(qi, wi, kic_ref[0, 0, pl.ds(c0, tkb), :].astype(BF16), HD)
        key_ref[:, pl.ds(c0, tkb)] = _sortable(s)
        return 0
    lax.fori_loop(0, n_c, score_blk, 0)

    knew = jnp.concatenate([kin_ref[...], jnp.zeros((tkb - L, LANES), F32)], axis=0).astype(BF16)
    s = _idx_scores(qi, wi, knew, LANES)
    col = lax.broadcasted_iota(jnp.int32, (L, tkb), 1)
    key_ref[:, pl.ds(P, tkb)] = _sortable(jnp.where(col < L, s, -jnp.inf))

    thr, j_hi = _select_topk(key_ref, n_c + 1, tkb, topk, L)
    neg_inf_key = _sortable(jnp.full((1, 1), -jnp.inf, F32))

    def write_blk(i, _):
        c0 = pl.multiple_of(i * tkb, tkb)
        blk = key_ref[:, pl.ds(c0, tkb)]
        cc = c0 + lax.broadcasted_iota(jnp.int32, (L, tkb), 1)
        sel = ((blk > thr) | ((blk == thr) & (cc <= j_hi))) & (blk > neg_inf_key)
        m_ref[:, pl.ds(c0, tkb)] = jnp.where(sel, 1.0, 0.0).astype(BF16)
        return 0
    lax.fori_loop(0, n_c + 1, write_blk, 0)


def _dsa_sel_sample(z, cache_ki, layer, row_off, nb, L, topk):
    P = cache_ki.shape[2]
    tkb = _pick(P, (512, 256, 128))
    rb = row_off // L
    return pl.pallas_call(
        functools.partial(_dsa_sel_sample_body, L=L, P=P, tkb=tkb, topk=topk),
        grid=(nb,),
        in_specs=[pl.BlockSpec((L, NH * LANES), lambda b: (rb + b, OFF_QI // (NH * LANES))),
                  pl.BlockSpec((L, LANES), lambda b: (rb + b, OFF_WI // LANES)),
                  pl.BlockSpec((L, LANES), lambda b: (rb + b, OFF_KI // LANES)),
                  pl.BlockSpec((1, 1, P, HD), lambda b: (layer, b, 0, 0))],
        out_specs=pl.BlockSpec((L, P + tkb), lambda b: (b, 0)),
        out_shape=jax.ShapeDtypeStruct((nb * L, P + tkb), BF16),
        scratch_shapes=[pltpu.VMEM((L, P + tkb), jnp.int32)],
        compiler_params=_cparams(("parallel",)),
        name="dsa_select_sample",
    )(z, z, z, cache_ki)


def _dsa_attn_body(*refs, tq, tkb, n_cache, prompt):
    if prompt:
        q_ref, m_ref, kn_ref, vn_ref, o_ref = refs
        qb = pl.program_id(1)
        n_blk = ((qb + 1) * tq + tkb - 1) // tkb
    else:
        q_ref, m_ref, kn_ref, vn_ref, kc_ref, vc_ref, o_ref = refs
        n_blk = n_cache
    qs = [q_ref[:, e * LANES:(e + 1) * LANES] for e in range(2)]

    def visit(state, kblk, vblk, mblk):
        kb16 = kblk.astype(BF16)
        vb16 = vblk.astype(BF16)
        sel = mblk > 0
        new = []
        for e in range(2):
            m, l, acc = state[e]
            s = jnp.where(sel, _dot_nt(qs[e], kb16), NEG_BIG)
            m_new = jnp.maximum(m, jnp.max(s, axis=1, keepdims=True))
            p = jnp.where(sel, jnp.exp(s - m_new), 0.0)
            alpha = jnp.exp(m - m_new)
            l = alpha * l + jnp.sum(p, axis=1, keepdims=True)
            acc = alpha * acc + jnp.dot(p.astype(BF16), vb16, preferred_element_type=F32)
            new.append((m_new, l, acc))
        return tuple(new)

    init = (jnp.full((tq, 1), NEG_BIG, F32), jnp.zeros((tq, 1), F32), jnp.zeros((tq, LANES), F32))
    state = (init, init)

    def step(i, st):
        c0 = pl.multiple_of(i * tkb, tkb)
        if prompt:
            return visit(st, kn_ref[pl.ds(c0, tkb), :], vn_ref[pl.ds(c0, tkb), :], m_ref[:, pl.ds(c0, tkb)])
        return visit(st, kc_ref[0, 0, pl.ds(c0, tkb), :], vc_ref[0, 0, pl.ds(c0, tkb), :],
                     m_ref[:, pl.ds(c0, tkb)])
    state = lax.fori_loop(0, n_blk, step, state)

    if not prompt:
        pad = jnp.zeros((tkb - tq, LANES), F32)
        knew = jnp.concatenate([kn_ref[...], pad], axis=0)
        vnew = jnp.concatenate([vn_ref[...], pad], axis=0)
        state = visit(state, knew, vnew, m_ref[:, pl.ds(n_cache * tkb, tkb)])

    lane = lax.broadcasted_iota(jnp.int32, (tq, LANES), 1)
    o0 = state[0][2] / state[0][1]
    o1 = state[1][2] / state[1][1]
    o_ref[...] = jnp.where(lane < HD, o0, o1)


def _dsa_attn_prompt(qn, kn, z, mask, n_rows):
    tq, tkb = 128, min(1024, n_rows)
    vblk = OFF_DV // LANES
    return pl.pallas_call(
        functools.partial(_dsa_attn_body, tq=tq, tkb=tkb, n_cache=0, prompt=True),
        grid=(NH // 2, n_rows // tq),
        in_specs=[pl.BlockSpec((tq, 2 * LANES), lambda p, i: (i, p)),
                  pl.BlockSpec((tq, n_rows), lambda p, i: (i, 0)),
                  pl.BlockSpec((n_rows, LANES), lambda p, i: (0, p)),
                  pl.BlockSpec((n_rows, LANES), lambda p, i: (0, vblk + p))],
        out_specs=pl.BlockSpec((tq, LANES), lambda p, i: (i, p)),
        out_shape=jax.ShapeDtypeStruct((n_rows, BW), F32),
        compiler_params=_cparams(("parallel", "arbitrary")),
        name="dsa_attn_prompt",
    )(qn, mask, kn, z)


def _dsa_attn_sample(qn, kn, z, mask, cache_k, cache_v, layer, row_off, nb, L):
    P = cache_k.shape[2]
    tkb = mask.shape[1] - P
    vblk = OFF_DV // LANES
    rb = row_off // L
    return pl.pallas_call(
        functools.partial(_dsa_attn_body, tq=L, tkb=tkb, n_cache=P // tkb, prompt=False),
        grid=(nb, NH // 2),
        in_specs=[pl.BlockSpec((L, 2 * LANES), lambda b, p: (rb + b, p)),
                  pl.BlockSpec((L, P + tkb), lambda b, p: (b, 0)),
                  pl.BlockSpec((L, LANES), lambda b, p: (rb + b, p)),
                  pl.BlockSpec((L, LANES), lambda b, p: (rb + b, vblk + p)),
                  pl.BlockSpec((1, 1, P, LANES), lambda b, p: (layer, b, 0, p)),
                  pl.BlockSpec((1, 1, P, LANES), lambda b, p: (layer, b, 0, p))],
        out_specs=pl.BlockSpec((L, LANES), lambda b, p: (b, p)),
        out_shape=jax.ShapeDtypeStruct((nb * L, BW), F32),
        compiler_params=_cparams(("parallel", "parallel")),
        name="dsa_attn_sample",
    )(qn, mask, kn, z, cache_k, cache_v)


def _gla_consts(C):
    nlev = int(math.log2(C))
    t = np.arange(C)
    cum = (t[:, None] >= t[None, :]).astype(np.float32)
    sel = np.zeros((nlev * C, C), np.float32)
    for l in range(nlev):
        m = 1 << l
        mid = (t // (2 * m)) * 2 * m + m
        sel[l * C + t, mid - 1] = 1.0
    return jnp.asarray(cum, BF16), jnp.asarray(sel, BF16)


def _gla_body(gq_ref, gk_ref, gv_ref, gr_ref, ga_ref, a2_ref, ab_ref, on_ref, cum_ref, sel_ref,
              s0_ref, o_ref, sout_ref, st_ref, *, C, n_chunks):
    ci = pl.program_id(1)
    nlev = int(math.log2(C))

    @pl.when(ci == 0)
    def _():
        st_ref[...] = s0_ref[0]

    x = jnp.dot(ga_ref[...].astype(BF16), a2_ref[...], preferred_element_type=F32) + ab_ref[...]
    g = _neg_softplus(-x) / GLA_GATE_NORM
    b = _split_dot_left(cum_ref[...], g)
    c_all = _split_dot_left(sel_ref[...], b)

    row = lax.broadcasted_iota(jnp.int32, (C, C), 0)
    col = lax.broadcasted_iota(jnp.int32, (C, C), 1)
    rowl = lax.broadcasted_iota(jnp.int32, (C, LANES), 0)

    for h in range(GLA_H):
        sl = slice(h * LANES, (h + 1) * LANES)
        q = gq_ref[:, sl] * (HD ** -0.5)
        k = gk_ref[:, sl]
        v = gv_ref[:, sl]
        bh = b[:, sl]
        a = jnp.where(row == col, jnp.sum(q * k, axis=1, keepdims=True), 0.0)
        for l in range(nlev):
            m = 1 << l
            later = (rowl & m) != 0
            ch = c_all[l * C:(l + 1) * C, sl]
            qe = jnp.where(later, q * jnp.exp(jnp.where(later, bh - ch, 0.0)), 0.0)
            ke = jnp.where(later, 0.0, k * jnp.exp(jnp.where(later, 0.0, ch - bh)))
            al = _dot_nt(qe.astype(BF16), ke.astype(BF16))
            a = a + jnp.where((row // (2 * m)) == (col // (2 * m)), al, 0.0)
        st = st_ref[h]
        o = jnp.dot(a.astype(BF16), v.astype(BF16), preferred_element_type=F32)
        o = o + _dot_nt((q * jnp.exp(bh)).astype(BF16), st.astype(BF16))
        b_last = bh[C - 1:C, :]
        kx = k * jnp.exp(b_last - bh)
        st_ref[h] = st * jnp.exp(b_last) + _dot_tn(v.astype(BF16), kx.astype(BF16))
        on = o * lax.rsqrt(jnp.mean(o * o, axis=-1, keepdims=True) + NORM_EPS) * on_ref[...]
        gr = gr_ref[:, sl]
        o_ref[:, sl] = on * (gr * jax.nn.sigmoid(gr))

    @pl.when(ci == n_chunks - 1)
    def _():
        sout_ref[0] = st_ref[...]


def _gla(z, a2p, abp, onorm, s0t, row_off, n_seq, L):
    C = min(CHUNK, L)
    n_chunks = L // C
    rb = row_off // C
    cum, sel = _gla_consts(C)
    blk = lambda off: pl.BlockSpec((C, BW), lambda s, c: (rb + s * n_chunks + c, off // BW))
    full = lambda a: pl.BlockSpec(a.shape, lambda s, c: (0,) * a.ndim)
    return pl.pallas_call(
        functools.partial(_gla_body, C=C, n_chunks=n_chunks),
        grid=(n_seq, n_chunks),
        in_specs=[blk(OFF_GQ), blk(OFF_GK), blk(OFF_GV), blk(OFF_GR),
                  pl.BlockSpec((C, LANES), lambda s, c: (rb + s * n_chunks + c, OFF_GA // LANES)),
                  full(a2p), full(abp), full(onorm), full(cum), full(sel),
                  pl.BlockSpec((1, GLA_H, GLA_DV, LANES), lambda s, c: (s, 0, 0, 0))],
        out_specs=[pl.BlockSpec((C, BW), lambda s, c: (s * n_chunks + c, 0)),
                   pl.BlockSpec((1, GLA_H, GLA_DV, LANES), lambda s, c: (s, 0, 0, 0))],
        out_shape=[jax.ShapeDtypeStruct((n_seq * L, BW), F32),
                   jax.ShapeDtypeStruct((n_seq, GLA_H, GLA_DV, LANES), F32)],
        scratch_shapes=[pltpu.VMEM((GLA_H, GLA_DV, LANES), F32)],
        compiler_params=_cparams(("parallel", "arbitrary")),
        name="gla",
    )(z, z, z, z, z, a2p, abp, onorm, cum, sel, s0t)


def _rw_prep_body(z_ref, prev_ref, mu_ref, w0_ref, a0_ref, kkp_ref, kap_ref, rkp_ref,
                  w2_ref, a2_ref, g2_ref, seg_ref,
                  r_ref, w_ref, k_ref, v_ref, nkk_ref, kka_ref, gate_ref, bonus_ref):
    zz = z_ref[...]
    zm = zz + mu_ref[...] * (prev_ref[...] - zz)
    rr = zm[:, 0:BW]
    rk = zm[:, BW:2 * BW]
    rv = zm[:, 2 * BW:3 * BW]
    lora = zm[:, 3 * BW:3 * BW + LANES]
    gl = zm[:, 3 * BW + LANES:3 * BW + 2 * LANES]
    xw = w0_ref[...] + jnp.dot(jnp.tanh(lora).astype(BF16), w2_ref[...], preferred_element_type=F32)
    wlog = _neg_softplus(-xw) - 0.5
    decay = jnp.exp(-jnp.exp(wlog))
    a = jax.nn.sigmoid(a0_ref[...] + jnp.dot(lora.astype(BF16), a2_ref[...], preferred_element_type=F32))
    gate = jnp.dot(jax.nn.sigmoid(gl).astype(BF16), g2_ref[...], preferred_element_type=F32)
    kkf = rk * kkp_ref[...]
    nrm = jnp.sqrt(_split_dot(kkf * kkf, seg_ref[...]))
    kk = kkf / jnp.maximum(nrm, 1e-12)
    kmod = rk * (1.0 + (a - 1.0) * kap_ref[...])
    coef = _split_dot(rr * kmod * rkp_ref[...], seg_ref[...])
    r_ref[...] = rr
    w_ref[...] = decay
    k_ref[...] = kmod
    v_ref[...] = rv
    nkk_ref[...] = -kk
    kka_ref[...] = kk * a
    gate_ref[...] = gate
    bonus_ref[...] = coef * rv


def _rw_prep(z, prev, mu, w0, a0, kkp, kap, rkp, w2p, a2p, g2, seg):
    T = z.shape[0]
    tm = _pick(T, (256, 128))
    W = 4 * BW
    par = lambda a: pl.BlockSpec(a.shape, lambda i: (0,) * a.ndim)
    out = pl.BlockSpec((tm, BW), lambda i: (i, 0))
    return pl.pallas_call(
        _rw_prep_body,
        grid=(T // tm,),
        in_specs=[pl.BlockSpec((tm, W), lambda i: (i, 0)), pl.BlockSpec((tm, W), lambda i: (i, 0)),
                  par(mu), par(w0), par(a0), par(kkp), par(kap), par(rkp),
                  par(w2p), par(a2p), par(g2), par(seg)],
        out_specs=[out] * 8,
        out_shape=[jax.ShapeDtypeStruct((T, BW), F32)] * 8,
        compiler_params=_cparams(("parallel",)),
        name="rwkv_prep",
    )(z, prev, mu, w0, a0, kkp, kap, rkp, w2p, a2p, g2, seg)


def _rw_rec_body(r_ref, w_ref, k_ref, v_ref, nkk_ref, kka_ref, g_ref, s0_ref,
                 ot_ref, sout_ref, s_ref, *, TT, L, NS, n_tiles):
    ti = pl.program_id(0)
    npair = NH // 2
    R = NS * npair * HD
    gm = g_ref[...]
    lane = lax.broadcasted_iota(jnp.int32, (R, LANES), 1)
    rowi = lax.broadcasted_iota(jnp.int32, (R, LANES), 0)
    eye2 = (lane % HD) == (rowi % HD)
    seq_lane0 = (rowi // (npair * HD)) * L
    refs =(r_ref, w_ref, k_ref, v_ref, nkk_ref, kka_ref)

    def token_group(gi, carry):
        s, o_lo, o_hi = carry
        t0 = pl.multiple_of(gi * SUBLANES, SUBLANES)
        blk = [[ref[pl.ds(q * L + t0, SUBLANES), :] for q in range(NS)] for ref in refs]
        for j in range(SUBLANES):
            def stacked(x):
                return jnp.concatenate(
                    [jnp.broadcast_to(x[q][j:j + 1, p * LANES:(p + 1) * LANES], (HD, LANES))
                     for q in range(NS) for p in range(npair)], axis=0)
            rr, ww, kk, vv, nk, ka = [stacked(x) for x in blk]
            sa = _split_dot(s * nk, gm)
            vb = jnp.dot(jnp.where(eye2, vv, 0.0).astype(BF16), gm, preferred_element_type=F32)
            s = s * ww + sa * ka + vb * kk
            oo = jnp.dot((s * rr).astype(BF16), gm, preferred_element_type=F32)
            hit = lane == seq_lane0 + t0 + j
            o_lo = jnp.where(hit, oo, o_lo)
            o_hi = jnp.where(hit, pltpu.roll(oo, HD, axis=1), o_hi)
        return s, o_lo, o_hi

    if NS == 1:
        @pl.when(ti == 0)
        def _():
            s_ref[...] = s0_ref[0]
        s_in = s_ref[...]
    else:
        s_in = s0_ref[...].reshape(R, LANES)
    zero = jnp.zeros((R, LANES), F32)
    s_out, o_lo, o_hi = lax.fori_loop(0, min(L, TT) // SUBLANES, token_group, (s_in, zero, zero))

    pr = npair * HD
    a1 = sum(o_lo[q * pr:(q + 1) * pr] for q in range(NS))
    a2 = sum(o_hi[q * pr:(q + 1) * pr] for q in range(NS))
    first_half = lax.broadcasted_iota(jnp.int32, (pr, LANES), 1) < HD
    lo = jnp.where(first_half, a1, a2)
    hi = jnp.where(first_half, a2, a1)
    for p in range(npair):
        ot_ref[2 * p * HD:(2 * p + 1) * HD, :] = lo[p * HD:(p + 1) * HD]
        ot_ref[(2 * p + 1) * HD:(2 * p + 2) * HD, :] = hi[p * HD:(p + 1) * HD]
    if NS == 1:
        s_ref[...] = s_out

        @pl.when(ti == n_tiles - 1)
        def _():
            sout_ref[0] = s_out
    else:
        sout_ref[...] = s_out.reshape(NS, pr, LANES)


def _rw_rec(r, w, k, v, nkk, kka, gmat, s0, row_off, n_seq, L):
    TT = LANES
    n_tok = n_seq * L
    n_tiles = n_tok // TT
    rb = row_off // TT
    ns = max(1, TT // L)
    pr = (NH // 2) * HD
    tok = pl.BlockSpec((TT, BW), lambda i: (rb + i, 0))
    if ns == 1:
        sspec = pl.BlockSpec((1, pr, LANES), lambda i: (0, 0, 0))
    else:
        sspec = pl.BlockSpec((ns, pr, LANES), lambda i: (i, 0, 0))
    return pl.pallas_call(
        functools.partial(_rw_rec_body, TT=TT, L=L, NS=ns, n_tiles=n_tiles),
        grid=(n_tiles,),
        in_specs=[tok] * 6 + [pl.BlockSpec((LANES, LANES), lambda i: (0, 0)), sspec],
        out_specs=[pl.BlockSpec((BW, TT), lambda i: (0, i)), sspec],
        out_shape=[jax.ShapeDtypeStruct((BW, n_tok), F32),
                   jax.ShapeDtypeStruct(s0.shape, F32)],
        scratch_shapes=[pltpu.VMEM((pr, LANES), F32)],
        compiler_params=_cparams(("arbitrary",)),
        name="rwkv_recurrence",
    )(r, w, k, v, nkk, kka, gmat, s0)


def _rw_post_body(o_ref, bonus_ref, gate_ref, lnw_ref, lnb_ref, seg_ref, out_ref):
    o = o_ref[...]
    mu = _split_dot(o, seg_ref[...]) * (1.0 / HD)
    d = o - mu
    var = _split_dot(d * d, seg_ref[...]) * (1.0 / HD)
    on = d * lax.rsqrt(var + RW_LN_EPS) * lnw_ref[...] + lnb_ref[...]
    out_ref[...] = (on + bonus_ref[...]) * gate_ref[...]


def _rw_post(o, bonus, gate, lnw, lnb, seg):
    T = o.shape[0]
    tm = _pick(T, (512, 256, 128))
    blk = pl.BlockSpec((tm, BW), lambda i: (i, 0))
    par = lambda a: pl.BlockSpec(a.shape, lambda i: (0,) * a.ndim)
    return pl.pallas_call(
        _rw_post_body,
        grid=(T // tm,),
        in_specs=[blk, blk, blk, par(lnw), par(lnb), par(seg)],
        out_specs=blk,
        out_shape=jax.ShapeDtypeStruct((T, BW), F32),
        compiler_params=_cparams(("parallel",)),
        name="rwkv_post",
    )(o, bonus, gate, lnw, lnb, seg)


def _heads_alt(w):
    d = w.shape[0]
    w4 = w.reshape(d, NH // 2, 2, HD)
    zero = jnp.zeros((d, NH // 2, HD), w.dtype)
    ev = jnp.concatenate([w4[:, :, 0], zero], -1)
    od = jnp.concatenate([zero, w4[:, :, 1]], -1)
    return jnp.stack([ev, od], 2).reshape(d, NH * LANES)


def _heads_lo(w, nh):
    d = w.shape[0]
    w3 = w.reshape(d, nh, HD)
    return jnp.concatenate([w3, jnp.zeros_like(w3)], -1).reshape(d, nh * LANES)


def _padc(w, n):
    return jnp.pad(w, ((0, 0), (0, n - w.shape[1])))


def _rw_reorder(a):
    o = 0
    parts = {}
    for name, wd in (("rr", BW), ("wl", RW_LORA), ("rk", BW), ("rv", BW), ("al", RW_LORA), ("gl", RW_GATE_LORA)):
        parts[name] = a[..., o:o + wd]
        o += wd
    return jnp.concatenate([parts[n] for n in ("rr", "rk", "rv", "wl", "al", "gl")], axis=-1)


def _rw_unorder(a):
    rr, rk, rv = a[..., 0:BW], a[..., BW:2 * BW], a[..., 2 * BW:3 * BW]
    wl = a[..., 3 * BW:3 * BW + RW_LORA]
    al = a[..., 3 * BW + RW_LORA:3 * BW + 2 * RW_LORA]
    gl = a[..., 3 * BW + 2 * RW_LORA:RW_COLS]
    return jnp.concatenate([rr, wl, rk, rv, al, gl], axis=-1)


def _relayout_w_in(w, d_model):
    w = w.astype(BF16)
    o = 0

    def take(n):
        nonlocal o
        s = w[:, o:o + n]
        o += n
        return s
    sq, sk, sv = take(BW), take(BW), take(BW)
    dq, dk, dv, qi, ki, wi = take(BW), take(BW), take(BW), take(BW), take(HD), take(NH)
    gq, gk = take(GLA_H * HD), take(GLA_H * HD)
    gv, gr, ga = take(BW), take(BW), take(GLA_LOWRANK)
    rw = take(RW_COLS)
    gate = take(4 * d_model)
    cols = [_padc(_rw_reorder(rw), 4 * BW), _heads_alt(sq), sk, sv, _heads_alt(dq), dk, dv,
            _heads_lo(qi, NH), _heads_lo(gq, GLA_H), _heads_lo(gk, GLA_H), gv, gr,
            _padc(ki, LANES), _padc(wi, LANES), _padc(ga, 2 * LANES), gate]
    return jnp.concatenate(cols, axis=1)


def _relayout_ffn(w_up, w_down, tf):
    d, f2 = w_up.shape
    f = f2 // 2
    fp = -(-f // tf) * tf
    g = _padc(w_up[:, :f].astype(BF16), fp).reshape(d, fp // tf, tf)
    u = _padc(w_up[:, f:].astype(BF16), fp).reshape(d, fp // tf, tf)
    wup = jnp.concatenate([g, u], axis=2).reshape(d, 2 * fp)
    wdn = jnp.pad(w_down.astype(BF16), ((0, fp - f), (0, 0)))
    return wup, wdn


def _gain_alt(g):
    return jnp.tile(g, 2 * NH)[None, :]


def kernel(x_prompt, x_sample, cache_sb_k, cache_sb_v, cache_dsa_k, cache_dsa_v, cache_dsa_kidx, state_gla, state_rwkv, state_rwkv_shift, ffn1_norm, ffn1_up, ffn1_down, mix_norm, w_in, sb_qnorm, sb_knorm, dsa_qnorm, dsa_knorm, gla_a2, gla_ab, gla_onorm, rwkv_mu, rwkv_w0, rwkv_w2, rwkv_a0, rwkv_a2, rwkv_g2, rwkv_kk, rwkv_ka, rwkv_rk, rwkv_lnw, rwkv_lnb, w_branch, w_out, ffn2_norm, ffn2_up, ffn2_down):
    depth = w_in.shape[0]
    bp, lp, d_model = x_prompt.shape
    nb, ls, _ = x_sample.shape
    assert bp == 1, "prompt group is one sequence"
    tp = bp * lp
    ts = nb * ls
    past = cache_sb_k.shape[2]
    tf = 512

    x = jnp.concatenate([x_prompt.reshape(tp, d_model), x_sample.reshape(ts, d_model)], axis=0)
    ck_sb = cache_sb_k.reshape(depth, nb, past, BW)
    cv_sb = cache_sb_v.reshape(depth, nb, past, BW)
    ck_dsa = cache_dsa_k.reshape(depth, nb, past, BW)
    cv_dsa = cache_dsa_v.reshape(depth, nb, past, BW)

    bd = jnp.asarray(np.kron(np.eye(2 * NH), np.full((HD, HD), 1.0 / HD)), BF16)
    seg = jnp.asarray(np.kron(np.eye(NH), np.ones((HD, HD))), BF16)
    gmat = jnp.asarray(np.kron(np.eye(2), np.ones((HD, HD))), BF16)
    topk_p = min(TOPK_MAX, lp // 4)
    topk_s = min(TOPK_MAX, (past + ls) // 4)

    new_p = [[] for _ in range(8)]
    new_s = [[] for _ in range(8)]
    for i in range(depth):
        wup1, wdn1 = _relayout_ffn(ffn1_up[i], ffn1_down[i], tf)
        wup2, wdn2 = _relayout_ffn(ffn2_up[i], ffn2_down[i], tf)
        win = _relayout_w_in(w_in[i], d_model)

        x = _ffn(x, ffn1_norm[i][None], wup1, wdn1, tf)
        z = _proj(x, mix_norm[i][None], win)

        sqn, skn, dqn, dkn = _qknorm(z, _gain_alt(sb_qnorm[i]), jnp.tile(sb_knorm[i], NH)[None],
                                     _gain_alt(dsa_qnorm[i]), jnp.tile(dsa_knorm[i], NH)[None], bd)
        o_sb = jnp.concatenate([
            _sb_prompt(sqn, skn, z, tp),
            _sb_sample(sqn, skn, z, ck_sb, cv_sb, i, tp, nb, ls)], axis=0)
        mask_p = _dsa_sel_prompt(z, tp, topk_p)
        mask_s = _dsa_sel_sample(z, cache_dsa_kidx, i, tp, nb, ls, topk_s)
        o_dsa = jnp.concatenate([
            _dsa_attn_prompt(dqn, dkn, z, mask_p, tp),
            _dsa_attn_sample(dqn, dkn, z, mask_s, ck_dsa, cv_dsa, i, tp, nb, ls)], axis=0)

        a2p = jnp.pad(_heads_lo(gla_a2[i], GLA_H), ((0, LANES - GLA_LOWRANK), (0, 0))).astype(BF16)
        abp = _heads_lo(gla_ab[i][None], GLA_H)
        onorm = gla_onorm[i][None]

        def gla_state_in(s):
            return jnp.pad(jnp.swapaxes(s, 2, 3), ((0, 0), (0, 0), (0, 0), (0, LANES - HD)))

        def gla_state_out(s):
            return jnp.swapaxes(s[..., :HD], 2, 3)
        og_p, sg_p = _gla(z, a2p, abp, onorm, jnp.zeros((bp, GLA_H, GLA_DV, LANES), F32), 0, bp, lp)
        og_s, sg_s = _gla(z, a2p, abp, onorm, gla_state_in(state_gla[i]), tp, nb, ls)
        o_gla = jnp.concatenate([og_p, og_s], axis=0)

        zrw = z[:, :RW_COLS]
        shift_s = _rw_reorder(state_rwkv_shift[i])
        prev_p = jnp.concatenate([jnp.zeros((1, RW_COLS), F32), zrw[:tp - 1]], axis=0)
        prev_s = jnp.concatenate([shift_s, zrw[tp:].reshape(nb, ls, RW_COLS)[:, :-1]], axis=1)
        prev = _padc(jnp.concatenate([prev_p, prev_s.reshape(ts, RW_COLS)], axis=0), 4 * BW)
        mu = _padc(_rw_reorder(rwkv_mu[i])[None], 4 * BW)
        w2p = jnp.concatenate([rwkv_w2[i], jnp.zeros_like(rwkv_a2[i])], axis=0).astype(BF16)
        a2q = jnp.concatenate([jnp.zeros_like(rwkv_w2[i]), rwkv_a2[i]], axis=0).astype(BF16)
        r_, w_, k_, v_, nkk_, kka_, gate_, bonus_ = _rw_prep(
            z, prev, mu, rwkv_w0[i][None], rwkv_a0[i][None], rwkv_kk[i][None], rwkv_ka[i][None],
            rwkv_rk[i][None], w2p, a2q, rwkv_g2[i].astype(BF16), seg)

        def rw_state_in(s):
            n = s.shape[0]
            return s.reshape(n, NH // 2, 2, HD, HD).transpose(0, 1, 3, 2, 4).reshape(n, NH // 2 * HD, LANES)

        def rw_state_out(s):
            n = s.shape[0]
            return s.reshape(n, NH // 2, HD, 2, HD).transpose(0, 1, 3, 2, 4).reshape(n, NH, HD, HD)
        ot_p, sr_p = _rw_rec(r_, w_, k_, v_, nkk_, kka_, gmat, jnp.zeros((bp, NH // 2 * HD, LANES), F32), 0, bp, lp)
        ot_s, sr_s = _rw_rec(r_, w_, k_, v_, nkk_, kka_, gmat, rw_state_in(state_rwkv[i]), tp, nb, ls)
        o_r = jnp.concatenate([ot_p.T, ot_s.T], axis=0)
        o_rw = _rw_post(o_r, bonus_, gate_, rwkv_lnw[i][None], rwkv_lnb[i][None], seg)

        x = _merge(x, z, o_sb, o_dsa, o_gla, o_rw, w_branch[i].astype(BF16), w_out[i].astype(BF16))
        x = _ffn(x, ffn2_norm[i][None], wup2, wdn2, tf)

        sv = z[:, OFF_SV:OFF_SV + BW]
        dv = z[:, OFF_DV:OFF_DV + BW]
        ki = z[:, OFF_KI:OFF_KI + HD]
        for dst, (lo, n, l) in ((new_p, (0, bp, lp)), (new_s, (tp, nb, ls))):
            hi = lo + n * l
            dst[0].append(skn[lo:hi].reshape(n, l, NH, HD))
            dst[1].append(sv[lo:hi].reshape(n, l, NH, HD))
            dst[2].append(dkn[lo:hi].reshape(n, l, NH, HD))
            dst[3].append(dv[lo:hi].reshape(n, l, NH, HD))
            dst[4].append(ki[lo:hi].reshape(n, l, HD))
            dst[7].append(_rw_unorder(zrw[lo:hi].reshape(n, l, RW_COLS)[:, -1:]))
        new_p[5].append(gla_state_out(sg_p))
        new_s[5].append(gla_state_out(sg_s))
        new_p[6].append(rw_state_out(sr_p))
        new_s[6].append(rw_state_out(sr_s))

    outs_p = [jnp.stack(l, axis=0) for l in new_p]
    outs_s = [jnp.stack(l, axis=0) for l in new_s]
    y_p = x[:tp].reshape(bp, lp, d_model)
    y_s = x[tp:].reshape(nb, ls, d_model)
    return (y_p, y_s, *outs_p, *outs_s)
```

```python
import functools
import math

import jax
import jax.numpy as jnp
import numpy as np
from jax import lax
from jax.experimental import pallas as pl
from jax.experimental.pallas import tpu as pltpu

F32 = jnp.float32
BF16 = jnp.bfloat16

LANES = 128
SUBLANES = 8
HD = 64
NH = 8
BW = NH * HD
GLA_H = 4
GLA_DV = 128
GLA_LOWRANK = 16
GLA_GATE_NORM = 16.0
CHUNK = 64
TOPK_MAX = 256
RW_LORA = 64
RW_GATE_LORA = 128
RW_COLS = 3 * BW + 2 * RW_LORA + RW_GATE_LORA
NORM_EPS = 1e-6
RW_LN_EPS = 64e-5
NEG_BIG = -1e30
SB_UNDERFLOW = -104.0
VMEM_LIMIT = 56 * 1024 * 1024

OFF_RW = 0
OFF_SQ = 2048
OFF_SK = 3072
OFF_SV = 3584
OFF_DQ = 4096
OFF_DK = 5120
OFF_DV = 5632
OFF_QI = 6144
OFF_GQ = 7168
OFF_GK = 7680
OFF_GV = 8192
OFF_GR = 8704
OFF_KI = 9216
OFF_WI = 9344
OFF_GA = 9472
OFF_GATE = 9728


def _cparams(sem):
    return pltpu.CompilerParams(dimension_semantics=sem, vmem_limit_bytes=VMEM_LIMIT)


def _pick(n, prefs):
    for p in prefs:
        if n % p == 0:
            return p
    raise ValueError(f"no tile for {n} in {prefs}")


def _split_dot(a, b):
    hi = a.astype(BF16)
    lo = (a - hi.astype(F32)).astype(BF16)
    return (jnp.dot(hi, b, preferred_element_type=F32)
            + jnp.dot(lo, b, preferred_element_type=F32))


def _split_dot_left(a, b):
    hi = b.astype(BF16)
    lo = (b - hi.astype(F32)).astype(BF16)
    return (jnp.dot(a, hi, preferred_element_type=F32)
            + jnp.dot(a, lo, preferred_element_type=F32))


def _dot_nt(a, b):
    return lax.dot_general(a, b, (((1,), (1,)), ((), ())), preferred_element_type=F32)


def _dot_tn(a, b):
    return lax.dot_general(a, b, (((0,), (0,)), ((), ())), preferred_element_type=F32)


def _neg_softplus(z):
    return -(jnp.maximum(z, 0.0) + jnp.log(1.0 + jnp.exp(-jnp.abs(z))))


def _ffn_body(x_ref, g_ref, wup_ref, wdn_ref, o_ref, xn_ref, *, tf):
    @pl.when(pl.program_id(1) == 0)
    def _():
        x = x_ref[...]
        ms = jnp.mean(x * x, axis=-1, keepdims=True)
        xn_ref[...] = (x * lax.rsqrt(ms + NORM_EPS) * g_ref[...]).astype(BF16)
        o_ref[...] = x

    gu = jnp.dot(xn_ref[...], wup_ref[...], preferred_element_type=F32)
    gate = gu[:, :tf]
    act = (gate * jax.nn.sigmoid(gate) * gu[:, tf:]).astype(BF16)
    o_ref[...] += 0.5 * jnp.dot(act, wdn_ref[...], preferred_element_type=F32)


def _ffn(x, g, wup, wdn, tf):
    T, D = x.shape
    fp = wdn.shape[0]
    tm = _pick(T, (512, 256, 128))
    return pl.pallas_call(
        functools.partial(_ffn_body, tf=tf),
        grid=(T // tm, fp // tf),
        in_specs=[pl.BlockSpec((tm, D), lambda i, j: (i, 0)),
                  pl.BlockSpec((1, D), lambda i, j: (0, 0)),
                  pl.BlockSpec((D, 2 * tf), lambda i, j: (0, j)),
                  pl.BlockSpec((tf, D), lambda i, j: (j, 0))],
        out_specs=pl.BlockSpec((tm, D), lambda i, j: (i, 0)),
        out_shape=jax.ShapeDtypeStruct((T, D), F32),
        scratch_shapes=[pltpu.VMEM((tm, D), BF16)],
        compiler_params=_cparams(("parallel", "arbitrary")),
        name="ffn",
    )(x, g, wup, wdn)


def _proj_body(x_ref, g_ref, w_ref, o_ref, xn_ref):
    @pl.when(pl.program_id(1) == 0)
    def _():
        x = x_ref[...]
        ms = jnp.mean(x * x, axis=-1, keepdims=True)
        xn_ref[...] = (x * lax.rsqrt(ms + NORM_EPS) * g_ref[...]).astype(BF16)

    o_ref[...] = jnp.dot(xn_ref[...], w_ref[...], preferred_element_type=F32)


def _proj(x, g, w):
    T, D = x.shape
    n = w.shape[1]
    tm = _pick(T, (1024, 512, 256, 128))
    tn = 512
    return pl.pallas_call(
        _proj_body,
        grid=(T // tm, n // tn),
        in_specs=[pl.BlockSpec((tm, D), lambda i, j: (i, 0)),
                  pl.BlockSpec((1, D), lambda i, j: (0, 0)),
                  pl.BlockSpec((D, tn), lambda i, j: (0, j))],
        out_specs=pl.BlockSpec((tm, tn), lambda i, j: (i, j)),
        out_shape=jax.ShapeDtypeStruct((T, n), F32),
        scratch_shapes=[pltpu.VMEM((tm, D), BF16)],
        compiler_params=_cparams(("parallel", "arbitrary")),
        name="in_proj",
    )(x, g, w)


def _merge_body(osb_ref, ods_ref, ogl_ref, orw_ref, g0_ref, g1_ref, g2_ref, g3_ref,
                wb_ref, wo_ref, x_ref, o_ref):
    @pl.when(pl.program_id(1) == 0)
    def _():
        o_ref[...] = x_ref[...]

    acc = None
    for n, (b_ref, g_ref) in enumerate(((osb_ref, g0_ref), (ods_ref, g1_ref),
                                        (ogl_ref, g2_ref), (orw_ref, g3_ref))):
        p = jnp.dot(b_ref[...].astype(BF16), wb_ref[n], preferred_element_type=F32)
        t = jax.nn.sigmoid(g_ref[...]) * p
        acc = t if acc is None else acc + t
    o_ref[...] += jnp.dot(acc.astype(BF16), wo_ref[...], preferred_element_type=F32)


def _merge(x, z, o_sb, o_dsa, o_gla, o_rw, wb, wo):
    T, D = x.shape
    tm = _pick(T, (512, 256, 128))
    tc = 512
    nc = D // tc
    gate_blk = OFF_GATE // tc
    bspec = pl.BlockSpec((tm, BW), lambda i, c: (i, 0))
    gspecs = [pl.BlockSpec((tm, tc), functools.partial(lambda i, c, n: (i, gate_blk + n * nc + c), n=n))
              for n in range(4)]
    return pl.pallas_call(
        _merge_body,
        grid=(T // tm, nc),
        in_specs=[bspec, bspec, bspec, bspec] + gspecs + [
            pl.BlockSpec((4, BW, tc), lambda i, c: (0, 0, c)),
            pl.BlockSpec((tc, D), lambda i, c: (c, 0)),
            pl.BlockSpec((tm, D), lambda i, c: (i, 0))],
        out_specs=pl.BlockSpec((tm, D), lambda i, c: (i, 0)),
        out_shape=jax.ShapeDtypeStruct((T, D), F32),
        compiler_params=_cparams(("parallel", "arbitrary")),
        name="merge",
    )(o_sb, o_dsa, o_gla, o_rw, z, z, z, z, wb, wo, x)


def _qknorm_body(sq_ref, sk_ref, dq_ref, dk_ref, gsq_ref, gsk_ref, gdq_ref, gdk_ref, bd_ref,
                 osq_ref, osk_ref, odq_ref, odk_ref):
    def norm(x, g, n):
        ms = _split_dot(x * x, bd_ref[:n, :n])
        return x * lax.rsqrt(ms + NORM_EPS) * g

    scale = HD ** -0.5
    osq_ref[...] = (norm(sq_ref[...], gsq_ref[...], 2 * BW) * scale).astype(BF16)
    osk_ref[...] = norm(sk_ref[...], gsk_ref[...], BW)
    odq_ref[...] = (norm(dq_ref[...], gdq_ref[...], 2 * BW) * scale).astype(BF16)
    odk_ref[...] = norm(dk_ref[...], gdk_ref[...], BW)


def _qknorm(z, gsq, gsk, gdq, gdk, bd):
    T = z.shape[0]
    tm = _pick(T, (512, 256, 128))
    wq, wk = 2 * BW, BW
    row = lambda w, off: pl.BlockSpec((tm, w), lambda i: (i, off // w))
    par = lambda w: pl.BlockSpec((1, w), lambda i: (0, 0))
    out = lambda w: pl.BlockSpec((tm, w), lambda i: (i, 0))
    return pl.pallas_call(
        _qknorm_body,
        grid=(T // tm,),
        in_specs=[row(wq, OFF_SQ), row(wk, OFF_SK), row(wq, OFF_DQ), row(wk, OFF_DK),
                  par(wq), par(wk), par(wq), par(wk),
                  pl.BlockSpec((wq, wq), lambda i: (0, 0))],
        out_specs=[out(wq), out(wk), out(wq), out(wk)],
        out_shape=[jax.ShapeDtypeStruct((T, wq), BF16), jax.ShapeDtypeStruct((T, wk), F32),
                   jax.ShapeDtypeStruct((T, wq), BF16), jax.ShapeDtypeStruct((T, wk), F32)],
        compiler_params=_cparams(("parallel",)),
        name="qk_norm",
    )(z, z, z, z, gsq, gsk, gdq, gdk, bd)


def _sb_body(*refs, tq, n_cache, tkc):
    if n_cache:
        q_ref, kn_ref, vn_ref, kc_ref, vc_ref, o_ref = refs
    else:
        q_ref, kn_ref, vn_ref, o_ref = refs
    qb = pl.program_id(1) if not n_cache else 0
    qs = [q_ref[:, e * LANES:(e + 1) * LANES] for e in range(2)]

    def tri(n):
        r = lax.broadcasted_iota(jnp.int32, (n, n), 0)
        c = lax.broadcasted_iota(jnp.int32, (n, n), 1)
        return (r > c).astype(BF16)

    def visit(state, kblk, vblk, u, valid):
        kb16 = kblk.astype(BF16)
        vb16 = vblk.astype(BF16)
        new = []
        for e in range(2):
            carry, acc = state[e]
            z = _dot_nt(qs[e], kb16)
            lg = _neg_softplus(z)
            if valid is not None:
                lg = jnp.where(valid, lg, 0.0)
            inner = _split_dot(lg, u)
            a = jnp.exp(z + lg + inner + carry)
            if valid is not None:
                a = jnp.where(valid, a, 0.0)
            acc = acc + jnp.dot(a.astype(BF16), vb16, preferred_element_type=F32)
            carry = carry + inner[:, :1] + lg[:, :1]
            new.append((carry, acc))
        return tuple(new)

    zero = (jnp.zeros((tq, 1), F32), jnp.zeros((tq, LANES), F32))
    state = (zero, zero)

    r = lax.broadcasted_iota(jnp.int32, (tq, tq), 0)
    c = lax.broadcasted_iota(jnp.int32, (tq, tq), 1)
    u_new = tri(tq)
    if n_cache:
        state = visit(state, kn_ref[...], vn_ref[...], u_new, c < r)
    else:
        row0 = pl.multiple_of(qb * tq, tq)
        state = visit(state, kn_ref[pl.ds(row0, tq), :], vn_ref[pl.ds(row0, tq), :], u_new, c < r)

    def live(c):
        i, st = c
        return (i < n_old) & (jnp.max(jnp.maximum(st[0][0], st[1][0])) > SB_UNDERFLOW)

    if not n_cache:
        n_old = qb

        def step(c):
            i, st = c
            r0 = pl.multiple_of((qb - 1 - i) * tq, tq)
            return i + 1, visit(st, kn_ref[pl.ds(r0, tq), :], vn_ref[pl.ds(r0, tq), :], u_new, None)
    else:
        n_old = n_cache
        u_c = tri(tkc)

        def step(c):
            i, st = c
            r0 = pl.multiple_of((n_cache - 1 - i) * tkc, tkc)
            return i + 1, visit(st, kc_ref[0, 0, pl.ds(r0, tkc), :], vc_ref[0, 0, pl.ds(r0, tkc), :],
                                u_c, None)
    _, state = lax.while_loop(live, step, (jnp.int32(0), state))

    lane = lax.broadcasted_iota(jnp.int32, (tq, LANES), 1)
    o_ref[...] = jnp.where(lane < HD, state[0][1], state[1][1])


def _sb_prompt(qn, kn, z, n_rows):
    tq = 128
    vblk = OFF_SV // LANES
    return pl.pallas_call(
        functools.partial(_sb_body, tq=tq, n_cache=0, tkc=0),
        grid=(NH // 2, n_rows // tq),
        in_specs=[pl.BlockSpec((tq, 2 * LANES), lambda p, i: (i, p)),
                  pl.BlockSpec((n_rows, LANES), lambda p, i: (0, p)),
                  pl.BlockSpec((n_rows, LANES), lambda p, i: (0, vblk + p))],
        out_specs=pl.BlockSpec((tq, LANES), lambda p, i: (i, p)),
        out_shape=jax.ShapeDtypeStruct((n_rows, BW), F32),
        compiler_params=_cparams(("parallel", "arbitrary")),
        name="sb_prompt",
    )(qn, kn, z)


def _sb_sample(qn, kn, z, cache_k, cache_v, layer, row_off, nb, L):
    P = cache_k.shape[2]
    tkc = _pick(P, (256, 128))
    vblk = OFF_SV // LANES
    rb = row_off // L
    return pl.pallas_call(
        functools.partial(_sb_body, tq=L, n_cache=P // tkc, tkc=tkc),
        grid=(nb, NH // 2),
        in_specs=[pl.BlockSpec((L, 2 * LANES), lambda b, p: (rb + b, p)),
                  pl.BlockSpec((L, LANES), lambda b, p: (rb + b, p)),
                  pl.BlockSpec((L, LANES), lambda b, p: (rb + b, vblk + p)),
                  pl.BlockSpec((1, 1, P, LANES), lambda b, p: (layer, b, 0, p)),
                  pl.BlockSpec((1, 1, P, LANES), lambda b, p: (layer, b, 0, p))],
        out_specs=pl.BlockSpec((L, LANES), lambda b, p: (b, p)),
        out_shape=jax.ShapeDtypeStruct((nb * L, BW), F32),
        compiler_params=_cparams(("parallel", "parallel")),
        name="sb_sample",
    )(qn, kn, z, cache_k, cache_v)


def _sortable(x):
    b = pltpu.bitcast(x + 0.0, jnp.int32)
    return jnp.where(b < 0, b ^ jnp.int32(0x7FFFFFFF), b)


def _select_topk(key_ref, n_blk, tkb, topk, rows):
    def count(pred):
        def body(i, acc):
            c0 = pl.multiple_of(i * tkb, tkb)
            for c in range(tkb // LANES):
                blk = key_ref[:, pl.ds(c0 + c * LANES, LANES)]
                col = c0 + c * LANES + lax.broadcasted_iota(jnp.int32, (rows, LANES), 1)
                acc = acc + jnp.where(pred(blk, col), 1.0, 0.0)
            return acc
        acc = lax.fori_loop(0, n_blk, body, jnp.zeros((rows, LANES), F32))
        return jnp.sum(acc, axis=1, keepdims=True)

    def unresolved(c):
        i, _, n_ge = c
        return (i < 32) & (jnp.max(jnp.abs(n_ge - topk)) > 0)

    def bit_step(c):
        i, thr, n_ge = c
        cand = thr + (jnp.int32(1) << (31 - i))
        n_cand = count(lambda blk, col: blk >= cand)
        take = n_cand >= topk
        return i + 1, jnp.where(take, cand, thr), jnp.where(take, n_cand, n_ge)

    thr0 = jnp.full((rows, 1), jnp.iinfo(jnp.int32).min, jnp.int32)
    n_all = jnp.full((rows, 1), 1.0, F32) * jnp.asarray(n_blk * tkb).astype(F32)
    _, thr, n_ge = lax.while_loop(unresolved, bit_step, (jnp.int32(0), thr0, n_all))
    n_col_bits = max(1, int(math.ceil(math.log2(key_ref.shape[1] + 1))))

    def tie_search():
        need = topk - count(lambda blk, col: blk > thr)

        def col_step(i, j):
            cand = j + (jnp.int32(1) << (n_col_bits - 1 - i))
            n = count(lambda blk, col: (blk == thr) & (col < cand))
            return jnp.where(n < need, cand, j)
        return lax.fori_loop(0, n_col_bits, col_step, jnp.zeros((rows, 1), jnp.int32))

    any_tie = jnp.max(jnp.abs(n_ge - topk)) > 0
    j_hi = lax.cond(any_tie, tie_search,
                    lambda: jnp.full((rows, 1), jnp.iinfo(jnp.int32).max, jnp.int32))
    return thr, j_hi


def _idx_scores(qi, wi, kblk16, k_real):
    acc = None
    for h in range(NH):
        qh = qi[:, h * LANES:h * LANES + k_real].astype(BF16)
        d = jnp.maximum(_dot_nt(qh, kblk16), 0.0) * (HD ** -0.5)
        t = (wi[:, h:h + 1] * (NH ** -0.5)) * d
        acc = t if acc is None else acc + t
    return acc


def _dsa_sel_prompt_body(qi_ref, wi_ref, ki_ref, m_ref, key_ref, *, tq, tkb, topk, n_keys):
    qb = pl.program_id(0)
    n_blk = ((qb + 1) * tq + tkb - 1) // tkb
    qi = qi_ref[...]
    wi = wi_ref[...]
    q_chunk = (qb * tq + lax.broadcasted_iota(jnp.int32, (tq, tkb), 0)) // CHUNK

    def score_blk(i, _):
        c0 = pl.multiple_of(i * tkb, tkb)
        s = _idx_scores(qi, wi, ki_ref[pl.ds(c0, tkb), :].astype(BF16), LANES)
        k_chunk = (c0 + lax.broadcasted_iota(jnp.int32, (tq, tkb), 1)) // CHUNK
        s = jnp.where(k_chunk <= q_chunk, s, -jnp.inf)
        key_ref[:, pl.ds(c0, tkb)] = _sortable(s)
        return 0
    lax.fori_loop(0, n_blk, score_blk, 0)

    thr, j_hi = _select_topk(key_ref, n_blk, tkb, topk, tq)
    neg_inf_key = _sortable(jnp.full((1, 1), -jnp.inf, F32))

    def write_blk(i, _):
        c0 = pl.multiple_of(i * tkb, tkb)
        blk = key_ref[:, pl.ds(c0, tkb)]
        col = c0 + lax.broadcasted_iota(jnp.int32, (tq, tkb), 1)
        sel = ((blk > thr) | ((blk == thr) & (col <= j_hi))) & (blk > neg_inf_key)
        m_ref[:, pl.ds(c0, tkb)] = jnp.where(sel, 1.0, 0.0).astype(BF16)
        return 0
    lax.fori_loop(0, n_blk, write_blk, 0)

    def zero_blk(i, _):
        m_ref[:, pl.ds(pl.multiple_of(i * tkb, tkb), tkb)] = jnp.zeros((tq, tkb), BF16)
        return 0
    lax.fori_loop(n_blk, n_keys // tkb, zero_blk, 0)


def _dsa_sel_prompt(z, n_rows, topk):
    tq, tkb = 128, 512
    return pl.pallas_call(
        functools.partial(_dsa_sel_prompt_body, tq=tq, tkb=tkb, topk=topk, n_keys=n_rows),
        grid=(n_rows // tq,),
        in_specs=[pl.BlockSpec((tq, NH * LANES), lambda i: (i, OFF_QI // (NH * LANES))),
                  pl.BlockSpec((tq, LANES), lambda i: (i, OFF_WI // LANES)),
                  pl.BlockSpec((n_rows, LANES), lambda i: (0, OFF_KI // LANES))],
        out_specs=pl.BlockSpec((tq, n_rows), lambda i: (i, 0)),
        out_shape=jax.ShapeDtypeStruct((n_rows, n_rows), BF16),
        scratch_shapes=[pltpu.VMEM((tq, n_rows), jnp.int32)],
        compiler_params=_cparams(("parallel",)),
        name="dsa_select_prompt",
    )(z, z, z)


def _dsa_sel_sample_body(qi_ref, wi_ref, kin_ref, kic_ref, m_ref, key_ref, *, L, P, tkb, topk):
    qi = qi_ref[...]
    wi = wi_ref[...]
    n_c = P // tkb

    def score_blk(i, _):
        c0 = pl.multiple_of(i * tkb, tkb)
        s = _idx_scores(qi, wi, kic_ref[0, 0, pl.ds(c0, tkb), :].astype(BF16), HD)
        key_ref[:, pl.ds(c0, tkb)] = _sortable(s)
        return 0
    lax.fori_loop(0, n_c, score_blk, 0)

    knew = jnp.concatenate([kin_ref[...], jnp.zeros((tkb - L, LANES), F32)], axis=0).astype(BF16)
    s = _idx_scores(qi, wi, knew, LANES)
    col = lax.broadcasted_iota(jnp.int32, (L, tkb), 1)
    key_ref[:, pl.ds(P, tkb)] = _sortable(jnp.where(col < L, s, -jnp.inf))

    thr, j_hi = _select_topk(key_ref, n_c + 1, tkb, topk, L)
    neg_inf_key = _sortable(jnp.full((1, 1), -jnp.inf, F32))

    def write_blk(i, _):
        c0 = pl.multiple_of(i * tkb, tkb)
        blk = key_ref[:, pl.ds(c0, tkb)]
        cc = c0 + lax.broadcasted_iota(jnp.int32, (L, tkb), 1)
        sel = ((blk > thr) | ((blk == thr) & (cc <= j_hi))) & (blk > neg_inf_key)
        m_ref[:, pl.ds(c0, tkb)] = jnp.where(sel, 1.0, 0.0).astype(BF16)
        return 0
    lax.fori_loop(0, n_c + 1, write_blk, 0)


def _dsa_sel_sample(z, cache_ki, layer, row_off, nb, L, topk):
    P = cache_ki.shape[2]
    tkb = _pick(P, (512, 256, 128))
    rb = row_off // L
    return pl.pallas_call(
        functools.partial(_dsa_sel_sample_body, L=L, P=P, tkb=tkb, topk=topk),
        grid=(nb,),
        in_specs=[pl.BlockSpec((L, NH * LANES), lambda b: (rb + b, OFF_QI // (NH * LANES))),
                  pl.BlockSpec((L, LANES), lambda b: (rb + b, OFF_WI // LANES)),
                  pl.BlockSpec((L, LANES), lambda b: (rb + b, OFF_KI // LANES)),
                  pl.BlockSpec((1, 1, P, HD), lambda b: (layer, b, 0, 0))],
        out_specs=pl.BlockSpec((L, P + tkb), lambda b: (b, 0)),
        out_shape=jax.ShapeDtypeStruct((nb * L, P + tkb), BF16),
        scratch_shapes=[pltpu.VMEM((L, P + tkb), jnp.int32)],
        compiler_params=_cparams(("parallel",)),
        name="dsa_select_sample",
    )(z, z, z, cache_ki)


def _dsa_attn_body(*refs, tq, tkb, n_cache, prompt):
    if prompt:
        q_ref, m_ref, kn_ref, vn_ref, o_ref = refs
        qb = pl.program_id(1)
        n_blk = ((qb + 1) * tq + tkb - 1) // tkb
    else:
        q_ref, m_ref, kn_ref, vn_ref, kc_ref, vc_ref, o_ref = refs
        n_blk = n_cache
    qs = [q_ref[:, e * LANES:(e + 1) * LANES] for e in range(2)]

    def visit(state, kblk, vblk, mblk):
        kb16 = kblk.astype(BF16)
        vb16 = vblk.astype(BF16)
        sel = mblk > 0
        new = []
        for e in range(2):
            m, l, acc = state[e]
            s = jnp.where(sel, _dot_nt(qs[e], kb16), NEG_BIG)
            m_new = jnp.maximum(m, jnp.max(s, axis=1, keepdims=True))
            p = jnp.where(sel, jnp.exp(s - m_new), 0.0)
            alpha = jnp.exp(m - m_new)
            l = alpha * l + jnp.sum(p, axis=1, keepdims=True)
            acc = alpha * acc + jnp.dot(p.astype(BF16), vb16, preferred_element_type=F32)
            new.append((m_new, l, acc))
        return tuple(new)

    init = (jnp.full((tq, 1), NEG_BIG, F32), jnp.zeros((tq, 1), F32), jnp.zeros((tq, LANES), F32))
    state = (init, init)

    def step(i, st):
        c0 = pl.multiple_of(i * tkb, tkb)
        if prompt:
            return visit(st, kn_ref[pl.ds(c0, tkb), :], vn_ref[pl.ds(c0, tkb), :], m_ref[:, pl.ds(c0, tkb)])
        return visit(st, kc_ref[0, 0, pl.ds(c0, tkb), :], vc_ref[0, 0, pl.ds(c0, tkb), :],
                     m_ref[:, pl.ds(c0, tkb)])
    state = lax.fori_loop(0, n_blk, step, state)

    if not prompt:
        pad = jnp.zeros((tkb - tq, LANES), F32)
        knew = jnp.concatenate([kn_ref[...], pad], axis=0)
        vnew = jnp.concatenate([vn_ref[...], pad], axis=0)
        state = visit(state, knew, vnew, m_ref[:, pl.ds(n_cache * tkb, tkb)])

    lane = lax.broadcasted_iota(jnp.int32, (tq, LANES), 1)
    o0 = state[0][2] / state[0][1]
    o1 = state[1][2] / state[1][1]
    o_ref[...] = jnp.where(lane < HD, o0, o1)


def _dsa_attn_prompt(qn, kn, z, mask, n_rows):
    tq, tkb = 256, min(1024, n_rows)
    vblk = OFF_DV // LANES
    return pl.pallas_call(
        functools.partial(_dsa_attn_body, tq=tq, tkb=tkb, n_cache=0, prompt=True),
        grid=(NH // 2, n_rows // tq),
        in_specs=[pl.BlockSpec((tq, 2 * LANES), lambda p, i: (i, p)),
                  pl.BlockSpec((tq, n_rows), lambda p, i: (i, 0)),
                  pl.BlockSpec((n_rows, LANES), lambda p, i: (0, p)),
                  pl.BlockSpec((n_rows, LANES), lambda p, i: (0, vblk + p))],
        out_specs=pl.BlockSpec((tq, LANES), lambda p, i: (i, p)),
        out_shape=jax.ShapeDtypeStruct((n_rows, BW), F32),
        compiler_params=_cparams(("parallel", "arbitrary")),
        name="dsa_attn_prompt",
    )(qn, mask, kn, z)


def _dsa_attn_sample(qn, kn, z, mask, cache_k, cache_v, layer, row_off, nb, L):
    P = cache_k.shape[2]
    tkb = mask.shape[1] - P
    vblk = OFF_DV // LANES
    rb = row_off // L
    return pl.pallas_call(
        functools.partial(_dsa_attn_body, tq=L, tkb=tkb, n_cache=P // tkb, prompt=False),
        grid=(nb, NH // 2),
        in_specs=[pl.BlockSpec((L, 2 * LANES), lambda b, p: (rb + b, p)),
                  pl.BlockSpec((L, P + tkb), lambda b, p: (b, 0)),
                  pl.BlockSpec((L, LANES), lambda b, p: (rb + b, p)),
                  pl.BlockSpec((L, LANES), lambda b, p: (rb + b, vblk + p)),
                  pl.BlockSpec((1, 1, P, LANES), lambda b, p: (layer, b, 0, p)),
                  pl.BlockSpec((1, 1, P, LANES), lambda b, p: (layer, b, 0, p))],
        out_specs=pl.BlockSpec((L, LANES), lambda b, p: (b, p)),
        out_shape=jax.ShapeDtypeStruct((nb * L, BW), F32),
        compiler_params=_cparams(("parallel", "parallel")),
        name="dsa_attn_sample",
    )(qn, mask, kn, z, cache_k, cache_v)


def _gla_consts(C):
    nlev = int(math.log2(C))
    t = np.arange(C)
    cum = (t[:, None] >= t[None, :]).astype(np.float32)
    sel = np.zeros((nlev * C, C), np.float32)
    for l in range(nlev):
        m = 1 << l
        mid = (t // (2 * m)) * 2 * m + m
        sel[l * C + t, mid - 1] = 1.0
    return jnp.asarray(cum, BF16), jnp.asarray(sel, BF16)


def _gla_body(gq_ref, gk_ref, gv_ref, gr_ref, ga_ref, a2_ref, ab_ref, on_ref, cum_ref, sel_ref,
              s0_ref, o_ref, sout_ref, st_ref, *, C, n_chunks):
    ci = pl.program_id(1)
    nlev = int(math.log2(C))

    @pl.when(ci == 0)
    def _():
        st_ref[...] = s0_ref[0]

    x = jnp.dot(ga_ref[...].astype(BF16), a2_ref[...], preferred_element_type=F32) + ab_ref[...]
    g = _neg_softplus(-x) / GLA_GATE_NORM
    b = _split_dot_left(cum_ref[...], g)
    c_all = _split_dot_left(sel_ref[...], b)

    row = lax.broadcasted_iota(jnp.int32, (C, C), 0)
    col = lax.broadcasted_iota(jnp.int32, (C, C), 1)
    rowl = lax.broadcasted_iota(jnp.int32, (C, LANES), 0)

    for h in range(GLA_H):
        sl = slice(h * LANES, (h + 1) * LANES)
        q = gq_ref[:, sl] * (HD ** -0.5)
        k = gk_ref[:, sl]
        v = gv_ref[:, sl]
        bh = b[:, sl]
        a = jnp.where(row == col, jnp.sum(q * k, axis=1, keepdims=True), 0.0)
        for l in range(nlev):
            m = 1 << l
            later = (rowl & m) != 0
            ch = c_all[l * C:(l + 1) * C, sl]
            qe = jnp.where(later, q * jnp.exp(jnp.where(later, bh - ch, 0.0)), 0.0)
            ke = jnp.where(later, 0.0, k * jnp.exp(jnp.where(later, 0.0, ch - bh)))
            al = _dot_nt(qe.astype(BF16), ke.astype(BF16))
            a = a + jnp.where((row // (2 * m)) == (col // (2 * m)), al, 0.0)
        st = st_ref[h]
        o = jnp.dot(a.astype(BF16), v.astype(BF16), preferred_element_type=F32)
        o = o + _dot_nt((q * jnp.exp(bh)).astype(BF16), st.astype(BF16))
        b_last = bh[C - 1:C, :]
        kx = k * jnp.exp(b_last - bh)
        st_ref[h] = st * jnp.exp(b_last) + _dot_tn(v.astype(BF16), kx.astype(BF16))
        on = o * lax.rsqrt(jnp.mean(o * o, axis=-1, keepdims=True) + NORM_EPS) * on_ref[...]
        gr = gr_ref[:, sl]
        o_ref[:, sl] = on * (gr * jax.nn.sigmoid(gr))

    @pl.when(ci == n_chunks - 1)
    def _():
        sout_ref[0] = st_ref[...]


def _gla(z, a2p, abp, onorm, s0t, row_off, n_seq, L):
    C = min(CHUNK, L)
    n_chunks = L // C
    rb = row_off // C
    cum, sel = _gla_consts(C)
    blk = lambda off: pl.BlockSpec((C, BW), lambda s, c: (rb + s * n_chunks + c, off // BW))
    full = lambda a: pl.BlockSpec(a.shape, lambda s, c: (0,) * a.ndim)
    return pl.pallas_call(
        functools.partial(_gla_body, C=C, n_chunks=n_chunks),
        grid=(n_seq, n_chunks),
        in_specs=[blk(OFF_GQ), blk(OFF_GK), blk(OFF_GV), blk(OFF_GR),
                  pl.BlockSpec((C, LANES), lambda s, c: (rb + s * n_chunks + c, OFF_GA // LANES)),
                  full(a2p), full(abp), full(onorm), full(cum), full(sel),
                  pl.BlockSpec((1, GLA_H, GLA_DV, LANES), lambda s, c: (s, 0, 0, 0))],
        out_specs=[pl.BlockSpec((C, BW), lambda s, c: (s * n_chunks + c, 0)),
                   pl.BlockSpec((1, GLA_H, GLA_DV, LANES), lambda s, c: (s, 0, 0, 0))],
        out_shape=[jax.ShapeDtypeStruct((n_seq * L, BW), F32),
                   jax.ShapeDtypeStruct((n_seq, GLA_H, GLA_DV, LANES), F32)],
        scratch_shapes=[pltpu.VMEM((GLA_H, GLA_DV, LANES), F32)],
        compiler_params=_cparams(("parallel", "arbitrary")),
        name="gla",
    )(z, z, z, z, z, a2p, abp, onorm, cum, sel, s0t)


def _rw_prep_body(z_ref, head_ref, mu_ref, w0_ref, a0_ref, kkp_ref, kap_ref, rkp_ref,
                  w2_ref, a2_ref, g2_ref, seg_ref,
                  r_ref, w_ref, k_ref, v_ref, nkk_ref, kka_ref, gate_ref, bonus_ref, *, grp):
    zz = z_ref[...]
    tm, wd = zz.shape
    heads = head_ref[...]
    head_rows = jnp.concatenate([jnp.broadcast_to(heads[g:g + 1, :], (grp, wd))
                                 for g in range(tm // grp)], axis=0)
    row = lax.broadcasted_iota(jnp.int32, (tm, wd), 0)
    prev = jnp.where(row % grp == 0, head_rows, pltpu.roll(zz, 1, axis=0))
    zm = zz + mu_ref[...] * (prev - zz)
    rr = zm[:, 0:BW]
    rk = zm[:, BW:2 * BW]
    rv = zm[:, 2 * BW:3 * BW]
    lora = zm[:, 3 * BW:3 * BW + LANES]
    gl = zm[:, 3 * BW + LANES:3 * BW + 2 * LANES]
    xw = w0_ref[...] + jnp.dot(jnp.tanh(lora).astype(BF16), w2_ref[...], preferred_element_type=F32)
    wlog = _neg_softplus(-xw) - 0.5
    decay = jnp.exp(-jnp.exp(wlog))
    a = jax.nn.sigmoid(a0_ref[...] + jnp.dot(lora.astype(BF16), a2_ref[...], preferred_element_type=F32))
    gate = jnp.dot(jax.nn.sigmoid(gl).astype(BF16), g2_ref[...], preferred_element_type=F32)
    kkf = rk * kkp_ref[...]
    nrm = jnp.sqrt(_split_dot(kkf * kkf, seg_ref[...]))
    kk = kkf / jnp.maximum(nrm, 1e-12)
    kmod = rk * (1.0 + (a - 1.0) * kap_ref[...])
    coef = _split_dot(rr * kmod * rkp_ref[...], seg_ref[...])
    r_ref[...] = rr
    w_ref[...] = decay
    k_ref[...] = kmod
    v_ref[...] = rv
    nkk_ref[...] = -kk
    kka_ref[...] = kk * a
    gate_ref[...] = gate
    bonus_ref[...] = coef * rv


def _rw_prep(z, heads, grp, mu, w0, a0, kkp, kap, rkp, w2p, a2p, g2, seg):
    T = z.shape[0]
    tm = SUBLANES * grp
    assert T % tm == 0, (T, tm)
    W = 4 * BW
    par = lambda a: pl.BlockSpec(a.shape, lambda i: (0,) * a.ndim)
    out = pl.BlockSpec((tm, BW), lambda i: (i, 0))
    return pl.pallas_call(
        functools.partial(_rw_prep_body, grp=grp),
        grid=(T // tm,),
        in_specs=[pl.BlockSpec((tm, W), lambda i: (i, 0)), pl.BlockSpec((SUBLANES, W), lambda i: (i, 0)),
                  par(mu), par(w0), par(a0), par(kkp), par(kap), par(rkp),
                  par(w2p), par(a2p), par(g2), par(seg)],
        out_specs=[out] * 8,
        out_shape=[jax.ShapeDtypeStruct((T, BW), F32)] * 8,
        compiler_params=_cparams(("parallel",)),
        name="rwkv_prep",
    )(z, heads, mu, w0, a0, kkp, kap, rkp, w2p, a2p, g2, seg)


def _rw_rec_body(r_ref, w_ref, k_ref, v_ref, nkk_ref, kka_ref, g_ref, s0_ref,
                 ot_ref, sout_ref, s_ref, *, TT, L, NS, n_tiles):
    ti = pl.program_id(0)
    npair = NH // 2
    R = NS * npair * HD
    gm = g_ref[...]
    lane = lax.broadcasted_iota(jnp.int32, (R, LANES), 1)
    rowi = lax.broadcasted_iota(jnp.int32, (R, LANES), 0)
    eye2 = (lane % HD) == (rowi % HD)
    seq_lane0 = (rowi // (npair * HD)) * L
    refs =(r_ref, w_ref, k_ref, v_ref, nkk_ref, kka_ref)

    def token_group(gi, carry):
        s, o_lo, o_hi = carry
        t0 = pl.multiple_of(gi * SUBLANES, SUBLANES)
        blk = [[ref[pl.ds(q * L + t0, SUBLANES), :] for q in range(NS)] for ref in refs]
        for j in range(SUBLANES):
            def stacked(x):
                return jnp.concatenate(
                    [jnp.broadcast_to(x[q][j:j + 1, p * LANES:(p + 1) * LANES], (HD, LANES))
                     for q in range(NS) for p in range(npair)], axis=0)
            rr, ww, kk, vv, nk, ka = [stacked(x) for x in blk]
            sa = _split_dot(s * nk, gm)
            vb = jnp.dot(jnp.where(eye2, vv, 0.0).astype(BF16), gm, preferred_element_type=F32)
            s = s * ww + sa * ka + vb * kk
            oo = jnp.dot((s * rr).astype(BF16), gm, preferred_element_type=F32)
            hit = lane == seq_lane0 + t0 + j
            o_lo = jnp.where(hit, oo, o_lo)
            o_hi = jnp.where(hit, pltpu.roll(oo, HD, axis=1), o_hi)
        return s, o_lo, o_hi

    if NS == 1:
        @pl.when(ti == 0)
        def _():
            s_ref[...] = s0_ref[0]
        s_in = s_ref[...]
    else:
        s_in = s0_ref[...].reshape(R, LANES)
    zero = jnp.zeros((R, LANES), F32)
    s_out, o_lo, o_hi = lax.fori_loop(0, min(L, TT) // SUBLANES, token_group, (s_in, zero, zero))

    pr = npair * HD
    a1 = sum(o_lo[q * pr:(q + 1) * pr] for q in range(NS))
    a2 = sum(o_hi[q * pr:(q + 1) * pr] for q in range(NS))
    first_half = lax.broadcasted_iota(jnp.int32, (pr, LANES), 1) < HD
    lo = jnp.where(first_half, a1, a2)
    hi = jnp.where(first_half, a2, a1)
    for p in range(npair):
        ot_ref[2 * p * HD:(2 * p + 1) * HD, :] = lo[p * HD:(p + 1) * HD]
        ot_ref[(2 * p + 1) * HD:(2 * p + 2) * HD, :] = hi[p * HD:(p + 1) * HD]
    if NS == 1:
        s_ref[...] = s_out

        @pl.when(ti == n_tiles - 1)
        def _():
            sout_ref[0] = s_out
    else:
        sout_ref[...] = s_out.reshape(NS, pr, LANES)


def _rw_rec(r, w, k, v, nkk, kka, gmat, s0, row_off, n_seq, L):
    TT = LANES
    n_tok = n_seq * L
    n_tiles = n_tok // TT
    rb = row_off // TT
    ns = max(1, TT // L)
    pr = (NH // 2) * HD
    tok = pl.BlockSpec((TT, BW), lambda i: (rb + i, 0))
    if ns == 1:
        sspec = pl.BlockSpec((1, pr, LANES), lambda i: (0, 0, 0))
    else:
        sspec = pl.BlockSpec((ns, pr, LANES), lambda i: (i, 0, 0))
    return pl.pallas_call(
        functools.partial(_rw_rec_body, TT=TT, L=L, NS=ns, n_tiles=n_tiles),
        grid=(n_tiles,),
        in_specs=[tok] * 6 + [pl.BlockSpec((LANES, LANES), lambda i: (0, 0)), sspec],
        out_specs=[pl.BlockSpec((BW, TT), lambda i: (0, i)), sspec],
        out_shape=[jax.ShapeDtypeStruct((BW, n_tok), F32),
                   jax.ShapeDtypeStruct(s0.shape, F32)],
        scratch_shapes=[pltpu.VMEM((pr, LANES), F32)],
        compiler_params=_cparams(("arbitrary",)),
        name="rwkv_recurrence",
    )(r, w, k, v, nkk, kka, gmat, s0)


def _rw_post_body(o_ref, bonus_ref, gate_ref, lnw_ref, lnb_ref, seg_ref, out_ref):
    o = o_ref[...]
    mu = _split_dot(o, seg_ref[...]) * (1.0 / HD)
    d = o - mu
    var = _split_dot(d * d, seg_ref[...]) * (1.0 / HD)
    on = d * lax.rsqrt(var + RW_LN_EPS) * lnw_ref[...] + lnb_ref[...]
    out_ref[...] = (on + bonus_ref[...]) * gate_ref[...]


def _rw_post(o, bonus, gate, lnw, lnb, seg):
    T = o.shape[0]
    tm = _pick(T, (512, 256, 128))
    blk = pl.BlockSpec((tm, BW), lambda i: (i, 0))
    par = lambda a: pl.BlockSpec(a.shape, lambda i: (0,) * a.ndim)
    return pl.pallas_call(
        _rw_post_body,
        grid=(T // tm,),
        in_specs=[blk, blk, blk, par(lnw), par(lnb), par(seg)],
        out_specs=blk,
        out_shape=jax.ShapeDtypeStruct((T, BW), F32),
        compiler_params=_cparams(("parallel",)),
        name="rwkv_post",
    )(o, bonus, gate, lnw, lnb, seg)


def _heads_alt(w):
    d = w.shape[0]
    w4 = w.reshape(d, NH // 2, 2, HD)
    zero = jnp.zeros((d, NH // 2, HD), w.dtype)
    ev = jnp.concatenate([w4[:, :, 0], zero], -1)
    od = jnp.concatenate([zero, w4[:, :, 1]], -1)
    return jnp.stack([ev, od], 2).reshape(d, NH * LANES)


def _heads_lo(w, nh):
    d = w.shape[0]
    w3 = w.reshape(d, nh, HD)
    return jnp.concatenate([w3, jnp.zeros_like(w3)], -1).reshape(d, nh * LANES)


def _padc(w, n):
    return jnp.pad(w, ((0, 0), (0, n - w.shape[1])))


def _rw_reorder(a):
    o = 0
    parts = {}
    for name, wd in (("rr", BW), ("wl", RW_LORA), ("rk", BW), ("rv", BW), ("al", RW_LORA), ("gl", RW_GATE_LORA)):
        parts[name] = a[..., o:o + wd]
        o += wd
    return jnp.concatenate([parts[n] for n in ("rr", "rk", "rv", "wl", "al", "gl")], axis=-1)


def _rw_unorder(a):
    rr, rk, rv = a[..., 0:BW], a[..., BW:2 * BW], a[..., 2 * BW:3 * BW]
    wl = a[..., 3 * BW:3 * BW + RW_LORA]
    al = a[..., 3 * BW + RW_LORA:3 * BW + 2 * RW_LORA]
    gl = a[..., 3 * BW + 2 * RW_LORA:RW_COLS]
    return jnp.concatenate([rr, wl, rk, rv, al, gl], axis=-1)


def _relayout_w_in(w, d_model):
    w = w.astype(BF16)
    o = 0

    def take(n):
        nonlocal o
        s = w[:, o:o + n]
        o += n
        return s
    sq, sk, sv = take(BW), take(BW), take(BW)
    dq, dk, dv, qi, ki, wi = take(BW), take(BW), take(BW), take(BW), take(HD), take(NH)
    gq, gk = take(GLA_H * HD), take(GLA_H * HD)
    gv, gr, ga = take(BW), take(BW), take(GLA_LOWRANK)
    rw = take(RW_COLS)
    gate = take(4 * d_model)
    cols = [_padc(_rw_reorder(rw), 4 * BW), _heads_alt(sq), sk, sv, _heads_alt(dq), dk, dv,
            _heads_lo(qi, NH), _heads_lo(gq, GLA_H), _heads_lo(gk, GLA_H), gv, gr,
            _padc(ki, LANES), _padc(wi, LANES), _padc(ga, 2 * LANES), gate]
    return jnp.concatenate(cols, axis=1)


def _relayout_ffn(w_up, w_down, tf):
    d, f2 = w_up.shape
    f = f2 // 2
    fp = -(-f // tf) * tf
    g = _padc(w_up[:, :f].astype(BF16), fp).reshape(d, fp // tf, tf)
    u = _padc(w_up[:, f:].astype(BF16), fp).reshape(d, fp // tf, tf)
    wup = jnp.concatenate([g, u], axis=2).reshape(d, 2 * fp)
    wdn = jnp.pad(w_down.astype(BF16), ((0, fp - f), (0, 0)))
    return wup, wdn


def _gain_alt(g):
    return jnp.tile(g, 2 * NH)[None, :]


def kernel(x_prompt, x_sample, cache_sb_k, cache_sb_v, cache_dsa_k, cache_dsa_v, cache_dsa_kidx, state_gla, state_rwkv, state_rwkv_shift, ffn1_norm, ffn1_up, ffn1_down, mix_norm, w_in, sb_qnorm, sb_knorm, dsa_qnorm, dsa_knorm, gla_a2, gla_ab, gla_onorm, rwkv_mu, rwkv_w0, rwkv_w2, rwkv_a0, rwkv_a2, rwkv_g2, rwkv_kk, rwkv_ka, rwkv_rk, rwkv_lnw, rwkv_lnb, w_branch, w_out, ffn2_norm, ffn2_up, ffn2_down):
    depth = w_in.shape[0]
    bp, lp, d_model = x_prompt.shape
    nb, ls, _ = x_sample.shape
    assert bp == 1, "prompt group is one sequence"
    tp = bp * lp
    ts = nb * ls
    past = cache_sb_k.shape[2]
    tf = 512

    x = jnp.concatenate([x_prompt.reshape(tp, d_model), x_sample.reshape(ts, d_model)], axis=0)
    ck_sb = cache_sb_k.reshape(depth, nb, past, BW)
    cv_sb = cache_sb_v.reshape(depth, nb, past, BW)
    ck_dsa = cache_dsa_k.reshape(depth, nb, past, BW)
    cv_dsa = cache_dsa_v.reshape(depth, nb, past, BW)

    bd = jnp.asarray(np.kron(np.eye(2 * NH), np.full((HD, HD), 1.0 / HD)), BF16)
    seg = jnp.asarray(np.kron(np.eye(NH), np.ones((HD, HD))), BF16)
    gmat = jnp.asarray(np.kron(np.eye(2), np.ones((HD, HD))), BF16)
    topk_p = min(TOPK_MAX, lp // 4)
    topk_s = min(TOPK_MAX, (past + ls) // 4)

    new_p = [[] for _ in range(8)]
    new_s = [[] for _ in range(8)]
    for i in range(depth):
        wup1, wdn1 = _relayout_ffn(ffn1_up[i], ffn1_down[i], tf)
        wup2, wdn2 = _relayout_ffn(ffn2_up[i], ffn2_down[i], tf)
        win = _relayout_w_in(w_in[i], d_model)

        x = _ffn(x, ffn1_norm[i][None], wup1, wdn1, tf)
        z = _proj(x, mix_norm[i][None], win)

        sqn, skn, dqn, dkn = _qknorm(z, _gain_alt(sb_qnorm[i]), jnp.tile(sb_knorm[i], NH)[None],
                                     _gain_alt(dsa_qnorm[i]), jnp.tile(dsa_knorm[i], NH)[None], bd)
        o_sb = jnp.concatenate([
            _sb_prompt(sqn, skn, z, tp),
            _sb_sample(sqn, skn, z, ck_sb, cv_sb, i, tp, nb, ls)], axis=0)
        mask_p = _dsa_sel_prompt(z, tp, topk_p)
        mask_s = _dsa_sel_sample(z, cache_dsa_kidx, i, tp, nb, ls, topk_s)
        o_dsa = jnp.concatenate([
            _dsa_attn_prompt(dqn, dkn, z, mask_p, tp),
            _dsa_attn_sample(dqn, dkn, z, mask_s, ck_dsa, cv_dsa, i, tp, nb, ls)], axis=0)

        a2p = jnp.pad(_heads_lo(gla_a2[i], GLA_H), ((0, LANES - GLA_LOWRANK), (0, 0))).astype(BF16)
        abp = _heads_lo(gla_ab[i][None], GLA_H)
        onorm = gla_onorm[i][None]

        def gla_state_in(s):
            return jnp.pad(jnp.swapaxes(s, 2, 3), ((0, 0), (0, 0), (0, 0), (0, LANES - HD)))

        def gla_state_out(s):
            return jnp.swapaxes(s[..., :HD], 2, 3)
        og_p, sg_p = _gla(z, a2p, abp, onorm, jnp.zeros((bp, GLA_H, GLA_DV, LANES), F32), 0, bp, lp)
        og_s, sg_s = _gla(z, a2p, abp, onorm, gla_state_in(state_gla[i]), tp, nb, ls)
        o_gla = jnp.concatenate([og_p, og_s], axis=0)

        zrw = z[:, :RW_COLS]
        shift_s = _rw_reorder(state_rwkv_shift[i])
        heads = jnp.concatenate([jnp.zeros((1, 4 * BW), F32), z[ls - 1:tp - 1:ls, :4 * BW],
                                 _padc(shift_s.reshape(nb, RW_COLS), 4 * BW)], axis=0)
        mu = _padc(_rw_reorder(rwkv_mu[i])[None], 4 * BW)
        w2p = jnp.concatenate([rwkv_w2[i], jnp.zeros_like(rwkv_a2[i])], axis=0).astype(BF16)
        a2q = jnp.concatenate([jnp.zeros_like(rwkv_w2[i]), rwkv_a2[i]], axis=0).astype(BF16)
        r_, w_, k_, v_, nkk_, kka_, gate_, bonus_ = _rw_prep(
            z, heads, ls, mu, rwkv_w0[i][None], rwkv_a0[i][None], rwkv_kk[i][None], rwkv_ka[i][None],
            rwkv_rk[i][None], w2p, a2q, rwkv_g2[i].astype(BF16), seg)

        def rw_state_in(s):
            n = s.shape[0]
            return s.reshape(n, NH // 2, 2, HD, HD).transpose(0, 1, 3, 2, 4).reshape(n, NH // 2 * HD, LANES)

        def rw_state_out(s):
            n = s.shape[0]
            return s.reshape(n, NH // 2, HD, 2, HD).transpose(0, 1, 3, 2, 4).reshape(n, NH, HD, HD)
        ot_p, sr_p = _rw_rec(r_, w_, k_, v_, nkk_, kka_, gmat, jnp.zeros((bp, NH // 2 * HD, LANES), F32), 0, bp, lp)
        ot_s, sr_s = _rw_rec(r_, w_, k_, v_, nkk_, kka_, gmat, rw_state_in(state_rwkv[i]), tp, nb, ls)
        o_r = jnp.concatenate([ot_p.T, ot_s.T], axis=0)
        o_rw = _rw_post(o_r, bonus_, gate_, rwkv_lnw[i][None], rwkv_lnb[i][None], seg)

        x = _merge(x, z, o_sb, o_dsa, o_gla, o_rw, w_branch[i].astype(BF16), w_out[i].astype(BF16))
        x = _ffn(x, ffn2_norm[i][None], wup2, wdn2, tf)

        sv = z[:, OFF_SV:OFF_SV + BW]
        dv = z[:, OFF_DV:OFF_DV + BW]
        ki = z[:, OFF_KI:OFF_KI + HD]
        for dst, (lo, n, l) in ((new_p, (0, bp, lp)), (new_s, (tp, nb, ls))):
            hi = lo + n * l
            dst[0].append(skn[lo:hi].reshape(n, l, NH, HD))
            dst[1].append(sv[lo:hi].reshape(n, l, NH, HD))
            dst[2].append(dkn[lo:hi].reshape(n, l, NH, HD))
            dst[3].append(dv[lo:hi].reshape(n, l, NH, HD))
            dst[4].append(ki[lo:hi].reshape(n, l, HD))
            dst[7].append(_rw_unorder(zrw[lo:hi].reshape(n, l, RW_COLS)[:, -1:]))
        new_p[5].append(gla_state_out(sg_p))
        new_s[5].append(gla_state_out(sg_s))
        new_p[6].append(rw_state_out(sr_p))
        new_s[6].append(rw_state_out(sr_s))

    outs_p = [jnp.stack(l, axis=0) for l in new_p]
    outs_s = [jnp.stack(l, axis=0) for l in new_s]
    y_p = x[:tp].reshape(bp, lp, d_model)
    y_s = x[tp:].reshape(nb, ls, d_model)
    return (y_p, y_s, *outs_p, *outs_s)
```

```python
import functools
import math

import jax
import jax.numpy as jnp
import numpy as np
from jax import lax
from jax.experimental import pallas as pl
from jax.experimental.pallas import tpu as pltpu

F32 = jnp.float32
BF16 = jnp.bfloat16

LANES = 128
SUBLANES = 8
HD = 64
NH = 8
BW = NH * HD
GLA_H = 4
GLA_DV = 128
GLA_LOWRANK = 16
GLA_GATE_NORM = 16.0
CHUNK = 64
TOPK_MAX = 256
RW_LORA = 64
RW_GATE_LORA = 128
RW_BLK = 16
RW_COLS = 3 * BW + 2 * RW_LORA + RW_GATE_LORA
NORM_EPS = 1e-6
RW_LN_EPS = 64e-5
NEG_BIG = -1e30
SB_UNDERFLOW = -104.0
VMEM_LIMIT = 56 * 1024 * 1024

OFF_RW = 0
OFF_SQ = 2048
OFF_SK = 3072
OFF_SV = 3584
OFF_DQ = 4096
OFF_DK = 5120
OFF_DV = 5632
OFF_QI = 6144
OFF_GQ = 7168
OFF_GK = 7680
OFF_GV = 8192
OFF_GR = 8704
OFF_KI = 9216
OFF_WI = 9344
OFF_GA = 9472
OFF_GATE = 9728


def _cparams(sem):
    return pltpu.CompilerParams(dimension_semantics=sem, vmem_limit_bytes=VMEM_LIMIT)


def _pick(n, prefs):
    for p in prefs:
        if n % p == 0:
            return p
    raise ValueError(f"no tile for {n} in {prefs}")


def _split_dot(a, b):
    hi = a.astype(BF16)
    lo = (a - hi.astype(F32)).astype(BF16)
    return (jnp.dot(hi, b, preferred_element_type=F32)
            + jnp.dot(lo, b, preferred_element_type=F32))


def _split_dot_left(a, b):
    hi = b.astype(BF16)
    lo = (b - hi.astype(F32)).astype(BF16)
    return (jnp.dot(a, hi, preferred_element_type=F32)
            + jnp.dot(a, lo, preferred_element_type=F32))


def _dot_nt(a, b):
    return lax.dot_general(a, b, (((1,), (1,)), ((), ())), preferred_element_type=F32)


def _dot_tn(a, b):
    return lax.dot_general(a, b, (((0,), (0,)), ((), ())), preferred_element_type=F32)


def _neg_softplus(z):
    return -(jnp.maximum(z, 0.0) + jnp.log(1.0 + jnp.exp(-jnp.abs(z))))


def _ffn_body(x_ref, g_ref, wup_ref, wdn_ref, o_ref, xn_ref, *, tf):
    @pl.when(pl.program_id(1) == 0)
    def _():
        x = x_ref[...]
        ms = jnp.mean(x * x, axis=-1, keepdims=True)
        xn_ref[...] = (x * lax.rsqrt(ms + NORM_EPS) * g_ref[...]).astype(BF16)
        o_ref[...] = x

    gu = jnp.dot(xn_ref[...], wup_ref[...], preferred_element_type=F32)
    gate = gu[:, :tf]
    act = (gate * jax.nn.sigmoid(gate) * gu[:, tf:]).astype(BF16)
    o_ref[...] += 0.5 * jnp.dot(act, wdn_ref[...], preferred_element_type=F32)


def _ffn(x, g, wup, wdn, tf):
    T, D = x.shape
    fp = wdn.shape[0]
    tm = _pick(T, (512, 256, 128))
    return pl.pallas_call(
        functools.partial(_ffn_body, tf=tf),
        grid=(T // tm, fp // tf),
        in_specs=[pl.BlockSpec((tm, D), lambda i, j: (i, 0)),
                  pl.BlockSpec((1, D), lambda i, j: (0, 0)),
                  pl.BlockSpec((D, 2 * tf), lambda i, j: (0, j)),
                  pl.BlockSpec((tf, D), lambda i, j: (j, 0))],
        out_specs=pl.BlockSpec((tm, D), lambda i, j: (i, 0)),
        out_shape=jax.ShapeDtypeStruct((T, D), F32),
        scratch_shapes=[pltpu.VMEM((tm, D), BF16)],
        compiler_params=_cparams(("parallel", "arbitrary")),
        name="ffn",
    )(x, g, wup, wdn)


def _proj_body(x_ref, g_ref, w_ref, o_ref, xn_ref):
    @pl.when(pl.program_id(1) == 0)
    def _():
        x = x_ref[...]
        ms = jnp.mean(x * x, axis=-1, keepdims=True)
        xn_ref[...] = (x * lax.rsqrt(ms + NORM_EPS) * g_ref[...]).astype(BF16)

    o_ref[...] = jnp.dot(xn_ref[...], w_ref[...], preferred_element_type=F32)


def _proj(x, g, w):
    T, D = x.shape
    n = w.shape[1]
    tm = _pick(T, (1024, 512, 256, 128))
    tn = 512
    return pl.pallas_call(
        _proj_body,
        grid=(T // tm, n // tn),
        in_specs=[pl.BlockSpec((tm, D), lambda i, j: (i, 0)),
                  pl.BlockSpec((1, D), lambda i, j: (0, 0)),
                  pl.BlockSpec((D, tn), lambda i, j: (0, j))],
        out_specs=pl.BlockSpec((tm, tn), lambda i, j: (i, j)),
        out_shape=jax.ShapeDtypeStruct((T, n), F32),
        scratch_shapes=[pltpu.VMEM((tm, D), BF16)],
        compiler_params=_cparams(("parallel", "arbitrary")),
        name="in_proj",
    )(x, g, w)


def _merge_body(osb_ref, ods_ref, ogl_ref, orw_ref, g0_ref, g1_ref, g2_ref, g3_ref,
                wb_ref, wo_ref, x_ref, o_ref):
    @pl.when(pl.program_id(1) == 0)
    def _():
        o_ref[...] = x_ref[...]

    acc = None
    for n, (b_ref, g_ref) in enumerate(((osb_ref, g0_ref), (ods_ref, g1_ref),
                                        (ogl_ref, g2_ref), (orw_ref, g3_ref))):
        p = jnp.dot(b_ref[...].astype(BF16), wb_ref[n], preferred_element_type=F32)
        t = jax.nn.sigmoid(g_ref[...]) * p
        acc = t if acc is None else acc + t
    o_ref[...] += jnp.dot(acc.astype(BF16), wo_ref[...], preferred_element_type=F32)


def _merge(x, z, o_sb, o_dsa, o_gla, o_rw, wb, wo):
    T, D = x.shape
    tm = _pick(T, (512, 256, 128))
    tc = 512
    nc = D // tc
    gate_blk = OFF_GATE // tc
    bspec = pl.BlockSpec((tm, BW), lambda i, c: (i, 0))
    gspecs = [pl.BlockSpec((tm, tc), functools.partial(lambda i, c, n: (i, gate_blk + n * nc + c), n=n))
              for n in range(4)]
    return pl.pallas_call(
        _merge_body,
        grid=(T // tm, nc),
        in_specs=[bspec, bspec, bspec, bspec] + gspecs + [
            pl.BlockSpec((4, BW, tc), lambda i, c: (0, 0, c)),
            pl.BlockSpec((tc, D), lambda i, c: (c, 0)),
            pl.BlockSpec((tm, D), lambda i, c: (i, 0))],
        out_specs=pl.BlockSpec((tm, D), lambda i, c: (i, 0)),
        out_shape=jax.ShapeDtypeStruct((T, D), F32),
        compiler_params=_cparams(("parallel", "arbitrary")),
        name="merge",
    )(o_sb, o_dsa, o_gla, o_rw, z, z, z, z, wb, wo, x)


def _qknorm_body(sq_ref, sk_ref, dq_ref, dk_ref, gsq_ref, gsk_ref, gdq_ref, gdk_ref, bd_ref,
                 osq_ref, osk_ref, odq_ref, odk_ref):
    def norm(x, g, n):
        ms = _split_dot(x * x, bd_ref[:n, :n])
        return x * lax.rsqrt(ms + NORM_EPS) * g

    scale = HD ** -0.5
    osq_ref[...] = (norm(sq_ref[...], gsq_ref[...], 2 * BW) * scale).astype(BF16)
    osk_ref[...] = norm(sk_ref[...], gsk_ref[...], BW)
    odq_ref[...] = (norm(dq_ref[...], gdq_ref[...], 2 * BW) * scale).astype(BF16)
    odk_ref[...] = norm(dk_ref[...], gdk_ref[...], BW)


def _qknorm(z, gsq, gsk, gdq, gdk, bd):
    T = z.shape[0]
    tm = _pick(T, (512, 256, 128))
    wq, wk = 2 * BW, BW
    row = lambda w, off: pl.BlockSpec((tm, w), lambda i: (i, off // w))
    par = lambda w: pl.BlockSpec((1, w), lambda i: (0, 0))
    out = lambda w: pl.BlockSpec((tm, w), lambda i: (i, 0))
    return pl.pallas_call(
        _qknorm_body,
        grid=(T // tm,),
        in_specs=[row(wq, OFF_SQ), row(wk, OFF_SK), row(wq, OFF_DQ), row(wk, OFF_DK),
                  par(wq), par(wk), par(wq), par(wk),
                  pl.BlockSpec((wq, wq), lambda i: (0, 0))],
        out_specs=[out(wq), out(wk), out(wq), out(wk)],
        out_shape=[jax.ShapeDtypeStruct((T, wq), BF16), jax.ShapeDtypeStruct((T, wk), F32),
                   jax.ShapeDtypeStruct((T, wq), BF16), jax.ShapeDtypeStruct((T, wk), F32)],
        compiler_params=_cparams(("parallel",)),
        name="qk_norm",
    )(z, z, z, z, gsq, gsk, gdq, gdk, bd)


def _sb_body(*refs, tq, n_cache, tkc):
    if n_cache:
        q_ref, kn_ref, vn_ref, kc_ref, vc_ref, o_ref = refs
    else:
        q_ref, kn_ref, vn_ref, o_ref = refs
    qb = pl.program_id(1) if not n_cache else 0
    qs = [q_ref[:, e * LANES:(e + 1) * LANES] for e in range(2)]

    def tri(n):
        r = lax.broadcasted_iota(jnp.int32, (n, n), 0)
        c = lax.broadcasted_iota(jnp.int32, (n, n), 1)
        return (r > c).astype(BF16)

    def visit(state, kblk, vblk, u, valid):
        kb16 = kblk.astype(BF16)
        vb16 = vblk.astype(BF16)
        new = []
        for e in range(2):
            carry, acc = state[e]
            z = _dot_nt(qs[e], kb16)
            lg = _neg_softplus(z)
            if valid is not None:
                lg = jnp.where(valid, lg, 0.0)
            inner = _split_dot(lg, u)
            a = jnp.exp(z + lg + inner + carry)
            if valid is not None:
                a = jnp.where(valid, a, 0.0)
            acc = acc + jnp.dot(a.astype(BF16), vb16, preferred_element_type=F32)
            carry = carry + inner[:, :1] + lg[:, :1]
            new.append((carry, acc))
        return tuple(new)

    zero = (jnp.zeros((tq, 1), F32), jnp.zeros((tq, LANES), F32))
    state = (zero, zero)

    r = lax.broadcasted_iota(jnp.int32, (tq, tq), 0)
    c = lax.broadcasted_iota(jnp.int32, (tq, tq), 1)
    u_new = tri(tq)
    if n_cache:
        state = visit(state, kn_ref[...], vn_ref[...], u_new, c < r)
    else:
        row0 = pl.multiple_of(qb * tq, tq)
        state = visit(state, kn_ref[pl.ds(row0, tq), :], vn_ref[pl.ds(row0, tq), :], u_new, c < r)

    def live(c):
        i, st = c
        return (i < n_old) & (jnp.max(jnp.maximum(st[0][0], st[1][0])) > SB_UNDERFLOW)

    if not n_cache:
        n_old = qb

        def step(c):
            i, st = c
            r0 = pl.multiple_of((qb - 1 - i) * tq, tq)
            return i + 1, visit(st, kn_ref[pl.ds(r0, tq), :], vn_ref[pl.ds(r0, tq), :], u_new, None)
    else:
        n_old = n_cache
        u_c = tri(tkc)

        def step(c):
            i, st = c
            r0 = pl.multiple_of((n_cache - 1 - i) * tkc, tkc)
            return i + 1, visit(st, kc_ref[0, 0, pl.ds(r0, tkc), :], vc_ref[0, 0, pl.ds(r0, tkc), :],
                                u_c, None)
    _, state = lax.while_loop(live, step, (jnp.int32(0), state))

    lane = lax.broadcasted_iota(jnp.int32, (tq, LANES), 1)
    o_ref[...] = jnp.where(lane < HD, state[0][1], state[1][1])


def _sb_prompt(qn, kn, z, n_rows):
    tq = 128
    vblk = OFF_SV // LANES
    return pl.pallas_call(
        functools.partial(_sb_body, tq=tq, n_cache=0, tkc=0),
        grid=(NH // 2, n_rows // tq),
        in_specs=[pl.BlockSpec((tq, 2 * LANES), lambda p, i: (i, p)),
                  pl.BlockSpec((n_rows, LANES), lambda p, i: (0, p)),
                  pl.BlockSpec((n_rows, LANES), lambda p, i: (0, vblk + p))],
        out_specs=pl.BlockSpec((tq, LANES), lambda p, i: (i, p)),
        out_shape=jax.ShapeDtypeStruct((n_rows, BW), F32),
        compiler_params=_cparams(("parallel", "arbitrary")),
        name="sb_prompt",
    )(qn, kn, z)


def _sb_sample(qn, kn, z, cache_k, cache_v, layer, row_off, nb, L):
    P = cache_k.shape[2]
    tkc = _pick(P, (256, 128))
    vblk = OFF_SV // LANES
    rb = row_off // L
    return pl.pallas_call(
        functools.partial(_sb_body, tq=L, n_cache=P // tkc, tkc=tkc),
        grid=(nb, NH // 2),
        in_specs=[pl.BlockSpec((L, 2 * LANES), lambda b, p: (rb + b, p)),
                  pl.BlockSpec((L, LANES), lambda b, p: (rb + b, p)),
                  pl.BlockSpec((L, LANES), lambda b, p: (rb + b, vblk + p)),
                  pl.BlockSpec((1, 1, P, LANES), lambda b, p: (layer, b, 0, p)),
                  pl.BlockSpec((1, 1, P, LANES), lambda b, p: (layer, b, 0, p))],
        out_specs=pl.BlockSpec((L, LANES), lambda b, p: (b, p)),
        out_shape=jax.ShapeDtypeStruct((nb * L, BW), F32),
        compiler_params=_cparams(("parallel", "parallel")),
        name="sb_sample",
    )(qn, kn, z, cache_k, cache_v)


def _sortable(x):
    b = pltpu.bitcast(x + 0.0, jnp.int32)
    return jnp.where(b < 0, b ^ jnp.int32(0x7FFFFFFF), b)


def _select_topk(key_ref, n_blk, tkb, topk, rows):
    def count(pred):
        def body(i, acc):
            c0 = pl.multiple_of(i * tkb, tkb)
            for c in range(tkb // LANES):
                blk = key_ref[:, pl.ds(c0 + c * LANES, LANES)]
                col = c0 + c * LANES + lax.broadcasted_iota(jnp.int32, (rows, LANES), 1)
                acc = acc + jnp.where(pred(blk, col), 1.0, 0.0)
            return acc
        acc = lax.fori_loop(0, n_blk, body, jnp.zeros((rows, LANES), F32))
        return jnp.sum(acc, axis=1, keepdims=True)

    def unresolved(c):
        i, _, n_ge = c
        return (i < 32) & (jnp.max(jnp.abs(n_ge - topk)) > 0)

    def bit_step(c):
        i, thr, n_ge = c
        cand = thr + (jnp.int32(1) << (31 - i))
        n_cand = count(lambda blk, col: blk >= cand)
        take = n_cand >= topk
        return i + 1, jnp.where(take, cand, thr), jnp.where(take, n_cand, n_ge)

    thr0 = jnp.full((rows, 1), jnp.iinfo(jnp.int32).min, jnp.int32)
    n_all = jnp.full((rows, 1), 1.0, F32) * jnp.asarray(n_blk * tkb).astype(F32)
    _, thr, n_ge = lax.while_loop(unresolved, bit_step, (jnp.int32(0), thr0, n_all))
    n_col_bits = max(1, int(math.ceil(math.log2(key_ref.shape[1] + 1))))

    def tie_search():
        need = topk - count(lambda blk, col: blk > thr)

        def col_step(i, j):
            cand = j + (jnp.int32(1) << (n_col_bits - 1 - i))
            n = count(lambda blk, col: (blk == thr) & (col < cand))
            return jnp.where(n < need, cand, j)
        return lax.fori_loop(0, n_col_bits, col_step, jnp.zeros((rows, 1), jnp.int32))

    any_tie = jnp.max(jnp.abs(n_ge - topk)) > 0
    j_hi = lax.cond(any_tie, tie_search,
                    lambda: jnp.full((rows, 1), jnp.iinfo(jnp.int32).max, jnp.int32))
    return thr, j_hi


def _idx_scores(qi, wi, kblk16, k_real):
    acc = None
    for h in range(NH):
        qh = qi[:, h * LANES:h * LANES + k_real].astype(BF16)
        d = jnp.maximum(_dot_nt(qh, kblk16), 0.0) * (HD ** -0.5)
        t = (wi[:, h:h + 1] * (NH ** -0.5)) * d
        acc = t if acc is None else acc + t
    return acc


def _dsa_sel_prompt_body(qi_ref, wi_ref, ki_ref, m_ref, key_ref, *, tq, tkb, topk, n_keys):
    qb = pl.program_id(0)
    n_blk = ((qb + 1) * tq + tkb - 1) // tkb
    qi = qi_ref[...]
    wi = wi_ref[...]
    q_chunk = (qb * tq + lax.broadcasted_iota(jnp.int32, (tq, tkb), 0)) // CHUNK

    def score_blk(i, _):
        c0 = pl.multiple_of(i * tkb, tkb)
        s = _idx_scores(qi, wi, ki_ref[pl.ds(c0, tkb), :].astype(BF16), LANES)
        k_chunk = (c0 + lax.broadcasted_iota(jnp.int32, (tq, tkb), 1)) // CHUNK
        s = jnp.where(k_chunk <= q_chunk, s, -jnp.inf)
        key_ref[:, pl.ds(c0, tkb)] = _sortable(s)
        return 0
    lax.fori_loop(0, n_blk, score_blk, 0)

    thr, j_hi = _select_topk(key_ref, n_blk, tkb, topk, tq)
    neg_inf_key = _sortable(jnp.full((1, 1), -jnp.inf, F32))

    def write_blk(i, _):
        c0 = pl.multiple_of(i * tkb, tkb)
        blk = key_ref[:, pl.ds(c0, tkb)]
        col = c0 + lax.broadcasted_iota(jnp.int32, (tq, tkb), 1)
        sel = ((blk > thr) | ((blk == thr) & (col <= j_hi))) & (blk > neg_inf_key)
        m_ref[:, pl.ds(c0, tkb)] = jnp.where(sel, 1.0, 0.0).astype(BF16)
        return 0
    lax.fori_loop(0, n_blk, write_blk, 0)

    def zero_blk(i, _):
        m_ref[:, pl.ds(pl.multiple_of(i * tkb, tkb), tkb)] = jnp.zeros((tq, tkb), BF16)
        return 0
    lax.fori_loop(n_blk, n_keys // tkb, zero_blk, 0)


def _dsa_sel_prompt(z, n_rows, topk):
    tq, tkb = 128, 512
    return pl.pallas_call(
        functools.partial(_dsa_sel_prompt_body, tq=tq, tkb=tkb, topk=topk, n_keys=n_rows),
        grid=(n_rows // tq,),
        in_specs=[pl.BlockSpec((tq, NH * LANES), lambda i: (i, OFF_QI // (NH * LANES))),
                  pl.BlockSpec((tq, LANES), lambda i: (i, OFF_WI // LANES)),
                  pl.BlockSpec((n_rows, LANES), lambda i: (0, OFF_KI // LANES))],
        out_specs=pl.BlockSpec((tq, n_rows), lambda i: (i, 0)),
        out_shape=jax.ShapeDtypeStruct((n_rows, n_rows), BF16),
        scratch_shapes=[pltpu.VMEM((tq, n_rows), jnp.int32)],
        compiler_params=_cparams(("parallel",)),
        name="dsa_select_prompt",
    )(z, z, z)


def _dsa_sel_sample_body(qi_ref, wi_ref, kin_ref, kic_ref, m_ref, key_ref, *, L, P, tkb, topk):
    qi = qi_ref[...]
    wi = wi_ref[...]
    n_c = P // tkb

    def score_blk(i, _):
        c0 = pl.multiple_of(i * tkb, tkb)
        s = _idx_scores(qi, wi, kic_ref[0, 0, pl.ds(c0, tkb), :].astype(BF16), HD)
        key_ref[:, pl.ds(c0, tkb)] = _sortable(s)
        return 0
    lax.fori_loop(0, n_c, score_blk, 0)

    knew = jnp.concatenate([kin_ref[...], jnp.zeros((tkb - L, LANES), F32)], axis=0).astype(BF16)
    s = _idx_scores(qi, wi, knew, LANES)
    col = lax.broadcasted_iota(jnp.int32, (L, tkb), 1)
    key_ref[:, pl.ds(P, tkb)] = _sortable(jnp.where(col < L, s, -jnp.inf))

    thr, j_hi = _select_topk(key_ref, n_c + 1, tkb, topk, L)
    neg_inf_key = _sortable(jnp.full((1, 1), -jnp.inf, F32))

    def write_blk(i, _):
        c0 = pl.multiple_of(i * tkb, tkb)
        blk = key_ref[:, pl.ds(c0, tkb)]
        cc = c0 + lax.broadcasted_iota(jnp.int32, (L, tkb), 1)
        sel = ((blk > thr) | ((blk == thr) & (cc <= j_hi))) & (blk > neg_inf_key)
        m_ref[:, pl.ds(c0, tkb)] = jnp.where(sel, 1.0, 0.0).astype(BF16)
        return 0
    lax.fori_loop(0, n_c + 1, write_blk, 0)


def _dsa_sel_sample(z, cache_ki, layer, row_off, nb, L, topk):
    P = cache_ki.shape[2]
    tkb = _pick(P, (512, 256, 128))
    rb = row_off // L
    return pl.pallas_call(
        functools.partial(_dsa_sel_sample_body, L=L, P=P, tkb=tkb, topk=topk),
        grid=(nb,),
        in_specs=[pl.BlockSpec((L, NH * LANES), lambda b: (rb + b, OFF_QI // (NH * LANES))),
                  pl.BlockSpec((L, LANES), lambda b: (rb + b, OFF_WI // LANES)),
                  pl.BlockSpec((L, LANES), lambda b: (rb + b, OFF_KI // LANES)),
                  pl.BlockSpec((1, 1, P, HD), lambda b: (layer, b, 0, 0))],
        out_specs=pl.BlockSpec((L, P + tkb), lambda b: (b, 0)),
        out_shape=jax.ShapeDtypeStruct((nb * L, P + tkb), BF16),
        scratch_shapes=[pltpu.VMEM((L, P + tkb), jnp.int32)],
        compiler_params=_cparams(("parallel",)),
        name="dsa_select_sample",
    )(z, z, z, cache_ki)


def _dsa_attn_body(*refs, tq, tkb, n_cache, prompt):
    if prompt:
        q_ref, m_ref, kn_ref, vn_ref, o_ref = refs
        qb = pl.program_id(1)
        n_blk = ((qb + 1) * tq + tkb - 1) // tkb
    else:
        q_ref, m_ref, kn_ref, vn_ref, kc_ref, vc_ref, o_ref = refs
        n_blk = n_cache
    qs = [q_ref[:, e * LANES:(e + 1) * LANES] for e in range(2)]

    def visit(state, kblk, vblk, mblk):
        kb16 = kblk.astype(BF16)
        vb16 = vblk.astype(BF16)
        sel = mblk > 0
        new = []
        for e in range(2):
            m, l, acc = state[e]
            s = jnp.where(sel, _dot_nt(qs[e], kb16), NEG_BIG)
            m_new = jnp.maximum(m, jnp.max(s, axis=1, keepdims=True))
            p = jnp.where(sel, jnp.exp(s - m_new), 0.0)
            alpha = jnp.exp(m - m_new)
            l = alpha * l + jnp.sum(p, axis=1, keepdims=True)
            acc = alpha * acc + jnp.dot(p.astype(BF16), vb16, preferred_element_type=F32)
            new.append((m_new, l, acc))
        return tuple(new)

    init = (jnp.full((tq, 1), NEG_BIG, F32), jnp.zeros((tq, 1), F32), jnp.zeros((tq, LANES), F32))
    state = (init, init)

    def step(i, st):
        c0 = pl.multiple_of(i * tkb, tkb)
        if prompt:
            return visit(st, kn_ref[pl.ds(c0, tkb), :], vn_ref[pl.ds(c0, tkb), :], m_ref[:, pl.ds(c0, tkb)])
        return visit(st, kc_ref[0, 0, pl.ds(c0, tkb), :], vc_ref[0, 0, pl.ds(c0, tkb), :],
                     m_ref[:, pl.ds(c0, tkb)])
    state = lax.fori_loop(0, n_blk, step, state)

    if not prompt:
        pad = jnp.zeros((tkb - tq, LANES), F32)
        knew = jnp.concatenate([kn_ref[...], pad], axis=0)
        vnew = jnp.concatenate([vn_ref[...], pad], axis=0)
        state = visit(state, knew, vnew, m_ref[:, pl.ds(n_cache * tkb, tkb)])

    lane = lax.broadcasted_iota(jnp.int32, (tq, LANES), 1)
    o0 = state[0][2] / state[0][1]
    o1 = state[1][2] / state[1][1]
    o_ref[...] = jnp.where(lane < HD, o0, o1)


def _dsa_attn_prompt(qn, kn, z, mask, n_rows):
    tq, tkb = 256, min(1024, n_rows)
    vblk = OFF_DV // LANES
    return pl.pallas_call(
        functools.partial(_dsa_attn_body, tq=tq, tkb=tkb, n_cache=0, prompt=True),
        grid=(NH // 2, n_rows // tq),
        in_specs=[pl.BlockSpec((tq, 2 * LANES), lambda p, i: (i, p)),
                  pl.BlockSpec((tq, n_rows), lambda p, i: (i, 0)),
                  pl.BlockSpec((n_rows, LANES), lambda p, i: (0, p)),
                  pl.BlockSpec((n_rows, LANES), lambda p, i: (0, vblk + p))],
        out_specs=pl.BlockSpec((tq, LANES), lambda p, i: (i, p)),
        out_shape=jax.ShapeDtypeStruct((n_rows, BW), F32),
        compiler_params=_cparams(("parallel", "arbitrary")),
        name="dsa_attn_prompt",
    )(qn, mask, kn, z)


def _dsa_attn_sample(qn, kn, z, mask, cache_k, cache_v, layer, row_off, nb, L):
    P = cache_k.shape[2]
    tkb = mask.shape[1] - P
    vblk = OFF_DV // LANES
    rb = row_off // L
    return pl.pallas_call(
        functools.partial(_dsa_attn_body, tq=L, tkb=tkb, n_cache=P // tkb, prompt=False),
        grid=(nb, NH // 2),
        in_specs=[pl.BlockSpec((L, 2 * LANES), lambda b, p: (rb + b, p)),
                  pl.BlockSpec((L, P + tkb), lambda b, p: (b, 0)),
                  pl.BlockSpec((L, LANES), lambda b, p: (rb + b, p)),
                  pl.BlockSpec((L, LANES), lambda b, p: (rb + b, vblk + p)),
                  pl.BlockSpec((1, 1, P, LANES), lambda b, p: (layer, b, 0, p)),
                  pl.BlockSpec((1, 1, P, LANES), lambda b, p: (layer, b, 0, p))],
        out_specs=pl.BlockSpec((L, LANES), lambda b, p: (b, p)),
        out_shape=jax.ShapeDtypeStruct((nb * L, BW), F32),
        compiler_params=_cparams(("parallel", "parallel")),
        name="dsa_attn_sample",
    )(qn, mask, kn, z, cache_k, cache_v)


def _gla_consts(C):
    nlev = int(math.log2(C))
    t = np.arange(C)
    cum = (t[:, None] >= t[None, :]).astype(np.float32)
    sel = np.zeros((nlev * C, C), np.float32)
    for l in range(nlev):
        m = 1 << l
        mid = (t // (2 * m)) * 2 * m + m
        sel[l * C + t, mid - 1] = 1.0
    return jnp.asarray(cum, BF16), jnp.asarray(sel, BF16)


def _gla_body(gq_ref, gk_ref, gv_ref, gr_ref, ga_ref, a2_ref, ab_ref, on_ref, cum_ref, sel_ref,
              s0_ref, o_ref, sout_ref, st_ref, *, C, n_chunks):
    ci = pl.program_id(1)
    nlev = int(math.log2(C))

    @pl.when(ci == 0)
    def _():
        st_ref[...] = s0_ref[0]

    x = jnp.dot(ga_ref[...].astype(BF16), a2_ref[...], preferred_element_type=F32) + ab_ref[...]
    g = _neg_softplus(-x) / GLA_GATE_NORM
    b = _split_dot_left(cum_ref[...], g)
    c_all = _split_dot_left(sel_ref[...], b)

    row = lax.broadcasted_iota(jnp.int32, (C, C), 0)
    col = lax.broadcasted_iota(jnp.int32, (C, C), 1)
    rowl = lax.broadcasted_iota(jnp.int32, (C, LANES), 0)

    for h in range(GLA_H):
        sl = slice(h * LANES, (h + 1) * LANES)
        q = gq_ref[:, sl] * (HD ** -0.5)
        k = gk_ref[:, sl]
        v = gv_ref[:, sl]
        bh = b[:, sl]
        a = jnp.where(row == col, jnp.sum(q * k, axis=1, keepdims=True), 0.0)
        for l in range(nlev):
            m = 1 << l
            later = (rowl & m) != 0
            ch = c_all[l * C:(l + 1) * C, sl]
            qe = jnp.where(later, q * jnp.exp(jnp.where(later, bh - ch, 0.0)), 0.0)
            ke = jnp.where(later, 0.0, k * jnp.exp(jnp.where(later, 0.0, ch - bh)))
            al = _dot_nt(qe.astype(BF16), ke.astype(BF16))
            a = a + jnp.where((row // (2 * m)) == (col // (2 * m)), al, 0.0)
        st = st_ref[h]
        o = jnp.dot(a.astype(BF16), v.astype(BF16), preferred_element_type=F32)
        o = o + _dot_nt((q * jnp.exp(bh)).astype(BF16), st.astype(BF16))
        b_last = bh[C - 1:C, :]
        kx = k * jnp.exp(b_last - bh)
        st_ref[h] = st * jnp.exp(b_last) + _dot_tn(v.astype(BF16), kx.astype(BF16))
        on = o * lax.rsqrt(jnp.mean(o * o, axis=-1, keepdims=True) + NORM_EPS) * on_ref[...]
        gr = gr_ref[:, sl]
        o_ref[:, sl] = on * (gr * jax.nn.sigmoid(gr))

    @pl.when(ci == n_chunks - 1)
    def _():
        sout_ref[0] = st_ref[...]


def _gla(z, a2p, abp, onorm, s0t, row_off, n_seq, L):
    C = min(CHUNK, L)
    n_chunks = L // C
    rb = row_off // C
    cum, sel = _gla_consts(C)
    blk = lambda off: pl.BlockSpec((C, BW), lambda s, c: (rb + s * n_chunks + c, off // BW))
    full = lambda a: pl.BlockSpec(a.shape, lambda s, c: (0,) * a.ndim)
    return pl.pallas_call(
        functools.partial(_gla_body, C=C, n_chunks=n_chunks),
        grid=(n_seq, n_chunks),
        in_specs=[blk(OFF_GQ), blk(OFF_GK), blk(OFF_GV), blk(OFF_GR),
                  pl.BlockSpec((C, LANES), lambda s, c: (rb + s * n_chunks + c, OFF_GA // LANES)),
                  full(a2p), full(abp), full(onorm), full(cum), full(sel),
                  pl.BlockSpec((1, GLA_H, GLA_DV, LANES), lambda s, c: (s, 0, 0, 0))],
        out_specs=[pl.BlockSpec((C, BW), lambda s, c: (s * n_chunks + c, 0)),
                   pl.BlockSpec((1, GLA_H, GLA_DV, LANES), lambda s, c: (s, 0, 0, 0))],
        out_shape=[jax.ShapeDtypeStruct((n_seq * L, BW), F32),
                   jax.ShapeDtypeStruct((n_seq, GLA_H, GLA_DV, LANES), F32)],
        scratch_shapes=[pltpu.VMEM((GLA_H, GLA_DV, LANES), F32)],
        compiler_params=_cparams(("parallel", "arbitrary")),
        name="gla",
    )(z, z, z, z, z, a2p, abp, onorm, cum, sel, s0t)


def _rw_prep_body(z_ref, head_ref, mu_ref, w0_ref, a0_ref, kkp_ref, kap_ref, rkp_ref,
                  w2_ref, a2_ref, g2_ref, seg_ref,
                  r_ref, w_ref, k_ref, v_ref, nkk_ref, kka_ref, gate_ref, bonus_ref, *, grp):
    zz = z_ref[...]
    tm, wd = zz.shape
    heads = head_ref[...]
    head_rows = jnp.concatenate([jnp.broadcast_to(heads[g:g + 1, :], (grp, wd))
                                 for g in range(tm // grp)], axis=0)
    row = lax.broadcasted_iota(jnp.int32, (tm, wd), 0)
    prev = jnp.where(row % grp == 0, head_rows, pltpu.roll(zz, 1, axis=0))
    zm = zz + mu_ref[...] * (prev - zz)
    rr = zm[:, 0:BW]
    rk = zm[:, BW:2 * BW]
    rv = zm[:, 2 * BW:3 * BW]
    lora = zm[:, 3 * BW:3 * BW + LANES]
    gl = zm[:, 3 * BW + LANES:3 * BW + 2 * LANES]
    xw = w0_ref[...] + jnp.dot(jnp.tanh(lora).astype(BF16), w2_ref[...], preferred_element_type=F32)
    wlog = _neg_softplus(-xw) - 0.5
    a = jax.nn.sigmoid(a0_ref[...] + jnp.dot(lora.astype(BF16), a2_ref[...], preferred_element_type=F32))
    gate = jnp.dot(jax.nn.sigmoid(gl).astype(BF16), g2_ref[...], preferred_element_type=F32)
    kkf = rk * kkp_ref[...]
    nrm = jnp.sqrt(_split_dot(kkf * kkf, seg_ref[...]))
    kk = kkf / jnp.maximum(nrm, 1e-12)
    kmod = rk * (1.0 + (a - 1.0) * kap_ref[...])
    coef = _split_dot(rr * kmod * rkp_ref[...], seg_ref[...])
    r_ref[...] = rr
    w_ref[...] = -jnp.exp(wlog)
    k_ref[...] = kmod
    v_ref[...] = rv
    nkk_ref[...] = -kk
    kka_ref[...] = kk * a
    gate_ref[...] = gate
    bonus_ref[...] = coef * rv


def _rw_prep(z, heads, grp, mu, w0, a0, kkp, kap, rkp, w2p, a2p, g2, seg):
    T = z.shape[0]
    tm = SUBLANES * grp
    assert T % tm == 0, (T, tm)
    W = 4 * BW
    par = lambda a: pl.BlockSpec(a.shape, lambda i: (0,) * a.ndim)
    out = pl.BlockSpec((tm, BW), lambda i: (i, 0))
    return pl.pallas_call(
        functools.partial(_rw_prep_body, grp=grp),
        grid=(T // tm,),
        in_specs=[pl.BlockSpec((tm, W), lambda i: (i, 0)), pl.BlockSpec((SUBLANES, W), lambda i: (i, 0)),
                  par(mu), par(w0), par(a0), par(kkp), par(kap), par(rkp),
                  par(w2p), par(a2p), par(g2), par(seg)],
        out_specs=[out] * 8,
        out_shape=[jax.ShapeDtypeStruct((T, BW), F32)] * 8,
        compiler_params=_cparams(("parallel",)),
        name="rwkv_prep",
    )(z, heads, mu, w0, a0, kkp, kap, rkp, w2p, a2p, g2, seg)


def _rw_rec_body(r_ref, w_ref, k_ref, v_ref, nkk_ref, kka_ref, g_ref, s0_ref,
                 ot_ref, sout_ref, s_ref, *, TT, L, NS, n_tiles):
    ti = pl.program_id(0)
    npair = NH // 2
    R = NS * npair * HD
    gm = g_ref[...]
    lane = lax.broadcasted_iota(jnp.int32, (R, LANES), 1)
    rowi = lax.broadcasted_iota(jnp.int32, (R, LANES), 0)
    eye2 = (lane % HD) == (rowi % HD)
    seq_lane0 = (rowi // (npair * HD)) * L
    refs =(r_ref, w_ref, k_ref, v_ref, nkk_ref, kka_ref)

    def token_group(gi, carry):
        s, o_lo, o_hi = carry
        t0 = pl.multiple_of(gi * SUBLANES, SUBLANES)
        blk = [[ref[pl.ds(q * L + t0, SUBLANES), :] for q in range(NS)] for ref in refs]
        for j in range(SUBLANES):
            def stacked(x):
                return jnp.concatenate(
                    [jnp.broadcast_to(x[q][j:j + 1, p * LANES:(p + 1) * LANES], (HD, LANES))
                     for q in range(NS) for p in range(npair)], axis=0)
            rr, ww, kk, vv, nk, ka = [stacked(x) for x in blk]
            sa = _split_dot(s * nk, gm)
            vb = jnp.dot(jnp.where(eye2, vv, 0.0).astype(BF16), gm, preferred_element_type=F32)
            s = s * ww + sa * ka + vb * kk
            oo = jnp.dot((s * rr).astype(BF16), gm, preferred_element_type=F32)
            hit = lane == seq_lane0 + t0 + j
            o_lo = jnp.where(hit, oo, o_lo)
            o_hi = jnp.where(hit, pltpu.roll(oo, HD, axis=1), o_hi)
        return s, o_lo, o_hi

    if NS == 1:
        @pl.when(ti == 0)
        def _():
            s_ref[...] = s0_ref[0]
        s_in = s_ref[...]
    else:
        s_in = s0_ref[...].reshape(R, LANES)
    zero = jnp.zeros((R, LANES), F32)
    s_out, o_lo, o_hi = lax.fori_loop(0, min(L, TT) // SUBLANES, token_group, (s_in, zero, zero))

    pr = npair * HD
    a1 = sum(o_lo[q * pr:(q + 1) * pr] for q in range(NS))
    a2 = sum(o_hi[q * pr:(q + 1) * pr] for q in range(NS))
    first_half = lax.broadcasted_iota(jnp.int32, (pr, LANES), 1) < HD
    lo = jnp.where(first_half, a1, a2)
    hi = jnp.where(first_half, a2, a1)
    for p in range(npair):
        ot_ref[2 * p * HD:(2 * p + 1) * HD, :] = lo[p * HD:(p + 1) * HD]
        ot_ref[(2 * p + 1) * HD:(2 * p + 2) * HD, :] = hi[p * HD:(p + 1) * HD]
    if NS == 1:
        s_ref[...] = s_out

        @pl.when(ti == n_tiles - 1)
        def _():
            sout_ref[0] = s_out
    else:
        sout_ref[...] = s_out.reshape(NS, pr, LANES)


def _rw_rec(r, w, k, v, nkk, kka, gmat, s0, row_off, n_seq, L):
    TT = LANES
    n_tok = n_seq * L
    n_tiles = n_tok // TT
    rb = row_off // TT
    ns = max(1, TT // L)
    pr = (NH // 2) * HD
    tok = pl.BlockSpec((TT, BW), lambda i: (rb + i, 0))
    if ns == 1:
        sspec = pl.BlockSpec((1, pr, LANES), lambda i: (0, 0, 0))
    else:
        sspec = pl.BlockSpec((ns, pr, LANES), lambda i: (i, 0, 0))
    return pl.pallas_call(
        functools.partial(_rw_rec_body, TT=TT, L=L, NS=ns, n_tiles=n_tiles),
        grid=(n_tiles,),
        in_specs=[tok] * 6 + [pl.BlockSpec((LANES, LANES), lambda i: (0, 0)), sspec],
        out_specs=[pl.BlockSpec((BW, TT), lambda i: (0, i)), sspec],
        out_shape=[jax.ShapeDtypeStruct((BW, n_tok), F32),
                   jax.ShapeDtypeStruct(s0.shape, F32)],
        scratch_shapes=[pltpu.VMEM((pr, LANES), F32)],
        compiler_params=_cparams(("arbitrary",)),
        name="rwkv_recurrence",
    )(r, w, k, v, nkk, kka, gmat, s0)


def _mm3(a, b):
    ah = a.astype(BF16)
    al = (a - ah.astype(F32)).astype(BF16)
    bh = b.astype(BF16)
    bl = (b - bh.astype(F32)).astype(BF16)
    d = lambda x, y: jnp.dot(x, y, preferred_element_type=F32)
    return d(ah, bh) + (d(ah, bl) + d(al, bh))


def _rw_chunk_body(r_ref, lw_ref, k_ref, v_ref, nkk_ref, kka_ref, s0_ref, o_ref, sout_ref, s_ref,
                   *, TT, L, n_tiles):
    ti = pl.program_id(0)
    npair = NH // 2
    nblk = TT // RW_BLK
    row = lax.broadcasted_iota(jnp.int32, (TT, TT), 0)
    col = lax.broadcasted_iota(jnp.int32, (TT, TT), 1)
    same = (row // RW_BLK) == (col // RW_BLK)
    strict = same & (col < row)
    incl = same & (col <= row)
    tri = jnp.where(incl, 1.0, 0.0).astype(BF16)
    last = jnp.where(same & (col % RW_BLK == RW_BLK - 1), 1.0, 0.0).astype(BF16)
    eye = jnp.where(row == col, 1.0, 0.0)
    pair_diag = (row < HD) == (col < HD)
    lo_half = lax.broadcasted_iota(jnp.int32, (TT, LANES), 1) < HD
    lo_half_b = lax.broadcasted_iota(jnp.int32, (RW_BLK, LANES), 1) < HD

    r, lw, k, v = r_ref[...], lw_ref[...], k_ref[...], v_ref[...]
    nk, ka = nkk_ref[...], kka_ref[...]
    lc = _split_dot_left(tri, lw)
    le = _split_dot_left(last, lc)
    nkt = nk * jnp.exp(lc - lw)
    rt = r * jnp.exp(lc)
    inv = jnp.exp(-lc)
    kah = ka * inv
    kh = k * inv
    rem = jnp.exp(le - lc)
    kaw = (ka * rem).astype(BF16)
    kw = (k * rem).astype(BF16)
    wb = jnp.exp(le)

    sls = [slice(p * LANES, (p + 1) * LANES) for p in range(npair)]
    v16 = [v[:, sl].astype(BF16) for sl in sls]
    heads = [(p, e) for p in range(npair) for e in range(2)]
    grams = []
    for p, e in heads:
        hm = lo_half if e == 0 else jnp.logical_not(lo_half)
        lhs = jnp.concatenate([jnp.where(hm, nkt[:, sls[p]], 0.0), jnp.where(hm, rt[:, sls[p]], 0.0)],
                              axis=0).astype(BF16)
        rhs = jnp.concatenate([kah[:, sls[p]], kh[:, sls[p]]], axis=0).astype(BF16)
        grams.append(_dot_nt(lhs, rhs))
    n1 = [jnp.where(strict, g[:TT, :TT], 0.0) for g in grams]
    bt = [jnp.where(strict, g[:TT, TT:], 0.0).astype(BF16) for g in grams]
    cmat = [jnp.where(incl, g[TT:, :TT], 0.0).astype(BF16) for g in grams]
    et = [jnp.where(incl, g[TT:, TT:], 0.0).astype(BF16) for g in grams]
    n2 = [_mm3(a, a) for a in n1]
    n4 = [_mm3(a, a) for a in n2]
    n8 = [_mm3(a, a) for a in n4]
    tinv = [eye + a for a in n1]
    for power in (n2, n4, n8):
        tinv = [t + _mm3(t, a) for t, a in zip(tinv, power)]
    vb_h = [jnp.dot(b_, v16[p], preferred_element_type=F32) for b_, (p, e) in zip(bt, heads)]
    ev_h = [jnp.dot(e_, v16[p], preferred_element_type=F32) for e_, (p, e) in zip(et, heads)]
    vb = [jnp.where(lo_half, vb_h[2 * p], vb_h[2 * p + 1]) for p in range(npair)]
    ev = [jnp.where(lo_half, ev_h[2 * p], ev_h[2 * p + 1]) for p in range(npair)]

    if L >= TT:
        @pl.when(ti == 0)
        def _():
            s_ref[...] = s0_ref[0]
        state = {(0, p): s_ref[p] for p in range(npair)}
        rounds = [[j] for j in range(nblk)]
    else:
        bps = L // RW_BLK
        state = {(q, p): s0_ref[q, p] for q in range(TT // L) for p in range(npair)}
        rounds = [[q * bps + i for q in range(TT // L)] for i in range(bps)]
    sa_rows = [[None] * nblk for _ in range(npair)]
    os_rows = [[None] * nblk for _ in range(npair)]
    for blocks in rounds:
        items = [(j, p) for j in blocks for p in range(npair)]
        seq = lambda j: (j * RW_BLK) // L if L < TT else 0
        xs = {}
        for j, p in items:
            bs = slice(j * RW_BLK, (j + 1) * RW_BLK)
            s = state[(seq(j), p)]
            s_hi = s.astype(BF16)
            s_lo = (s - s_hi.astype(F32)).astype(BF16)
            lhs = jnp.concatenate([nkt[bs, sls[p]], rt[bs, sls[p]]], axis=0).astype(BF16)
            xs[(j, p)] = (jnp.dot(lhs, s_hi, preferred_element_type=F32)
                          + jnp.dot(lhs, s_lo, preferred_element_type=F32))
        sa = {}
        for j, p in items:
            b0 = j * RW_BLK
            bs = slice(b0, b0 + RW_BLK)
            parts = [xs[(j, p)][:RW_BLK] + vb[p][bs]]
            if b0:
                parts.insert(0, jnp.zeros((b0, LANES), F32))
            if TT - b0 - RW_BLK:
                parts.append(jnp.zeros((TT - b0 - RW_BLK, LANES), F32))
            xfull = jnp.concatenate(parts, axis=0)
            sa[(j, p)] = jnp.where(lo_half_b, _mm3(tinv[2 * p][bs], xfull),
                                   _mm3(tinv[2 * p + 1][bs], xfull))
        for j, p in items:
            b0 = j * RW_BLK
            bs = slice(b0, b0 + RW_BLK)
            u = (_dot_tn(kaw[bs, sls[p]], sa[(j, p)].astype(BF16)) + _dot_tn(kw[bs, sls[p]], v16[p][bs]))
            wcol = jnp.broadcast_to(wb[b0:b0 + 1, sls[p]], (LANES, LANES)).T
            state[(seq(j), p)] = wcol * state[(seq(j), p)] + jnp.where(pair_diag, u, 0.0)
            sa_rows[p][j] = sa[(j, p)]
            os_rows[p][j] = xs[(j, p)][RW_BLK:]
    if L < TT:
        for (q, p), s in state.items():
            sout_ref[q, p] = s

    for p in range(npair):
        sa_t = jnp.concatenate(sa_rows[p], axis=0).astype(BF16)
        o_sa = jnp.where(lo_half, jnp.dot(cmat[2 * p], sa_t, preferred_element_type=F32),
                         jnp.dot(cmat[2 * p + 1], sa_t, preferred_element_type=F32))
        o_ref[:, sls[p]] = jnp.concatenate(os_rows[p], axis=0) + o_sa + ev[p]
    if L >= TT:
        for p in range(npair):
            s_ref[p] = state[(0, p)]

        @pl.when(ti == n_tiles - 1)
        def _():
            for p in range(npair):
                sout_ref[0, p] = state[(0, p)]


def _rw_chunk(r, lw, k, v, nkk, kka, s0, row_off, n_seq, L):
    TT = LANES
    assert L % RW_BLK == 0 and (L % TT == 0 or TT % L == 0)
    n_tok = n_seq * L
    n_tiles = n_tok // TT
    rb = row_off // TT
    ns = max(1, TT // L)
    tok = pl.BlockSpec((TT, BW), lambda i: (rb + i, 0))
    if ns == 1:
        sspec = pl.BlockSpec((1, NH // 2, LANES, LANES), lambda i: (0, 0, 0, 0))
    else:
        sspec = pl.BlockSpec((ns, NH // 2, LANES, LANES), lambda i: (i, 0, 0, 0))
    return pl.pallas_call(
        functools.partial(_rw_chunk_body, TT=TT, L=L, n_tiles=n_tiles),
        grid=(n_tiles,),
        in_specs=[tok] * 6 + [sspec],
        out_specs=[pl.BlockSpec((TT, BW), lambda i: (i, 0)), sspec],
        out_shape=[jax.ShapeDtypeStruct((n_tok, BW), F32),
                   jax.ShapeDtypeStruct(s0.shape, F32)],
        scratch_shapes=[pltpu.VMEM((NH // 2, LANES, LANES), F32)],
        compiler_params=_cparams(("arbitrary",)),
        name="rwkv_chunk",
    )(r, lw, k, v, nkk, kka, s0)


def _rw_post_body(o_ref, bonus_ref, gate_ref, lnw_ref, lnb_ref, seg_ref, out_ref):
    o = o_ref[...]
    mu = _split_dot(o, seg_ref[...]) * (1.0 / HD)
    d = o - mu
    var = _split_dot(d * d, seg_ref[...]) * (1.0 / HD)
    on = d * lax.rsqrt(var + RW_LN_EPS) * lnw_ref[...] + lnb_ref[...]
    out_ref[...] = (on + bonus_ref[...]) * gate_ref[...]


def _rw_post(o, bonus, gate, lnw, lnb, seg):
    T = o.shape[0]
    tm = _pick(T, (512, 256, 128))
    blk = pl.BlockSpec((tm, BW), lambda i: (i, 0))
    par = lambda a: pl.BlockSpec(a.shape, lambda i: (0,) * a.ndim)
    return pl.pallas_call(
        _rw_post_body,
        grid=(T // tm,),
        in_specs=[blk, blk, blk, par(lnw), par(lnb), par(seg)],
        out_specs=blk,
        out_shape=jax.ShapeDtypeStruct((T, BW), F32),
        compiler_params=_cparams(("parallel",)),
        name="rwkv_post",
    )(o, bonus, gate, lnw, lnb, seg)


def _heads_alt(w):
    d = w.shape[0]
    w4 = w.reshape(d, NH // 2, 2, HD)
    zero = jnp.zeros((d, NH // 2, HD), w.dtype)
    ev = jnp.concatenate([w4[:, :, 0], zero], -1)
    od = jnp.concatenate([zero, w4[:, :, 1]], -1)
    return jnp.stack([ev, od], 2).reshape(d, NH * LANES)


def _heads_lo(w, nh):
    d = w.shape[0]
    w3 = w.reshape(d, nh, HD)
    return jnp.concatenate([w3, jnp.zeros_like(w3)], -1).reshape(d, nh * LANES)


def _padc(w, n):
    return jnp.pad(w, ((0, 0), (0, n - w.shape[1])))


def _rw_reorder(a):
    o = 0
    parts = {}
    for name, wd in (("rr", BW), ("wl", RW_LORA), ("rk", BW), ("rv", BW), ("al", RW_LORA), ("gl", RW_GATE_LORA)):
        parts[name] = a[..., o:o + wd]
        o += wd
    return jnp.concatenate([parts[n] for n in ("rr", "rk", "rv", "wl", "al", "gl")], axis=-1)


def _rw_unorder(a):
    rr, rk, rv = a[..., 0:BW], a[..., BW:2 * BW], a[..., 2 * BW:3 * BW]
    wl = a[..., 3 * BW:3 * BW + RW_LORA]
    al = a[..., 3 * BW + RW_LORA:3 * BW + 2 * RW_LORA]
    gl = a[..., 3 * BW + 2 * RW_LORA:RW_COLS]
    return jnp.concatenate([rr, wl, rk, rv, al, gl], axis=-1)


def _relayout_w_in(w, d_model):
    w = w.astype(BF16)
    o = 0

    def take(n):
        nonlocal o
        s = w[:, o:o + n]
        o += n
        return s
    sq, sk, sv = take(BW), take(BW), take(BW)
    dq, dk, dv, qi, ki, wi = take(BW), take(BW), take(BW), take(BW), take(HD), take(NH)
    gq, gk = take(GLA_H * HD), take(GLA_H * HD)
    gv, gr, ga = take(BW), take(BW), take(GLA_LOWRANK)
    rw = take(RW_COLS)
    gate = take(4 * d_model)
    cols = [_padc(_rw_reorder(rw), 4 * BW), _heads_alt(sq), sk, sv, _heads_alt(dq), dk, dv,
            _heads_lo(qi, NH), _heads_lo(gq, GLA_H), _heads_lo(gk, GLA_H), gv, gr,
            _padc(ki, LANES), _padc(wi, LANES), _padc(ga, 2 * LANES), gate]
    return jnp.concatenate(cols, axis=1)


def _relayout_ffn(w_up, w_down, tf):
    d, f2 = w_up.shape
    f = f2 // 2
    fp = -(-f // tf) * tf
    g = _padc(w_up[:, :f].astype(BF16), fp).reshape(d, fp // tf, tf)
    u = _padc(w_up[:, f:].astype(BF16), fp).reshape(d, fp // tf, tf)
    wup = jnp.concatenate([g, u], axis=2).reshape(d, 2 * fp)
    wdn = jnp.pad(w_down.astype(BF16), ((0, fp - f), (0, 0)))
    return wup, wdn


def _gain_alt(g):
    return jnp.tile(g, 2 * NH)[None, :]


def kernel(x_prompt, x_sample, cache_sb_k, cache_sb_v, cache_dsa_k, cache_dsa_v, cache_dsa_kidx, state_gla, state_rwkv, state_rwkv_shift, ffn1_norm, ffn1_up, ffn1_down, mix_norm, w_in, sb_qnorm, sb_knorm, dsa_qnorm, dsa_knorm, gla_a2, gla_ab, gla_onorm, rwkv_mu, rwkv_w0, rwkv_w2, rwkv_a0, rwkv_a2, rwkv_g2, rwkv_kk, rwkv_ka, rwkv_rk, rwkv_lnw, rwkv_lnb, w_branch, w_out, ffn2_norm, ffn2_up, ffn2_down):
    depth = w_in.shape[0]
    bp, lp, d_model = x_prompt.shape
    nb, ls, _ = x_sample.shape
    assert bp == 1, "prompt group is one sequence"
    tp = bp * lp
    ts = nb * ls
    past = cache_sb_k.shape[2]
    tf = 512

    x = jnp.concatenate([x_prompt.reshape(tp, d_model), x_sample.reshape(ts, d_model)], axis=0)
    ck_sb = cache_sb_k.reshape(depth, nb, past, BW)
    cv_sb = cache_sb_v.reshape(depth, nb, past, BW)
    ck_dsa = cache_dsa_k.reshape(depth, nb, past, BW)
    cv_dsa = cache_dsa_v.reshape(depth, nb, past, BW)

    bd = jnp.asarray(np.kron(np.eye(2 * NH), np.full((HD, HD), 1.0 / HD)), BF16)
    seg = jnp.asarray(np.kron(np.eye(NH), np.ones((HD, HD))), BF16)
    gmat = jnp.asarray(np.kron(np.eye(2), np.ones((HD, HD))), BF16)
    topk_p = min(TOPK_MAX, lp // 4)
    topk_s = min(TOPK_MAX, (past + ls) // 4)

    new_p = [[] for _ in range(8)]
    new_s = [[] for _ in range(8)]
    for i in range(depth):
        wup1, wdn1 = _relayout_ffn(ffn1_up[i], ffn1_down[i], tf)
        wup2, wdn2 = _relayout_ffn(ffn2_up[i], ffn2_down[i], tf)
        win = _relayout_w_in(w_in[i], d_model)

        x = _ffn(x, ffn1_norm[i][None], wup1, wdn1, tf)
        z = _proj(x, mix_norm[i][None], win)

        sqn, skn, dqn, dkn = _qknorm(z, _gain_alt(sb_qnorm[i]), jnp.tile(sb_knorm[i], NH)[None],
                                     _gain_alt(dsa_qnorm[i]), jnp.tile(dsa_knorm[i], NH)[None], bd)
        o_sb = jnp.concatenate([
            _sb_prompt(sqn, skn, z, tp),
            _sb_sample(sqn, skn, z, ck_sb, cv_sb, i, tp, nb, ls)], axis=0)
        mask_p = _dsa_sel_prompt(z, tp, topk_p)
        mask_s = _dsa_sel_sample(z, cache_dsa_kidx, i, tp, nb, ls, topk_s)
        o_dsa = jnp.concatenate([
            _dsa_attn_prompt(dqn, dkn, z, mask_p, tp),
            _dsa_attn_sample(dqn, dkn, z, mask_s, ck_dsa, cv_dsa, i, tp, nb, ls)], axis=0)

        a2p = jnp.pad(_heads_lo(gla_a2[i], GLA_H), ((0, LANES - GLA_LOWRANK), (0, 0))).astype(BF16)
        abp = _heads_lo(gla_ab[i][None], GLA_H)
        onorm = gla_onorm[i][None]

        def gla_state_in(s):
            return jnp.pad(jnp.swapaxes(s, 2, 3), ((0, 0), (0, 0), (0, 0), (0, LANES - HD)))

        def gla_state_out(s):
            return jnp.swapaxes(s[..., :HD], 2, 3)
        og_p, sg_p = _gla(z, a2p, abp, onorm, jnp.zeros((bp, GLA_H, GLA_DV, LANES), F32), 0, bp, lp)
        og_s, sg_s = _gla(z, a2p, abp, onorm, gla_state_in(state_gla[i]), tp, nb, ls)
        o_gla = jnp.concatenate([og_p, og_s], axis=0)

        zrw = z[:, :RW_COLS]
        shift_s = _rw_reorder(state_rwkv_shift[i])
        heads = jnp.concatenate([jnp.zeros((1, 4 * BW), F32), z[ls - 1:tp - 1:ls, :4 * BW],
                                 _padc(shift_s.reshape(nb, RW_COLS), 4 * BW)], axis=0)
        mu = _padc(_rw_reorder(rwkv_mu[i])[None], 4 * BW)
        w2p = jnp.concatenate([rwkv_w2[i], jnp.zeros_like(rwkv_a2[i])], axis=0).astype(BF16)
        a2q = jnp.concatenate([jnp.zeros_like(rwkv_w2[i]), rwkv_a2[i]], axis=0).astype(BF16)
        r_, w_, k_, v_, nkk_, kka_, gate_, bonus_ = _rw_prep(
            z, heads, ls, mu, rwkv_w0[i][None], rwkv_a0[i][None], rwkv_kk[i][None], rwkv_ka[i][None],
            rwkv_rk[i][None], w2p, a2q, rwkv_g2[i].astype(BF16), seg)

        def rw_state_in(s):
            n = s.shape[0]
            st = jnp.swapaxes(s, 2, 3).reshape(n, NH // 2, 2, HD, HD)
            zero = jnp.zeros((n, NH // 2, HD, HD), s.dtype)
            return jnp.concatenate([jnp.concatenate([st[:, :, 0], zero], axis=-1),
                                    jnp.concatenate([zero, st[:, :, 1]], axis=-1)], axis=-2)

        def rw_state_out(s):
            n = s.shape[0]
            st = jnp.stack([s[:, :, :HD, :HD], s[:, :, HD:, HD:]], axis=2)
            return jnp.swapaxes(st, 3, 4).reshape(n, NH, HD, HD)
        o_rp, sr_p = _rw_chunk(r_, w_, k_, v_, nkk_, kka_, jnp.zeros((bp, NH // 2, LANES, LANES), F32), 0, bp, lp)
        o_rs, sr_s = _rw_chunk(r_, w_, k_, v_, nkk_, kka_, rw_state_in(state_rwkv[i]), tp, nb, ls)
        o_r = jnp.concatenate([o_rp, o_rs], axis=0)
        o_rw = _rw_post(o_r, bonus_, gate_, rwkv_lnw[i][None], rwkv_lnb[i][None], seg)

        x = _merge(x, z, o_sb, o_dsa, o_gla, o_rw, w_branch[i].astype(BF16), w_out[i].astype(BF16))
        x = _ffn(x, ffn2_norm[i][None], wup2, wdn2, tf)

        sv = z[:, OFF_SV:OFF_SV + BW]
        dv = z[:, OFF_DV:OFF_DV + BW]
        ki = z[:, OFF_KI:OFF_KI + HD]
        for dst, (lo, n, l) in ((new_p, (0, bp, lp)), (new_s, (tp, nb, ls))):
            hi = lo + n * l
            dst[0].append(skn[lo:hi].reshape(n, l, NH, HD))
            dst[1].append(sv[lo:hi].reshape(n, l, NH, HD))
            dst[2].append(dkn[lo:hi].reshape(n, l, NH, HD))
            dst[3].append(dv[lo:hi].reshape(n, l, NH, HD))
            dst[4].append(ki[lo:hi].reshape(n, l, HD))
            dst[7].append(_rw_unorder(zrw[lo:hi].reshape(n, l, RW_COLS)[:, -1:]))
        new_p[5].append(gla_state_out(sg_p))
        new_s[5].append(gla_state_out(sg_s))
        new_p[6].append(rw_state_out(sr_p))
        new_s[6].append(rw_state_out(sr_s))

    outs_p = [jnp.stack(l, axis=0) for l in new_p]
    outs_s = [jnp.stack(l, axis=0) for l in new_s]
    y_p = x[:tp].reshape(bp, lp, d_model)
    y_s = x[tp:].reshape(nb, ls, d_model)
    return (y_p, y_s, *outs_p, *outs_s)
```

```python
import functools
import math

import jax
import jax.numpy as jnp
import numpy as np
from jax import lax
from jax.experimental import pallas as pl
from jax.experimental.pallas import tpu as pltpu

F32 = jnp.float32
BF16 = jnp.bfloat16

LANES = 128
SUBLANES = 8
HD = 64
NH = 8
BW = NH * HD
GLA_H = 4
GLA_DV = 128
GLA_LOWRANK = 16
GLA_GATE_NORM = 16.0
CHUNK = 64
TOPK_MAX = 256
RW_LORA = 64
RW_GATE_LORA = 128
RW_BLK = 16
RW_COLS = 3 * BW + 2 * RW_LORA + RW_GATE_LORA
NORM_EPS = 1e-6
RW_LN_EPS = 64e-5
NEG_BIG = -1e30
SB_UNDERFLOW = -104.0
VMEM_LIMIT = 56 * 1024 * 1024

OFF_RW = 0
OFF_SQ = 2048
OFF_SK = 3072
OFF_SV = 3584
OFF_DQ = 4096
OFF_DK = 5120
OFF_DV = 5632
OFF_QI = 6144
OFF_GQ = 7168
OFF_GK = 7680
OFF_GV = 8192
OFF_GR = 8704
OFF_KI = 9216
OFF_WI = 9344
OFF_GA = 9472
OFF_GATE = 9728


def _cparams(sem):
    return pltpu.CompilerParams(dimension_semantics=sem, vmem_limit_bytes=VMEM_LIMIT)


def _pick(n, prefs):
    for p in prefs:
        if n % p == 0:
            return p
    raise ValueError(f"no tile for {n} in {prefs}")


def _split_dot(a, b):
    hi = a.astype(BF16)
    lo = (a - hi.astype(F32)).astype(BF16)
    return (jnp.dot(hi, b, preferred_element_type=F32)
            + jnp.dot(lo, b, preferred_element_type=F32))


def _split_dot_left(a, b):
    hi = b.astype(BF16)
    lo = (b - hi.astype(F32)).astype(BF16)
    return (jnp.dot(a, hi, preferred_element_type=F32)
            + jnp.dot(a, lo, preferred_element_type=F32))


def _dot_nt(a, b):
    return lax.dot_general(a, b, (((1,), (1,)), ((), ())), preferred_element_type=F32)


def _dot_tn(a, b):
    return lax.dot_general(a, b, (((0,), (0,)), ((), ())), preferred_element_type=F32)


def _neg_softplus(z):
    return -(jnp.maximum(z, 0.0) + jnp.log(1.0 + jnp.exp(-jnp.abs(z))))


def _ffn_body(x_ref, g_ref, wup_ref, wdn_ref, o_ref, xn_ref, *, tf):
    @pl.when(pl.program_id(1) == 0)
    def _():
        x = x_ref[...]
        ms = jnp.mean(x * x, axis=-1, keepdims=True)
        xn_ref[...] = (x * lax.rsqrt(ms + NORM_EPS) * g_ref[...]).astype(BF16)
        o_ref[...] = x

    gu = jnp.dot(xn_ref[...], wup_ref[...], preferred_element_type=F32)
    gate = gu[:, :tf]
    act = (gate * jax.nn.sigmoid(gate) * gu[:, tf:]).astype(BF16)
    o_ref[...] += 0.5 * jnp.dot(act, wdn_ref[...], preferred_element_type=F32)


def _ffn(x, g, wup, wdn, tf):
    T, D = x.shape
    fp = wdn.shape[0]
    tm = _pick(T, (512, 256, 128))
    return pl.pallas_call(
        functools.partial(_ffn_body, tf=tf),
        grid=(T // tm, fp // tf),
        in_specs=[pl.BlockSpec((tm, D), lambda i, j: (i, 0)),
                  pl.BlockSpec((1, D), lambda i, j: (0, 0)),
                  pl.BlockSpec((D, 2 * tf), lambda i, j: (0, j)),
                  pl.BlockSpec((tf, D), lambda i, j: (j, 0))],
        out_specs=pl.BlockSpec((tm, D), lambda i, j: (i, 0)),
        out_shape=jax.ShapeDtypeStruct((T, D), F32),
        scratch_shapes=[pltpu.VMEM((tm, D), BF16)],
        compiler_params=_cparams(("parallel", "arbitrary")),
        name="ffn",
    )(x, g, wup, wdn)


def _proj_body(x_ref, g_ref, w_ref, o_ref, xn_ref):
    @pl.when(pl.program_id(1) == 0)
    def _():
        x = x_ref[...]
        ms = jnp.mean(x * x, axis=-1, keepdims=True)
        xn_ref[...] = (x * lax.rsqrt(ms + NORM_EPS) * g_ref[...]).astype(BF16)

    o_ref[...] = jnp.dot(xn_ref[...], w_ref[...], preferred_element_type=F32)


def _proj(x, g, w):
    T, D = x.shape
    n = w.shape[1]
    tm = _pick(T, (1024, 512, 256, 128))
    tn = _pick(n, (1280, 512))
    return pl.pallas_call(
        _proj_body,
        grid=(T // tm, n // tn),
        in_specs=[pl.BlockSpec((tm, D), lambda i, j: (i, 0)),
                  pl.BlockSpec((1, D), lambda i, j: (0, 0)),
                  pl.BlockSpec((D, tn), lambda i, j: (0, j))],
        out_specs=pl.BlockSpec((tm, tn), lambda i, j: (i, j)),
        out_shape=jax.ShapeDtypeStruct((T, n), F32),
        scratch_shapes=[pltpu.VMEM((tm, D), BF16)],
        compiler_params=_cparams(("parallel", "arbitrary")),
        name="in_proj",
    )(x, g, w)


def _merge_body(osb_ref, ods_ref, ogl_ref, orw_ref, g0_ref, g1_ref, g2_ref, g3_ref,
                wb_ref, wo_ref, x_ref, o_ref):
    @pl.when(pl.program_id(1) == 0)
    def _():
        o_ref[...] = x_ref[...]

    acc = None
    for n, (b_ref, g_ref) in enumerate(((osb_ref, g0_ref), (ods_ref, g1_ref),
                                        (ogl_ref, g2_ref), (orw_ref, g3_ref))):
        p = jnp.dot(b_ref[...].astype(BF16), wb_ref[n], preferred_element_type=F32)
        t = jax.nn.sigmoid(g_ref[...]) * p
        acc = t if acc is None else acc + t
    o_ref[...] += jnp.dot(acc.astype(BF16), wo_ref[...], preferred_element_type=F32)


def _merge(x, z, o_sb, o_dsa, o_gla, o_rw, wb, wo):
    T, D = x.shape
    tm = _pick(T, (512, 256, 128))
    tc = 512
    nc = D // tc
    gate_blk = OFF_GATE // tc
    bspec = pl.BlockSpec((tm, BW), lambda i, c: (i, 0))
    gspecs = [pl.BlockSpec((tm, tc), functools.partial(lambda i, c, n: (i, gate_blk + n * nc + c), n=n))
              for n in range(4)]
    return pl.pallas_call(
        _merge_body,
        grid=(T // tm, nc),
        in_specs=[bspec, bspec, bspec, bspec] + gspecs + [
            pl.BlockSpec((4, BW, tc), lambda i, c: (0, 0, c)),
            pl.BlockSpec((tc, D), lambda i, c: (c, 0)),
            pl.BlockSpec((tm, D), lambda i, c: (i, 0))],
        out_specs=pl.BlockSpec((tm, D), lambda i, c: (i, 0)),
        out_shape=jax.ShapeDtypeStruct((T, D), F32),
        compiler_params=_cparams(("parallel", "arbitrary")),
        name="merge",
    )(o_sb, o_dsa, o_gla, o_rw, z, z, z, z, wb, wo, x)


def _qknorm_body(sq_ref, sk_ref, dq_ref, dk_ref, gsq_ref, gsk_ref, gdq_ref, gdk_ref, bd_ref,
                 osq_ref, osk_ref, odq_ref, odk_ref):
    def norm(x, g, n):
        ms = _split_dot(x * x, bd_ref[:n, :n])
        return x * lax.rsqrt(ms + NORM_EPS) * g

    scale = HD ** -0.5
    osq_ref[...] = (norm(sq_ref[...], gsq_ref[...], 2 * BW) * scale).astype(BF16)
    osk_ref[...] = norm(sk_ref[...], gsk_ref[...], BW)
    odq_ref[...] = (norm(dq_ref[...], gdq_ref[...], 2 * BW) * scale).astype(BF16)
    odk_ref[...] = norm(dk_ref[...], gdk_ref[...], BW)


def _qknorm(z, gsq, gsk, gdq, gdk, bd):
    T = z.shape[0]
    tm = _pick(T, (512, 256, 128))
    wq, wk = 2 * BW, BW
    row = lambda w, off: pl.BlockSpec((tm, w), lambda i: (i, off // w))
    par = lambda w: pl.BlockSpec((1, w), lambda i: (0, 0))
    out = lambda w: pl.BlockSpec((tm, w), lambda i: (i, 0))
    return pl.pallas_call(
        _qknorm_body,
        grid=(T // tm,),
        in_specs=[row(wq, OFF_SQ), row(wk, OFF_SK), row(wq, OFF_DQ), row(wk, OFF_DK),
                  par(wq), par(wk), par(wq), par(wk),
                  pl.BlockSpec((wq, wq), lambda i: (0, 0))],
        out_specs=[out(wq), out(wk), out(wq), out(wk)],
        out_shape=[jax.ShapeDtypeStruct((T, wq), BF16), jax.ShapeDtypeStruct((T, wk), F32),
                   jax.ShapeDtypeStruct((T, wq), BF16), jax.ShapeDtypeStruct((T, wk), F32)],
        compiler_params=_cparams(("parallel",)),
        name="qk_norm",
    )(z, z, z, z, gsq, gsk, gdq, gdk, bd)


def _sb_body(*refs, tq, n_cache, tkc):
    if n_cache:
        q_ref, kn_ref, vn_ref, kc_ref, vc_ref, o_ref = refs
    else:
        q_ref, kn_ref, vn_ref, o_ref = refs
    qb = pl.program_id(1) if not n_cache else 0
    qs = [q_ref[:, e * LANES:(e + 1) * LANES] for e in range(2)]

    def tri(n):
        r = lax.broadcasted_iota(jnp.int32, (n, n), 0)
        c = lax.broadcasted_iota(jnp.int32, (n, n), 1)
        return (r > c).astype(BF16)

    def visit(state, kblk, vblk, u, valid):
        kb16 = kblk.astype(BF16)
        vb16 = vblk.astype(BF16)
        new = []
        for e in range(2):
            carry, acc = state[e]
            z = _dot_nt(qs[e], kb16)
            lg = _neg_softplus(z)
            if valid is not None:
                lg = jnp.where(valid, lg, 0.0)
            inner = _split_dot(lg, u)
            a = jnp.exp(z + lg + inner + carry)
            if valid is not None:
                a = jnp.where(valid, a, 0.0)
            acc = acc + jnp.dot(a.astype(BF16), vb16, preferred_element_type=F32)
            carry = carry + inner[:, :1] + lg[:, :1]
            new.append((carry, acc))
        return tuple(new)

    zero = (jnp.zeros((tq, 1), F32), jnp.zeros((tq, LANES), F32))
    state = (zero, zero)

    r = lax.broadcasted_iota(jnp.int32, (tq, tq), 0)
    c = lax.broadcasted_iota(jnp.int32, (tq, tq), 1)
    u_new = tri(tq)
    if n_cache:
        state = visit(state, kn_ref[...], vn_ref[...], u_new, c < r)
    else:
        row0 = pl.multiple_of(qb * tq, tq)
        state = visit(state, kn_ref[pl.ds(row0, tq), :], vn_ref[pl.ds(row0, tq), :], u_new, c < r)

    def live(c):
        i, st = c
        return (i < n_old) & (jnp.max(jnp.maximum(st[0][0], st[1][0])) > SB_UNDERFLOW)

    if not n_cache:
        n_old = qb

        def step(c):
            i, st = c
            r0 = pl.multiple_of((qb - 1 - i) * tq, tq)
            return i + 1, visit(st, kn_ref[pl.ds(r0, tq), :], vn_ref[pl.ds(r0, tq), :], u_new, None)
    else:
        n_old = n_cache
        u_c = tri(tkc)

        def step(c):
            i, st = c
            r0 = pl.multiple_of((n_cache - 1 - i) * tkc, tkc)
            return i + 1, visit(st, kc_ref[0, 0, pl.ds(r0, tkc), :], vc_ref[0, 0, pl.ds(r0, tkc), :],
                                u_c, None)
    _, state = lax.while_loop(live, step, (jnp.int32(0), state))

    lane = lax.broadcasted_iota(jnp.int32, (tq, LANES), 1)
    o_ref[...] = jnp.where(lane < HD, state[0][1], state[1][1])


def _sb_prompt(qn, kn, z, n_rows):
    tq = 256
    vblk = OFF_SV // LANES
    return pl.pallas_call(
        functools.partial(_sb_body, tq=tq, n_cache=0, tkc=0),
        grid=(NH // 2, n_rows // tq),
        in_specs=[pl.BlockSpec((tq, 2 * LANES), lambda p, i: (i, p)),
                  pl.BlockSpec((n_rows, LANES), lambda p, i: (0, p)),
                  pl.BlockSpec((n_rows, LANES), lambda p, i: (0, vblk + p))],
        out_specs=pl.BlockSpec((tq, LANES), lambda p, i: (i, p)),
        out_shape=jax.ShapeDtypeStruct((n_rows, BW), F32),
        compiler_params=_cparams(("parallel", "arbitrary")),
        name="sb_prompt",
    )(qn, kn, z)


def _q_head(q_ref, h):
    lo = h * LANES + HD * (h % 2)
    return q_ref[:, lo:lo + HD]


def _sb_sample_body(q_ref, kn_ref, vn_ref, kc_ref, vc_ref, o_ref, *, L, n_cache, tkc):
    qs = [_q_head(q_ref, h) for h in range(NH)]

    def tri(n):
        r = lax.broadcasted_iota(jnp.int32, (n, n), 0)
        c = lax.broadcasted_iota(jnp.int32, (n, n), 1)
        return (r > c).astype(BF16)

    def visit(state, kget, vget, u, valid):
        ks = [kget(h).astype(BF16) for h in range(NH)]
        vs = [vget(h).astype(BF16) for h in range(NH)]
        zs = [_dot_nt(qs[h], ks[h]) for h in range(NH)]
        lgs = [_neg_softplus(z) for z in zs]
        if valid is not None:
            lgs = [jnp.where(valid, lg, 0.0) for lg in lgs]
        inners = [_split_dot(lg, u) for lg in lgs]
        ws = [jnp.exp(zs[h] + lgs[h] + inners[h] + state[h][0]) for h in range(NH)]
        if valid is not None:
            ws = [jnp.where(valid, w, 0.0) for w in ws]
        return tuple((state[h][0] + inners[h][:, :1] + lgs[h][:, :1],
                      state[h][1] + jnp.dot(ws[h].astype(BF16), vs[h], preferred_element_type=F32))
                     for h in range(NH))

    zero = (jnp.zeros((L, 1), F32), jnp.zeros((L, HD), F32))
    r = lax.broadcasted_iota(jnp.int32, (L, L), 0)
    c = lax.broadcasted_iota(jnp.int32, (L, L), 1)
    state = visit((zero,) * NH, lambda h: kn_ref[:, h * HD:(h + 1) * HD],
                  lambda h: vn_ref[:, h * HD:(h + 1) * HD], tri(L), c < r)
    u_c = tri(tkc)

    def live(cr):
        i, st = cr
        top = functools.reduce(jnp.maximum, [s[0] for s in st])
        return (i < n_cache) & (jnp.max(top) > SB_UNDERFLOW)

    def step(cr):
        i, st = cr
        r0 = pl.multiple_of((n_cache - 1 - i) * tkc, tkc)
        return i + 1, visit(st, lambda h: kc_ref[0, 0, pl.ds(r0, tkc), h, :],
                            lambda h: vc_ref[0, 0, pl.ds(r0, tkc), h, :], u_c, None)
    _, state = lax.while_loop(live, step, (jnp.int32(0), state))
    for h in range(NH):
        o_ref[:, h * HD:(h + 1) * HD] = state[h][1]


def _sb_sample(qn, kn, z, cache_k, cache_v, layer, row_off, nb, L):
    P = cache_k.shape[2]
    tkc = _pick(P, (256, 128))
    rb = row_off // L
    cspec = pl.BlockSpec((1, 1, P, NH, HD), lambda b: (layer, b, 0, 0, 0))
    return pl.pallas_call(
        functools.partial(_sb_sample_body, L=L, n_cache=P // tkc, tkc=tkc),
        grid=(nb,),
        in_specs=[pl.BlockSpec((L, NH * LANES), lambda b: (rb + b, 0)),
                  pl.BlockSpec((L, BW), lambda b: (rb + b, 0)),
                  pl.BlockSpec((L, BW), lambda b: (rb + b, OFF_SV // BW)),
                  cspec, cspec],
        out_specs=pl.BlockSpec((L, BW), lambda b: (b, 0)),
        out_shape=jax.ShapeDtypeStruct((nb * L, BW), F32),
        compiler_params=_cparams(("parallel",)),
        name="sb_sample",
    )(qn, kn, z, cache_k, cache_v)


def _sortable(x):
    b = pltpu.bitcast(x + 0.0, jnp.int32)
    return jnp.where(b < 0, b ^ jnp.int32(0x7FFFFFFF), b)


def _select_topk(key_ref, n_blk, tkb, topk, rows):
    def count(pred):
        def body(i, acc):
            c0 = pl.multiple_of(i * tkb, tkb)
            for c in range(tkb // LANES):
                blk = key_ref[:, pl.ds(c0 + c * LANES, LANES)]
                col = c0 + c * LANES + lax.broadcasted_iota(jnp.int32, (rows, LANES), 1)
                acc = acc + jnp.where(pred(blk, col), 1.0, 0.0)
            return acc
        acc = lax.fori_loop(0, n_blk, body, jnp.zeros((rows, LANES), F32))
        return jnp.sum(acc, axis=1, keepdims=True)

    def unresolved(c):
        i, _, n_ge = c
        return (i < 32) & (jnp.max(jnp.abs(n_ge - topk)) > 0)

    def bit_step(c):
        i, thr, n_ge = c
        cand = thr + (jnp.int32(1) << (31 - i))
        n_cand = count(lambda blk, col: blk >= cand)
        take = n_cand >= topk
        return i + 1, jnp.where(take, cand, thr), jnp.where(take, n_cand, n_ge)

    thr0 = jnp.full((rows, 1), jnp.iinfo(jnp.int32).min, jnp.int32)
    n_all = jnp.full((rows, 1), 1.0, F32) * jnp.asarray(n_blk * tkb).astype(F32)
    _, thr, n_ge = lax.while_loop(unresolved, bit_step, (jnp.int32(0), thr0, n_all))
    n_col_bits = max(1, int(math.ceil(math.log2(key_ref.shape[1] + 1))))

    def tie_search():
        need = topk - count(lambda blk, col: blk > thr)

        def col_step(i, j):
            cand = j + (jnp.int32(1) << (n_col_bits - 1 - i))
            n = count(lambda blk, col: (blk == thr) & (col < cand))
            return jnp.where(n < need, cand, j)
        return lax.fori_loop(0, n_col_bits, col_step, jnp.zeros((rows, 1), jnp.int32))

    any_tie = jnp.max(jnp.abs(n_ge - topk)) > 0
    j_hi = lax.cond(any_tie, tie_search,
                    lambda: jnp.full((rows, 1), jnp.iinfo(jnp.int32).max, jnp.int32))
    return thr, j_hi


def _idx_scores(qi, wi, kblk16, k_real):
    acc = None
    for h in range(NH):
        qh = qi[:, h * LANES:h * LANES + k_real].astype(BF16)
        d = jnp.maximum(_dot_nt(qh, kblk16), 0.0) * (HD ** -0.5)
        t = (wi[:, h:h + 1] * (NH ** -0.5)) * d
        acc = t if acc is None else acc + t
    return acc


def _dsa_sel_prompt_body(qi_ref, wi_ref, ki_ref, m_ref, key_ref, *, tq, tkb, topk, n_keys):
    qb = pl.program_id(0)
    n_blk = ((qb + 1) * tq + tkb - 1) // tkb
    qi = qi_ref[...]
    wi = wi_ref[...]
    q_chunk = (qb * tq + lax.broadcasted_iota(jnp.int32, (tq, tkb), 0)) // CHUNK

    def score_blk(i, _):
        c0 = pl.multiple_of(i * tkb, tkb)
        s = _idx_scores(qi, wi, ki_ref[pl.ds(c0, tkb), :].astype(BF16), LANES)
        k_chunk = (c0 + lax.broadcasted_iota(jnp.int32, (tq, tkb), 1)) // CHUNK
        s = jnp.where(k_chunk <= q_chunk, s, -jnp.inf)
        key_ref[:, pl.ds(c0, tkb)] = _sortable(s)
        return 0
    lax.fori_loop(0, n_blk, score_blk, 0)

    thr, j_hi = _select_topk(key_ref, n_blk, tkb, topk, tq)
    neg_inf_key = _sortable(jnp.full((1, 1), -jnp.inf, F32))

    def write_blk(i, _):
        c0 = pl.multiple_of(i * tkb, tkb)
        blk = key_ref[:, pl.ds(c0, tkb)]
        col = c0 + lax.broadcasted_iota(jnp.int32, (tq, tkb), 1)
        sel = ((blk > thr) | ((blk == thr) & (col <= j_hi))) & (blk > neg_inf_key)
        m_ref[:, pl.ds(c0, tkb)] = jnp.where(sel, 0.0, NEG_BIG).astype(BF16)
        return 0
    lax.fori_loop(0, n_blk, write_blk, 0)

    def zero_blk(i, _):
        m_ref[:, pl.ds(pl.multiple_of(i * tkb, tkb), tkb)] = jnp.full((tq, tkb), NEG_BIG, BF16)
        return 0
    lax.fori_loop(n_blk, n_keys // tkb, zero_blk, 0)


def _dsa_sel_prompt(z, n_rows, topk):
    tq, tkb = 128, 512
    return pl.pallas_call(
        functools.partial(_dsa_sel_prompt_body, tq=tq, tkb=tkb, topk=topk, n_keys=n_rows),
        grid=(n_rows // tq,),
        in_specs=[pl.BlockSpec((tq, NH * LANES), lambda i: (i, OFF_QI // (NH * LANES))),
                  pl.BlockSpec((tq, LANES), lambda i: (i, OFF_WI // LANES)),
                  pl.BlockSpec((n_rows, LANES), lambda i: (0, OFF_KI // LANES))],
        out_specs=pl.BlockSpec((tq, n_rows), lambda i: (i, 0)),
        out_shape=jax.ShapeDtypeStruct((n_rows, n_rows), BF16),
        scratch_shapes=[pltpu.VMEM((tq, n_rows), jnp.int32)],
        compiler_params=_cparams(("parallel",)),
        name="dsa_select_prompt",
    )(z, z, z)


def _dsa_sel_sample_body(qi_ref, wi_ref, kin_ref, kic_ref, m_ref, key_ref, *, L, P, tkb, topk):
    qi = qi_ref[...]
    wi = wi_ref[...]
    n_c = P // tkb

    def score_blk(i, _):
        c0 = pl.multiple_of(i * tkb, tkb)
        s = _idx_scores(qi, wi, kic_ref[0, 0, pl.ds(c0, tkb), :].astype(BF16), HD)
        key_ref[:, pl.ds(c0, tkb)] = _sortable(s)
        return 0
    lax.fori_loop(0, n_c, score_blk, 0)

    knew = jnp.concatenate([kin_ref[...], jnp.zeros((tkb - L, LANES), F32)], axis=0).astype(BF16)
    s = _idx_scores(qi, wi, knew, LANES)
    col = lax.broadcasted_iota(jnp.int32, (L, tkb), 1)
    key_ref[:, pl.ds(P, tkb)] = _sortable(jnp.where(col < L, s, -jnp.inf))

    thr, j_hi = _select_topk(key_ref, n_c + 1, tkb, topk, L)
    neg_inf_key = _sortable(jnp.full((1, 1), -jnp.inf, F32))

    def write_blk(i, _):
        c0 = pl.multiple_of(i * tkb, tkb)
        blk = key_ref[:, pl.ds(c0, tkb)]
        cc = c0 + lax.broadcasted_iota(jnp.int32, (L, tkb), 1)
        sel = ((blk > thr) | ((blk == thr) & (cc <= j_hi))) & (blk > neg_inf_key)
        m_ref[:, pl.ds(c0, tkb)] = jnp.where(sel, 0.0, NEG_BIG).astype(BF16)
        return 0
    lax.fori_loop(0, n_c + 1, write_blk, 0)


def _dsa_sel_sample(z, cache_ki, layer, row_off, nb, L, topk):
    P = cache_ki.shape[2]
    tkb = _pick(P, (512, 256, 128))
    rb = row_off // L
    return pl.pallas_call(
        functools.partial(_dsa_sel_sample_body, L=L, P=P, tkb=tkb, topk=topk),
        grid=(nb,),
        in_specs=[pl.BlockSpec((L, NH * LANES), lambda b: (rb + b, OFF_QI // (NH * LANES))),
                  pl.BlockSpec((L, LANES), lambda b: (rb + b, OFF_WI // LANES)),
                  pl.BlockSpec((L, LANES), lambda b: (rb + b, OFF_KI // LANES)),
                  pl.BlockSpec((1, 1, P, HD), lambda b: (layer, b, 0, 0))],
        out_specs=pl.BlockSpec((L, P + tkb), lambda b: (b, 0)),
        out_shape=jax.ShapeDtypeStruct((nb * L, P + tkb), BF16),
        scratch_shapes=[pltpu.VMEM((L, P + tkb), jnp.int32)],
        compiler_params=_cparams(("parallel",)),
        name="dsa_select_sample",
    )(z, z, z, cache_ki)


def _dsa_attn_body(*refs, tq, tkb, n_cache, prompt):
    if prompt:
        q_ref, m_ref, kn_ref, vn_ref, o_ref = refs
        qb = pl.program_id(1)
        n_blk = ((qb + 1) * tq + tkb - 1) // tkb
    else:
        q_ref, m_ref, kn_ref, vn_ref, kc_ref, vc_ref, o_ref = refs
        n_blk = n_cache
    qs = [q_ref[:, e * LANES:(e + 1) * LANES] for e in range(2)]

    def visit(state, kblk, vblk, mblk):
        kb16 = kblk.astype(BF16)
        vb16 = vblk.astype(BF16)
        bias = mblk.astype(F32)
        new = []
        for e in range(2):
            m, l, acc = state[e]
            s = _dot_nt(qs[e], kb16) + bias
            m_new = jnp.maximum(m, jnp.max(s, axis=1, keepdims=True))
            p = jnp.exp(s - m_new)
            alpha = jnp.exp(m - m_new)
            l = alpha * l + jnp.sum(p, axis=1, keepdims=True)
            acc = alpha * acc + jnp.dot(p.astype(BF16), vb16, preferred_element_type=F32)
            new.append((m_new, l, acc))
        return tuple(new)

    init = (jnp.full((tq, 1), NEG_BIG, F32), jnp.zeros((tq, 1), F32), jnp.zeros((tq, LANES), F32))
    state = (init, init)

    def step(i, st):
        c0 = pl.multiple_of(i * tkb, tkb)
        if prompt:
            return visit(st, kn_ref[pl.ds(c0, tkb), :], vn_ref[pl.ds(c0, tkb), :], m_ref[:, pl.ds(c0, tkb)])
        return visit(st, kc_ref[0, 0, pl.ds(c0, tkb), :], vc_ref[0, 0, pl.ds(c0, tkb), :],
                     m_ref[:, pl.ds(c0, tkb)])
    state = lax.fori_loop(0, n_blk, step, state)

    if not prompt:
        pad = jnp.zeros((tkb - tq, LANES), F32)
        knew = jnp.concatenate([kn_ref[...], pad], axis=0)
        vnew = jnp.concatenate([vn_ref[...], pad], axis=0)
        state = visit(state, knew, vnew, m_ref[:, pl.ds(n_cache * tkb, tkb)])

    lane = lax.broadcasted_iota(jnp.int32, (tq, LANES), 1)
    o0 = state[0][2] / state[0][1]
    o1 = state[1][2] / state[1][1]
    o_ref[...] = jnp.where(lane < HD, o0, o1)


def _dsa_attn_prompt(qn, kn, z, mask, n_rows):
    tq, tkb = 256, min(1024, n_rows)
    vblk = OFF_DV // LANES
    return pl.pallas_call(
        functools.partial(_dsa_attn_body, tq=tq, tkb=tkb, n_cache=0, prompt=True),
        grid=(NH // 2, n_rows // tq),
        in_specs=[pl.BlockSpec((tq, 2 * LANES), lambda p, i: (i, p)),
                  pl.BlockSpec((tq, n_rows), lambda p, i: (i, 0)),
                  pl.BlockSpec((n_rows, LANES), lambda p, i: (0, p)),
                  pl.BlockSpec((n_rows, LANES), lambda p, i: (0, vblk + p))],
        out_specs=pl.BlockSpec((tq, LANES), lambda p, i: (i, p)),
        out_shape=jax.ShapeDtypeStruct((n_rows, BW), F32),
        compiler_params=_cparams(("parallel", "arbitrary")),
        name="dsa_attn_prompt",
    )(qn, mask, kn, z)


def _dsa_attn_sample_body(q_ref, m_ref, kn_ref, vn_ref, kc_ref, vc_ref, o_ref, *, L, tkb, n_cache):
    qs = [_q_head(q_ref, h) for h in range(NH)]

    def visit(state, kget, vget, mblk):
        bias = mblk.astype(F32)
        ks = [kget(h).astype(BF16) for h in range(NH)]
        vs = [vget(h).astype(BF16) for h in range(NH)]
        ss = [_dot_nt(qs[h], ks[h]) + bias for h in range(NH)]
        ms = [jnp.maximum(state[h][0], jnp.max(ss[h], axis=1, keepdims=True)) for h in range(NH)]
        ps = [jnp.exp(ss[h] - ms[h]) for h in range(NH)]
        al = [jnp.exp(state[h][0] - ms[h]) for h in range(NH)]
        return tuple((ms[h], al[h] * state[h][1] + jnp.sum(ps[h], axis=1, keepdims=True),
                      al[h] * state[h][2] + jnp.dot(ps[h].astype(BF16), vs[h], preferred_element_type=F32))
                     for h in range(NH))

    init = (jnp.full((L, 1), NEG_BIG, F32), jnp.zeros((L, 1), F32), jnp.zeros((L, HD), F32))

    def step(i, st):
        c0 = pl.multiple_of(i * tkb, tkb)
        return visit(st, lambda h: kc_ref[0, 0, pl.ds(c0, tkb), h, :],
                     lambda h: vc_ref[0, 0, pl.ds(c0, tkb), h, :], m_ref[:, pl.ds(c0, tkb)])
    state = lax.fori_loop(0, n_cache, step, (init,) * NH)

    pad = jnp.zeros((tkb - L, HD), F32)
    state = visit(state, lambda h: jnp.concatenate([kn_ref[:, h * HD:(h + 1) * HD], pad], axis=0),
                  lambda h: jnp.concatenate([vn_ref[:, h * HD:(h + 1) * HD], pad], axis=0),
                  m_ref[:, pl.ds(n_cache * tkb, tkb)])
    for h in range(NH):
        o_ref[:, h * HD:(h + 1) * HD] = state[h][2] / state[h][1]


def _dsa_attn_sample(qn, kn, z, mask, cache_k, cache_v, layer, row_off, nb, L):
    P = cache_k.shape[2]
    tkb = mask.shape[1] - P
    rb = row_off // L
    cspec = pl.BlockSpec((1, 1, P, NH, HD), lambda b: (layer, b, 0, 0, 0))
    return pl.pallas_call(
        functools.partial(_dsa_attn_sample_body, L=L, tkb=tkb, n_cache=P // tkb),
        grid=(nb,),
        in_specs=[pl.BlockSpec((L, NH * LANES), lambda b: (rb + b, 0)),
                  pl.BlockSpec((L, P + tkb), lambda b: (b, 0)),
                  pl.BlockSpec((L, BW), lambda b: (rb + b, 0)),
                  pl.BlockSpec((L, BW), lambda b: (rb + b, OFF_DV // BW)),
                  cspec, cspec],
        out_specs=pl.BlockSpec((L, BW), lambda b: (b, 0)),
        out_shape=jax.ShapeDtypeStruct((nb * L, BW), F32),
        compiler_params=_cparams(("parallel",)),
        name="dsa_attn_sample",
    )(qn, mask, kn, z, cache_k, cache_v)


def _gla_consts(C):
    nlev = int(math.log2(C))
    t = np.arange(C)
    cum = (t[:, None] >= t[None, :]).astype(np.float32)
    sel = np.zeros((nlev * C, C), np.float32)
    for l in range(nlev):
        m = 1 << l
        mid = (t // (2 * m)) * 2 * m + m
        sel[l * C + t, mid - 1] = 1.0
    return jnp.asarray(cum, BF16), jnp.asarray(sel, BF16)


def _gla_body(gq_ref, gk_ref, gv_ref, gr_ref, ga_ref, a2_ref, ab_ref, on_ref, cum_ref, sel_ref,
              s0_ref, o_ref, sout_ref, st_ref, *, C, n_chunks):
    ci = pl.program_id(1)
    nlev = int(math.log2(C))

    @pl.when(ci == 0)
    def _():
        st_ref[...] = s0_ref[0]

    x = jnp.dot(ga_ref[...].astype(BF16), a2_ref[...], preferred_element_type=F32) + ab_ref[...]
    g = _neg_softplus(-x) / GLA_GATE_NORM
    b = _split_dot_left(cum_ref[...], g)
    c_all = _split_dot_left(sel_ref[...], b)

    row = lax.broadcasted_iota(jnp.int32, (C, C), 0)
    col = lax.broadcasted_iota(jnp.int32, (C, C), 1)
    rowl = lax.broadcasted_iota(jnp.int32, (C, LANES), 0)

    for h in range(GLA_H):
        sl = slice(h * LANES, (h + 1) * LANES)
        q = gq_ref[:, sl] * (HD ** -0.5)
        k = gk_ref[:, sl]
        v = gv_ref[:, sl]
        bh = b[:, sl]
        a = jnp.where(row == col, jnp.sum(q * k, axis=1, keepdims=True), 0.0)
        for l in range(nlev):
            m = 1 << l
            later = (rowl & m) != 0
            ch = c_all[l * C:(l + 1) * C, sl]
            qe = jnp.where(later, q * jnp.exp(jnp.where(later, bh - ch, 0.0)), 0.0)
            ke = jnp.where(later, 0.0, k * jnp.exp(jnp.where(later, 0.0, ch - bh)))
            al = _dot_nt(qe.astype(BF16), ke.astype(BF16))
            a = a + jnp.where((row // (2 * m)) == (col // (2 * m)), al, 0.0)
        st = st_ref[h]
        o = jnp.dot(a.astype(BF16), v.astype(BF16), preferred_element_type=F32)
        o = o + _dot_nt((q * jnp.exp(bh)).astype(BF16), st.astype(BF16))
        b_last = bh[C - 1:C, :]
        kx = k * jnp.exp(b_last - bh)
        st_ref[h] = st * jnp.exp(b_last) + _dot_tn(v.astype(BF16), kx.astype(BF16))
        on = o * lax.rsqrt(jnp.mean(o * o, axis=-1, keepdims=True) + NORM_EPS) * on_ref[...]
        gr = gr_ref[:, sl]
        o_ref[:, sl] = on * (gr * jax.nn.sigmoid(gr))

    @pl.when(ci == n_chunks - 1)
    def _():
        sout_ref[0] = st_ref[...]


def _gla(z, a2p, abp, onorm, s0t, row_off, n_seq, L):
    C = min(CHUNK, L)
    n_chunks = L // C
    rb = row_off // C
    cum, sel = _gla_consts(C)
    blk = lambda off: pl.BlockSpec((C, BW), lambda s, c: (rb + s * n_chunks + c, off // BW))
    full = lambda a: pl.BlockSpec(a.shape, lambda s, c: (0,) * a.ndim)
    return pl.pallas_call(
        functools.partial(_gla_body, C=C, n_chunks=n_chunks),
        grid=(n_seq, n_chunks),
        in_specs=[blk(OFF_GQ), blk(OFF_GK), blk(OFF_GV), blk(OFF_GR),
                  pl.BlockSpec((C, LANES), lambda s, c: (rb + s * n_chunks + c, OFF_GA // LANES)),
                  full(a2p), full(abp), full(onorm), full(cum), full(sel),
                  pl.BlockSpec((1, GLA_H, GLA_DV, LANES), lambda s, c: (s, 0, 0, 0))],
        out_specs=[pl.BlockSpec((C, BW), lambda s, c: (s * n_chunks + c, 0)),
                   pl.BlockSpec((1, GLA_H, GLA_DV, LANES), lambda s, c: (s, 0, 0, 0))],
        out_shape=[jax.ShapeDtypeStruct((n_seq * L, BW), F32),
                   jax.ShapeDtypeStruct((n_seq, GLA_H, GLA_DV, LANES), F32)],
        scratch_shapes=[pltpu.VMEM((GLA_H, GLA_DV, LANES), F32)],
        compiler_params=_cparams(("parallel", "arbitrary")),
        name="gla",
    )(z, z, z, z, z, a2p, abp, onorm, cum, sel, s0t)


def _rw_prep_body(z_ref, head_ref, mu_ref, w0_ref, a0_ref, kkp_ref, kap_ref, rkp_ref,
                  w2_ref, a2_ref, g2_ref, seg_ref,
                  r_ref, w_ref, k_ref, v_ref, nkk_ref, kka_ref, gate_ref, bonus_ref, *, grp):
    zz = z_ref[...]
    tm, wd = zz.shape
    heads = head_ref[...]
    head_rows = jnp.concatenate([jnp.broadcast_to(heads[g:g + 1, :], (grp, wd))
                                 for g in range(tm // grp)], axis=0)
    row = lax.broadcasted_iota(jnp.int32, (tm, wd), 0)
    prev = jnp.where(row % grp == 0, head_rows, pltpu.roll(zz, 1, axis=0))
    zm = zz + mu_ref[...] * (prev - zz)
    rr = zm[:, 0:BW]
    rk = zm[:, BW:2 * BW]
    rv = zm[:, 2 * BW:3 * BW]
    lora = zm[:, 3 * BW:3 * BW + LANES]
    gl = zm[:, 3 * BW + LANES:3 * BW + 2 * LANES]
    xw = w0_ref[...] + jnp.dot(jnp.tanh(lora).astype(BF16), w2_ref[...], preferred_element_type=F32)
    wlog = _neg_softplus(-xw) - 0.5
    a = jax.nn.sigmoid(a0_ref[...] + jnp.dot(lora.astype(BF16), a2_ref[...], preferred_element_type=F32))
    gate = jnp.dot(jax.nn.sigmoid(gl).astype(BF16), g2_ref[...], preferred_element_type=F32)
    kkf = rk * kkp_ref[...]
    nrm = jnp.sqrt(_split_dot(kkf * kkf, seg_ref[...]))
    kk = kkf / jnp.maximum(nrm, 1e-12)
    kmod = rk * (1.0 + (a - 1.0) * kap_ref[...])
    coef = _split_dot(rr * kmod * rkp_ref[...], seg_ref[...])
    r_ref[...] = rr
    w_ref[...] = -jnp.exp(wlog)
    k_ref[...] = kmod
    v_ref[...] = rv
    nkk_ref[...] = -kk
    kka_ref[...] = kk * a
    gate_ref[...] = gate
    bonus_ref[...] = coef * rv


def _rw_prep(z, heads, grp, mu, w0, a0, kkp, kap, rkp, w2p, a2p, g2, seg):
    T = z.shape[0]
    tm = SUBLANES * grp
    assert T % tm == 0, (T, tm)
    W = 4 * BW
    par = lambda a: pl.BlockSpec(a.shape, lambda i: (0,) * a.ndim)
    out = pl.BlockSpec((tm, BW), lambda i: (i, 0))
    return pl.pallas_call(
        functools.partial(_rw_prep_body, grp=grp),
        grid=(T // tm,),
        in_specs=[pl.BlockSpec((tm, W), lambda i: (i, 0)), pl.BlockSpec((SUBLANES, W), lambda i: (i, 0)),
                  par(mu), par(w0), par(a0), par(kkp), par(kap), par(rkp),
                  par(w2p), par(a2p), par(g2), par(seg)],
        out_specs=[out] * 8,
        out_shape=[jax.ShapeDtypeStruct((T, BW), F32)] * 8,
        compiler_params=_cparams(("parallel",)),
        name="rwkv_prep",
    )(z, heads, mu, w0, a0, kkp, kap, rkp, w2p, a2p, g2, seg)


def _rw_rec_body(r_ref, w_ref, k_ref, v_ref, nkk_ref, kka_ref, g_ref, s0_ref,
                 ot_ref, sout_ref, s_ref, *, TT, L, NS, n_tiles):
    ti = pl.program_id(0)
    npair = NH // 2
    R = NS * npair * HD
    gm = g_ref[...]
    lane = lax.broadcasted_iota(jnp.int32, (R, LANES), 1)
    rowi = lax.broadcasted_iota(jnp.int32, (R, LANES), 0)
    eye2 = (lane % HD) == (rowi % HD)
    seq_lane0 = (rowi // (npair * HD)) * L
    refs =(r_ref, w_ref, k_ref, v_ref, nkk_ref, kka_ref)

    def token_group(gi, carry):
        s, o_lo, o_hi = carry
        t0 = pl.multiple_of(gi * SUBLANES, SUBLANES)
        blk = [[ref[pl.ds(q * L + t0, SUBLANES), :] for q in range(NS)] for ref in refs]
        for j in range(SUBLANES):
            def stacked(x):
                return jnp.concatenate(
                    [jnp.broadcast_to(x[q][j:j + 1, p * LANES:(p + 1) * LANES], (HD, LANES))
                     for q in range(NS) for p in range(npair)], axis=0)
            rr, ww, kk, vv, nk, ka = [stacked(x) for x in blk]
            sa = _split_dot(s * nk, gm)
            vb = jnp.dot(jnp.where(eye2, vv, 0.0).astype(BF16), gm, preferred_element_type=F32)
            s = s * ww + sa * ka + vb * kk
            oo = jnp.dot((s * rr).astype(BF16), gm, preferred_element_type=F32)
            hit = lane == seq_lane0 + t0 + j
            o_lo = jnp.where(hit, oo, o_lo)
            o_hi = jnp.where(hit, pltpu.roll(oo, HD, axis=1), o_hi)
        return s, o_lo, o_hi

    if NS == 1:
        @pl.when(ti == 0)
        def _():
            s_ref[...] = s0_ref[0]
        s_in = s_ref[...]
    else:
        s_in = s0_ref[...].reshape(R, LANES)
    zero = jnp.zeros((R, LANES), F32)
    s_out, o_lo, o_hi = lax.fori_loop(0, min(L, TT) // SUBLANES, token_group, (s_in, zero, zero))

    pr = npair * HD
    a1 = sum(o_lo[q * pr:(q + 1) * pr] for q in range(NS))
    a2 = sum(o_hi[q * pr:(q + 1) * pr] for q in range(NS))
    first_half = lax.broadcasted_iota(jnp.int32, (pr, LANES), 1) < HD
    lo = jnp.where(first_half, a1, a2)
    hi = jnp.where(first_half, a2, a1)
    for p in range(npair):
        ot_ref[2 * p * HD:(2 * p + 1) * HD, :] = lo[p * HD:(p + 1) * HD]
        ot_ref[(2 * p + 1) * HD:(2 * p + 2) * HD, :] = hi[p * HD:(p + 1) * HD]
    if NS == 1:
        s_ref[...] = s_out

        @pl.when(ti == n_tiles - 1)
        def _():
            sout_ref[0] = s_out
    else:
        sout_ref[...] = s_out.reshape(NS, pr, LANES)


def _rw_rec(r, w, k, v, nkk, kka, gmat, s0, row_off, n_seq, L):
    TT = LANES
    n_tok = n_seq * L
    n_tiles = n_tok // TT
    rb = row_off // TT
    ns = max(1, TT // L)
    pr = (NH // 2) * HD
    tok = pl.BlockSpec((TT, BW), lambda i: (rb + i, 0))
    if ns == 1:
        sspec = pl.BlockSpec((1, pr, LANES), lambda i: (0, 0, 0))
    else:
        sspec = pl.BlockSpec((ns, pr, LANES), lambda i: (i, 0, 0))
    return pl.pallas_call(
        functools.partial(_rw_rec_body, TT=TT, L=L, NS=ns, n_tiles=n_tiles),
        grid=(n_tiles,),
        in_specs=[tok] * 6 + [pl.BlockSpec((LANES, LANES), lambda i: (0, 0)), sspec],
        out_specs=[pl.BlockSpec((BW, TT), lambda i: (0, i)), sspec],
        out_shape=[jax.ShapeDtypeStruct((BW, n_tok), F32),
                   jax.ShapeDtypeStruct(s0.shape, F32)],
        scratch_shapes=[pltpu.VMEM((pr, LANES), F32)],
        compiler_params=_cparams(("arbitrary",)),
        name="rwkv_recurrence",
    )(r, w, k, v, nkk, kka, gmat, s0)


def _mm3(a, b):
    ah = a.astype(BF16)
    al = (a - ah.astype(F32)).astype(BF16)
    bh = b.astype(BF16)
    bl = (b - bh.astype(F32)).astype(BF16)
    d = lambda x, y: jnp.dot(x, y, preferred_element_type=F32)
    return d(ah, bh) + (d(ah, bl) + d(al, bh))


def _rw_chunk_body(r_ref, lw_ref, k_ref, v_ref, nkk_ref, kka_ref, s0_ref, o_ref, sout_ref, s_ref,
                   *, TT, L, n_tiles):
    ti = pl.program_id(0)
    npair = NH // 2
    nblk = TT // RW_BLK
    row = lax.broadcasted_iota(jnp.int32, (TT, TT), 0)
    col = lax.broadcasted_iota(jnp.int32, (TT, TT), 1)
    same = (row // RW_BLK) == (col // RW_BLK)
    strict = same & (col < row)
    incl = same & (col <= row)
    tri = jnp.where(incl, 1.0, 0.0).astype(BF16)
    last = jnp.where(same & (col % RW_BLK == RW_BLK - 1), 1.0, 0.0).astype(BF16)
    eye = jnp.where(row == col, 1.0, 0.0)
    pair_diag = (row < HD) == (col < HD)
    lo_half = lax.broadcasted_iota(jnp.int32, (TT, LANES), 1) < HD
    lo_half_b = lax.broadcasted_iota(jnp.int32, (RW_BLK, LANES), 1) < HD

    r, lw, k, v = r_ref[...], lw_ref[...], k_ref[...], v_ref[...]
    nk, ka = nkk_ref[...], kka_ref[...]
    lc = _split_dot_left(tri, lw)
    le = _split_dot_left(last, lc)
    nkt = nk * jnp.exp(lc - lw)
    rt = r * jnp.exp(lc)
    inv = jnp.exp(-lc)
    kah = ka * inv
    kh = k * inv
    rem = jnp.exp(le - lc)
    kaw = (ka * rem).astype(BF16)
    kw = (k * rem).astype(BF16)
    wb = jnp.exp(le)

    sls = [slice(p * LANES, (p + 1) * LANES) for p in range(npair)]
    v16 = [v[:, sl].astype(BF16) for sl in sls]
    heads = [(p, e) for p in range(npair) for e in range(2)]
    grams = []
    for p, e in heads:
        hm = lo_half if e == 0 else jnp.logical_not(lo_half)
        lhs = jnp.concatenate([jnp.where(hm, nkt[:, sls[p]], 0.0), jnp.where(hm, rt[:, sls[p]], 0.0)],
                              axis=0).astype(BF16)
        rhs = jnp.concatenate([kah[:, sls[p]], kh[:, sls[p]]], axis=0).astype(BF16)
        grams.append(_dot_nt(lhs, rhs))
    n1 = [jnp.where(strict, g[:TT, :TT], 0.0) for g in grams]
    bt = [jnp.where(strict, g[:TT, TT:], 0.0).astype(BF16) for g in grams]
    cmat = [jnp.where(incl, g[TT:, :TT], 0.0).astype(BF16) for g in grams]
    et = [jnp.where(incl, g[TT:, TT:], 0.0).astype(BF16) for g in grams]
    n2 = [_mm3(a, a) for a in n1]
    n4 = [_mm3(a, a) for a in n2]
    n8 = [_mm3(a, a) for a in n4]
    tinv = [eye + a for a in n1]
    for power in (n2, n4, n8):
        tinv = [t + _mm3(t, a) for t, a in zip(tinv, power)]
    vb_h = [jnp.dot(b_, v16[p], preferred_element_type=F32) for b_, (p, e) in zip(bt, heads)]
    ev_h = [jnp.dot(e_, v16[p], preferred_element_type=F32) for e_, (p, e) in zip(et, heads)]
    vb = [jnp.where(lo_half, vb_h[2 * p], vb_h[2 * p + 1]) for p in range(npair)]
    ev = [jnp.where(lo_half, ev_h[2 * p], ev_h[2 * p + 1]) for p in range(npair)]

    if L >= TT:
        @pl.when(ti == 0)
        def _():
            s_ref[...] = s0_ref[0]
        state = {(0, p): s_ref[p] for p in range(npair)}
        rounds = [[j] for j in range(nblk)]
    else:
        bps = L // RW_BLK
        state = {(q, p): s0_ref[q, p] for q in range(TT // L) for p in range(npair)}
        rounds = [[q * bps + i for q in range(TT // L)] for i in range(bps)]
    sa_rows = [[None] * nblk for _ in range(npair)]
    os_rows = [[None] * nblk for _ in range(npair)]
    for blocks in rounds:
        items = [(j, p) for j in blocks for p in range(npair)]
        seq = lambda j: (j * RW_BLK) // L if L < TT else 0
        xs = {}
        for j, p in items:
            bs = slice(j * RW_BLK, (j + 1) * RW_BLK)
            s = state[(seq(j), p)]
            s_hi = s.astype(BF16)
            s_lo = (s - s_hi.astype(F32)).astype(BF16)
            lhs = jnp.concatenate([nkt[bs, sls[p]], rt[bs, sls[p]]], axis=0).astype(BF16)
            xs[(j, p)] = (jnp.dot(lhs, s_hi, preferred_element_type=F32)
                          + jnp.dot(lhs, s_lo, preferred_element_type=F32))
        sa = {}
        for j, p in items:
            b0 = j * RW_BLK
            bs = slice(b0, b0 + RW_BLK)
            parts = [xs[(j, p)][:RW_BLK] + vb[p][bs]]
            if b0:
                parts.insert(0, jnp.zeros((b0, LANES), F32))
            if TT - b0 - RW_BLK:
                parts.append(jnp.zeros((TT - b0 - RW_BLK, LANES), F32))
            xfull = jnp.concatenate(parts, axis=0)
            sa[(j, p)] = jnp.where(lo_half_b, _mm3(tinv[2 * p][bs], xfull),
                                   _mm3(tinv[2 * p + 1][bs], xfull))
        for j, p in items:
            b0 = j * RW_BLK
            bs = slice(b0, b0 + RW_BLK)
            u = (_dot_tn(kaw[bs, sls[p]], sa[(j, p)].astype(BF16)) + _dot_tn(kw[bs, sls[p]], v16[p][bs]))
            wcol = jnp.broadcast_to(wb[b0:b0 + 1, sls[p]], (LANES, LANES)).T
            state[(seq(j), p)] = wcol * state[(seq(j), p)] + jnp.where(pair_diag, u, 0.0)
            sa_rows[p][j] = sa[(j, p)]
            os_rows[p][j] = xs[(j, p)][RW_BLK:]
    if L < TT:
        for (q, p), s in state.items():
            sout_ref[q, p] = s

    for p in range(npair):
        sa_t = jnp.concatenate(sa_rows[p], axis=0).astype(BF16)
        o_sa = jnp.where(lo_half, jnp.dot(cmat[2 * p], sa_t, preferred_element_type=F32),
                         jnp.dot(cmat[2 * p + 1], sa_t, preferred_element_type=F32))
        o_ref[:, sls[p]] = jnp.concatenate(os_rows[p], axis=0) + o_sa + ev[p]
    if L >= TT:
        for p in range(npair):
            s_ref[p] = state[(0, p)]

        @pl.when(ti == n_tiles - 1)
        def _():
            for p in range(npair):
                sout_ref[0, p] = state[(0, p)]


def _rw_chunk(r, lw, k, v, nkk, kka, s0, row_off, n_seq, L):
    TT = LANES
    assert L % RW_BLK == 0 and (L % TT == 0 or TT % L == 0)
    n_tok = n_seq * L
    n_tiles = n_tok // TT
    rb = row_off // TT
    ns = max(1, TT // L)
    tok = pl.BlockSpec((TT, BW), lambda i: (rb + i, 0))
    if ns == 1:
        sspec = pl.BlockSpec((1, NH // 2, LANES, LANES), lambda i: (0, 0, 0, 0))
    else:
        sspec = pl.BlockSpec((ns, NH // 2, LANES, LANES), lambda i: (i, 0, 0, 0))
    return pl.pallas_call(
        functools.partial(_rw_chunk_body, TT=TT, L=L, n_tiles=n_tiles),
        grid=(n_tiles,),
        in_specs=[tok] * 6 + [sspec],
        out_specs=[pl.BlockSpec((TT, BW), lambda i: (i, 0)), sspec],
        out_shape=[jax.ShapeDtypeStruct((n_tok, BW), F32),
                   jax.ShapeDtypeStruct(s0.shape, F32)],
        scratch_shapes=[pltpu.VMEM((NH // 2, LANES, LANES), F32)],
        compiler_params=_cparams(("arbitrary",)),
        name="rwkv_chunk",
    )(r, lw, k, v, nkk, kka, s0)


def _rw_post_body(o_ref, bonus_ref, gate_ref, lnw_ref, lnb_ref, seg_ref, out_ref):
    o = o_ref[...]
    mu = _split_dot(o, seg_ref[...]) * (1.0 / HD)
    d = o - mu
    var = _split_dot(d * d, seg_ref[...]) * (1.0 / HD)
    on = d * lax.rsqrt(var + RW_LN_EPS) * lnw_ref[...] + lnb_ref[...]
    out_ref[...] = (on + bonus_ref[...]) * gate_ref[...]


def _rw_post(o, bonus, gate, lnw, lnb, seg):
    T = o.shape[0]
    tm = _pick(T, (512, 256, 128))
    blk = pl.BlockSpec((tm, BW), lambda i: (i, 0))
    par = lambda a: pl.BlockSpec(a.shape, lambda i: (0,) * a.ndim)
    return pl.pallas_call(
        _rw_post_body,
        grid=(T // tm,),
        in_specs=[blk, blk, blk, par(lnw), par(lnb), par(seg)],
        out_specs=blk,
        out_shape=jax.ShapeDtypeStruct((T, BW), F32),
        compiler_params=_cparams(("parallel",)),
        name="rwkv_post",
    )(o, bonus, gate, lnw, lnb, seg)


def _heads_alt(w):
    d = w.shape[0]
    w4 = w.reshape(d, NH // 2, 2, HD)
    zero = jnp.zeros((d, NH // 2, HD), w.dtype)
    ev = jnp.concatenate([w4[:, :, 0], zero], -1)
    od = jnp.concatenate([zero, w4[:, :, 1]], -1)
    return jnp.stack([ev, od], 2).reshape(d, NH * LANES)


def _heads_lo(w, nh):
    d = w.shape[0]
    w3 = w.reshape(d, nh, HD)
    return jnp.concatenate([w3, jnp.zeros_like(w3)], -1).reshape(d, nh * LANES)


def _padc(w, n):
    return jnp.pad(w, ((0, 0), (0, n - w.shape[1])))


def _rw_reorder(a):
    o = 0
    parts = {}
    for name, wd in (("rr", BW), ("wl", RW_LORA), ("rk", BW), ("rv", BW), ("al", RW_LORA), ("gl", RW_GATE_LORA)):
        parts[name] = a[..., o:o + wd]
        o += wd
    return jnp.concatenate([parts[n] for n in ("rr", "rk", "rv", "wl", "al", "gl")], axis=-1)


def _rw_unorder(a):
    rr, rk, rv = a[..., 0:BW], a[..., BW:2 * BW], a[..., 2 * BW:3 * BW]
    wl = a[..., 3 * BW:3 * BW + RW_LORA]
    al = a[..., 3 * BW + RW_LORA:3 * BW + 2 * RW_LORA]
    gl = a[..., 3 * BW + 2 * RW_LORA:RW_COLS]
    return jnp.concatenate([rr, wl, rk, rv, al, gl], axis=-1)


def _relayout_w_in(w, d_model):
    w = w.astype(BF16)
    o = 0

    def take(n):
        nonlocal o
        s = w[:, o:o + n]
        o += n
        return s
    sq, sk, sv = take(BW), take(BW), take(BW)
    dq, dk, dv, qi, ki, wi = take(BW), take(BW), take(BW), take(BW), take(HD), take(NH)
    gq, gk = take(GLA_H * HD), take(GLA_H * HD)
    gv, gr, ga = take(BW), take(BW), take(GLA_LOWRANK)
    rw = take(RW_COLS)
    gate = take(4 * d_model)
    cols = [_padc(_rw_reorder(rw), 4 * BW), _heads_alt(sq), sk, sv, _heads_alt(dq), dk, dv,
            _heads_lo(qi, NH), _heads_lo(gq, GLA_H), _heads_lo(gk, GLA_H), gv, gr,
            _padc(ki, LANES), _padc(wi, LANES), _padc(ga, 2 * LANES), gate]
    return jnp.concatenate(cols, axis=1)


def _relayout_ffn(w_up, w_down, tf):
    d, f2 = w_up.shape
    f = f2 // 2
    fp = -(-f // tf) * tf
    g = _padc(w_up[:, :f].astype(BF16), fp).reshape(d, fp // tf, tf)
    u = _padc(w_up[:, f:].astype(BF16), fp).reshape(d, fp // tf, tf)
    wup = jnp.concatenate([g, u], axis=2).reshape(d, 2 * fp)
    wdn = jnp.pad(w_down.astype(BF16), ((0, fp - f), (0, 0)))
    return wup, wdn


def _gain_alt(g):
    return jnp.tile(g, 2 * NH)[None, :]


def kernel(x_prompt, x_sample, cache_sb_k, cache_sb_v, cache_dsa_k, cache_dsa_v, cache_dsa_kidx, state_gla, state_rwkv, state_rwkv_shift, ffn1_norm, ffn1_up, ffn1_down, mix_norm, w_in, sb_qnorm, sb_knorm, dsa_qnorm, dsa_knorm, gla_a2, gla_ab, gla_onorm, rwkv_mu, rwkv_w0, rwkv_w2, rwkv_a0, rwkv_a2, rwkv_g2, rwkv_kk, rwkv_ka, rwkv_rk, rwkv_lnw, rwkv_lnb, w_branch, w_out, ffn2_norm, ffn2_up, ffn2_down):
    depth = w_in.shape[0]
    bp, lp, d_model = x_prompt.shape
    nb, ls, _ = x_sample.shape
    assert bp == 1, "prompt group is one sequence"
    tp = bp * lp
    ts = nb * ls
    past = cache_sb_k.shape[2]
    tf = 512

    x = jnp.concatenate([x_prompt.reshape(tp, d_model), x_sample.reshape(ts, d_model)], axis=0)

    bd = jnp.asarray(np.kron(np.eye(2 * NH), np.full((HD, HD), 1.0 / HD)), BF16)
    seg = jnp.asarray(np.kron(np.eye(NH), np.ones((HD, HD))), BF16)
    gmat = jnp.asarray(np.kron(np.eye(2), np.ones((HD, HD))), BF16)
    topk_p = min(TOPK_MAX, lp // 4)
    topk_s = min(TOPK_MAX, (past + ls) // 4)

    new_p = [[] for _ in range(8)]
    new_s = [[] for _ in range(8)]
    for i in range(depth):
        wup1, wdn1 = _relayout_ffn(ffn1_up[i], ffn1_down[i], tf)
        wup2, wdn2 = _relayout_ffn(ffn2_up[i], ffn2_down[i], tf)
        win = _relayout_w_in(w_in[i], d_model)

        x = _ffn(x, ffn1_norm[i][None], wup1, wdn1, tf)
        z = _proj(x, mix_norm[i][None], win)

        sqn, skn, dqn, dkn = _qknorm(z, _gain_alt(sb_qnorm[i]), jnp.tile(sb_knorm[i], NH)[None],
                                     _gain_alt(dsa_qnorm[i]), jnp.tile(dsa_knorm[i], NH)[None], bd)
        o_sb = jnp.concatenate([
            _sb_prompt(sqn, skn, z, tp),
            _sb_sample(sqn, skn, z, cache_sb_k, cache_sb_v, i, tp, nb, ls)], axis=0)
        mask_p = _dsa_sel_prompt(z, tp, topk_p)
        mask_s = _dsa_sel_sample(z, cache_dsa_kidx, i, tp, nb, ls, topk_s)
        o_dsa = jnp.concatenate([
            _dsa_attn_prompt(dqn, dkn, z, mask_p, tp),
            _dsa_attn_sample(dqn, dkn, z, mask_s, cache_dsa_k, cache_dsa_v, i, tp, nb, ls)], axis=0)

        a2p = jnp.pad(_heads_lo(gla_a2[i], GLA_H), ((0, LANES - GLA_LOWRANK), (0, 0))).astype(BF16)
        abp = _heads_lo(gla_ab[i][None], GLA_H)
        onorm = gla_onorm[i][None]

        def gla_state_in(s):
            return jnp.pad(jnp.swapaxes(s, 2, 3), ((0, 0), (0, 0), (0, 0), (0, LANES - HD)))

        def gla_state_out(s):
            return jnp.swapaxes(s[..., :HD], 2, 3)
        og_p, sg_p = _gla(z, a2p, abp, onorm, jnp.zeros((bp, GLA_H, GLA_DV, LANES), F32), 0, bp, lp)
        og_s, sg_s = _gla(z, a2p, abp, onorm, gla_state_in(state_gla[i]), tp, nb, ls)
        o_gla = jnp.concatenate([og_p, og_s], axis=0)

        zrw = z[:, :RW_COLS]
        shift_s = _rw_reorder(state_rwkv_shift[i])
        heads = jnp.concatenate([jnp.zeros((1, 4 * BW), F32), z[ls - 1:tp - 1:ls, :4 * BW],
                                 _padc(shift_s.reshape(nb, RW_COLS), 4 * BW)], axis=0)
        mu = _padc(_rw_reorder(rwkv_mu[i])[None], 4 * BW)
        w2p = jnp.concatenate([rwkv_w2[i], jnp.zeros_like(rwkv_a2[i])], axis=0).astype(BF16)
        a2q = jnp.concatenate([jnp.zeros_like(rwkv_w2[i]), rwkv_a2[i]], axis=0).astype(BF16)
        r_, w_, k_, v_, nkk_, kka_, gate_, bonus_ = _rw_prep(
            z, heads, ls, mu, rwkv_w0[i][None], rwkv_a0[i][None], rwkv_kk[i][None], rwkv_ka[i][None],
            rwkv_rk[i][None], w2p, a2q, rwkv_g2[i].astype(BF16), seg)

        def rw_state_in(s):
            n = s.shape[0]
            st = jnp.swapaxes(s, 2, 3).reshape(n, NH // 2, 2, HD, HD)
            zero = jnp.zeros((n, NH // 2, HD, HD), s.dtype)
            return jnp.concatenate([jnp.concatenate([st[:, :, 0], zero], axis=-1),
                                    jnp.concatenate([zero, st[:, :, 1]], axis=-1)], axis=-2)

        def rw_state_out(s):
            n = s.shape[0]
            st = jnp.stack([s[:, :, :HD, :HD], s[:, :, HD:, HD:]], axis=2)
            return jnp.swapaxes(st, 3, 4).reshape(n, NH, HD, HD)
        o_rp, sr_p = _rw_chunk(r_, w_, k_, v_, nkk_, kka_, jnp.zeros((bp, NH // 2, LANES, LANES), F32), 0, bp, lp)
        o_rs, sr_s = _rw_chunk(r_, w_, k_, v_, nkk_, kka_, rw_state_in(state_rwkv[i]), tp, nb, ls)
        o_r = jnp.concatenate([o_rp, o_rs], axis=0)
        o_rw = _rw_post(o_r, bonus_, gate_, rwkv_lnw[i][None], rwkv_lnb[i][None], seg)

        x = _merge(x, z, o_sb, o_dsa, o_gla, o_rw, w_branch[i].astype(BF16), w_out[i].astype(BF16))
        x = _ffn(x, ffn2_norm[i][None], wup2, wdn2, tf)

        sv = z[:, OFF_SV:OFF_SV + BW]
        dv = z[:, OFF_DV:OFF_DV + BW]
        ki = z[:, OFF_KI:OFF_KI + HD]
        for dst, (lo, n, l) in ((new_p, (0, bp, lp)), (new_s, (tp, nb, ls))):
            hi = lo + n * l
            dst[0].append(skn[lo:hi].reshape(n, l, NH, HD))
            dst[1].append(sv[lo:hi].reshape(n, l, NH, HD))
            dst[2].append(dkn[lo:hi].reshape(n, l, NH, HD))
            dst[3].append(dv[lo:hi].reshape(n, l, NH, HD))
            dst[4].append(ki[lo:hi].reshape(n, l, HD))
            dst[7].append(_rw_unorder(zrw[lo:hi].reshape(n, l, RW_COLS)[:, -1:]))
        new_p[5].append(gla_state_out(sg_p))
        new_s[5].append(gla_state_out(sg_s))
        new_p[6].append(rw_state_out(sr_p))
        new_s[6].append(rw_state_out(sr_s))

    outs_p = [jnp.stack(l, axis=0) for l in new_p]
    outs_s = [jnp.stack(l, axis=0) for l in new_s]
    y_p = x[:tp].reshape(bp, lp, d_model)
    y_s = x[tp:].reshape(nb, ls, d_model)
    return (y_p, y_s, *outs_p, *outs_s)
```

```python
import functools
import math

import jax
import jax.numpy as jnp
import numpy as np
from jax import lax
from jax.experimental import pallas as pl
from jax.experimental.pallas import tpu as pltpu

F32 = jnp.float32
BF16 = jnp.bfloat16

LANES = 128
SUBLANES = 8
HD = 64
NH = 8
BW = NH * HD
GLA_H = 4
GLA_DV = 128
GLA_LOWRANK = 16
GLA_GATE_NORM = 16.0
CHUNK = 64
TOPK_MAX = 256
RW_LORA = 64
RW_GATE_LORA = 128
RW_BLK = 16
RW_COLS = 3 * BW + 2 * RW_LORA + RW_GATE_LORA
NORM_EPS = 1e-6
RW_LN_EPS = 64e-5
NEG_BIG = -1e30
SB_UNDERFLOW = -104.0
VMEM_LIMIT = 56 * 1024 * 1024

OFF_RW = 0
OFF_SQ = 2048
OFF_SK = 3072
OFF_SV = 3584
OFF_DQ = 4096
OFF_DK = 5120
OFF_DV = 5632
OFF_QI = 6144
OFF_GQ = 7168
OFF_GK = 7680
OFF_GV = 8192
OFF_GR = 8704
OFF_KI = 9216
OFF_WI = 9344
OFF_GA = 9472
OFF_GATE = 9728


def _cparams(sem):
    return pltpu.CompilerParams(dimension_semantics=sem, vmem_limit_bytes=VMEM_LIMIT)


def _pick(n, prefs):
    for p in prefs:
        if n % p == 0:
            return p
    raise ValueError(f"no tile for {n} in {prefs}")


def _split_dot(a, b):
    hi = a.astype(BF16)
    lo = (a - hi.astype(F32)).astype(BF16)
    return (jnp.dot(hi, b, preferred_element_type=F32)
            + jnp.dot(lo, b, preferred_element_type=F32))


def _split_dot_left(a, b):
    hi = b.astype(BF16)
    lo = (b - hi.astype(F32)).astype(BF16)
    return (jnp.dot(a, hi, preferred_element_type=F32)
            + jnp.dot(a, lo, preferred_element_type=F32))


def _dot_nt(a, b):
    return lax.dot_general(a, b, (((1,), (1,)), ((), ())), preferred_element_type=F32)


def _dot_tn(a, b):
    return lax.dot_general(a, b, (((0,), (0,)), ((), ())), preferred_element_type=F32)


def _neg_softplus(z):
    return -(jnp.maximum(z, 0.0) + jnp.log(1.0 + jnp.exp(-jnp.abs(z))))


def _ffn_body(x_ref, g_ref, wup_ref, wdn_ref, o_ref, xn_ref, *, tf):
    @pl.when(pl.program_id(1) == 0)
    def _():
        x = x_ref[...]
        ms = jnp.mean(x * x, axis=-1, keepdims=True)
        xn_ref[...] = (x * lax.rsqrt(ms + NORM_EPS) * g_ref[...]).astype(BF16)
        o_ref[...] = x

    gu = jnp.dot(xn_ref[...], wup_ref[...], preferred_element_type=F32)
    gate = gu[:, :tf]
    act = (gate * jax.nn.sigmoid(gate) * gu[:, tf:]).astype(BF16)
    o_ref[...] += 0.5 * jnp.dot(act, wdn_ref[...], preferred_element_type=F32)


def _ffn(x, g, wup, wdn, tf):
    T, D = x.shape
    fp = wdn.shape[0]
    tm = _pick(T, (512, 256, 128))
    return pl.pallas_call(
        functools.partial(_ffn_body, tf=tf),
        grid=(T // tm, fp // tf),
        in_specs=[pl.BlockSpec((tm, D), lambda i, j: (i, 0)),
                  pl.BlockSpec((1, D), lambda i, j: (0, 0)),
                  pl.BlockSpec((D, 2 * tf), lambda i, j: (0, j)),
                  pl.BlockSpec((tf, D), lambda i, j: (j, 0))],
        out_specs=pl.BlockSpec((tm, D), lambda i, j: (i, 0)),
        out_shape=jax.ShapeDtypeStruct((T, D), F32),
        scratch_shapes=[pltpu.VMEM((tm, D), BF16)],
        compiler_params=_cparams(("parallel", "arbitrary")),
        name="ffn",
    )(x, g, wup, wdn)


def _proj_body(x_ref, g_ref, w_ref, o_ref, xn_ref):
    @pl.when(pl.program_id(1) == 0)
    def _():
        x = x_ref[...]
        ms = jnp.mean(x * x, axis=-1, keepdims=True)
        xn_ref[...] = (x * lax.rsqrt(ms + NORM_EPS) * g_ref[...]).astype(BF16)

    o_ref[...] = jnp.dot(xn_ref[...], w_ref[...], preferred_element_type=F32)


def _proj(x, g, w):
    T, D = x.shape
    n = w.shape[1]
    tm = _pick(T, (1024, 512, 256, 128))
    tn = _pick(n, (1280, 512))
    return pl.pallas_call(
        _proj_body,
        grid=(T // tm, n // tn),
        in_specs=[pl.BlockSpec((tm, D), lambda i, j: (i, 0)),
                  pl.BlockSpec((1, D), lambda i, j: (0, 0)),
                  pl.BlockSpec((D, tn), lambda i, j: (0, j))],
        out_specs=pl.BlockSpec((tm, tn), lambda i, j: (i, j)),
        out_shape=jax.ShapeDtypeStruct((T, n), F32),
        scratch_shapes=[pltpu.VMEM((tm, D), BF16)],
        compiler_params=_cparams(("parallel", "arbitrary")),
        name="in_proj",
    )(x, g, w)


def _merge_body(osb_ref, ods_ref, ogl_ref, orw_ref, g0_ref, g1_ref, g2_ref, g3_ref,
                wb_ref, wo_ref, x_ref, o_ref):
    @pl.when(pl.program_id(1) == 0)
    def _():
        o_ref[...] = x_ref[...]

    acc = None
    for n, (b_ref, g_ref) in enumerate(((osb_ref, g0_ref), (ods_ref, g1_ref),
                                        (ogl_ref, g2_ref), (orw_ref, g3_ref))):
        p = jnp.dot(b_ref[...].astype(BF16), wb_ref[n], preferred_element_type=F32)
        t = jax.nn.sigmoid(g_ref[...]) * p
        acc = t if acc is None else acc + t
    o_ref[...] += jnp.dot(acc.astype(BF16), wo_ref[...], preferred_element_type=F32)


def _merge(x, z, o_sb, o_dsa, o_gla, o_rw, wb, wo):
    T, D = x.shape
    tm = _pick(T, (512, 256, 128))
    tc = 512
    nc = D // tc
    gate_blk = OFF_GATE // tc
    bspec = pl.BlockSpec((tm, BW), lambda i, c: (i, 0))
    gspecs = [pl.BlockSpec((tm, tc), functools.partial(lambda i, c, n: (i, gate_blk + n * nc + c), n=n))
              for n in range(4)]
    return pl.pallas_call(
        _merge_body,
        grid=(T // tm, nc),
        in_specs=[bspec, bspec, bspec, bspec] + gspecs + [
            pl.BlockSpec((4, BW, tc), lambda i, c: (0, 0, c)),
            pl.BlockSpec((tc, D), lambda i, c: (c, 0)),
            pl.BlockSpec((tm, D), lambda i, c: (i, 0))],
        out_specs=pl.BlockSpec((tm, D), lambda i, c: (i, 0)),
        out_shape=jax.ShapeDtypeStruct((T, D), F32),
        compiler_params=_cparams(("parallel", "arbitrary")),
        name="merge",
    )(o_sb, o_dsa, o_gla, o_rw, z, z, z, z, wb, wo, x)


def _qknorm_body(sq_ref, sk_ref, dq_ref, dk_ref, gsq_ref, gsk_ref, gdq_ref, gdk_ref, bd_ref,
                 osq_ref, osk_ref, odq_ref, odk_ref):
    def norm(x, g, n):
        ms = _split_dot(x * x, bd_ref[:n, :n])
        return x * lax.rsqrt(ms + NORM_EPS) * g

    scale = HD ** -0.5
    osq_ref[...] = (norm(sq_ref[...], gsq_ref[...], 2 * BW) * scale).astype(BF16)
    osk_ref[...] = norm(sk_ref[...], gsk_ref[...], BW)
    odq_ref[...] = (norm(dq_ref[...], gdq_ref[...], 2 * BW) * scale).astype(BF16)
    odk_ref[...] = norm(dk_ref[...], gdk_ref[...], BW)


def _qknorm(z, gsq, gsk, gdq, gdk, bd):
    T = z.shape[0]
    tm = _pick(T, (512, 256, 128))
    wq, wk = 2 * BW, BW
    row = lambda w, off: pl.BlockSpec((tm, w), lambda i: (i, off // w))
    par = lambda w: pl.BlockSpec((1, w), lambda i: (0, 0))
    out = lambda w: pl.BlockSpec((tm, w), lambda i: (i, 0))
    return pl.pallas_call(
        _qknorm_body,
        grid=(T // tm,),
        in_specs=[row(wq, OFF_SQ), row(wk, OFF_SK), row(wq, OFF_DQ), row(wk, OFF_DK),
                  par(wq), par(wk), par(wq), par(wk),
                  pl.BlockSpec((wq, wq), lambda i: (0, 0))],
        out_specs=[out(wq), out(wk), out(wq), out(wk)],
        out_shape=[jax.ShapeDtypeStruct((T, wq), BF16), jax.ShapeDtypeStruct((T, wk), F32),
                   jax.ShapeDtypeStruct((T, wq), BF16), jax.ShapeDtypeStruct((T, wk), F32)],
        compiler_params=_cparams(("parallel",)),
        name="qk_norm",
    )(z, z, z, z, gsq, gsk, gdq, gdk, bd)


def _sb_body(*refs, tq, n_cache, tkc):
    if n_cache:
        q_ref, kn_ref, vn_ref, kc_ref, vc_ref, o_ref = refs
    else:
        q_ref, kn_ref, vn_ref, o_ref = refs
    qb = pl.program_id(1) if not n_cache else 0
    qs = [q_ref[:, e * LANES:(e + 1) * LANES] for e in range(2)]

    def tri(n):
        r = lax.broadcasted_iota(jnp.int32, (n, n), 0)
        c = lax.broadcasted_iota(jnp.int32, (n, n), 1)
        return (r > c).astype(BF16)

    def visit(state, kblk, vblk, u, valid):
        kb16 = kblk.astype(BF16)
        vb16 = vblk.astype(BF16)
        new = []
        for e in range(2):
            carry, acc = state[e]
            z = _dot_nt(qs[e], kb16)
            lg = _neg_softplus(z)
            if valid is not None:
                lg = jnp.where(valid, lg, 0.0)
            inner = _split_dot(lg, u)
            a = jnp.exp(z + lg + inner + carry)
            if valid is not None:
                a = jnp.where(valid, a, 0.0)
            acc = acc + jnp.dot(a.astype(BF16), vb16, preferred_element_type=F32)
            carry = carry + inner[:, :1] + lg[:, :1]
            new.append((carry, acc))
        return tuple(new)

    zero = (jnp.zeros((tq, 1), F32), jnp.zeros((tq, LANES), F32))
    state = (zero, zero)

    r = lax.broadcasted_iota(jnp.int32, (tq, tq), 0)
    c = lax.broadcasted_iota(jnp.int32, (tq, tq), 1)
    u_new = tri(tq)
    if n_cache:
        state = visit(state, kn_ref[...], vn_ref[...], u_new, c < r)
    else:
        row0 = pl.multiple_of(qb * tq, tq)
        state = visit(state, kn_ref[pl.ds(row0, tq), :], vn_ref[pl.ds(row0, tq), :], u_new, c < r)

    def live(c):
        i, st = c
        return (i < n_old) & (jnp.max(jnp.maximum(st[0][0], st[1][0])) > SB_UNDERFLOW)

    if not n_cache:
        n_old = qb

        def step(c):
            i, st = c
            r0 = pl.multiple_of((qb - 1 - i) * tq, tq)
            return i + 1, visit(st, kn_ref[pl.ds(r0, tq), :], vn_ref[pl.ds(r0, tq), :], u_new, None)
    else:
        n_old = n_cache
        u_c = tri(tkc)

        def step(c):
            i, st = c
            r0 = pl.multiple_of((n_cache - 1 - i) * tkc, tkc)
            return i + 1, visit(st, kc_ref[0, 0, pl.ds(r0, tkc), :], vc_ref[0, 0, pl.ds(r0, tkc), :],
                                u_c, None)
    _, state = lax.while_loop(live, step, (jnp.int32(0), state))

    lane = lax.broadcasted_iota(jnp.int32, (tq, LANES), 1)
    o_ref[...] = jnp.where(lane < HD, state[0][1], state[1][1])


def _sb_prompt(qn, kn, z, n_rows):
    tq = 256
    vblk = OFF_SV // LANES
    return pl.pallas_call(
        functools.partial(_sb_body, tq=tq, n_cache=0, tkc=0),
        grid=(NH // 2, n_rows // tq),
        in_specs=[pl.BlockSpec((tq, 2 * LANES), lambda p, i: (i, p)),
                  pl.BlockSpec((n_rows, LANES), lambda p, i: (0, p)),
                  pl.BlockSpec((n_rows, LANES), lambda p, i: (0, vblk + p))],
        out_specs=pl.BlockSpec((tq, LANES), lambda p, i: (i, p)),
        out_shape=jax.ShapeDtypeStruct((n_rows, BW), F32),
        compiler_params=_cparams(("parallel", "arbitrary")),
        name="sb_prompt",
    )(qn, kn, z)


def _q_head(q_ref, h):
    lo = h * LANES + HD * (h % 2)
    return q_ref[:, lo:lo + HD]


def _sb_sample_body(q_ref, kn_ref, vn_ref, kc_ref, vc_ref, o_ref, *, L, n_cache, tkc):
    qs = [_q_head(q_ref, h) for h in range(NH)]

    def tri(n):
        r = lax.broadcasted_iota(jnp.int32, (n, n), 0)
        c = lax.broadcasted_iota(jnp.int32, (n, n), 1)
        return (r > c).astype(BF16)

    def visit(state, kget, vget, u, valid, cached):
        ks = [kget(h).astype(BF16) for h in range(NH)]
        vs = [vget(h).astype(BF16) for h in range(NH)]
        if cached:
            zs = [jnp.dot(qs[h], ks[h], preferred_element_type=F32) for h in range(NH)]
        else:
            zs = [_dot_nt(qs[h], ks[h]) for h in range(NH)]
        lgs = [_neg_softplus(z) for z in zs]
        if valid is not None:
            lgs = [jnp.where(valid, lg, 0.0) for lg in lgs]
        inners = [_split_dot(lg, u) for lg in lgs]
        ws = [jnp.exp(zs[h] + lgs[h] + inners[h] + state[h][0]) for h in range(NH)]
        if valid is not None:
            ws = [jnp.where(valid, w, 0.0) for w in ws]
        ws = [w.astype(BF16) for w in ws]
        if cached:
            pv = [_dot_nt(ws[h], vs[h]) for h in range(NH)]
        else:
            pv = [jnp.dot(ws[h], vs[h], preferred_element_type=F32) for h in range(NH)]
        return tuple((state[h][0] + inners[h][:, :1] + lgs[h][:, :1], state[h][1] + pv[h])
                     for h in range(NH))

    zero = (jnp.zeros((L, 1), F32), jnp.zeros((L, HD), F32))
    r = lax.broadcasted_iota(jnp.int32, (L, L), 0)
    c = lax.broadcasted_iota(jnp.int32, (L, L), 1)
    state = visit((zero,) * NH, lambda h: kn_ref[:, h * HD:(h + 1) * HD],
                  lambda h: vn_ref[:, h * HD:(h + 1) * HD], tri(L), c < r, False)
    u_c = tri(tkc)

    def live(cr):
        i, st = cr
        top = functools.reduce(jnp.maximum, [s[0] for s in st])
        return (i < n_cache) & (jnp.max(top) > SB_UNDERFLOW)

    def step(cr):
        i, st = cr
        r0 = pl.multiple_of((n_cache - 1 - i) * tkc, tkc)
        return i + 1, visit(st, lambda h: kc_ref[0, 0, h, :, pl.ds(r0, tkc)],
                            lambda h: vc_ref[0, 0, h, :, pl.ds(r0, tkc)], u_c, None, True)
    _, state = lax.while_loop(live, step, (jnp.int32(0), state))
    for h in range(NH):
        o_ref[:, h * HD:(h + 1) * HD] = state[h][1]


def _sb_sample(qn, kn, z, cache_k, cache_v, layer, row_off, nb, L):
    P = cache_k.shape[4]
    tkc = _pick(P, (256, 128))
    rb = row_off // L
    cspec = pl.BlockSpec((1, 1, NH, HD, P), lambda b: (layer, b, 0, 0, 0))
    return pl.pallas_call(
        functools.partial(_sb_sample_body, L=L, n_cache=P // tkc, tkc=tkc),
        grid=(nb,),
        in_specs=[pl.BlockSpec((L, NH * LANES), lambda b: (rb + b, 0)),
                  pl.BlockSpec((L, BW), lambda b: (rb + b, 0)),
                  pl.BlockSpec((L, BW), lambda b: (rb + b, OFF_SV // BW)),
                  cspec, cspec],
        out_specs=pl.BlockSpec((L, BW), lambda b: (b, 0)),
        out_shape=jax.ShapeDtypeStruct((nb * L, BW), F32),
        compiler_params=_cparams(("parallel",)),
        name="sb_sample",
    )(qn, kn, z, cache_k, cache_v)


def _sortable(x):
    b = pltpu.bitcast(x + 0.0, jnp.int32)
    return jnp.where(b < 0, b ^ jnp.int32(0x7FFFFFFF), b)


def _select_topk(key_ref, n_blk, tkb, topk, rows):
    def count(pred):
        def body(i, acc):
            c0 = pl.multiple_of(i * tkb, tkb)
            for c in range(tkb // LANES):
                blk = key_ref[:, pl.ds(c0 + c * LANES, LANES)]
                col = c0 + c * LANES + lax.broadcasted_iota(jnp.int32, (rows, LANES), 1)
                acc = acc + jnp.where(pred(blk, col), 1.0, 0.0)
            return acc
        acc = lax.fori_loop(0, n_blk, body, jnp.zeros((rows, LANES), F32))
        return jnp.sum(acc, axis=1, keepdims=True)

    def unresolved(c):
        i, _, n_ge = c
        return (i < 32) & (jnp.max(jnp.abs(n_ge - topk)) > 0)

    def bit_step(c):
        i, thr, n_ge = c
        cand = thr + (jnp.int32(1) << (31 - i))
        n_cand = count(lambda blk, col: blk >= cand)
        take = n_cand >= topk
        return i + 1, jnp.where(take, cand, thr), jnp.where(take, n_cand, n_ge)

    thr0 = jnp.full((rows, 1), jnp.iinfo(jnp.int32).min, jnp.int32)
    n_all = jnp.full((rows, 1), 1.0, F32) * jnp.asarray(n_blk * tkb).astype(F32)
    _, thr, n_ge = lax.while_loop(unresolved, bit_step, (jnp.int32(0), thr0, n_all))
    n_col_bits = max(1, int(math.ceil(math.log2(key_ref.shape[1] + 1))))

    def tie_search():
        need = topk - count(lambda blk, col: blk > thr)

        def col_step(i, j):
            cand = j + (jnp.int32(1) << (n_col_bits - 1 - i))
            n = count(lambda blk, col: (blk == thr) & (col < cand))
            return jnp.where(n < need, cand, j)
        return lax.fori_loop(0, n_col_bits, col_step, jnp.zeros((rows, 1), jnp.int32))

    any_tie = jnp.max(jnp.abs(n_ge - topk)) > 0
    j_hi = lax.cond(any_tie, tie_search,
                    lambda: jnp.full((rows, 1), jnp.iinfo(jnp.int32).max, jnp.int32))
    return thr, j_hi


def _idx_scores(qi, wi, kblk16, k_real, keys_transposed=False):
    acc = None
    for h in range(NH):
        qh = qi[:, h * LANES:h * LANES + k_real].astype(BF16)
        if keys_transposed:
            d = jnp.dot(qh, kblk16, preferred_element_type=F32)
        else:
            d = _dot_nt(qh, kblk16)
        d = jnp.maximum(d, 0.0) * (HD ** -0.5)
        t = (wi[:, h:h + 1] * (NH ** -0.5)) * d
        acc = t if acc is None else acc + t
    return acc


def _dsa_sel_prompt_body(qi_ref, wi_ref, ki_ref, m_ref, key_ref, *, tq, tkb, topk, n_keys):
    qb = pl.program_id(0)
    n_blk = ((qb + 1) * tq + tkb - 1) // tkb
    qi = qi_ref[...]
    wi = wi_ref[...]
    q_chunk = (qb * tq + lax.broadcasted_iota(jnp.int32, (tq, tkb), 0)) // CHUNK

    def score_blk(i, _):
        c0 = pl.multiple_of(i * tkb, tkb)
        s = _idx_scores(qi, wi, ki_ref[pl.ds(c0, tkb), :].astype(BF16), LANES)
        k_chunk = (c0 + lax.broadcasted_iota(jnp.int32, (tq, tkb), 1)) // CHUNK
        s = jnp.where(k_chunk <= q_chunk, s, -jnp.inf)
        key_ref[:, pl.ds(c0, tkb)] = _sortable(s)
        return 0
    lax.fori_loop(0, n_blk, score_blk, 0)

    thr, j_hi = _select_topk(key_ref, n_blk, tkb, topk, tq)
    neg_inf_key = _sortable(jnp.full((1, 1), -jnp.inf, F32))

    def write_blk(i, _):
        c0 = pl.multiple_of(i * tkb, tkb)
        blk = key_ref[:, pl.ds(c0, tkb)]
        col = c0 + lax.broadcasted_iota(jnp.int32, (tq, tkb), 1)
        sel = ((blk > thr) | ((blk == thr) & (col <= j_hi))) & (blk > neg_inf_key)
        m_ref[:, pl.ds(c0, tkb)] = jnp.where(sel, 0.0, NEG_BIG).astype(BF16)
        return 0
    lax.fori_loop(0, n_blk, write_blk, 0)

    def zero_blk(i, _):
        m_ref[:, pl.ds(pl.multiple_of(i * tkb, tkb), tkb)] = jnp.full((tq, tkb), NEG_BIG, BF16)
        return 0
    lax.fori_loop(n_blk, n_keys // tkb, zero_blk, 0)


def _dsa_sel_prompt(z, n_rows, topk):
    tq, tkb = 128, 512
    return pl.pallas_call(
        functools.partial(_dsa_sel_prompt_body, tq=tq, tkb=tkb, topk=topk, n_keys=n_rows),
        grid=(n_rows // tq,),
        in_specs=[pl.BlockSpec((tq, NH * LANES), lambda i: (i, OFF_QI // (NH * LANES))),
                  pl.BlockSpec((tq, LANES), lambda i: (i, OFF_WI // LANES)),
                  pl.BlockSpec((n_rows, LANES), lambda i: (0, OFF_KI // LANES))],
        out_specs=pl.BlockSpec((tq, n_rows), lambda i: (i, 0)),
        out_shape=jax.ShapeDtypeStruct((n_rows, n_rows), BF16),
        scratch_shapes=[pltpu.VMEM((tq, n_rows), jnp.int32)],
        compiler_params=_cparams(("parallel",)),
        name="dsa_select_prompt",
    )(z, z, z)


def _dsa_sel_sample_body(qi_ref, wi_ref, kin_ref, kic_ref, m_ref, key_ref, *, L, P, tkb, topk):
    qi = qi_ref[...]
    wi = wi_ref[...]
    n_c = P // tkb

    def score_blk(i, _):
        c0 = pl.multiple_of(i * tkb, tkb)
        s = _idx_scores(qi, wi, kic_ref[0, 0, :, pl.ds(c0, tkb)].astype(BF16), HD, keys_transposed=True)
        key_ref[:, pl.ds(c0, tkb)] = _sortable(s)
        return 0
    lax.fori_loop(0, n_c, score_blk, 0)

    knew = jnp.concatenate([kin_ref[...], jnp.zeros((tkb - L, LANES), F32)], axis=0).astype(BF16)
    s = _idx_scores(qi, wi, knew, LANES)
    col = lax.broadcasted_iota(jnp.int32, (L, tkb), 1)
    key_ref[:, pl.ds(P, tkb)] = _sortable(jnp.where(col < L, s, -jnp.inf))

    thr, j_hi = _select_topk(key_ref, n_c + 1, tkb, topk, L)
    neg_inf_key = _sortable(jnp.full((1, 1), -jnp.inf, F32))

    def write_blk(i, _):
        c0 = pl.multiple_of(i * tkb, tkb)
        blk = key_ref[:, pl.ds(c0, tkb)]
        cc = c0 + lax.broadcasted_iota(jnp.int32, (L, tkb), 1)
        sel = ((blk > thr) | ((blk == thr) & (cc <= j_hi))) & (blk > neg_inf_key)
        m_ref[:, pl.ds(c0, tkb)] = jnp.where(sel, 0.0, NEG_BIG).astype(BF16)
        return 0
    lax.fori_loop(0, n_c + 1, write_blk, 0)


def _dsa_sel_sample(z, cache_ki, layer, row_off, nb, L, topk):
    P = cache_ki.shape[3]
    tkb = _pick(P, (512, 256, 128))
    rb = row_off // L
    return pl.pallas_call(
        functools.partial(_dsa_sel_sample_body, L=L, P=P, tkb=tkb, topk=topk),
        grid=(nb,),
        in_specs=[pl.BlockSpec((L, NH * LANES), lambda b: (rb + b, OFF_QI // (NH * LANES))),
                  pl.BlockSpec((L, LANES), lambda b: (rb + b, OFF_WI // LANES)),
                  pl.BlockSpec((L, LANES), lambda b: (rb + b, OFF_KI // LANES)),
                  pl.BlockSpec((1, 1, HD, P), lambda b: (layer, b, 0, 0))],
        out_specs=pl.BlockSpec((L, P + tkb), lambda b: (b, 0)),
        out_shape=jax.ShapeDtypeStruct((nb * L, P + tkb), BF16),
        scratch_shapes=[pltpu.VMEM((L, P + tkb), jnp.int32)],
        compiler_params=_cparams(("parallel",)),
        name="dsa_select_sample",
    )(z, z, z, cache_ki)


def _dsa_attn_body(*refs, tq, tkb, n_cache, prompt):
    if prompt:
        q_ref, m_ref, kn_ref, vn_ref, o_ref = refs
        qb = pl.program_id(1)
        n_blk = ((qb + 1) * tq + tkb - 1) // tkb
    else:
        q_ref, m_ref, kn_ref, vn_ref, kc_ref, vc_ref, o_ref = refs
        n_blk = n_cache
    qs = [q_ref[:, e * LANES:(e + 1) * LANES] for e in range(2)]

    def visit(state, kblk, vblk, mblk):
        kb16 = kblk.astype(BF16)
        vb16 = vblk.astype(BF16)
        bias = mblk.astype(F32)
        new = []
        for e in range(2):
            m, l, acc = state[e]
            s = _dot_nt(qs[e], kb16) + bias
            m_new = jnp.maximum(m, jnp.max(s, axis=1, keepdims=True))
            p = jnp.exp(s - m_new)
            alpha = jnp.exp(m - m_new)
            l = alpha * l + jnp.sum(p, axis=1, keepdims=True)
            acc = alpha * acc + jnp.dot(p.astype(BF16), vb16, preferred_element_type=F32)
            new.append((m_new, l, acc))
        return tuple(new)

    init = (jnp.full((tq, 1), NEG_BIG, F32), jnp.zeros((tq, 1), F32), jnp.zeros((tq, LANES), F32))
    state = (init, init)

    def step(i, st):
        c0 = pl.multiple_of(i * tkb, tkb)
        if prompt:
            return visit(st, kn_ref[pl.ds(c0, tkb), :], vn_ref[pl.ds(c0, tkb), :], m_ref[:, pl.ds(c0, tkb)])
        return visit(st, kc_ref[0, 0, pl.ds(c0, tkb), :], vc_ref[0, 0, pl.ds(c0, tkb), :],
                     m_ref[:, pl.ds(c0, tkb)])
    state = lax.fori_loop(0, n_blk, step, state)

    if not prompt:
        pad = jnp.zeros((tkb - tq, LANES), F32)
        knew = jnp.concatenate([kn_ref[...], pad], axis=0)
        vnew = jnp.concatenate([vn_ref[...], pad], axis=0)
        state = visit(state, knew, vnew, m_ref[:, pl.ds(n_cache * tkb, tkb)])

    lane = lax.broadcasted_iota(jnp.int32, (tq, LANES), 1)
    o0 = state[0][2] / state[0][1]
    o1 = state[1][2] / state[1][1]
    o_ref[...] = jnp.where(lane < HD, o0, o1)


def _dsa_attn_prompt(qn, kn, z, mask, n_rows):
    tq, tkb = 256, min(1024, n_rows)
    vblk = OFF_DV // LANES
    return pl.pallas_call(
        functools.partial(_dsa_attn_body, tq=tq, tkb=tkb, n_cache=0, prompt=True),
        grid=(NH // 2, n_rows // tq),
        in_specs=[pl.BlockSpec((tq, 2 * LANES), lambda p, i: (i, p)),
                  pl.BlockSpec((tq, n_rows), lambda p, i: (i, 0)),
                  pl.BlockSpec((n_rows, LANES), lambda p, i: (0, p)),
                  pl.BlockSpec((n_rows, LANES), lambda p, i: (0, vblk + p))],
        out_specs=pl.BlockSpec((tq, LANES), lambda p, i: (i, p)),
        out_shape=jax.ShapeDtypeStruct((n_rows, BW), F32),
        compiler_params=_cparams(("parallel", "arbitrary")),
        name="dsa_attn_prompt",
    )(qn, mask, kn, z)


def _dsa_attn_sample_body(q_ref, m_ref, kn_ref, vn_ref, kc_ref, vc_ref, o_ref, *, L, tkb, n_cache):
    qs = [_q_head(q_ref, h) for h in range(NH)]

    def visit(state, kget, vget, mblk, cached):
        bias = mblk.astype(F32)
        ks = [kget(h).astype(BF16) for h in range(NH)]
        vs = [vget(h).astype(BF16) for h in range(NH)]
        if cached:
            ss = [jnp.dot(qs[h], ks[h], preferred_element_type=F32) + bias for h in range(NH)]
        else:
            ss = [_dot_nt(qs[h], ks[h]) + bias for h in range(NH)]
        ms = [jnp.maximum(state[h][0], jnp.max(ss[h], axis=1, keepdims=True)) for h in range(NH)]
        ps = [jnp.exp(ss[h] - ms[h]) for h in range(NH)]
        al = [jnp.exp(state[h][0] - ms[h]) for h in range(NH)]
        if cached:
            pv = [_dot_nt(ps[h].astype(BF16), vs[h]) for h in range(NH)]
        else:
            pv = [jnp.dot(ps[h].astype(BF16), vs[h], preferred_element_type=F32) for h in range(NH)]
        return tuple((ms[h], al[h] * state[h][1] + jnp.sum(ps[h], axis=1, keepdims=True),
                      al[h] * state[h][2] + pv[h]) for h in range(NH))

    init = (jnp.full((L, 1), NEG_BIG, F32), jnp.zeros((L, 1), F32), jnp.zeros((L, HD), F32))

    def step(i, st):
        c0 = pl.multiple_of(i * tkb, tkb)
        return visit(st, lambda h: kc_ref[0, 0, h, :, pl.ds(c0, tkb)],
                     lambda h: vc_ref[0, 0, h, :, pl.ds(c0, tkb)], m_ref[:, pl.ds(c0, tkb)], True)
    state = lax.fori_loop(0, n_cache, step, (init,) * NH)

    pad = jnp.zeros((tkb - L, HD), F32)
    state = visit(state, lambda h: jnp.concatenate([kn_ref[:, h * HD:(h + 1) * HD], pad], axis=0),
                  lambda h: jnp.concatenate([vn_ref[:, h * HD:(h + 1) * HD], pad], axis=0),
                  m_ref[:, pl.ds(n_cache * tkb, tkb)], False)
    for h in range(NH):
        o_ref[:, h * HD:(h + 1) * HD] = state[h][2] / state[h][1]


def _dsa_attn_sample(qn, kn, z, mask, cache_k, cache_v, layer, row_off, nb, L):
    P = cache_k.shape[4]
    tkb = mask.shape[1] - P
    rb = row_off // L
    cspec = pl.BlockSpec((1, 1, NH, HD, P), lambda b: (layer, b, 0, 0, 0))
    return pl.pallas_call(
        functools.partial(_dsa_attn_sample_body, L=L, tkb=tkb, n_cache=P // tkb),
        grid=(nb,),
        in_specs=[pl.BlockSpec((L, NH * LANES), lambda b: (rb + b, 0)),
                  pl.BlockSpec((L, P + tkb), lambda b: (b, 0)),
                  pl.BlockSpec((L, BW), lambda b: (rb + b, 0)),
                  pl.BlockSpec((L, BW), lambda b: (rb + b, OFF_DV // BW)),
                  cspec, cspec],
        out_specs=pl.BlockSpec((L, BW), lambda b: (b, 0)),
        out_shape=jax.ShapeDtypeStruct((nb * L, BW), F32),
        compiler_params=_cparams(("parallel",)),
        name="dsa_attn_sample",
    )(qn, mask, kn, z, cache_k, cache_v)


def _gla_consts(C):
    nlev = int(math.log2(C))
    t = np.arange(C)
    cum = (t[:, None] >= t[None, :]).astype(np.float32)
    sel = np.zeros((nlev * C, C), np.float32)
    for l in range(nlev):
        m = 1 << l
        mid = (t // (2 * m)) * 2 * m + m
        sel[l * C + t, mid - 1] = 1.0
    return jnp.asarray(cum, BF16), jnp.asarray(sel, BF16)


def _gla_body(gq_ref, gk_ref, gv_ref, gr_ref, ga_ref, a2_ref, ab_ref, on_ref, cum_ref, sel_ref,
              s0_ref, o_ref, sout_ref, st_ref, *, C, n_chunks):
    ci = pl.program_id(1)
    nlev = int(math.log2(C))

    @pl.when(ci == 0)
    def _():
        st_ref[...] = s0_ref[0]

    x = jnp.dot(ga_ref[...].astype(BF16), a2_ref[...], preferred_element_type=F32) + ab_ref[...]
    g = _neg_softplus(-x) / GLA_GATE_NORM
    b = _split_dot_left(cum_ref[...], g)
    c_all = _split_dot_left(sel_ref[...], b)

    row = lax.broadcasted_iota(jnp.int32, (C, C), 0)
    col = lax.broadcasted_iota(jnp.int32, (C, C), 1)
    rowl = lax.broadcasted_iota(jnp.int32, (C, LANES), 0)

    for h in range(GLA_H):
        sl = slice(h * LANES, (h + 1) * LANES)
        q = gq_ref[:, sl] * (HD ** -0.5)
        k = gk_ref[:, sl]
        v = gv_ref[:, sl]
        bh = b[:, sl]
        a = jnp.where(row == col, jnp.sum(q * k, axis=1, keepdims=True), 0.0)
        for l in range(nlev):
            m = 1 << l
            later = (rowl & m) != 0
            ch = c_all[l * C:(l + 1) * C, sl]
            qe = jnp.where(later, q * jnp.exp(jnp.where(later, bh - ch, 0.0)), 0.0)
            ke = jnp.where(later, 0.0, k * jnp.exp(jnp.where(later, 0.0, ch - bh)))
            al = _dot_nt(qe.astype(BF16), ke.astype(BF16))
            a = a + jnp.where((row // (2 * m)) == (col // (2 * m)), al, 0.0)
        st = st_ref[h]
        o = jnp.dot(a.astype(BF16), v.astype(BF16), preferred_element_type=F32)
        o = o + _dot_nt((q * jnp.exp(bh)).astype(BF16), st.astype(BF16))
        b_last = bh[C - 1:C, :]
        kx = k * jnp.exp(b_last - bh)
        st_ref[h] = st * jnp.exp(b_last) + _dot_tn(v.astype(BF16), kx.astype(BF16))
        on = o * lax.rsqrt(jnp.mean(o * o, axis=-1, keepdims=True) + NORM_EPS) * on_ref[...]
        gr = gr_ref[:, sl]
        o_ref[:, sl] = on * (gr * jax.nn.sigmoid(gr))

    @pl.when(ci == n_chunks - 1)
    def _():
        sout_ref[0] = st_ref[...]


def _gla(z, a2p, abp, onorm, s0t, row_off, n_seq, L):
    C = min(CHUNK, L)
    n_chunks = L // C
    rb = row_off // C
    cum, sel = _gla_consts(C)
    blk = lambda off: pl.BlockSpec((C, BW), lambda s, c: (rb + s * n_chunks + c, off // BW))
    full = lambda a: pl.BlockSpec(a.shape, lambda s, c: (0,) * a.ndim)
    return pl.pallas_call(
        functools.partial(_gla_body, C=C, n_chunks=n_chunks),
        grid=(n_seq, n_chunks),
        in_specs=[blk(OFF_GQ), blk(OFF_GK), blk(OFF_GV), blk(OFF_GR),
                  pl.BlockSpec((C, LANES), lambda s, c: (rb + s * n_chunks + c, OFF_GA // LANES)),
                  full(a2p), full(abp), full(onorm), full(cum), full(sel),
                  pl.BlockSpec((1, GLA_H, GLA_DV, LANES), lambda s, c: (s, 0, 0, 0))],
        out_specs=[pl.BlockSpec((C, BW), lambda s, c: (s * n_chunks + c, 0)),
                   pl.BlockSpec((1, GLA_H, GLA_DV, LANES), lambda s, c: (s, 0, 0, 0))],
        out_shape=[jax.ShapeDtypeStruct((n_seq * L, BW), F32),
                   jax.ShapeDtypeStruct((n_seq, GLA_H, GLA_DV, LANES), F32)],
        scratch_shapes=[pltpu.VMEM((GLA_H, GLA_DV, LANES), F32)],
        compiler_params=_cparams(("parallel", "arbitrary")),
        name="gla",
    )(z, z, z, z, z, a2p, abp, onorm, cum, sel, s0t)


def _rw_prep_body(z_ref, head_ref, mu_ref, w0_ref, a0_ref, kkp_ref, kap_ref, rkp_ref,
                  w2_ref, a2_ref, g2_ref, seg_ref,
                  r_ref, w_ref, k_ref, v_ref, nkk_ref, kka_ref, gate_ref, bonus_ref, *, grp):
    zz = z_ref[...]
    tm, wd = zz.shape
    heads = head_ref[...]
    head_rows = jnp.concatenate([jnp.broadcast_to(heads[g:g + 1, :], (grp, wd))
                                 for g in range(tm // grp)], axis=0)
    row = lax.broadcasted_iota(jnp.int32, (tm, wd), 0)
    prev = jnp.where(row % grp == 0, head_rows, pltpu.roll(zz, 1, axis=0))
    zm = zz + mu_ref[...] * (prev - zz)
    rr = zm[:, 0:BW]
    rk = zm[:, BW:2 * BW]
    rv = zm[:, 2 * BW:3 * BW]
    lora = zm[:, 3 * BW:3 * BW + LANES]
    gl = zm[:, 3 * BW + LANES:3 * BW + 2 * LANES]
    xw = w0_ref[...] + jnp.dot(jnp.tanh(lora).astype(BF16), w2_ref[...], preferred_element_type=F32)
    wlog = _neg_softplus(-xw) - 0.5
    a = jax.nn.sigmoid(a0_ref[...] + jnp.dot(lora.astype(BF16), a2_ref[...], preferred_element_type=F32))
    gate = jnp.dot(jax.nn.sigmoid(gl).astype(BF16), g2_ref[...], preferred_element_type=F32)
    kkf = rk * kkp_ref[...]
    nrm = jnp.sqrt(_split_dot(kkf * kkf, seg_ref[...]))
    kk = kkf / jnp.maximum(nrm, 1e-12)
    kmod = rk * (1.0 + (a - 1.0) * kap_ref[...])
    coef = _split_dot(rr * kmod * rkp_ref[...], seg_ref[...])
    r_ref[...] = rr
    w_ref[...] = -jnp.exp(wlog)
    k_ref[...] = kmod
    v_ref[...] = rv
    nkk_ref[...] = -kk
    kka_ref[...] = kk * a
    gate_ref[...] = gate
    bonus_ref[...] = coef * rv


def _rw_prep(z, heads, grp, mu, w0, a0, kkp, kap, rkp, w2p, a2p, g2, seg):
    T = z.shape[0]
    tm = SUBLANES * grp
    assert T % tm == 0, (T, tm)
    W = 4 * BW
    par = lambda a: pl.BlockSpec(a.shape, lambda i: (0,) * a.ndim)
    out = pl.BlockSpec((tm, BW), lambda i: (i, 0))
    return pl.pallas_call(
        functools.partial(_rw_prep_body, grp=grp),
        grid=(T // tm,),
        in_specs=[pl.BlockSpec((tm, W), lambda i: (i, 0)), pl.BlockSpec((SUBLANES, W), lambda i: (i, 0)),
                  par(mu), par(w0), par(a0), par(kkp), par(kap), par(rkp),
                  par(w2p), par(a2p), par(g2), par(seg)],
        out_specs=[out] * 8,
        out_shape=[jax.ShapeDtypeStruct((T, BW), F32)] * 8,
        compiler_params=_cparams(("parallel",)),
        name="rwkv_prep",
    )(z, heads, mu, w0, a0, kkp, kap, rkp, w2p, a2p, g2, seg)


def _rw_rec_body(r_ref, w_ref, k_ref, v_ref, nkk_ref, kka_ref, g_ref, s0_ref,
                 ot_ref, sout_ref, s_ref, *, TT, L, NS, n_tiles):
    ti = pl.program_id(0)
    npair = NH // 2
    R = NS * npair * HD
    gm = g_ref[...]
    lane = lax.broadcasted_iota(jnp.int32, (R, LANES), 1)
    rowi = lax.broadcasted_iota(jnp.int32, (R, LANES), 0)
    eye2 = (lane % HD) == (rowi % HD)
    seq_lane0 = (rowi // (npair * HD)) * L
    refs =(r_ref, w_ref, k_ref, v_ref, nkk_ref, kka_ref)

    def token_group(gi, carry):
        s, o_lo, o_hi = carry
        t0 = pl.multiple_of(gi * SUBLANES, SUBLANES)
        blk = [[ref[pl.ds(q * L + t0, SUBLANES), :] for q in range(NS)] for ref in refs]
        for j in range(SUBLANES):
            def stacked(x):
                return jnp.concatenate(
                    [jnp.broadcast_to(x[q][j:j + 1, p * LANES:(p + 1) * LANES], (HD, LANES))
                     for q in range(NS) for p in range(npair)], axis=0)
            rr, ww, kk, vv, nk, ka = [stacked(x) for x in blk]
            sa = _split_dot(s * nk, gm)
            vb = jnp.dot(jnp.where(eye2, vv, 0.0).astype(BF16), gm, preferred_element_type=F32)
            s = s * ww + sa * ka + vb * kk
            oo = jnp.dot((s * rr).astype(BF16), gm, preferred_element_type=F32)
            hit = lane == seq_lane0 + t0 + j
            o_lo = jnp.where(hit, oo, o_lo)
            o_hi = jnp.where(hit, pltpu.roll(oo, HD, axis=1), o_hi)
        return s, o_lo, o_hi

    if NS == 1:
        @pl.when(ti == 0)
        def _():
            s_ref[...] = s0_ref[0]
        s_in = s_ref[...]
    else:
        s_in = s0_ref[...].reshape(R, LANES)
    zero = jnp.zeros((R, LANES), F32)
    s_out, o_lo, o_hi = lax.fori_loop(0, min(L, TT) // SUBLANES, token_group, (s_in, zero, zero))

    pr = npair * HD
    a1 = sum(o_lo[q * pr:(q + 1) * pr] for q in range(NS))
    a2 = sum(o_hi[q * pr:(q + 1) * pr] for q in range(NS))
    first_half = lax.broadcasted_iota(jnp.int32, (pr, LANES), 1) < HD
    lo = jnp.where(first_half, a1, a2)
    hi = jnp.where(first_half, a2, a1)
    for p in range(npair):
        ot_ref[2 * p * HD:(2 * p + 1) * HD, :] = lo[p * HD:(p + 1) * HD]
        ot_ref[(2 * p + 1) * HD:(2 * p + 2) * HD, :] = hi[p * HD:(p + 1) * HD]
    if NS == 1:
        s_ref[...] = s_out

        @pl.when(ti == n_tiles - 1)
        def _():
            sout_ref[0] = s_out
    else:
        sout_ref[...] = s_out.reshape(NS, pr, LANES)


def _rw_rec(r, w, k, v, nkk, kka, gmat, s0, row_off, n_seq, L):
    TT = LANES
    n_tok = n_seq * L
    n_tiles = n_tok // TT
    rb = row_off // TT
    ns = max(1, TT // L)
    pr = (NH // 2) * HD
    tok = pl.BlockSpec((TT, BW), lambda i: (rb + i, 0))
    if ns == 1:
        sspec = pl.BlockSpec((1, pr, LANES), lambda i: (0, 0, 0))
    else:
        sspec = pl.BlockSpec((ns, pr, LANES), lambda i: (i, 0, 0))
    return pl.pallas_call(
        functools.partial(_rw_rec_body, TT=TT, L=L, NS=ns, n_tiles=n_tiles),
        grid=(n_tiles,),
        in_specs=[tok] * 6 + [pl.BlockSpec((LANES, LANES), lambda i: (0, 0)), sspec],
        out_specs=[pl.BlockSpec((BW, TT), lambda i: (0, i)), sspec],
        out_shape=[jax.ShapeDtypeStruct((BW, n_tok), F32),
                   jax.ShapeDtypeStruct(s0.shape, F32)],
        scratch_shapes=[pltpu.VMEM((pr, LANES), F32)],
        compiler_params=_cparams(("arbitrary",)),
        name="rwkv_recurrence",
    )(r, w, k, v, nkk, kka, gmat, s0)


def _mm3(a, b):
    ah = a.astype(BF16)
    al = (a - ah.astype(F32)).astype(BF16)
    bh = b.astype(BF16)
    bl = (b - bh.astype(F32)).astype(BF16)
    d = lambda x, y: jnp.dot(x, y, preferred_element_type=F32)
    return d(ah, bh) + (d(ah, bl) + d(al, bh))


def _rw_chunk_body(r_ref, lw_ref, k_ref, v_ref, nkk_ref, kka_ref, s0_ref, o_ref, sout_ref, s_ref,
                   *, TT, L, n_tiles):
    ti = pl.program_id(0)
    npair = NH // 2
    nblk = TT // RW_BLK
    row = lax.broadcasted_iota(jnp.int32, (TT, TT), 0)
    col = lax.broadcasted_iota(jnp.int32, (TT, TT), 1)
    same = (row // RW_BLK) == (col // RW_BLK)
    strict = same & (col < row)
    incl = same & (col <= row)
    tri = jnp.where(incl, 1.0, 0.0).astype(BF16)
    last = jnp.where(same & (col % RW_BLK == RW_BLK - 1), 1.0, 0.0).astype(BF16)
    eye = jnp.where(row == col, 1.0, 0.0)
    pair_diag = (row < HD) == (col < HD)
    lo_half = lax.broadcasted_iota(jnp.int32, (TT, LANES), 1) < HD
    lo_half_b = lax.broadcasted_iota(jnp.int32, (RW_BLK, LANES), 1) < HD

    r, lw, k, v = r_ref[...], lw_ref[...], k_ref[...], v_ref[...]
    nk, ka = nkk_ref[...], kka_ref[...]
    lc = _split_dot_left(tri, lw)
    le = _split_dot_left(last, lc)
    nkt = nk * jnp.exp(lc - lw)
    rt = r * jnp.exp(lc)
    inv = jnp.exp(-lc)
    kah = ka * inv
    kh = k * inv
    rem = jnp.exp(le - lc)
    kaw = (ka * rem).astype(BF16)
    kw = (k * rem).astype(BF16)
    wb = jnp.exp(le)

    sls = [slice(p * LANES, (p + 1) * LANES) for p in range(npair)]
    v16 = [v[:, sl].astype(BF16) for sl in sls]
    heads = [(p, e) for p in range(npair) for e in range(2)]
    grams = []
    for p, e in heads:
        hm = lo_half if e == 0 else jnp.logical_not(lo_half)
        lhs = jnp.concatenate([jnp.where(hm, nkt[:, sls[p]], 0.0), jnp.where(hm, rt[:, sls[p]], 0.0)],
                              axis=0).astype(BF16)
        rhs = jnp.concatenate([kah[:, sls[p]], kh[:, sls[p]]], axis=0).astype(BF16)
        grams.append(_dot_nt(lhs, rhs))
    n1 = [jnp.where(strict, g[:TT, :TT], 0.0) for g in grams]
    bt = [jnp.where(strict, g[:TT, TT:], 0.0).astype(BF16) for g in grams]
    cmat = [jnp.where(incl, g[TT:, :TT], 0.0).astype(BF16) for g in grams]
    et = [jnp.where(incl, g[TT:, TT:], 0.0).astype(BF16) for g in grams]
    n2 = [_mm3(a, a) for a in n1]
    n4 = [_mm3(a, a) for a in n2]
    n8 = [_mm3(a, a) for a in n4]
    tinv = [eye + a for a in n1]
    for power in (n2, n4, n8):
        tinv = [t + _mm3(t, a) for t, a in zip(tinv, power)]
    vb_h = [jnp.dot(b_, v16[p], preferred_element_type=F32) for b_, (p, e) in zip(bt, heads)]
    ev_h = [jnp.dot(e_, v16[p], preferred_element_type=F32) for e_, (p, e) in zip(et, heads)]
    vb = [jnp.where(lo_half, vb_h[2 * p], vb_h[2 * p + 1]) for p in range(npair)]
    ev = [jnp.where(lo_half, ev_h[2 * p], ev_h[2 * p + 1]) for p in range(npair)]

    if L >= TT:
        @pl.when(ti == 0)
        def _():
            s_ref[...] = s0_ref[0]
        state = {(0, p): s_ref[p] for p in range(npair)}
        rounds = [[j] for j in range(nblk)]
    else:
        bps = L // RW_BLK
        state = {(q, p): s0_ref[q, p] for q in range(TT // L) for p in range(npair)}
        rounds = [[q * bps + i for q in range(TT // L)] for i in range(bps)]
    sa_rows = [[None] * nblk for _ in range(npair)]
    os_rows = [[None] * nblk for _ in range(npair)]
    for blocks in rounds:
        items = [(j, p) for j in blocks for p in range(npair)]
        seq = lambda j: (j * RW_BLK) // L if L < TT else 0
        xs = {}
        for j, p in items:
            bs = slice(j * RW_BLK, (j + 1) * RW_BLK)
            s = state[(seq(j), p)]
            s_hi = s.astype(BF16)
            s_lo = (s - s_hi.astype(F32)).astype(BF16)
            lhs = jnp.concatenate([nkt[bs, sls[p]], rt[bs, sls[p]]], axis=0).astype(BF16)
            xs[(j, p)] = (jnp.dot(lhs, s_hi, preferred_element_type=F32)
                          + jnp.dot(lhs, s_lo, preferred_element_type=F32))
        sa = {}
        for j, p in items:
            b0 = j * RW_BLK
            bs = slice(b0, b0 + RW_BLK)
            parts = [xs[(j, p)][:RW_BLK] + vb[p][bs]]
            if b0:
                parts.insert(0, jnp.zeros((b0, LANES), F32))
            if TT - b0 - RW_BLK:
                parts.append(jnp.zeros((TT - b0 - RW_BLK, LANES), F32))
            xfull = jnp.concatenate(parts, axis=0)
            sa[(j, p)] = jnp.where(lo_half_b, _mm3(tinv[2 * p][bs], xfull),
                                   _mm3(tinv[2 * p + 1][bs], xfull))
        for j, p in items:
            b0 = j * RW_BLK
            bs = slice(b0, b0 + RW_BLK)
            u = (_dot_tn(kaw[bs, sls[p]], sa[(j, p)].astype(BF16)) + _dot_tn(kw[bs, sls[p]], v16[p][bs]))
            wcol = jnp.broadcast_to(wb[b0:b0 + 1, sls[p]], (LANES, LANES)).T
            state[(seq(j), p)] = wcol * state[(seq(j), p)] + jnp.where(pair_diag, u, 0.0)
            sa_rows[p][j] = sa[(j, p)]
            os_rows[p][j] = xs[(j, p)][RW_BLK:]
    if L < TT:
        for (q, p), s in state.items():
            sout_ref[q, p] = s

    for p in range(npair):
        sa_t = jnp.concatenate(sa_rows[p], axis=0).astype(BF16)
        o_sa = jnp.where(lo_half, jnp.dot(cmat[2 * p], sa_t, preferred_element_type=F32),
                         jnp.dot(cmat[2 * p + 1], sa_t, preferred_element_type=F32))
        o_ref[:, sls[p]] = jnp.concatenate(os_rows[p], axis=0) + o_sa + ev[p]
    if L >= TT:
        for p in range(npair):
            s_ref[p] = state[(0, p)]

        @pl.when(ti == n_tiles - 1)
        def _():
            for p in range(npair):
                sout_ref[0, p] = state[(0, p)]


def _rw_chunk(r, lw, k, v, nkk, kka, s0, row_off, n_seq, L):
    TT = LANES
    assert L % RW_BLK == 0 and (L % TT == 0 or TT % L == 0)
    n_tok = n_seq * L
    n_tiles = n_tok // TT
    rb = row_off // TT
    ns = max(1, TT // L)
    tok = pl.BlockSpec((TT, BW), lambda i: (rb + i, 0))
    if ns == 1:
        sspec = pl.BlockSpec((1, NH // 2, LANES, LANES), lambda i: (0, 0, 0, 0))
    else:
        sspec = pl.BlockSpec((ns, NH // 2, LANES, LANES), lambda i: (i, 0, 0, 0))
    return pl.pallas_call(
        functools.partial(_rw_chunk_body, TT=TT, L=L, n_tiles=n_tiles),
        grid=(n_tiles,),
        in_specs=[tok] * 6 + [sspec],
        out_specs=[pl.BlockSpec((TT, BW), lambda i: (i, 0)), sspec],
        out_shape=[jax.ShapeDtypeStruct((n_tok, BW), F32),
                   jax.ShapeDtypeStruct(s0.shape, F32)],
        scratch_shapes=[pltpu.VMEM((NH // 2, LANES, LANES), F32)],
        compiler_params=_cparams(("arbitrary",)),
        name="rwkv_chunk",
    )(r, lw, k, v, nkk, kka, s0)


def _rw_post_body(o_ref, bonus_ref, gate_ref, lnw_ref, lnb_ref, seg_ref, out_ref):
    o = o_ref[...]
    mu = _split_dot(o, seg_ref[...]) * (1.0 / HD)
    d = o - mu
    var = _split_dot(d * d, seg_ref[...]) * (1.0 / HD)
    on = d * lax.rsqrt(var + RW_LN_EPS) * lnw_ref[...] + lnb_ref[...]
    out_ref[...] = (on + bonus_ref[...]) * gate_ref[...]


def _rw_post(o, bonus, gate, lnw, lnb, seg):
    T = o.shape[0]
    tm = _pick(T, (512, 256, 128))
    blk = pl.BlockSpec((tm, BW), lambda i: (i, 0))
    par = lambda a: pl.BlockSpec(a.shape, lambda i: (0,) * a.ndim)
    return pl.pallas_call(
        _rw_post_body,
        grid=(T // tm,),
        in_specs=[blk, blk, blk, par(lnw), par(lnb), par(seg)],
        out_specs=blk,
        out_shape=jax.ShapeDtypeStruct((T, BW), F32),
        compiler_params=_cparams(("parallel",)),
        name="rwkv_post",
    )(o, bonus, gate, lnw, lnb, seg)


def _heads_alt(w):
    d = w.shape[0]
    w4 = w.reshape(d, NH // 2, 2, HD)
    zero = jnp.zeros((d, NH // 2, HD), w.dtype)
    ev = jnp.concatenate([w4[:, :, 0], zero], -1)
    od = jnp.concatenate([zero, w4[:, :, 1]], -1)
    return jnp.stack([ev, od], 2).reshape(d, NH * LANES)


def _heads_lo(w, nh):
    d = w.shape[0]
    w3 = w.reshape(d, nh, HD)
    return jnp.concatenate([w3, jnp.zeros_like(w3)], -1).reshape(d, nh * LANES)


def _padc(w, n):
    return jnp.pad(w, ((0, 0), (0, n - w.shape[1])))


def _rw_reorder(a):
    o = 0
    parts = {}
    for name, wd in (("rr", BW), ("wl", RW_LORA), ("rk", BW), ("rv", BW), ("al", RW_LORA), ("gl", RW_GATE_LORA)):
        parts[name] = a[..., o:o + wd]
        o += wd
    return jnp.concatenate([parts[n] for n in ("rr", "rk", "rv", "wl", "al", "gl")], axis=-1)


def _rw_unorder(a):
    rr, rk, rv = a[..., 0:BW], a[..., BW:2 * BW], a[..., 2 * BW:3 * BW]
    wl = a[..., 3 * BW:3 * BW + RW_LORA]
    al = a[..., 3 * BW + RW_LORA:3 * BW + 2 * RW_LORA]
    gl = a[..., 3 * BW + 2 * RW_LORA:RW_COLS]
    return jnp.concatenate([rr, wl, rk, rv, al, gl], axis=-1)


def _relayout_w_in(w, d_model):
    w = w.astype(BF16)
    o = 0

    def take(n):
        nonlocal o
        s = w[:, o:o + n]
        o += n
        return s
    sq, sk, sv = take(BW), take(BW), take(BW)
    dq, dk, dv, qi, ki, wi = take(BW), take(BW), take(BW), take(BW), take(HD), take(NH)
    gq, gk = take(GLA_H * HD), take(GLA_H * HD)
    gv, gr, ga = take(BW), take(BW), take(GLA_LOWRANK)
    rw = take(RW_COLS)
    gate = take(4 * d_model)
    cols = [_padc(_rw_reorder(rw), 4 * BW), _heads_alt(sq), sk, sv, _heads_alt(dq), dk, dv,
            _heads_lo(qi, NH), _heads_lo(gq, GLA_H), _heads_lo(gk, GLA_H), gv, gr,
            _padc(ki, LANES), _padc(wi, LANES), _padc(ga, 2 * LANES), gate]
    return jnp.concatenate(cols, axis=1)


def _relayout_ffn(w_up, w_down, tf):
    d, f2 = w_up.shape
    f = f2 // 2
    fp = -(-f // tf) * tf
    g = _padc(w_up[:, :f].astype(BF16), fp).reshape(d, fp // tf, tf)
    u = _padc(w_up[:, f:].astype(BF16), fp).reshape(d, fp // tf, tf)
    wup = jnp.concatenate([g, u], axis=2).reshape(d, 2 * fp)
    wdn = jnp.pad(w_down.astype(BF16), ((0, fp - f), (0, 0)))
    return wup, wdn


def _gain_alt(g):
    return jnp.tile(g, 2 * NH)[None, :]


def kernel(x_prompt, x_sample, cache_sb_k, cache_sb_v, cache_dsa_k, cache_dsa_v, cache_dsa_kidx, state_gla, state_rwkv, state_rwkv_shift, ffn1_norm, ffn1_up, ffn1_down, mix_norm, w_in, sb_qnorm, sb_knorm, dsa_qnorm, dsa_knorm, gla_a2, gla_ab, gla_onorm, rwkv_mu, rwkv_w0, rwkv_w2, rwkv_a0, rwkv_a2, rwkv_g2, rwkv_kk, rwkv_ka, rwkv_rk, rwkv_lnw, rwkv_lnb, w_branch, w_out, ffn2_norm, ffn2_up, ffn2_down):
    depth = w_in.shape[0]
    bp, lp, d_model = x_prompt.shape
    nb, ls, _ = x_sample.shape
    assert bp == 1, "prompt group is one sequence"
    tp = bp * lp
    ts = nb * ls
    past = cache_sb_k.shape[2]
    to_pos_minor = lambda c: jnp.transpose(c, (0, 1, 3, 4, 2))
    ck_sb, cv_sb = to_pos_minor(cache_sb_k), to_pos_minor(cache_sb_v)
    ck_dsa, cv_dsa = to_pos_minor(cache_dsa_k), to_pos_minor(cache_dsa_v)
    cki = jnp.transpose(cache_dsa_kidx, (0, 1, 3, 2))
    tf = 512

    x = jnp.concatenate([x_prompt.reshape(tp, d_model), x_sample.reshape(ts, d_model)], axis=0)

    bd = jnp.asarray(np.kron(np.eye(2 * NH), np.full((HD, HD), 1.0 / HD)), BF16)
    seg = jnp.asarray(np.kron(np.eye(NH), np.ones((HD, HD))), BF16)
    gmat = jnp.asarray(np.kron(np.eye(2), np.ones((HD, HD))), BF16)
    topk_p = min(TOPK_MAX, lp // 4)
    topk_s = min(TOPK_MAX, (past + ls) // 4)

    new_p = [[] for _ in range(8)]
    new_s = [[] for _ in range(8)]
    for i in range(depth):
        wup1, wdn1 = _relayout_ffn(ffn1_up[i], ffn1_down[i], tf)
        wup2, wdn2 = _relayout_ffn(ffn2_up[i], ffn2_down[i], tf)
        win = _relayout_w_in(w_in[i], d_model)

        x = _ffn(x, ffn1_norm[i][None], wup1, wdn1, tf)
        z = _proj(x, mix_norm[i][None], win)

        sqn, skn, dqn, dkn = _qknorm(z, _gain_alt(sb_qnorm[i]), jnp.tile(sb_knorm[i], NH)[None],
                                     _gain_alt(dsa_qnorm[i]), jnp.tile(dsa_knorm[i], NH)[None], bd)
        o_sb = jnp.concatenate([
            _sb_prompt(sqn, skn, z, tp),
            _sb_sample(sqn, skn, z, ck_sb, cv_sb, i, tp, nb, ls)], axis=0)
        mask_p = _dsa_sel_prompt(z, tp, topk_p)
        mask_s = _dsa_sel_sample(z, cki, i, tp, nb, ls, topk_s)
        o_dsa = jnp.concatenate([
            _dsa_attn_prompt(dqn, dkn, z, mask_p, tp),
            _dsa_attn_sample(dqn, dkn, z, mask_s, ck_dsa, cv_dsa, i, tp, nb, ls)], axis=0)

        a2p = jnp.pad(_heads_lo(gla_a2[i], GLA_H), ((0, LANES - GLA_LOWRANK), (0, 0))).astype(BF16)
        abp = _heads_lo(gla_ab[i][None], GLA_H)
        onorm = gla_onorm[i][None]

        def gla_state_in(s):
            return jnp.pad(jnp.swapaxes(s, 2, 3), ((0, 0), (0, 0), (0, 0), (0, LANES - HD)))

        def gla_state_out(s):
            return jnp.swapaxes(s[..., :HD], 2, 3)
        og_p, sg_p = _gla(z, a2p, abp, onorm, jnp.zeros((bp, GLA_H, GLA_DV, LANES), F32), 0, bp, lp)
        og_s, sg_s = _gla(z, a2p, abp, onorm, gla_state_in(state_gla[i]), tp, nb, ls)
        o_gla = jnp.concatenate([og_p, og_s], axis=0)

        zrw = z[:, :RW_COLS]
        shift_s = _rw_reorder(state_rwkv_shift[i])
        heads = jnp.concatenate([jnp.zeros((1, 4 * BW), F32), z[ls - 1:tp - 1:ls, :4 * BW],
                                 _padc(shift_s.reshape(nb, RW_COLS), 4 * BW)], axis=0)
        mu = _padc(_rw_reorder(rwkv_mu[i])[None], 4 * BW)
        w2p = jnp.concatenate([rwkv_w2[i], jnp.zeros_like(rwkv_a2[i])], axis=0).astype(BF16)
        a2q = jnp.concatenate([jnp.zeros_like(rwkv_w2[i]), rwkv_a2[i]], axis=0).astype(BF16)
        r_, w_, k_, v_, nkk_, kka_, gate_, bonus_ = _rw_prep(
            z, heads, ls, mu, rwkv_w0[i][None], rwkv_a0[i][None], rwkv_kk[i][None], rwkv_ka[i][None],
            rwkv_rk[i][None], w2p, a2q, rwkv_g2[i].astype(BF16), seg)

        def rw_state_in(s):
            n = s.shape[0]
            st = jnp.swapaxes(s, 2, 3).reshape(n, NH // 2, 2, HD, HD)
            zero = jnp.zeros((n, NH // 2, HD, HD), s.dtype)
            return jnp.concatenate([jnp.concatenate([st[:, :, 0], zero], axis=-1),
                                    jnp.concatenate([zero, st[:, :, 1]], axis=-1)], axis=-2)

        def rw_state_out(s):
            n = s.shape[0]
            st = jnp.stack([s[:, :, :HD, :HD], s[:, :, HD:, HD:]], axis=2)
            return jnp.swapaxes(st, 3, 4).reshape(n, NH, HD, HD)
        o_rp, sr_p = _rw_chunk(r_, w_, k_, v_, nkk_, kka_, jnp.zeros((bp, NH // 2, LANES, LANES), F32), 0, bp, lp)
        o_rs, sr_s = _rw_chunk(r_, w_, k_, v_, nkk_, kka_, rw_state_in(state_rwkv[i]), tp, nb, ls)
        o_r = jnp.concatenate([o_rp, o_rs], axis=0)
        o_rw = _rw_post(o_r, bonus_, gate_, rwkv_lnw[i][None], rwkv_lnb[i][None], seg)

        x = _merge(x, z, o_sb, o_dsa, o_gla, o_rw, w_branch[i].astype(BF16), w_out[i].astype(BF16))
        x = _ffn(x, ffn2_norm[i][None], wup2, wdn2, tf)

        sv = z[:, OFF_SV:OFF_SV + BW]
        dv = z[:, OFF_DV:OFF_DV + BW]
        ki = z[:, OFF_KI:OFF_KI + HD]
        for dst, (lo, n, l) in ((new_p, (0, bp, lp)), (new_s, (tp, nb, ls))):
            hi = lo + n * l
            dst[0].append(skn[lo:hi].reshape(n, l, NH, HD))
            dst[1].append(sv[lo:hi].reshape(n, l, NH, HD))
            dst[2].append(dkn[lo:hi].reshape(n, l, NH, HD))
            dst[3].append(dv[lo:hi].reshape(n, l, NH, HD))
            dst[4].append(ki[lo:hi].reshape(n, l, HD))
            dst[7].append(_rw_unorder(zrw[lo:hi].reshape(n, l, RW_COLS)[:, -1:]))
        new_p[5].append(gla_state_out(sg_p))
        new_s[5].append(gla_state_out(sg_s))
        new_p[6].append(rw_state_out(sr_p))
        new_s[6].append(rw_state_out(sr_s))

    outs_p = [jnp.stack(l, axis=0) for l in new_p]
    outs_s = [jnp.stack(l, axis=0) for l in new_s]
    y_p = x[:tp].reshape(bp, lp, d_model)
    y_s = x[tp:].reshape(nb, ls, d_model)
    return (y_p, y_s, *outs_p, *outs_s)
```

```python
import functools
import math

import jax
import jax.numpy as jnp
import numpy as np
from jax import lax
from jax.experimental import pallas as pl
from jax.experimental.pallas import tpu as pltpu

F32 = jnp.float32
BF16 = jnp.bfloat16

LANES = 128
SUBLANES = 8
HD = 64
NH = 8
BW = NH * HD
GLA_H = 4
GLA_DV = 128
GLA_LOWRANK = 16
GLA_GATE_NORM = 16.0
CHUNK = 64
TOPK_MAX = 256
RW_LORA = 64
RW_GATE_LORA = 128
RW_BLK = 16
RW_COLS = 3 * BW + 2 * RW_LORA + RW_GATE_LORA
NORM_EPS = 1e-6
RW_LN_EPS = 64e-5
NEG_BIG = -1e30
SB_UNDERFLOW = -104.0
VMEM_LIMIT = 56 * 1024 * 1024

OFF_RW = 0
OFF_SQ = 2048
OFF_SK = 3072
OFF_SV = 3584
OFF_DQ = 4096
OFF_DK = 5120
OFF_DV = 5632
OFF_QI = 6144
OFF_GQ = 7168
OFF_GK = 7680
OFF_GV = 8192
OFF_GR = 8704
OFF_KI = 9216
OFF_WI = 9344
OFF_GA = 9472
OFF_GATE = 9728


def _cparams(sem):
    return pltpu.CompilerParams(dimension_semantics=sem, vmem_limit_bytes=VMEM_LIMIT)


def _pick(n, prefs):
    for p in prefs:
        if n % p == 0:
            return p
    raise ValueError(f"no tile for {n} in {prefs}")


def _split_dot(a, b):
    hi = a.astype(BF16)
    lo = (a - hi.astype(F32)).astype(BF16)
    return (jnp.dot(hi, b, preferred_element_type=F32)
            + jnp.dot(lo, b, preferred_element_type=F32))


def _split_dot_left(a, b):
    hi = b.astype(BF16)
    lo = (b - hi.astype(F32)).astype(BF16)
    return (jnp.dot(a, hi, preferred_element_type=F32)
            + jnp.dot(a, lo, preferred_element_type=F32))


def _dot_nt(a, b):
    return lax.dot_general(a, b, (((1,), (1,)), ((), ())), preferred_element_type=F32)


def _dot_tn(a, b):
    return lax.dot_general(a, b, (((0,), (0,)), ((), ())), preferred_element_type=F32)


def _neg_softplus(z):
    return -(jnp.maximum(z, 0.0) + jnp.log(1.0 + jnp.exp(-jnp.abs(z))))


def _ffn_body(x_ref, g_ref, wg_ref, wu_ref, wdn_ref, o_ref, xn_ref):
    @pl.when(pl.program_id(1) == 0)
    def _():
        x = x_ref[...]
        ms = jnp.mean(x * x, axis=-1, keepdims=True)
        xn_ref[...] = (x * lax.rsqrt(ms + NORM_EPS) * g_ref[0]).astype(BF16)
        o_ref[...] = x

    xn = xn_ref[...]
    gate = jnp.dot(xn, wg_ref[0], preferred_element_type=F32)
    up = jnp.dot(xn, wu_ref[0], preferred_element_type=F32)
    act = (gate * jax.nn.sigmoid(gate) * up).astype(BF16)
    o_ref[...] += 0.5 * jnp.dot(act, wdn_ref[0], preferred_element_type=F32)


def _ffn(x, g, wup, wdn, tf, layer):
    T, D = x.shape
    nf = wdn.shape[1] // tf
    tm = _pick(T, (512, 256, 128))
    return pl.pallas_call(
        _ffn_body,
        grid=(T // tm, nf),
        in_specs=[pl.BlockSpec((tm, D), lambda i, j: (i, 0)),
                  pl.BlockSpec((1, 1, D), lambda i, j: (layer, 0, 0)),
                  pl.BlockSpec((1, D, tf), lambda i, j: (layer, 0, j)),
                  pl.BlockSpec((1, D, tf), lambda i, j: (layer, 0, nf + j)),
                  pl.BlockSpec((1, tf, D), lambda i, j: (layer, j, 0))],
        out_specs=pl.BlockSpec((tm, D), lambda i, j: (i, 0)),
        out_shape=jax.ShapeDtypeStruct((T, D), F32),
        scratch_shapes=[pltpu.VMEM((tm, D), BF16)],
        compiler_params=_cparams(("parallel", "arbitrary")),
        name="ffn",
    )(x, g, wup, wup, wdn)


def _proj_body(x_ref, g_ref, w_ref, o_ref, xn_ref):
    @pl.when(pl.program_id(1) == 0)
    def _():
        x = x_ref[...]
        ms = jnp.mean(x * x, axis=-1, keepdims=True)
        xn_ref[...] = (x * lax.rsqrt(ms + NORM_EPS) * g_ref[0]).astype(BF16)

    o_ref[...] = jnp.dot(xn_ref[...], w_ref[0], preferred_element_type=F32)


def _proj(x, g, w, layer):
    T, D = x.shape
    n = w.shape[2]
    tm = _pick(T, (1024, 512, 256, 128))
    tn = _pick(n, (1280, 512))
    return pl.pallas_call(
        _proj_body,
        grid=(T // tm, n // tn),
        in_specs=[pl.BlockSpec((tm, D), lambda i, j: (i, 0)),
                  pl.BlockSpec((1, 1, D), lambda i, j: (layer, 0, 0)),
                  pl.BlockSpec((1, D, tn), lambda i, j: (layer, 0, j))],
        out_specs=pl.BlockSpec((tm, tn), lambda i, j: (i, j)),
        out_shape=jax.ShapeDtypeStruct((T, n), F32),
        scratch_shapes=[pltpu.VMEM((tm, D), BF16)],
        compiler_params=_cparams(("parallel", "arbitrary")),
        name="in_proj",
    )(x, g, w)


def _merge_body(osb_ref, ods_ref, ogl_ref, orw_ref, g0_ref, g1_ref, g2_ref, g3_ref,
                wb_ref, wo_ref, x_ref, o_ref):
    @pl.when(pl.program_id(1) == 0)
    def _():
        o_ref[...] = x_ref[...]

    acc = None
    for n, (b_ref, g_ref) in enumerate(((osb_ref, g0_ref), (ods_ref, g1_ref),
                                        (ogl_ref, g2_ref), (orw_ref, g3_ref))):
        p = jnp.dot(b_ref[...].astype(BF16), wb_ref[0, n], preferred_element_type=F32)
        t = jax.nn.sigmoid(g_ref[...]) * p
        acc = t if acc is None else acc + t
    o_ref[...] += jnp.dot(acc.astype(BF16), wo_ref[0], preferred_element_type=F32)


def _merge(x, z, o_sb, o_dsa, o_gla, o_rw, wb, wo, layer):
    T, D = x.shape
    tm = _pick(T, (512, 256, 128))
    tc = 512
    nc = D // tc
    gate_blk = OFF_GATE // tc
    bspec = pl.BlockSpec((tm, BW), lambda i, c: (i, 0))
    gspecs = [pl.BlockSpec((tm, tc), functools.partial(lambda i, c, n: (i, gate_blk + n * nc + c), n=n))
              for n in range(4)]
    return pl.pallas_call(
        _merge_body,
        grid=(T // tm, nc),
        in_specs=[bspec, bspec, bspec, bspec] + gspecs + [
            pl.BlockSpec((1, 4, BW, tc), lambda i, c: (layer, 0, 0, c)),
            pl.BlockSpec((1, tc, D), lambda i, c: (layer, c, 0)),
            pl.BlockSpec((tm, D), lambda i, c: (i, 0))],
        out_specs=pl.BlockSpec((tm, D), lambda i, c: (i, 0)),
        out_shape=jax.ShapeDtypeStruct((T, D), F32),
        compiler_params=_cparams(("parallel", "arbitrary")),
        name="merge",
    )(o_sb, o_dsa, o_gla, o_rw, z, z, z, z, wb, wo, x)


def _qknorm_body(sq_ref, sk_ref, dq_ref, dk_ref, gsq_ref, gsk_ref, gdq_ref, gdk_ref, bd_ref,
                 osq_ref, osk_ref, odq_ref, odk_ref):
    def norm(x, g, n):
        ms = _split_dot(x * x, bd_ref[:n, :n])
        return x * lax.rsqrt(ms + NORM_EPS) * g

    scale = HD ** -0.5
    osq_ref[...] = (norm(sq_ref[...], gsq_ref[...], 2 * BW) * scale).astype(BF16)
    osk_ref[...] = norm(sk_ref[...], gsk_ref[...], BW)
    odq_ref[...] = (norm(dq_ref[...], gdq_ref[...], 2 * BW) * scale).astype(BF16)
    odk_ref[...] = norm(dk_ref[...], gdk_ref[...], BW)


def _qknorm(z, gsq, gsk, gdq, gdk, bd):
    T = z.shape[0]
    tm = _pick(T, (512, 256, 128))
    wq, wk = 2 * BW, BW
    row = lambda w, off: pl.BlockSpec((tm, w), lambda i: (i, off // w))
    par = lambda w: pl.BlockSpec((1, w), lambda i: (0, 0))
    out = lambda w: pl.BlockSpec((tm, w), lambda i: (i, 0))
    return pl.pallas_call(
        _qknorm_body,
        grid=(T // tm,),
        in_specs=[row(wq, OFF_SQ), row(wk, OFF_SK), row(wq, OFF_DQ), row(wk, OFF_DK),
                  par(wq), par(wk), par(wq), par(wk),
                  pl.BlockSpec((wq, wq), lambda i: (0, 0))],
        out_specs=[out(wq), out(wk), out(wq), out(wk)],
        out_shape=[jax.ShapeDtypeStruct((T, wq), BF16), jax.ShapeDtypeStruct((T, wk), F32),
                   jax.ShapeDtypeStruct((T, wq), BF16), jax.ShapeDtypeStruct((T, wk), F32)],
        compiler_params=_cparams(("parallel",)),
        name="qk_norm",
    )(z, z, z, z, gsq, gsk, gdq, gdk, bd)


def _sb_body(*refs, tq, n_cache, tkc):
    if n_cache:
        q_ref, kn_ref, vn_ref, kc_ref, vc_ref, o_ref = refs
    else:
        q_ref, kn_ref, vn_ref, o_ref = refs
    qb = pl.program_id(1) if not n_cache else 0
    qs = [q_ref[:, e * LANES:(e + 1) * LANES] for e in range(2)]

    def tri(n):
        r = lax.broadcasted_iota(jnp.int32, (n, n), 0)
        c = lax.broadcasted_iota(jnp.int32, (n, n), 1)
        return (r > c).astype(BF16)

    def visit(state, kblk, vblk, u, valid):
        kb16 = kblk.astype(BF16)
        vb16 = vblk.astype(BF16)
        new = []
        for e in range(2):
            carry, acc = state[e]
            z = _dot_nt(qs[e], kb16)
            lg = _neg_softplus(z)
            if valid is not None:
                lg = jnp.where(valid, lg, 0.0)
            inner = _split_dot(lg, u)
            a = jnp.exp(z + lg + inner + carry)
            if valid is not None:
                a = jnp.where(valid, a, 0.0)
            acc = acc + jnp.dot(a.astype(BF16), vb16, preferred_element_type=F32)
            carry = carry + inner[:, :1] + lg[:, :1]
            new.append((carry, acc))
        return tuple(new)

    zero = (jnp.zeros((tq, 1), F32), jnp.zeros((tq, LANES), F32))
    state = (zero, zero)

    r = lax.broadcasted_iota(jnp.int32, (tq, tq), 0)
    c = lax.broadcasted_iota(jnp.int32, (tq, tq), 1)
    u_new = tri(tq)
    if n_cache:
        state = visit(state, kn_ref[...], vn_ref[...], u_new, c < r)
    else:
        row0 = pl.multiple_of(qb * tq, tq)
        state = visit(state, kn_ref[pl.ds(row0, tq), :], vn_ref[pl.ds(row0, tq), :], u_new, c < r)

    def live(c):
        i, st = c
        return (i < n_old) & (jnp.max(jnp.maximum(st[0][0], st[1][0])) > SB_UNDERFLOW)

    if not n_cache:
        n_old = qb

        def step(c):
            i, st = c
            r0 = pl.multiple_of((qb - 1 - i) * tq, tq)
            return i + 1, visit(st, kn_ref[pl.ds(r0, tq), :], vn_ref[pl.ds(r0, tq), :], u_new, None)
    else:
        n_old = n_cache
        u_c = tri(tkc)

        def step(c):
            i, st = c
            r0 = pl.multiple_of((n_cache - 1 - i) * tkc, tkc)
            return i + 1, visit(st, kc_ref[0, 0, pl.ds(r0, tkc), :], vc_ref[0, 0, pl.ds(r0, tkc), :],
                                u_c, None)
    _, state = lax.while_loop(live, step, (jnp.int32(0), state))

    lane = lax.broadcasted_iota(jnp.int32, (tq, LANES), 1)
    o_ref[...] = jnp.where(lane < HD, state[0][1], state[1][1])


def _sb_prompt(qn, kn, z, n_rows):
    tq = 256
    vblk = OFF_SV // LANES
    return pl.pallas_call(
        functools.partial(_sb_body, tq=tq, n_cache=0, tkc=0),
        grid=(NH // 2, n_rows // tq),
        in_specs=[pl.BlockSpec((tq, 2 * LANES), lambda p, i: (i, p)),
                  pl.BlockSpec((n_rows, LANES), lambda p, i: (0, p)),
                  pl.BlockSpec((n_rows, LANES), lambda p, i: (0, vblk + p))],
        out_specs=pl.BlockSpec((tq, LANES), lambda p, i: (i, p)),
        out_shape=jax.ShapeDtypeStruct((n_rows, BW), F32),
        compiler_params=_cparams(("parallel", "arbitrary")),
        name="sb_prompt",
    )(qn, kn, z)


def _q_head(q_ref, h):
    lo = h * LANES + HD * (h % 2)
    return q_ref[:, lo:lo + HD]


def _sb_sample_body(q_ref, kn_ref, vn_ref, kc_ref, vc_ref, o_ref, *, L, n_cache, tkc):
    qs = [_q_head(q_ref, h) for h in range(NH)]

    def tri(n):
        r = lax.broadcasted_iota(jnp.int32, (n, n), 0)
        c = lax.broadcasted_iota(jnp.int32, (n, n), 1)
        return (r > c).astype(BF16)

    def visit(state, kget, vget, u, valid, cached):
        ks = [kget(h).astype(BF16) for h in range(NH)]
        vs = [vget(h).astype(BF16) for h in range(NH)]
        if cached:
            zs = [jnp.dot(qs[h], ks[h], preferred_element_type=F32) for h in range(NH)]
        else:
            zs = [_dot_nt(qs[h], ks[h]) for h in range(NH)]
        lgs = [_neg_softplus(z) for z in zs]
        if valid is not None:
            lgs = [jnp.where(valid, lg, 0.0) for lg in lgs]
        inners = [_split_dot(lg, u) for lg in lgs]
        ws = [jnp.exp(zs[h] + lgs[h] + inners[h] + state[h][0]) for h in range(NH)]
        if valid is not None:
            ws = [jnp.where(valid, w, 0.0) for w in ws]
        ws = [w.astype(BF16) for w in ws]
        if cached:
            pv = [_dot_nt(ws[h], vs[h]) for h in range(NH)]
        else:
            pv = [jnp.dot(ws[h], vs[h], preferred_element_type=F32) for h in range(NH)]
        return tuple((state[h][0] + inners[h][:, :1] + lgs[h][:, :1], state[h][1] + pv[h])
                     for h in range(NH))

    zero = (jnp.zeros((L, 1), F32), jnp.zeros((L, HD), F32))
    r = lax.broadcasted_iota(jnp.int32, (L, L), 0)
    c = lax.broadcasted_iota(jnp.int32, (L, L), 1)
    state = visit((zero,) * NH, lambda h: kn_ref[:, h * HD:(h + 1) * HD],
                  lambda h: vn_ref[:, h * HD:(h + 1) * HD], tri(L), c < r, False)
    u_c = tri(tkc)

    def live(cr):
        i, st = cr
        top = functools.reduce(jnp.maximum, [s[0] for s in st])
        return (i < n_cache) & (jnp.max(top) > SB_UNDERFLOW)

    def step(cr):
        i, st = cr
        r0 = pl.multiple_of((n_cache - 1 - i) * tkc, tkc)
        return i + 1, visit(st, lambda h: kc_ref[0, 0, h, :, pl.ds(r0, tkc)],
                            lambda h: vc_ref[0, 0, h, :, pl.ds(r0, tkc)], u_c, None, True)
    _, state = lax.while_loop(live, step, (jnp.int32(0), state))
    for h in range(NH):
        o_ref[:, h * HD:(h + 1) * HD] = state[h][1]


def _sb_sample(qn, kn, z, cache_k, cache_v, layer, row_off, nb, L):
    P = cache_k.shape[4]
    tkc = _pick(P, (256, 128))
    rb = row_off // L
    cspec = pl.BlockSpec((1, 1, NH, HD, P), lambda b: (layer, b, 0, 0, 0))
    return pl.pallas_call(
        functools.partial(_sb_sample_body, L=L, n_cache=P // tkc, tkc=tkc),
        grid=(nb,),
        in_specs=[pl.BlockSpec((L, NH * LANES), lambda b: (rb + b, 0)),
                  pl.BlockSpec((L, BW), lambda b: (rb + b, 0)),
                  pl.BlockSpec((L, BW), lambda b: (rb + b, OFF_SV // BW)),
                  cspec, cspec],
        out_specs=pl.BlockSpec((L, BW), lambda b: (b, 0)),
        out_shape=jax.ShapeDtypeStruct((nb * L, BW), F32),
        compiler_params=_cparams(("parallel",)),
        name="sb_sample",
    )(qn, kn, z, cache_k, cache_v)


def _sortable(x):
    b = pltpu.bitcast(x + 0.0, jnp.int32)
    return jnp.where(b < 0, b ^ jnp.int32(0x7FFFFFFF), b)


def _select_topk(key_ref, n_blk, tkb, topk, rows):
    def count(pred):
        def body(i, acc):
            c0 = pl.multiple_of(i * tkb, tkb)
            for c in range(tkb // LANES):
                blk = key_ref[:, pl.ds(c0 + c * LANES, LANES)]
                col = c0 + c * LANES + lax.broadcasted_iota(jnp.int32, (rows, LANES), 1)
                acc = acc + jnp.where(pred(blk, col), 1.0, 0.0)
            return acc
        acc = lax.fori_loop(0, n_blk, body, jnp.zeros((rows, LANES), F32))
        return jnp.sum(acc, axis=1, keepdims=True)

    def unresolved(c):
        i, _, n_ge = c
        return (i < 32) & (jnp.max(jnp.abs(n_ge - topk)) > 0)

    def bit_step(c):
        i, thr, n_ge = c
        cand = thr + (jnp.int32(1) << (31 - i))
        n_cand = count(lambda blk, col: blk >= cand)
        take = n_cand >= topk
        return i + 1, jnp.where(take, cand, thr), jnp.where(take, n_cand, n_ge)

    thr0 = jnp.full((rows, 1), jnp.iinfo(jnp.int32).min, jnp.int32)
    n_all = jnp.full((rows, 1), 1.0, F32) * jnp.asarray(n_blk * tkb).astype(F32)
    _, thr, n_ge = lax.while_loop(unresolved, bit_step, (jnp.int32(0), thr0, n_all))
    n_col_bits = max(1, int(math.ceil(math.log2(key_ref.shape[1] + 1))))

    def tie_search():
        need = topk - count(lambda blk, col: blk > thr)

        def col_step(i, j):
            cand = j + (jnp.int32(1) << (n_col_bits - 1 - i))
            n = count(lambda blk, col: (blk == thr) & (col < cand))
            return jnp.where(n < need, cand, j)
        return lax.fori_loop(0, n_col_bits, col_step, jnp.zeros((rows, 1), jnp.int32))

    any_tie = jnp.max(jnp.abs(n_ge - topk)) > 0
    j_hi = lax.cond(any_tie, tie_search,
                    lambda: jnp.full((rows, 1), jnp.iinfo(jnp.int32).max, jnp.int32))
    return thr, j_hi


def _idx_scores(qi, wi, kblk16, k_real, keys_transposed=False):
    acc = None
    for h in range(NH):
        qh = qi[:, h * LANES:h * LANES + k_real].astype(BF16)
        if keys_transposed:
            d = jnp.dot(qh, kblk16, preferred_element_type=F32)
        else:
            d = _dot_nt(qh, kblk16)
        t = ((wi[:, h:h + 1] * (NH ** -0.5)) * (HD ** -0.5)) * jnp.maximum(d, 0.0)
        acc = t if acc is None else acc + t
    return acc


def _dsa_sel_prompt_body(qi_ref, wi_ref, ki_ref, m_ref, key_ref, *, tq, tkb, topk, n_keys):
    qb = pl.program_id(0)
    n_blk = ((qb + 1) * tq + tkb - 1) // tkb
    qi = qi_ref[...]
    wi = wi_ref[...]
    q_chunk = (qb * tq + lax.broadcasted_iota(jnp.int32, (tq, tkb), 0)) // CHUNK

    def score_blk(i, _):
        c0 = pl.multiple_of(i * tkb, tkb)
        s = _idx_scores(qi, wi, ki_ref[pl.ds(c0, tkb), :].astype(BF16), LANES)
        k_chunk = (c0 + lax.broadcasted_iota(jnp.int32, (tq, tkb), 1)) // CHUNK
        s = jnp.where(k_chunk <= q_chunk, s, -jnp.inf)
        key_ref[:, pl.ds(c0, tkb)] = _sortable(s)
        return 0
    lax.fori_loop(0, n_blk, score_blk, 0)

    thr, j_hi = _select_topk(key_ref, n_blk, tkb, topk, tq)
    neg_inf_key = _sortable(jnp.full((1, 1), -jnp.inf, F32))

    def write_blk(i, _):
        c0 = pl.multiple_of(i * tkb, tkb)
        blk = key_ref[:, pl.ds(c0, tkb)]
        col = c0 + lax.broadcasted_iota(jnp.int32, (tq, tkb), 1)
        sel = ((blk > thr) | ((blk == thr) & (col <= j_hi))) & (blk > neg_inf_key)
        m_ref[:, pl.ds(c0, tkb)] = jnp.where(sel, 0.0, NEG_BIG).astype(BF16)
        return 0
    lax.fori_loop(0, n_blk, write_blk, 0)

    def zero_blk(i, _):
        m_ref[:, pl.ds(pl.multiple_of(i * tkb, tkb), tkb)] = jnp.full((tq, tkb), NEG_BIG, BF16)
        return 0
    lax.fori_loop(n_blk, n_keys // tkb, zero_blk, 0)


def _dsa_sel_prompt(z, n_rows, topk):
    tq, tkb = 128, 512
    return pl.pallas_call(
        functools.partial(_dsa_sel_prompt_body, tq=tq, tkb=tkb, topk=topk, n_keys=n_rows),
        grid=(n_rows // tq,),
        in_specs=[pl.BlockSpec((tq, NH * LANES), lambda i: (i, OFF_QI // (NH * LANES))),
                  pl.BlockSpec((tq, LANES), lambda i: (i, OFF_WI // LANES)),
                  pl.BlockSpec((n_rows, LANES), lambda i: (0, OFF_KI // LANES))],
        out_specs=pl.BlockSpec((tq, n_rows), lambda i: (i, 0)),
        out_shape=jax.ShapeDtypeStruct((n_rows, n_rows), BF16),
        scratch_shapes=[pltpu.VMEM((tq, n_rows), jnp.int32)],
        compiler_params=_cparams(("parallel",)),
        name="dsa_select_prompt",
    )(z, z, z)


def _dsa_sel_sample_body(qi_ref, wi_ref, kin_ref, kic_ref, m_ref, key_ref, *, L, P, tkb, topk):
    qi = qi_ref[...]
    wi = wi_ref[...]
    n_c = P // tkb

    def score_blk(i, _):
        c0 = pl.multiple_of(i * tkb, tkb)
        s = _idx_scores(qi, wi, kic_ref[0, 0, :, pl.ds(c0, tkb)].astype(BF16), HD, keys_transposed=True)
        key_ref[:, pl.ds(c0, tkb)] = _sortable(s)
        return 0
    lax.fori_loop(0, n_c, score_blk, 0)

    knew = jnp.concatenate([kin_ref[...], jnp.zeros((tkb - L, LANES), F32)], axis=0).astype(BF16)
    s = _idx_scores(qi, wi, knew, LANES)
    col = lax.broadcasted_iota(jnp.int32, (L, tkb), 1)
    key_ref[:, pl.ds(P, tkb)] = _sortable(jnp.where(col < L, s, -jnp.inf))

    thr, j_hi = _select_topk(key_ref, n_c + 1, tkb, topk, L)
    neg_inf_key = _sortable(jnp.full((1, 1), -jnp.inf, F32))

    def write_blk(i, _):
        c0 = pl.multiple_of(i * tkb, tkb)
        blk = key_ref[:, pl.ds(c0, tkb)]
        cc = c0 + lax.broadcasted_iota(jnp.int32, (L, tkb), 1)
        sel = ((blk > thr) | ((blk == thr) & (cc <= j_hi))) & (blk > neg_inf_key)
        m_ref[:, pl.ds(c0, tkb)] = jnp.where(sel, 0.0, NEG_BIG).astype(BF16)
        return 0
    lax.fori_loop(0, n_c + 1, write_blk, 0)


def _dsa_sel_sample(z, cache_ki, layer, row_off, nb, L, topk):
    P = cache_ki.shape[3]
    tkb = _pick(P, (512, 256, 128))
    rb = row_off // L
    return pl.pallas_call(
        functools.partial(_dsa_sel_sample_body, L=L, P=P, tkb=tkb, topk=topk),
        grid=(nb,),
        in_specs=[pl.BlockSpec((L, NH * LANES), lambda b: (rb + b, OFF_QI // (NH * LANES))),
                  pl.BlockSpec((L, LANES), lambda b: (rb + b, OFF_WI // LANES)),
                  pl.BlockSpec((L, LANES), lambda b: (rb + b, OFF_KI // LANES)),
                  pl.BlockSpec((1, 1, HD, P), lambda b: (layer, b, 0, 0))],
        out_specs=pl.BlockSpec((L, P + tkb), lambda b: (b, 0)),
        out_shape=jax.ShapeDtypeStruct((nb * L, P + tkb), BF16),
        scratch_shapes=[pltpu.VMEM((L, P + tkb), jnp.int32)],
        compiler_params=_cparams(("parallel",)),
        name="dsa_select_sample",
    )(z, z, z, cache_ki)


def _dsa_attn_body(*refs, tq, tkb, n_cache, prompt):
    if prompt:
        q_ref, m_ref, kn_ref, vn_ref, o_ref = refs
        qb = pl.program_id(1)
        n_blk = ((qb + 1) * tq + tkb - 1) // tkb
    else:
        q_ref, m_ref, kn_ref, vn_ref, kc_ref, vc_ref, o_ref = refs
        n_blk = n_cache
    qs = [q_ref[:, e * LANES:(e + 1) * LANES] for e in range(2)]

    def visit(state, kblk, vblk, mblk):
        kb16 = kblk.astype(BF16)
        vb16 = vblk.astype(BF16)
        bias = mblk.astype(F32)
        new = []
        for e in range(2):
            m, l, acc = state[e]
            s = _dot_nt(qs[e], kb16) + bias
            m_new = jnp.maximum(m, jnp.max(s, axis=1, keepdims=True))
            p = jnp.exp(s - m_new)
            alpha = jnp.exp(m - m_new)
            l = alpha * l + jnp.sum(p, axis=1, keepdims=True)
            acc = alpha * acc + jnp.dot(p.astype(BF16), vb16, preferred_element_type=F32)
            new.append((m_new, l, acc))
        return tuple(new)

    init = (jnp.full((tq, 1), NEG_BIG, F32), jnp.zeros((tq, 1), F32), jnp.zeros((tq, LANES), F32))
    state = (init, init)

    def step(i, st):
        c0 = pl.multiple_of(i * tkb, tkb)
        if prompt:
            return visit(st, kn_ref[pl.ds(c0, tkb), :], vn_ref[pl.ds(c0, tkb), :], m_ref[:, pl.ds(c0, tkb)])
        return visit(st, kc_ref[0, 0, pl.ds(c0, tkb), :], vc_ref[0, 0, pl.ds(c0, tkb), :],
                     m_ref[:, pl.ds(c0, tkb)])
    state = lax.fori_loop(0, n_blk, step, state)

    if not prompt:
        pad = jnp.zeros((tkb - tq, LANES), F32)
        knew = jnp.concatenate([kn_ref[...], pad], axis=0)
        vnew = jnp.concatenate([vn_ref[...], pad], axis=0)
        state = visit(state, knew, vnew, m_ref[:, pl.ds(n_cache * tkb, tkb)])

    lane = lax.broadcasted_iota(jnp.int32, (tq, LANES), 1)
    o0 = state[0][2] / state[0][1]
    o1 = state[1][2] / state[1][1]
    o_ref[...] = jnp.where(lane < HD, o0, o1)


def _dsa_attn_prompt(qn, kn, z, mask, n_rows):
    tq, tkb = 256, min(1024, n_rows)
    vblk = OFF_DV // LANES
    return pl.pallas_call(
        functools.partial(_dsa_attn_body, tq=tq, tkb=tkb, n_cache=0, prompt=True),
        grid=(NH // 2, n_rows // tq),
        in_specs=[pl.BlockSpec((tq, 2 * LANES), lambda p, i: (i, p)),
                  pl.BlockSpec((tq, n_rows), lambda p, i: (i, 0)),
                  pl.BlockSpec((n_rows, LANES), lambda p, i: (0, p)),
                  pl.BlockSpec((n_rows, LANES), lambda p, i: (0, vblk + p))],
        out_specs=pl.BlockSpec((tq, LANES), lambda p, i: (i, p)),
        out_shape=jax.ShapeDtypeStruct((n_rows, BW), F32),
        compiler_params=_cparams(("parallel", "arbitrary")),
        name="dsa_attn_prompt",
    )(qn, mask, kn, z)


def _dsa_attn_sample_body(q_ref, m_ref, kn_ref, vn_ref, kc_ref, vc_ref, o_ref, *, L, tkb, n_cache):
    qs = [_q_head(q_ref, h) for h in range(NH)]

    def visit(state, kget, vget, mblk, cached):
        bias = mblk.astype(F32)
        ks = [kget(h).astype(BF16) for h in range(NH)]
        vs = [vget(h).astype(BF16) for h in range(NH)]
        if cached:
            ss = [jnp.dot(qs[h], ks[h], preferred_element_type=F32) + bias for h in range(NH)]
        else:
            ss = [_dot_nt(qs[h], ks[h]) + bias for h in range(NH)]
        ms = [jnp.maximum(state[h][0], jnp.max(ss[h], axis=1, keepdims=True)) for h in range(NH)]
        ps = [jnp.exp(ss[h] - ms[h]) for h in range(NH)]
        al = [jnp.exp(state[h][0] - ms[h]) for h in range(NH)]
        if cached:
            pv = [_dot_nt(ps[h].astype(BF16), vs[h]) for h in range(NH)]
        else:
            pv = [jnp.dot(ps[h].astype(BF16), vs[h], preferred_element_type=F32) for h in range(NH)]
        return tuple((ms[h], al[h] * state[h][1] + jnp.sum(ps[h], axis=1, keepdims=True),
                      al[h] * state[h][2] + pv[h]) for h in range(NH))

    init = (jnp.full((L, 1), NEG_BIG, F32), jnp.zeros((L, 1), F32), jnp.zeros((L, HD), F32))

    def step(i, st):
        c0 = pl.multiple_of(i * tkb, tkb)
        return visit(st, lambda h: kc_ref[0, 0, h, :, pl.ds(c0, tkb)],
                     lambda h: vc_ref[0, 0, h, :, pl.ds(c0, tkb)], m_ref[:, pl.ds(c0, tkb)], True)
    state = lax.fori_loop(0, n_cache, step, (init,) * NH)

    pad = jnp.zeros((tkb - L, HD), F32)
    state = visit(state, lambda h: jnp.concatenate([kn_ref[:, h * HD:(h + 1) * HD], pad], axis=0),
                  lambda h: jnp.concatenate([vn_ref[:, h * HD:(h + 1) * HD], pad], axis=0),
                  m_ref[:, pl.ds(n_cache * tkb, tkb)], False)
    for h in range(NH):
        o_ref[:, h * HD:(h + 1) * HD] = state[h][2] / state[h][1]


def _dsa_attn_sample(qn, kn, z, mask, cache_k, cache_v, layer, row_off, nb, L):
    P = cache_k.shape[4]
    tkb = mask.shape[1] - P
    rb = row_off // L
    cspec = pl.BlockSpec((1, 1, NH, HD, P), lambda b: (layer, b, 0, 0, 0))
    return pl.pallas_call(
        functools.partial(_dsa_attn_sample_body, L=L, tkb=tkb, n_cache=P // tkb),
        grid=(nb,),
        in_specs=[pl.BlockSpec((L, NH * LANES), lambda b: (rb + b, 0)),
                  pl.BlockSpec((L, P + tkb), lambda b: (b, 0)),
                  pl.BlockSpec((L, BW), lambda b: (rb + b, 0)),
                  pl.BlockSpec((L, BW), lambda b: (rb + b, OFF_DV // BW)),
                  cspec, cspec],
        out_specs=pl.BlockSpec((L, BW), lambda b: (b, 0)),
        out_shape=jax.ShapeDtypeStruct((nb * L, BW), F32),
        compiler_params=_cparams(("parallel",)),
        name="dsa_attn_sample",
    )(qn, mask, kn, z, cache_k, cache_v)


def _gla_consts(C):
    nlev = int(math.log2(C))
    t = np.arange(C)
    cum = (t[:, None] >= t[None, :]).astype(np.float32)
    sel = np.zeros((nlev * C, C), np.float32)
    for l in range(nlev):
        m = 1 << l
        mid = (t // (2 * m)) * 2 * m + m
        sel[l * C + t, mid - 1] = 1.0
    return jnp.asarray(cum, BF16), jnp.asarray(sel, BF16)


def _gla_body(gq_ref, gk_ref, gv_ref, gr_ref, ga_ref, a2_ref, ab_ref, on_ref, cum_ref, sel_ref,
              s0_ref, o_ref, sout_ref, st_ref, *, C, n_chunks):
    ci = pl.program_id(1)
    nlev = int(math.log2(C))

    @pl.when(ci == 0)
    def _():
        st_ref[...] = s0_ref[0]

    x = jnp.dot(ga_ref[...].astype(BF16), a2_ref[...], preferred_element_type=F32) + ab_ref[...]
    g = _neg_softplus(-x) / GLA_GATE_NORM
    b = _split_dot_left(cum_ref[...], g)
    c_all = _split_dot_left(sel_ref[...], b)

    row = lax.broadcasted_iota(jnp.int32, (C, C), 0)
    col = lax.broadcasted_iota(jnp.int32, (C, C), 1)
    rowl = lax.broadcasted_iota(jnp.int32, (C, LANES), 0)

    hs = range(GLA_H)
    sls = [slice(h * LANES, (h + 1) * LANES) for h in hs]
    q = [gq_ref[:, sl] * (HD ** -0.5) for sl in sls]
    k = [gk_ref[:, sl] for sl in sls]
    v16 = [gv_ref[:, sl].astype(BF16) for sl in sls]
    bh = [b[:, sl] for sl in sls]
    st = [st_ref[h] for h in hs]
    o_inter = [_dot_nt((q[h] * jnp.exp(bh[h])).astype(BF16), st[h].astype(BF16)) for h in hs]
    b_last = [bh[h][C - 1:C, :] for h in hs]
    kv = [_dot_tn(v16[h], (k[h] * jnp.exp(b_last[h] - bh[h])).astype(BF16)) for h in hs]
    a = [jnp.where(row == col, jnp.sum(q[h] * k[h], axis=1, keepdims=True), 0.0) for h in hs]
    for l in range(nlev):
        m = 1 << l
        later = (rowl & m) != 0
        same_seg = (row // (2 * m)) == (col // (2 * m))
        ch = [c_all[l * C:(l + 1) * C, sl] for sl in sls]
        qe = [jnp.where(later, q[h] * jnp.exp(jnp.where(later, bh[h] - ch[h], 0.0)), 0.0) for h in hs]
        ke = [jnp.where(later, 0.0, k[h] * jnp.exp(jnp.where(later, 0.0, ch[h] - bh[h]))) for h in hs]
        al = [_dot_nt(qe[h].astype(BF16), ke[h].astype(BF16)) for h in hs]
        a = [a[h] + jnp.where(same_seg, al[h], 0.0) for h in hs]
    o = [jnp.dot(a[h].astype(BF16), v16[h], preferred_element_type=F32) + o_inter[h] for h in hs]
    for h in hs:
        st_ref[h] = st[h] * jnp.exp(b_last[h]) + kv[h]
        on = o[h] * lax.rsqrt(jnp.mean(o[h] * o[h], axis=-1, keepdims=True) + NORM_EPS) * on_ref[...]
        gr = gr_ref[:, sls[h]]
        o_ref[:, sls[h]] = on * (gr * jax.nn.sigmoid(gr))

    @pl.when(ci == n_chunks - 1)
    def _():
        sout_ref[0] = st_ref[...]


def _gla(z, a2p, abp, onorm, s0t, row_off, n_seq, L):
    C = min(CHUNK, L)
    n_chunks = L // C
    rb = row_off // C
    cum, sel = _gla_consts(C)
    blk = lambda off: pl.BlockSpec((C, BW), lambda s, c: (rb + s * n_chunks + c, off // BW))
    full = lambda a: pl.BlockSpec(a.shape, lambda s, c: (0,) * a.ndim)
    return pl.pallas_call(
        functools.partial(_gla_body, C=C, n_chunks=n_chunks),
        grid=(n_seq, n_chunks),
        in_specs=[blk(OFF_GQ), blk(OFF_GK), blk(OFF_GV), blk(OFF_GR),
                  pl.BlockSpec((C, LANES), lambda s, c: (rb + s * n_chunks + c, OFF_GA // LANES)),
                  full(a2p), full(abp), full(onorm), full(cum), full(sel),
                  pl.BlockSpec((1, GLA_H, GLA_DV, LANES), lambda s, c: (s, 0, 0, 0))],
        out_specs=[pl.BlockSpec((C, BW), lambda s, c: (s * n_chunks + c, 0)),
                   pl.BlockSpec((1, GLA_H, GLA_DV, LANES), lambda s, c: (s, 0, 0, 0))],
        out_shape=[jax.ShapeDtypeStruct((n_seq * L, BW), F32),
                   jax.ShapeDtypeStruct((n_seq, GLA_H, GLA_DV, LANES), F32)],
        scratch_shapes=[pltpu.VMEM((GLA_H, GLA_DV, LANES), F32)],
        compiler_params=_cparams(("parallel", "arbitrary")),
        name="gla",
    )(z, z, z, z, z, a2p, abp, onorm, cum, sel, s0t)


def _rw_prep_body(z_ref, head_ref, mu_ref, w0_ref, a0_ref, kkp_ref, kap_ref, rkp_ref,
                  w2_ref, a2_ref, g2_ref, seg_ref,
                  r_ref, w_ref, k_ref, v_ref, nkk_ref, kka_ref, gate_ref, bonus_ref, *, grp):
    zz = z_ref[...]
    tm, wd = zz.shape
    heads = head_ref[...]
    head_rows = jnp.concatenate([jnp.broadcast_to(heads[g:g + 1, :], (grp, wd))
                                 for g in range(tm // grp)], axis=0)
    row = lax.broadcasted_iota(jnp.int32, (tm, wd), 0)
    prev = jnp.where(row % grp == 0, head_rows, pltpu.roll(zz, 1, axis=0))
    zm = zz + mu_ref[...] * (prev - zz)
    rr = zm[:, 0:BW]
    rk = zm[:, BW:2 * BW]
    rv = zm[:, 2 * BW:3 * BW]
    lora = zm[:, 3 * BW:3 * BW + LANES]
    gl = zm[:, 3 * BW + LANES:3 * BW + 2 * LANES]
    xw = w0_ref[...] + jnp.dot(jnp.tanh(lora).astype(BF16), w2_ref[...], preferred_element_type=F32)
    wlog = _neg_softplus(-xw) - 0.5
    a = jax.nn.sigmoid(a0_ref[...] + jnp.dot(lora.astype(BF16), a2_ref[...], preferred_element_type=F32))
    gate = jnp.dot(jax.nn.sigmoid(gl).astype(BF16), g2_ref[...], preferred_element_type=F32)
    kkf = rk * kkp_ref[...]
    nrm = jnp.sqrt(_split_dot(kkf * kkf, seg_ref[...]))
    kk = kkf / jnp.maximum(nrm, 1e-12)
    kmod = rk * (1.0 + (a - 1.0) * kap_ref[...])
    coef = _split_dot(rr * kmod * rkp_ref[...], seg_ref[...])
    r_ref[...] = rr
    w_ref[...] = -jnp.exp(wlog)
    k_ref[...] = kmod
    v_ref[...] = rv
    nkk_ref[...] = -kk
    kka_ref[...] = kk * a
    gate_ref[...] = gate
    bonus_ref[...] = coef * rv


def _rw_prep(z, heads, grp, mu, w0, a0, kkp, kap, rkp, w2p, a2p, g2, seg):
    T = z.shape[0]
    tm = SUBLANES * grp
    assert T % tm == 0, (T, tm)
    W = 4 * BW
    par = lambda a: pl.BlockSpec(a.shape, lambda i: (0,) * a.ndim)
    out = pl.BlockSpec((tm, BW), lambda i: (i, 0))
    return pl.pallas_call(
        functools.partial(_rw_prep_body, grp=grp),
        grid=(T // tm,),
        in_specs=[pl.BlockSpec((tm, W), lambda i: (i, 0)), pl.BlockSpec((SUBLANES, W), lambda i: (i, 0)),
                  par(mu), par(w0), par(a0), par(kkp), par(kap), par(rkp),
                  par(w2p), par(a2p), par(g2), par(seg)],
        out_specs=[out] * 8,
        out_shape=[jax.ShapeDtypeStruct((T, BW), F32)] * 8,
        compiler_params=_cparams(("parallel",)),
        name="rwkv_prep",
    )(z, heads, mu, w0, a0, kkp, kap, rkp, w2p, a2p, g2, seg)


def _rw_rec_body(r_ref, w_ref, k_ref, v_ref, nkk_ref, kka_ref, g_ref, s0_ref,
                 ot_ref, sout_ref, s_ref, *, TT, L, NS, n_tiles):
    ti = pl.program_id(0)
    npair = NH // 2
    R = NS * npair * HD
    gm = g_ref[...]
    lane = lax.broadcasted_iota(jnp.int32, (R, LANES), 1)
    rowi = lax.broadcasted_iota(jnp.int32, (R, LANES), 0)
    eye2 = (lane % HD) == (rowi % HD)
    seq_lane0 = (rowi // (npair * HD)) * L
    refs =(r_ref, w_ref, k_ref, v_ref, nkk_ref, kka_ref)

    def token_group(gi, carry):
        s, o_lo, o_hi = carry
        t0 = pl.multiple_of(gi * SUBLANES, SUBLANES)
        blk = [[ref[pl.ds(q * L + t0, SUBLANES), :] for q in range(NS)] for ref in refs]
        for j in range(SUBLANES):
            def stacked(x):
                return jnp.concatenate(
                    [jnp.broadcast_to(x[q][j:j + 1, p * LANES:(p + 1) * LANES], (HD, LANES))
                     for q in range(NS) for p in range(npair)], axis=0)
            rr, ww, kk, vv, nk, ka = [stacked(x) for x in blk]
            sa = _split_dot(s * nk, gm)
            vb = jnp.dot(jnp.where(eye2, vv, 0.0).astype(BF16), gm, preferred_element_type=F32)
            s = s * ww + sa * ka + vb * kk
            oo = jnp.dot((s * rr).astype(BF16), gm, preferred_element_type=F32)
            hit = lane == seq_lane0 + t0 + j
            o_lo = jnp.where(hit, oo, o_lo)
            o_hi = jnp.where(hit, pltpu.roll(oo, HD, axis=1), o_hi)
        return s, o_lo, o_hi

    if NS == 1:
        @pl.when(ti == 0)
        def _():
            s_ref[...] = s0_ref[0]
        s_in = s_ref[...]
    else:
        s_in = s0_ref[...].reshape(R, LANES)
    zero = jnp.zeros((R, LANES), F32)
    s_out, o_lo, o_hi = lax.fori_loop(0, min(L, TT) // SUBLANES, token_group, (s_in, zero, zero))

    pr = npair * HD
    a1 = sum(o_lo[q * pr:(q + 1) * pr] for q in range(NS))
    a2 = sum(o_hi[q * pr:(q + 1) * pr] for q in range(NS))
    first_half = lax.broadcasted_iota(jnp.int32, (pr, LANES), 1) < HD
    lo = jnp.where(first_half, a1, a2)
    hi = jnp.where(first_half, a2, a1)
    for p in range(npair):
        ot_ref[2 * p * HD:(2 * p + 1) * HD, :] = lo[p * HD:(p + 1) * HD]
        ot_ref[(2 * p + 1) * HD:(2 * p + 2) * HD, :] = hi[p * HD:(p + 1) * HD]
    if NS == 1:
        s_ref[...] = s_out

        @pl.when(ti == n_tiles - 1)
        def _():
            sout_ref[0] = s_out
    else:
        sout_ref[...] = s_out.reshape(NS, pr, LANES)


def _rw_rec(r, w, k, v, nkk, kka, gmat, s0, row_off, n_seq, L):
    TT = LANES
    n_tok = n_seq * L
    n_tiles = n_tok // TT
    rb = row_off // TT
    ns = max(1, TT // L)
    pr = (NH // 2) * HD
    tok = pl.BlockSpec((TT, BW), lambda i: (rb + i, 0))
    if ns == 1:
        sspec = pl.BlockSpec((1, pr, LANES), lambda i: (0, 0, 0))
    else:
        sspec = pl.BlockSpec((ns, pr, LANES), lambda i: (i, 0, 0))
    return pl.pallas_call(
        functools.partial(_rw_rec_body, TT=TT, L=L, NS=ns, n_tiles=n_tiles),
        grid=(n_tiles,),
        in_specs=[tok] * 6 + [pl.BlockSpec((LANES, LANES), lambda i: (0, 0)), sspec],
        out_specs=[pl.BlockSpec((BW, TT), lambda i: (0, i)), sspec],
        out_shape=[jax.ShapeDtypeStruct((BW, n_tok), F32),
                   jax.ShapeDtypeStruct(s0.shape, F32)],
        scratch_shapes=[pltpu.VMEM((pr, LANES), F32)],
        compiler_params=_cparams(("arbitrary",)),
        name="rwkv_recurrence",
    )(r, w, k, v, nkk, kka, gmat, s0)


def _mm3(a, b):
    ah = a.astype(BF16)
    al = (a - ah.astype(F32)).astype(BF16)
    bh = b.astype(BF16)
    bl = (b - bh.astype(F32)).astype(BF16)
    d = lambda x, y: jnp.dot(x, y, preferred_element_type=F32)
    return d(ah, bh) + (d(ah, bl) + d(al, bh))


def _rw_chunk_body(r_ref, lw_ref, k_ref, v_ref, nkk_ref, kka_ref, s0_ref, o_ref, sout_ref, s_ref,
                   *, TT, L, n_tiles):
    ti = pl.program_id(0)
    npair = NH // 2
    nblk = TT // RW_BLK
    row = lax.broadcasted_iota(jnp.int32, (TT, TT), 0)
    col = lax.broadcasted_iota(jnp.int32, (TT, TT), 1)
    same = (row // RW_BLK) == (col // RW_BLK)
    strict = same & (col < row)
    incl = same & (col <= row)
    tri = jnp.where(incl, 1.0, 0.0).astype(BF16)
    last = jnp.where(same & (col % RW_BLK == RW_BLK - 1), 1.0, 0.0).astype(BF16)
    eye = jnp.where(row == col, 1.0, 0.0)
    pair_diag = (row < HD) == (col < HD)
    lo_half = lax.broadcasted_iota(jnp.int32, (TT, LANES), 1) < HD
    lo_half_b = lax.broadcasted_iota(jnp.int32, (RW_BLK, LANES), 1) < HD

    r, lw, k, v = r_ref[...], lw_ref[...], k_ref[...], v_ref[...]
    nk, ka = nkk_ref[...], kka_ref[...]
    lc = _split_dot_left(tri, lw)
    le = _split_dot_left(last, lc)
    nkt = nk * jnp.exp(lc - lw)
    rt = r * jnp.exp(lc)
    inv = jnp.exp(-lc)
    kah = ka * inv
    kh = k * inv
    rem = jnp.exp(le - lc)
    kaw = (ka * rem).astype(BF16)
    kw = (k * rem).astype(BF16)
    wb = jnp.exp(le)

    sls = [slice(p * LANES, (p + 1) * LANES) for p in range(npair)]
    v16 = [v[:, sl].astype(BF16) for sl in sls]
    heads = [(p, e) for p in range(npair) for e in range(2)]
    grams = []
    for p, e in heads:
        hm = lo_half if e == 0 else jnp.logical_not(lo_half)
        lhs = jnp.concatenate([jnp.where(hm, nkt[:, sls[p]], 0.0), jnp.where(hm, rt[:, sls[p]], 0.0)],
                              axis=0).astype(BF16)
        rhs = jnp.concatenate([kah[:, sls[p]], kh[:, sls[p]]], axis=0).astype(BF16)
        grams.append(_dot_nt(lhs, rhs))
    n1 = [jnp.where(strict, g[:TT, :TT], 0.0) for g in grams]
    bt = [jnp.where(strict, g[:TT, TT:], 0.0).astype(BF16) for g in grams]
    cmat = [jnp.where(incl, g[TT:, :TT], 0.0).astype(BF16) for g in grams]
    et = [jnp.where(incl, g[TT:, TT:], 0.0).astype(BF16) for g in grams]
    n2 = [_mm3(a, a) for a in n1]
    n4 = [_mm3(a, a) for a in n2]
    n8 = [_mm3(a, a) for a in n4]
    tinv = [eye + a for a in n1]
    for power in (n2, n4, n8):
        tinv = [t + _mm3(t, a) for t, a in zip(tinv, power)]
    vb_h = [jnp.dot(b_, v16[p], preferred_element_type=F32) for b_, (p, e) in zip(bt, heads)]
    ev_h = [jnp.dot(e_, v16[p], preferred_element_type=F32) for e_, (p, e) in zip(et, heads)]
    vb = [jnp.where(lo_half, vb_h[2 * p], vb_h[2 * p + 1]) for p in range(npair)]
    ev = [jnp.where(lo_half, ev_h[2 * p], ev_h[2 * p + 1]) for p in range(npair)]

    if L >= TT:
        @pl.when(ti == 0)
        def _():
            s_ref[...] = s0_ref[0]
        state = {(0, p): s_ref[p] for p in range(npair)}
        rounds = [[j] for j in range(nblk)]
    else:
        bps = L // RW_BLK
        state = {(q, p): s0_ref[q, p] for q in range(TT // L) for p in range(npair)}
        rounds = [[q * bps + i for q in range(TT // L)] for i in range(bps)]
    sa_rows = [[None] * nblk for _ in range(npair)]
    os_rows = [[None] * nblk for _ in range(npair)]
    for blocks in rounds:
        items = [(j, p) for j in blocks for p in range(npair)]
        seq = lambda j: (j * RW_BLK) // L if L < TT else 0
        xs = {}
        for j, p in items:
            bs = slice(j * RW_BLK, (j + 1) * RW_BLK)
            s = state[(seq(j), p)]
            s_hi = s.astype(BF16)
            s_lo = (s - s_hi.astype(F32)).astype(BF16)
            lhs = jnp.concatenate([nkt[bs, sls[p]], rt[bs, sls[p]]], axis=0).astype(BF16)
            xs[(j, p)] = (jnp.dot(lhs, s_hi, preferred_element_type=F32)
                          + jnp.dot(lhs, s_lo, preferred_element_type=F32))
        sa = {}
        for j, p in items:
            b0 = j * RW_BLK
            bs = slice(b0, b0 + RW_BLK)
            parts = [xs[(j, p)][:RW_BLK] + vb[p][bs]]
            if b0:
                parts.insert(0, jnp.zeros((b0, LANES), F32))
            if TT - b0 - RW_BLK:
                parts.append(jnp.zeros((TT - b0 - RW_BLK, LANES), F32))
            xfull = jnp.concatenate(parts, axis=0)
            sa[(j, p)] = jnp.where(lo_half_b, _mm3(tinv[2 * p][bs], xfull),
                                   _mm3(tinv[2 * p + 1][bs], xfull))
        for j, p in items:
            b0 = j * RW_BLK
            bs = slice(b0, b0 + RW_BLK)
            u = (_dot_tn(kaw[bs, sls[p]], sa[(j, p)].astype(BF16)) + _dot_tn(kw[bs, sls[p]], v16[p][bs]))
            wcol = jnp.broadcast_to(wb[b0:b0 + 1, sls[p]], (LANES, LANES)).T
            state[(seq(j), p)] = wcol * state[(seq(j), p)] + jnp.where(pair_diag, u, 0.0)
            sa_rows[p][j] = sa[(j, p)]
            os_rows[p][j] = xs[(j, p)][RW_BLK:]
    if L < TT:
        for (q, p), s in state.items():
            sout_ref[q, p] = s

    for p in range(npair):
        sa_t = jnp.concatenate(sa_rows[p], axis=0).astype(BF16)
        o_sa = jnp.where(lo_half, jnp.dot(cmat[2 * p], sa_t, preferred_element_type=F32),
                         jnp.dot(cmat[2 * p + 1], sa_t, preferred_element_type=F32))
        o_ref[:, sls[p]] = jnp.concatenate(os_rows[p], axis=0) + o_sa + ev[p]
    if L >= TT:
        for p in range(npair):
            s_ref[p] = state[(0, p)]

        @pl.when(ti == n_tiles - 1)
        def _():
            for p in range(npair):
                sout_ref[0, p] = state[(0, p)]


def _rw_chunk(r, lw, k, v, nkk, kka, s0, row_off, n_seq, L):
    TT = LANES
    assert L % RW_BLK == 0 and (L % TT == 0 or TT % L == 0)
    n_tok = n_seq * L
    n_tiles = n_tok // TT
    rb = row_off // TT
    ns = max(1, TT // L)
    tok = pl.BlockSpec((TT, BW), lambda i: (rb + i, 0))
    if ns == 1:
        sspec = pl.BlockSpec((1, NH // 2, LANES, LANES), lambda i: (0, 0, 0, 0))
    else:
        sspec = pl.BlockSpec((ns, NH // 2, LANES, LANES), lambda i: (i, 0, 0, 0))
    return pl.pallas_call(
        functools.partial(_rw_chunk_body, TT=TT, L=L, n_tiles=n_tiles),
        grid=(n_tiles,),
        in_specs=[tok] * 6 + [sspec],
        out_specs=[pl.BlockSpec((TT, BW), lambda i: (i, 0)), sspec],
        out_shape=[jax.ShapeDtypeStruct((n_tok, BW), F32),
                   jax.ShapeDtypeStruct(s0.shape, F32)],
        scratch_shapes=[pltpu.VMEM((NH // 2, LANES, LANES), F32)],
        compiler_params=_cparams(("arbitrary",)),
        name="rwkv_chunk",
    )(r, lw, k, v, nkk, kka, s0)


def _rw_post_body(o_ref, bonus_ref, gate_ref, lnw_ref, lnb_ref, seg_ref, out_ref):
    o = o_ref[...]
    mu = _split_dot(o, seg_ref[...]) * (1.0 / HD)
    d = o - mu
    var = _split_dot(d * d, seg_ref[...]) * (1.0 / HD)
    on = d * lax.rsqrt(var + RW_LN_EPS) * lnw_ref[...] + lnb_ref[...]
    out_ref[...] = (on + bonus_ref[...]) * gate_ref[...]


def _rw_post(o, bonus, gate, lnw, lnb, seg):
    T = o.shape[0]
    tm = _pick(T, (512, 256, 128))
    blk = pl.BlockSpec((tm, BW), lambda i: (i, 0))
    par = lambda a: pl.BlockSpec(a.shape, lambda i: (0,) * a.ndim)
    return pl.pallas_call(
        _rw_post_body,
        grid=(T // tm,),
        in_specs=[blk, blk, blk, par(lnw), par(lnb), par(seg)],
        out_specs=blk,
        out_shape=jax.ShapeDtypeStruct((T, BW), F32),
        compiler_params=_cparams(("parallel",)),
        name="rwkv_post",
    )(o, bonus, gate, lnw, lnb, seg)


def _heads_alt(w):
    lead = w.shape[:-1]
    w4 = w.reshape(*lead, NH // 2, 2, HD)
    zero = jnp.zeros((*lead, NH // 2, HD), w.dtype)
    ev = jnp.concatenate([w4[..., 0, :], zero], -1)
    od = jnp.concatenate([zero, w4[..., 1, :]], -1)
    return jnp.stack([ev, od], -2).reshape(*lead, NH * LANES)


def _heads_lo(w, nh):
    lead = w.shape[:-1]
    w3 = w.reshape(*lead, nh, HD)
    return jnp.concatenate([w3, jnp.zeros_like(w3)], -1).reshape(*lead, nh * LANES)


def _padc(w, n):
    return jnp.pad(w, [(0, 0)] * (w.ndim - 1) + [(0, n - w.shape[-1])])


def _rw_reorder(a):
    o = 0
    parts = {}
    for name, wd in (("rr", BW), ("wl", RW_LORA), ("rk", BW), ("rv", BW), ("al", RW_LORA), ("gl", RW_GATE_LORA)):
        parts[name] = a[..., o:o + wd]
        o += wd
    return jnp.concatenate([parts[n] for n in ("rr", "rk", "rv", "wl", "al", "gl")], axis=-1)


def _rw_unorder(a):
    rr, rk, rv = a[..., 0:BW], a[..., BW:2 * BW], a[..., 2 * BW:3 * BW]
    wl = a[..., 3 * BW:3 * BW + RW_LORA]
    al = a[..., 3 * BW + RW_LORA:3 * BW + 2 * RW_LORA]
    gl = a[..., 3 * BW + 2 * RW_LORA:RW_COLS]
    return jnp.concatenate([rr, wl, rk, rv, al, gl], axis=-1)


def _relayout_w_in(w, d_model):
    w = w.astype(BF16)
    o = 0

    def take(n):
        nonlocal o
        s = w[..., o:o + n]
        o += n
        return s
    sq, sk, sv = take(BW), take(BW), take(BW)
    dq, dk, dv, qi, ki, wi = take(BW), take(BW), take(BW), take(BW), take(HD), take(NH)
    gq, gk = take(GLA_H * HD), take(GLA_H * HD)
    gv, gr, ga = take(BW), take(BW), take(GLA_LOWRANK)
    rw = take(RW_COLS)
    gate = take(4 * d_model)
    cols = [_padc(_rw_reorder(rw), 4 * BW), _heads_alt(sq), sk, sv, _heads_alt(dq), dk, dv,
            _heads_lo(qi, NH), _heads_lo(gq, GLA_H), _heads_lo(gk, GLA_H), gv, gr,
            _padc(ki, LANES), _padc(wi, LANES), _padc(ga, 2 * LANES), gate]
    return jnp.concatenate(cols, axis=-1)


def _relayout_ffn(w_up, w_down, tf):
    f = w_up.shape[-1] // 2
    fp = -(-f // tf) * tf
    wup = jnp.concatenate([_padc(w_up[..., :f].astype(BF16), fp), _padc(w_up[..., f:].astype(BF16), fp)],
                          axis=-1)
    wdn = jnp.pad(w_down.astype(BF16), ((0, 0), (0, fp - f), (0, 0)))
    return wup, wdn


def _gain_alt(g):
    return jnp.tile(g, 2 * NH)[None, :]


def kernel(x_prompt, x_sample, cache_sb_k, cache_sb_v, cache_dsa_k, cache_dsa_v, cache_dsa_kidx, state_gla, state_rwkv, state_rwkv_shift, ffn1_norm, ffn1_up, ffn1_down, mix_norm, w_in, sb_qnorm, sb_knorm, dsa_qnorm, dsa_knorm, gla_a2, gla_ab, gla_onorm, rwkv_mu, rwkv_w0, rwkv_w2, rwkv_a0, rwkv_a2, rwkv_g2, rwkv_kk, rwkv_ka, rwkv_rk, rwkv_lnw, rwkv_lnb, w_branch, w_out, ffn2_norm, ffn2_up, ffn2_down):
    depth = w_in.shape[0]
    bp, lp, d_model = x_prompt.shape
    nb, ls, _ = x_sample.shape
    assert bp == 1, "prompt group is one sequence"
    tp = bp * lp
    ts = nb * ls
    past = cache_sb_k.shape[2]
    to_pos_minor = lambda c: jnp.transpose(c, (0, 1, 3, 4, 2))
    ck_sb, cv_sb = to_pos_minor(cache_sb_k), to_pos_minor(cache_sb_v)
    ck_dsa, cv_dsa = to_pos_minor(cache_dsa_k), to_pos_minor(cache_dsa_v)
    cki = jnp.transpose(cache_dsa_kidx, (0, 1, 3, 2))
    tf = 512

    x = jnp.concatenate([x_prompt.reshape(tp, d_model), x_sample.reshape(ts, d_model)], axis=0)

    bd = jnp.asarray(np.kron(np.eye(2 * NH), np.full((HD, HD), 1.0 / HD)), BF16)
    seg = jnp.asarray(np.kron(np.eye(NH), np.ones((HD, HD))), BF16)
    gmat = jnp.asarray(np.kron(np.eye(2), np.ones((HD, HD))), BF16)
    topk_p = min(TOPK_MAX, lp // 4)
    topk_s = min(TOPK_MAX, (past + ls) // 4)

    new_p = [[] for _ in range(8)]
    new_s = [[] for _ in range(8)]
    wup1, wdn1 = _relayout_ffn(ffn1_up, ffn1_down, tf)
    wup2, wdn2 = _relayout_ffn(ffn2_up, ffn2_down, tf)
    win = _relayout_w_in(w_in, d_model)
    wbr = w_branch.astype(BF16)
    wout = w_out.astype(BF16)
    for i in range(depth):
        x = _ffn(x, ffn1_norm[:, None], wup1, wdn1, tf, i)
        z = _proj(x, mix_norm[:, None], win, i)

        sqn, skn, dqn, dkn = _qknorm(z, _gain_alt(sb_qnorm[i]), jnp.tile(sb_knorm[i], NH)[None],
                                     _gain_alt(dsa_qnorm[i]), jnp.tile(dsa_knorm[i], NH)[None], bd)
        o_sb = jnp.concatenate([
            _sb_prompt(sqn, skn, z, tp),
            _sb_sample(sqn, skn, z, ck_sb, cv_sb, i, tp, nb, ls)], axis=0)
        mask_p = _dsa_sel_prompt(z, tp, topk_p)
        mask_s = _dsa_sel_sample(z, cki, i, tp, nb, ls, topk_s)
        o_dsa = jnp.concatenate([
            _dsa_attn_prompt(dqn, dkn, z, mask_p, tp),
            _dsa_attn_sample(dqn, dkn, z, mask_s, ck_dsa, cv_dsa, i, tp, nb, ls)], axis=0)

        a2p = jnp.pad(_heads_lo(gla_a2[i], GLA_H), ((0, LANES - GLA_LOWRANK), (0, 0))).astype(BF16)
        abp = _heads_lo(gla_ab[i][None], GLA_H)
        onorm = gla_onorm[i][None]

        def gla_state_in(s):
            return jnp.pad(jnp.swapaxes(s, 2, 3), ((0, 0), (0, 0), (0, 0), (0, LANES - HD)))

        def gla_state_out(s):
            return jnp.swapaxes(s[..., :HD], 2, 3)
        og_p, sg_p = _gla(z, a2p, abp, onorm, jnp.zeros((bp, GLA_H, GLA_DV, LANES), F32), 0, bp, lp)
        og_s, sg_s = _gla(z, a2p, abp, onorm, gla_state_in(state_gla[i]), tp, nb, ls)
        o_gla = jnp.concatenate([og_p, og_s], axis=0)

        zrw = z[:, :RW_COLS]
        shift_s = _rw_reorder(state_rwkv_shift[i])
        heads = jnp.concatenate([jnp.zeros((1, 4 * BW), F32), z[ls - 1:tp - 1:ls, :4 * BW],
                                 _padc(shift_s.reshape(nb, RW_COLS), 4 * BW)], axis=0)
        mu = _padc(_rw_reorder(rwkv_mu[i])[None], 4 * BW)
        w2p = jnp.concatenate([rwkv_w2[i], jnp.zeros_like(rwkv_a2[i])], axis=0).astype(BF16)
        a2q = jnp.concatenate([jnp.zeros_like(rwkv_w2[i]), rwkv_a2[i]], axis=0).astype(BF16)
        r_, w_, k_, v_, nkk_, kka_, gate_, bonus_ = _rw_prep(
            z, heads, ls, mu, rwkv_w0[i][None], rwkv_a0[i][None], rwkv_kk[i][None], rwkv_ka[i][None],
            rwkv_rk[i][None], w2p, a2q, rwkv_g2[i].astype(BF16), seg)

        def rw_state_in(s):
            n = s.shape[0]
            st = jnp.swapaxes(s, 2, 3).reshape(n, NH // 2, 2, HD, HD)
            zero = jnp.zeros((n, NH // 2, HD, HD), s.dtype)
            return jnp.concatenate([jnp.concatenate([st[:, :, 0], zero], axis=-1),
                                    jnp.concatenate([zero, st[:, :, 1]], axis=-1)], axis=-2)

        def rw_state_out(s):
            n = s.shape[0]
            st = jnp.stack([s[:, :, :HD, :HD], s[:, :, HD:, HD:]], axis=2)
            return jnp.swapaxes(st, 3, 4).reshape(n, NH, HD, HD)
        o_rp, sr_p = _rw_chunk(r_, w_, k_, v_, nkk_, kka_, jnp.zeros((bp, NH // 2, LANES, LANES), F32), 0, bp, lp)
        o_rs, sr_s = _rw_chunk(r_, w_, k_, v_, nkk_, kka_, rw_state_in(state_rwkv[i]), tp, nb, ls)
        o_r = jnp.concatenate([o_rp, o_rs], axis=0)
        o_rw = _rw_post(o_r, bonus_, gate_, rwkv_lnw[i][None], rwkv_lnb[i][None], seg)

        x = _merge(x, z, o_sb, o_dsa, o_gla, o_rw, wbr, wout, i)
        x = _ffn(x, ffn2_norm[:, None], wup2, wdn2, tf, i)

        sv = z[:, OFF_SV:OFF_SV + BW]
        dv = z[:, OFF_DV:OFF_DV + BW]
        ki = z[:, OFF_KI:OFF_KI + HD]
        for dst, (lo, n, l) in ((new_p, (0, bp, lp)), (new_s, (tp, nb, ls))):
            hi = lo + n * l
            dst[0].append(skn[lo:hi].reshape(n, l, NH, HD))
            dst[1].append(sv[lo:hi].reshape(n, l, NH, HD))
            dst[2].append(dkn[lo:hi].reshape(n, l, NH, HD))
            dst[3].append(dv[lo:hi].reshape(n, l, NH, HD))
            dst[4].append(ki[lo:hi].reshape(n, l, HD))
            dst[7].append(_rw_unorder(zrw[lo:hi].reshape(n, l, RW_COLS)[:, -1:]))
        new_p[5].append(gla_state_out(sg_p))
        new_s[5].append(gla_state_out(sg_s))
        new_p[6].append(rw_state_out(sr_p))
        new_s[6].append(rw_state_out(sr_s))

    outs_p = [jnp.stack(l, axis=0) for l in new_p]
    outs_s = [jnp.stack(l, axis=0) for l in new_s]
    y_p = x[:tp].reshape(bp, lp, d_model)
    y_s = x[tp:].reshape(nb, ls, d_model)
    return (y_p, y_s, *outs_p, *outs_s)
```

```python
import functools
import math

import jax
import jax.numpy as jnp
import numpy as np
from jax import lax
from jax.experimental import pallas as pl
from jax.experimental.pallas import tpu as pltpu

F32 = jnp.float32
BF16 = jnp.bfloat16

LANES = 128
SUBLANES = 8
HD = 64
NH = 8
BW = NH * HD
GLA_H = 4
GLA_DV = 128
GLA_LOWRANK = 16
GLA_GATE_NORM = 16.0
CHUNK = 64
TOPK_MAX = 256
RW_LORA = 64
RW_GATE_LORA = 128
RW_BLK = 16
RW_COLS = 3 * BW + 2 * RW_LORA + RW_GATE_LORA
NORM_EPS = 1e-6
RW_LN_EPS = 64e-5
NEG_BIG = -1e30
LOG2E = math.log2(math.e)
SB_UNDERFLOW = -104.0
VMEM_LIMIT = 56 * 1024 * 1024

OFF_RW = 0
OFF_SQ = 2048
OFF_SK = 3072
OFF_SV = 3584
OFF_DQ = 4096
OFF_DK = 5120
OFF_DV = 5632
OFF_QI = 6144
OFF_GQ = 7168
OFF_GK = 7680
OFF_GV = 8192
OFF_GR = 8704
OFF_KI = 9216
OFF_WI = 9344
OFF_GA = 9472
OFF_GATE = 9728


def _cparams(sem):
    return pltpu.CompilerParams(dimension_semantics=sem, vmem_limit_bytes=VMEM_LIMIT)


def _pick(n, prefs):
    for p in prefs:
        if n % p == 0:
            return p
    raise ValueError(f"no tile for {n} in {prefs}")


def _split_dot(a, b):
    hi = a.astype(BF16)
    lo = (a - hi.astype(F32)).astype(BF16)
    return (jnp.dot(hi, b, preferred_element_type=F32)
            + jnp.dot(lo, b, preferred_element_type=F32))


def _split_dot_left(a, b):
    hi = b.astype(BF16)
    lo = (b - hi.astype(F32)).astype(BF16)
    return (jnp.dot(a, hi, preferred_element_type=F32)
            + jnp.dot(a, lo, preferred_element_type=F32))


def _dot_nt(a, b):
    return lax.dot_general(a, b, (((1,), (1,)), ((), ())), preferred_element_type=F32)


def _dot_tn(a, b):
    return lax.dot_general(a, b, (((0,), (0,)), ((), ())), preferred_element_type=F32)


def _neg_softplus(z):
    return -(jnp.maximum(z, 0.0) + jnp.log(1.0 + jnp.exp(-jnp.abs(z))))


def _ffn_body(x_ref, g_ref, wg_ref, wu_ref, wdn_ref, o_ref, xn_ref):
    @pl.when(pl.program_id(1) == 0)
    def _():
        x = x_ref[...]
        ms = jnp.mean(x * x, axis=-1, keepdims=True)
        xn_ref[...] = (x * lax.rsqrt(ms + NORM_EPS) * g_ref[0]).astype(BF16)
        o_ref[...] = x

    xn = xn_ref[...]
    gate = jnp.dot(xn, wg_ref[0], preferred_element_type=F32)
    up = jnp.dot(xn, wu_ref[0], preferred_element_type=F32)
    act = (gate * jax.nn.sigmoid(gate) * up).astype(BF16)
    o_ref[...] += 0.5 * jnp.dot(act, wdn_ref[0], preferred_element_type=F32)


def _ffn(x, g, wup, wdn, tf, layer):
    T, D = x.shape
    nf = wdn.shape[1] // tf
    tm = _pick(T, (512, 256, 128))
    return pl.pallas_call(
        _ffn_body,
        grid=(T // tm, nf),
        in_specs=[pl.BlockSpec((tm, D), lambda i, j: (i, 0)),
                  pl.BlockSpec((1, 1, D), lambda i, j: (layer, 0, 0)),
                  pl.BlockSpec((1, D, tf), lambda i, j: (layer, 0, j)),
                  pl.BlockSpec((1, D, tf), lambda i, j: (layer, 0, nf + j)),
                  pl.BlockSpec((1, tf, D), lambda i, j: (layer, j, 0))],
        out_specs=pl.BlockSpec((tm, D), lambda i, j: (i, 0)),
        out_shape=jax.ShapeDtypeStruct((T, D), F32),
        scratch_shapes=[pltpu.VMEM((tm, D), BF16)],
        compiler_params=_cparams(("parallel", "arbitrary")),
        name="ffn",
    )(x, g, wup, wup, wdn)


def _proj_body(x_ref, g_ref, w_ref, o_ref, xn_ref):
    @pl.when(pl.program_id(1) == 0)
    def _():
        x = x_ref[...]
        ms = jnp.mean(x * x, axis=-1, keepdims=True)
        xn_ref[...] = (x * lax.rsqrt(ms + NORM_EPS) * g_ref[0]).astype(BF16)

    o_ref[...] = jnp.dot(xn_ref[...], w_ref[0], preferred_element_type=F32)


def _proj(x, g, w, layer):
    T, D = x.shape
    n = w.shape[2]
    tm = _pick(T, (1024, 512, 256, 128))
    tn = _pick(n, (1280, 512))
    return pl.pallas_call(
        _proj_body,
        grid=(T // tm, n // tn),
        in_specs=[pl.BlockSpec((tm, D), lambda i, j: (i, 0)),
                  pl.BlockSpec((1, 1, D), lambda i, j: (layer, 0, 0)),
                  pl.BlockSpec((1, D, tn), lambda i, j: (layer, 0, j))],
        out_specs=pl.BlockSpec((tm, tn), lambda i, j: (i, j)),
        out_shape=jax.ShapeDtypeStruct((T, n), F32),
        scratch_shapes=[pltpu.VMEM((tm, D), BF16)],
        compiler_params=_cparams(("parallel", "arbitrary")),
        name="in_proj",
    )(x, g, w)


def _merge_body(*refs, split, n_prompt_tiles):
    refs = list(refs)
    on_prompt = pl.program_id(0) < n_prompt_tiles
    branches = []
    for two in split:
        if two:
            bp, bs = refs.pop(0), refs.pop(0)
            branches.append(jnp.where(on_prompt, bp[...], bs[...]))
        else:
            branches.append(refs.pop(0)[...])
    g_refs, (wb_ref, wo_ref, x_ref, o_ref) = refs[:4], refs[4:]

    @pl.when(pl.program_id(1) == 0)
    def _():
        o_ref[...] = x_ref[...]

    acc = None
    for n in range(4):
        p = jnp.dot(branches[n].astype(BF16), wb_ref[0, n], preferred_element_type=F32)
        t = jax.nn.sigmoid(g_refs[n][...]) * p
        acc = t if acc is None else acc + t
    o_ref[...] += jnp.dot(acc.astype(BF16), wo_ref[0], preferred_element_type=F32)


def _merge(x, z, branches, wb, wo, layer, n_prompt):
    T, D = x.shape
    tm = _pick(math.gcd(T, n_prompt), (512, 256, 128))
    npt = n_prompt // tm
    tc = 512
    nc = D // tc
    gate_blk = OFF_GATE // tc
    split = tuple(isinstance(b, tuple) for b in branches)
    b_args, b_specs = [], []
    for b in branches:
        if isinstance(b, tuple):
            b_args += list(b)
            b_specs += [pl.BlockSpec((tm, BW), lambda i, c: (jnp.minimum(i, npt - 1), 0)),
                        pl.BlockSpec((tm, BW), lambda i, c: (jnp.maximum(i - npt, 0), 0))]
        else:
            b_args.append(b)
            b_specs.append(pl.BlockSpec((tm, BW), lambda i, c: (i, 0)))
    gspecs = [pl.BlockSpec((tm, tc), functools.partial(lambda i, c, n: (i, gate_blk + n * nc + c), n=n))
              for n in range(4)]
    return pl.pallas_call(
        functools.partial(_merge_body, split=split, n_prompt_tiles=npt),
        grid=(T // tm, nc),
        in_specs=b_specs + gspecs + [
            pl.BlockSpec((1, 4, BW, tc), lambda i, c: (layer, 0, 0, c)),
            pl.BlockSpec((1, tc, D), lambda i, c: (layer, c, 0)),
            pl.BlockSpec((tm, D), lambda i, c: (i, 0))],
        out_specs=pl.BlockSpec((tm, D), lambda i, c: (i, 0)),
        out_shape=jax.ShapeDtypeStruct((T, D), F32),
        compiler_params=_cparams(("parallel", "arbitrary")),
        name="merge",
    )(*b_args, z, z, z, z, wb, wo, x)


def _qknorm_body(sq_ref, sk_ref, dq_ref, dk_ref, gsq_ref, gsk_ref, gdq_ref, gdk_ref, bd_ref,
                 osq_ref, osk_ref, odq_ref, odk_ref):
    def norm(x, g, n):
        ms = _split_dot(x * x, bd_ref[:n, :n])
        return x * lax.rsqrt(ms + NORM_EPS) * g

    scale = HD ** -0.5
    osq_ref[...] = (norm(sq_ref[...], gsq_ref[...], 2 * BW) * scale).astype(BF16)
    osk_ref[...] = norm(sk_ref[...], gsk_ref[...], BW)
    odq_ref[...] = (norm(dq_ref[...], gdq_ref[...], 2 * BW) * (scale * LOG2E)).astype(BF16)
    odk_ref[...] = norm(dk_ref[...], gdk_ref[...], BW)


def _qknorm(z, gsq, gsk, gdq, gdk, bd):
    T = z.shape[0]
    tm = _pick(T, (512, 256, 128))
    wq, wk = 2 * BW, BW
    row = lambda w, off: pl.BlockSpec((tm, w), lambda i: (i, off // w))
    par = lambda w: pl.BlockSpec((1, w), lambda i: (0, 0))
    out = lambda w: pl.BlockSpec((tm, w), lambda i: (i, 0))
    return pl.pallas_call(
        _qknorm_body,
        grid=(T // tm,),
        in_specs=[row(wq, OFF_SQ), row(wk, OFF_SK), row(wq, OFF_DQ), row(wk, OFF_DK),
                  par(wq), par(wk), par(wq), par(wk),
                  pl.BlockSpec((wq, wq), lambda i: (0, 0))],
        out_specs=[out(wq), out(wk), out(wq), out(wk)],
        out_shape=[jax.ShapeDtypeStruct((T, wq), BF16), jax.ShapeDtypeStruct((T, wk), F32),
                   jax.ShapeDtypeStruct((T, wq), BF16), jax.ShapeDtypeStruct((T, wk), F32)],
        compiler_params=_cparams(("parallel",)),
        name="qk_norm",
    )(z, z, z, z, gsq, gsk, gdq, gdk, bd)


def _sb_body(q_ref, kn_ref, vn_ref, o_ref, *, tq):
    qb = pl.program_id(1)
    qs = [q_ref[:, e * LANES:(e + 1) * LANES] for e in range(2)]

    def tri(n):
        r = lax.broadcasted_iota(jnp.int32, (n, n), 0)
        c = lax.broadcasted_iota(jnp.int32, (n, n), 1)
        return (r > c).astype(BF16)

    def visit(state, kblk, vblk, u, valid):
        kb16 = kblk.astype(BF16)
        vb16 = vblk.astype(BF16)
        new = []
        for e in range(2):
            carry, acc = state[e]
            z = _dot_nt(qs[e], kb16)
            lg = _neg_softplus(z)
            if valid is not None:
                lg = jnp.where(valid, lg, 0.0)
            inner = _split_dot(lg, u)
            a = jnp.exp(z + lg + inner + carry)
            if valid is not None:
                a = jnp.where(valid, a, 0.0)
            acc = acc + jnp.dot(a.astype(BF16), vb16, preferred_element_type=F32)
            carry = carry + inner[:, :1] + lg[:, :1]
            new.append((carry, acc))
        return tuple(new)

    zero = (jnp.zeros((tq, 1), F32), jnp.zeros((tq, LANES), F32))
    state = (zero, zero)

    r = lax.broadcasted_iota(jnp.int32, (tq, tq), 0)
    c = lax.broadcasted_iota(jnp.int32, (tq, tq), 1)
    u_new = tri(tq)
    row0 = pl.multiple_of(qb * tq, tq)
    state = visit(state, kn_ref[pl.ds(row0, tq), :], vn_ref[pl.ds(row0, tq), :], u_new, c < r)

    def live(c):
        i, st = c
        return (i < qb) & (jnp.max(jnp.maximum(st[0][0], st[1][0])) > SB_UNDERFLOW)

    def step(c):
        i, st = c
        r0 = pl.multiple_of((qb - 1 - i) * tq, tq)
        return i + 1, visit(st, kn_ref[pl.ds(r0, tq), :], vn_ref[pl.ds(r0, tq), :], u_new, None)
    _, state = lax.while_loop(live, step, (jnp.int32(0), state))

    lane = lax.broadcasted_iota(jnp.int32, (tq, LANES), 1)
    o_ref[...] = jnp.where(lane < HD, state[0][1], state[1][1])


def _sb_prompt(qn, kn, z, n_rows):
    tq = 256
    vblk = OFF_SV // LANES
    return pl.pallas_call(
        functools.partial(_sb_body, tq=tq),
        grid=(NH // 2, n_rows // tq),
        in_specs=[pl.BlockSpec((tq, 2 * LANES), lambda p, i: (i, p)),
                  pl.BlockSpec((n_rows, LANES), lambda p, i: (0, p)),
                  pl.BlockSpec((n_rows, LANES), lambda p, i: (0, vblk + p))],
        out_specs=pl.BlockSpec((tq, LANES), lambda p, i: (i, p)),
        out_shape=jax.ShapeDtypeStruct((n_rows, BW), F32),
        compiler_params=_cparams(("parallel", "arbitrary")),
        name="sb_prompt",
    )(qn, kn, z)


def _q_head(q_ref, h):
    lo = h * LANES + HD * (h % 2)
    return q_ref[:, lo:lo + HD]


def _sb_sample_body(q_ref, kn_ref, vn_ref, kc_ref, vc_ref, o_ref, *, L, n_cache, tkc):
    qs = [_q_head(q_ref, h) for h in range(NH)]

    def tri(n):
        r = lax.broadcasted_iota(jnp.int32, (n, n), 0)
        c = lax.broadcasted_iota(jnp.int32, (n, n), 1)
        return (r > c).astype(BF16)

    def visit(state, kget, vget, u, valid, cached):
        ks = [kget(h).astype(BF16) for h in range(NH)]
        vs = [vget(h).astype(BF16) for h in range(NH)]
        if cached:
            zs = [jnp.dot(qs[h], ks[h], preferred_element_type=F32) for h in range(NH)]
        else:
            zs = [_dot_nt(qs[h], ks[h]) for h in range(NH)]
        lgs = [_neg_softplus(z) for z in zs]
        if valid is not None:
            lgs = [jnp.where(valid, lg, 0.0) for lg in lgs]
        inners = [_split_dot(lg, u) for lg in lgs]
        ws = [jnp.exp(zs[h] + lgs[h] + inners[h] + state[h][0]) for h in range(NH)]
        if valid is not None:
            ws = [jnp.where(valid, w, 0.0) for w in ws]
        ws = [w.astype(BF16) for w in ws]
        if cached:
            pv = [_dot_nt(ws[h], vs[h]) for h in range(NH)]
        else:
            pv = [jnp.dot(ws[h], vs[h], preferred_element_type=F32) for h in range(NH)]
        return tuple((state[h][0] + inners[h][:, :1] + lgs[h][:, :1], state[h][1] + pv[h])
                     for h in range(NH))

    zero = (jnp.zeros((L, 1), F32), jnp.zeros((L, HD), F32))
    r = lax.broadcasted_iota(jnp.int32, (L, L), 0)
    c = lax.broadcasted_iota(jnp.int32, (L, L), 1)
    state = visit((zero,) * NH, lambda h: kn_ref[:, h * HD:(h + 1) * HD],
                  lambda h: vn_ref[:, h * HD:(h + 1) * HD], tri(L), c < r, False)
    u_c = tri(tkc)

    def live(cr):
        i, st = cr
        top = functools.reduce(jnp.maximum, [s[0] for s in st])
        return (i < n_cache) & (jnp.max(top) > SB_UNDERFLOW)

    def step(cr):
        i, st = cr
        r0 = pl.multiple_of((n_cache - 1 - i) * tkc, tkc)
        return i + 1, visit(st, lambda h: kc_ref[0, 0, h, :, pl.ds(r0, tkc)],
                            lambda h: vc_ref[0, 0, h, :, pl.ds(r0, tkc)], u_c, None, True)
    _, state = lax.while_loop(live, step, (jnp.int32(0), state))
    for h in range(NH):
        o_ref[:, h * HD:(h + 1) * HD] = state[h][1]


def _sb_sample(qn, kn, z, cache_k, cache_v, layer, row_off, nb, L):
    P = cache_k.shape[4]
    tkc = _pick(P, (256, 128))
    rb = row_off // L
    cspec = pl.BlockSpec((1, 1, NH, HD, P), lambda b: (layer, b, 0, 0, 0))
    return pl.pallas_call(
        functools.partial(_sb_sample_body, L=L, n_cache=P // tkc, tkc=tkc),
        grid=(nb,),
        in_specs=[pl.BlockSpec((L, NH * LANES), lambda b: (rb + b, 0)),
                  pl.BlockSpec((L, BW), lambda b: (rb + b, 0)),
                  pl.BlockSpec((L, BW), lambda b: (rb + b, OFF_SV // BW)),
                  cspec, cspec],
        out_specs=pl.BlockSpec((L, BW), lambda b: (b, 0)),
        out_shape=jax.ShapeDtypeStruct((nb * L, BW), F32),
        compiler_params=_cparams(("parallel",)),
        name="sb_sample",
    )(qn, kn, z, cache_k, cache_v)


def _sortable(x):
    b = pltpu.bitcast(x + 0.0, jnp.int32)
    return jnp.where(b < 0, b ^ jnp.int32(0x7FFFFFFF), b)


def _select_topk(key_ref, n_blk, tkb, topk, rows):
    def count(pred):
        def body(i, acc):
            c0 = pl.multiple_of(i * tkb, tkb)
            for c in range(tkb // LANES):
                blk = key_ref[:, pl.ds(c0 + c * LANES, LANES)]
                col = c0 + c * LANES + lax.broadcasted_iota(jnp.int32, (rows, LANES), 1)
                acc = acc + jnp.where(pred(blk, col), 1.0, 0.0)
            return acc
        acc = lax.fori_loop(0, n_blk, body, jnp.zeros((rows, LANES), F32))
        return jnp.sum(acc, axis=1, keepdims=True)

    def unresolved(c):
        i, _, n_ge = c
        return (i < 32) & (jnp.max(jnp.abs(n_ge - topk)) > 0)

    def bit_step(c):
        i, thr, n_ge = c
        cand = thr + (jnp.int32(1) << (31 - i))
        n_cand = count(lambda blk, col: blk >= cand)
        take = n_cand >= topk
        return i + 1, jnp.where(take, cand, thr), jnp.where(take, n_cand, n_ge)

    thr0 = jnp.full((rows, 1), jnp.iinfo(jnp.int32).min, jnp.int32)
    n_all = jnp.full((rows, 1), 1.0, F32) * jnp.asarray(n_blk * tkb).astype(F32)
    _, thr, n_ge = lax.while_loop(unresolved, bit_step, (jnp.int32(0), thr0, n_all))
    n_col_bits = max(1, int(math.ceil(math.log2(key_ref.shape[1] + 1))))

    def tie_search():
        need = topk - count(lambda blk, col: blk > thr)

        def col_step(i, j):
            cand = j + (jnp.int32(1) << (n_col_bits - 1 - i))
            n = count(lambda blk, col: (blk == thr) & (col < cand))
            return jnp.where(n < need, cand, j)
        return lax.fori_loop(0, n_col_bits, col_step, jnp.zeros((rows, 1), jnp.int32))

    any_tie = jnp.max(jnp.abs(n_ge - topk)) > 0
    j_hi = lax.cond(any_tie, tie_search,
                    lambda: jnp.full((rows, 1), jnp.iinfo(jnp.int32).max, jnp.int32))
    return thr, j_hi


def _idx_scores(qi, wi, kblk16, k_real, keys_transposed=False):
    acc = None
    for h in range(NH):
        qh = qi[:, h * LANES:h * LANES + k_real].astype(BF16)
        if keys_transposed:
            d = jnp.dot(qh, kblk16, preferred_element_type=F32)
        else:
            d = _dot_nt(qh, kblk16)
        t = ((wi[:, h:h + 1] * (NH ** -0.5)) * (HD ** -0.5)) * jnp.maximum(d, 0.0)
        acc = t if acc is None else acc + t
    return acc


def _dsa_sel_prompt_body(qi_ref, wi_ref, ki_ref, m_ref, key_ref, *, tq, tkb, topk, n_keys):
    qb = pl.program_id(0)
    n_blk = ((qb + 1) * tq + tkb - 1) // tkb
    qi = qi_ref[...]
    wi = wi_ref[...]
    q_chunk = (qb * tq + lax.broadcasted_iota(jnp.int32, (tq, tkb), 0)) // CHUNK

    def score_blk(i, _):
        c0 = pl.multiple_of(i * tkb, tkb)
        s = _idx_scores(qi, wi, ki_ref[pl.ds(c0, tkb), :].astype(BF16), LANES)
        k_chunk = (c0 + lax.broadcasted_iota(jnp.int32, (tq, tkb), 1)) // CHUNK
        s = jnp.where(k_chunk <= q_chunk, s, -jnp.inf)
        key_ref[:, pl.ds(c0, tkb)] = _sortable(s)
        return 0
    lax.fori_loop(0, n_blk, score_blk, 0)

    thr, j_hi = _select_topk(key_ref, n_blk, tkb, topk, tq)
    neg_inf_key = _sortable(jnp.full((1, 1), -jnp.inf, F32))

    def write_blk(i, _):
        c0 = pl.multiple_of(i * tkb, tkb)
        blk = key_ref[:, pl.ds(c0, tkb)]
        col = c0 + lax.broadcasted_iota(jnp.int32, (tq, tkb), 1)
        sel = ((blk > thr) | ((blk == thr) & (col <= j_hi))) & (blk > neg_inf_key)
        m_ref[:, pl.ds(c0, tkb)] = jnp.where(sel, 0.0, NEG_BIG).astype(BF16)
        return 0
    lax.fori_loop(0, n_blk, write_blk, 0)

    def zero_blk(i, _):
        m_ref[:, pl.ds(pl.multiple_of(i * tkb, tkb), tkb)] = jnp.full((tq, tkb), NEG_BIG, BF16)
        return 0
    lax.fori_loop(n_blk, n_keys // tkb, zero_blk, 0)


def _dsa_sel_prompt(z, n_rows, topk):
    tq, tkb = 128, 512
    return pl.pallas_call(
        functools.partial(_dsa_sel_prompt_body, tq=tq, tkb=tkb, topk=topk, n_keys=n_rows),
        grid=(n_rows // tq,),
        in_specs=[pl.BlockSpec((tq, NH * LANES), lambda i: (i, OFF_QI // (NH * LANES))),
                  pl.BlockSpec((tq, LANES), lambda i: (i, OFF_WI // LANES)),
                  pl.BlockSpec((n_rows, LANES), lambda i: (0, OFF_KI // LANES))],
        out_specs=pl.BlockSpec((tq, n_rows), lambda i: (i, 0)),
        out_shape=jax.ShapeDtypeStruct((n_rows, n_rows), BF16),
        scratch_shapes=[pltpu.VMEM((tq, n_rows), jnp.int32)],
        compiler_params=_cparams(("parallel",)),
        name="dsa_select_prompt",
    )(z, z, z)


def _dsa_sel_sample_body(qi_ref, wi_ref, kin_ref, kic_ref, m_ref, key_ref, *, L, P, tkb, topk):
    qi = qi_ref[...]
    wi = wi_ref[...]
    n_c = P // tkb

    def score_blk(i, _):
        c0 = pl.multiple_of(i * tkb, tkb)
        s = _idx_scores(qi, wi, kic_ref[0, 0, :, pl.ds(c0, tkb)].astype(BF16), HD, keys_transposed=True)
        key_ref[:, pl.ds(c0, tkb)] = _sortable(s)
        return 0
    lax.fori_loop(0, n_c, score_blk, 0)

    knew = jnp.concatenate([kin_ref[...], jnp.zeros((tkb - L, LANES), F32)], axis=0).astype(BF16)
    s = _idx_scores(qi, wi, knew, LANES)
    col = lax.broadcasted_iota(jnp.int32, (L, tkb), 1)
    key_ref[:, pl.ds(P, tkb)] = _sortable(jnp.where(col < L, s, -jnp.inf))

    thr, j_hi = _select_topk(key_ref, n_c + 1, tkb, topk, L)
    neg_inf_key = _sortable(jnp.full((1, 1), -jnp.inf, F32))

    def write_blk(i, _):
        c0 = pl.multiple_of(i * tkb, tkb)
        blk = key_ref[:, pl.ds(c0, tkb)]
        cc = c0 + lax.broadcasted_iota(jnp.int32, (L, tkb), 1)
        sel = ((blk > thr) | ((blk == thr) & (cc <= j_hi))) & (blk > neg_inf_key)
        m_ref[:, pl.ds(c0, tkb)] = jnp.where(sel, 0.0, NEG_BIG).astype(BF16)
        return 0
    lax.fori_loop(0, n_c + 1, write_blk, 0)


def _dsa_sel_sample(z, cache_ki, layer, row_off, nb, L, topk):
    P = cache_ki.shape[3]
    tkb = _pick(P, (512, 256, 128))
    rb = row_off // L
    return pl.pallas_call(
        functools.partial(_dsa_sel_sample_body, L=L, P=P, tkb=tkb, topk=topk),
        grid=(nb,),
        in_specs=[pl.BlockSpec((L, NH * LANES), lambda b: (rb + b, OFF_QI // (NH * LANES))),
                  pl.BlockSpec((L, LANES), lambda b: (rb + b, OFF_WI // LANES)),
                  pl.BlockSpec((L, LANES), lambda b: (rb + b, OFF_KI // LANES)),
                  pl.BlockSpec((1, 1, HD, P), lambda b: (layer, b, 0, 0))],
        out_specs=pl.BlockSpec((L, P + tkb), lambda b: (b, 0)),
        out_shape=jax.ShapeDtypeStruct((nb * L, P + tkb), BF16),
        scratch_shapes=[pltpu.VMEM((L, P + tkb), jnp.int32)],
        compiler_params=_cparams(("parallel",)),
        name="dsa_select_sample",
    )(z, z, z, cache_ki)


def _dsa_attn_body(q_ref, m_ref, kn_ref, vn_ref, o_ref, *, tq, tkb):
    qb = pl.program_id(1)
    n_blk = ((qb + 1) * tq + tkb - 1) // tkb
    qs = [q_ref[:, e * LANES:(e + 1) * LANES] for e in range(2)]

    def visit(state, kblk, vblk, mblk):
        kb16 = kblk.astype(BF16)
        vb16 = vblk.astype(BF16)
        bias = mblk.astype(F32)
        new = []
        for e in range(2):
            m, l, acc = state[e]
            s = _dot_nt(qs[e], kb16) + bias
            m_new = jnp.maximum(m, jnp.max(s, axis=1, keepdims=True))
            p = jnp.exp2(s - m_new)
            alpha = jnp.exp2(m - m_new)
            l = alpha * l + jnp.sum(p, axis=1, keepdims=True)
            acc = alpha * acc + jnp.dot(p.astype(BF16), vb16, preferred_element_type=F32)
            new.append((m_new, l, acc))
        return tuple(new)

    init = (jnp.full((tq, 1), NEG_BIG, F32), jnp.zeros((tq, 1), F32), jnp.zeros((tq, LANES), F32))
    state = (init, init)

    def step(i, st):
        c0 = pl.multiple_of(i * tkb, tkb)
        return visit(st, kn_ref[pl.ds(c0, tkb), :], vn_ref[pl.ds(c0, tkb), :], m_ref[:, pl.ds(c0, tkb)])
    state = lax.fori_loop(0, n_blk, step, state)

    lane = lax.broadcasted_iota(jnp.int32, (tq, LANES), 1)
    o0 = state[0][2] / state[0][1]
    o1 = state[1][2] / state[1][1]
    o_ref[...] = jnp.where(lane < HD, o0, o1)


def _dsa_attn_prompt(qn, kn, z, mask, n_rows):
    tq, tkb = 256, min(1024, n_rows)
    vblk = OFF_DV // LANES
    return pl.pallas_call(
        functools.partial(_dsa_attn_body, tq=tq, tkb=tkb),
        grid=(NH // 2, n_rows // tq),
        in_specs=[pl.BlockSpec((tq, 2 * LANES), lambda p, i: (i, p)),
                  pl.BlockSpec((tq, n_rows), lambda p, i: (i, 0)),
                  pl.BlockSpec((n_rows, LANES), lambda p, i: (0, p)),
                  pl.BlockSpec((n_rows, LANES), lambda p, i: (0, vblk + p))],
        out_specs=pl.BlockSpec((tq, LANES), lambda p, i: (i, p)),
        out_shape=jax.ShapeDtypeStruct((n_rows, BW), F32),
        compiler_params=_cparams(("parallel", "arbitrary")),
        name="dsa_attn_prompt",
    )(qn, mask, kn, z)


def _dsa_attn_sample_body(q_ref, m_ref, kn_ref, vn_ref, kc_ref, vc_ref, o_ref, *, L, tkb, n_cache):
    qs = [_q_head(q_ref, h) for h in range(NH)]

    def visit(state, kget, vget, mblk, cached):
        bias = mblk.astype(F32)
        ks = [kget(h).astype(BF16) for h in range(NH)]
        vs = [vget(h).astype(BF16) for h in range(NH)]
        if cached:
            ss = [jnp.dot(qs[h], ks[h], preferred_element_type=F32) + bias for h in range(NH)]
        else:
            ss = [_dot_nt(qs[h], ks[h]) + bias for h in range(NH)]
        ms = [jnp.maximum(state[h][0], jnp.max(ss[h], axis=1, keepdims=True)) for h in range(NH)]
        ps = [jnp.exp2(ss[h] - ms[h]) for h in range(NH)]
        al = [jnp.exp2(state[h][0] - ms[h]) for h in range(NH)]
        if cached:
            pv = [_dot_nt(ps[h].astype(BF16), vs[h]) for h in range(NH)]
        else:
            pv = [jnp.dot(ps[h].astype(BF16), vs[h], preferred_element_type=F32) for h in range(NH)]
        return tuple((ms[h], al[h] * state[h][1] + jnp.sum(ps[h], axis=1, keepdims=True),
                      al[h] * state[h][2] + pv[h]) for h in range(NH))

    init = (jnp.full((L, 1), NEG_BIG, F32), jnp.zeros((L, 1), F32), jnp.zeros((L, HD), F32))

    def step(i, st):
        c0 = pl.multiple_of(i * tkb, tkb)
        return visit(st, lambda h: kc_ref[0, 0, h, :, pl.ds(c0, tkb)],
                     lambda h: vc_ref[0, 0, h, :, pl.ds(c0, tkb)], m_ref[:, pl.ds(c0, tkb)], True)
    state = lax.fori_loop(0, n_cache, step, (init,) * NH)

    pad = jnp.zeros((tkb - L, HD), F32)
    state = visit(state, lambda h: jnp.concatenate([kn_ref[:, h * HD:(h + 1) * HD], pad], axis=0),
                  lambda h: jnp.concatenate([vn_ref[:, h * HD:(h + 1) * HD], pad], axis=0),
                  m_ref[:, pl.ds(n_cache * tkb, tkb)], False)
    for h in range(NH):
        o_ref[:, h * HD:(h + 1) * HD] = state[h][2] / state[h][1]


def _dsa_attn_sample(qn, kn, z, mask, cache_k, cache_v, layer, row_off, nb, L):
    P = cache_k.shape[4]
    tkb = mask.shape[1] - P
    rb = row_off // L
    cspec = pl.BlockSpec((1, 1, NH, HD, P), lambda b: (layer, b, 0, 0, 0))
    return pl.pallas_call(
        functools.partial(_dsa_attn_sample_body, L=L, tkb=tkb, n_cache=P // tkb),
        grid=(nb,),
        in_specs=[pl.BlockSpec((L, NH * LANES), lambda b: (rb + b, 0)),
                  pl.BlockSpec((L, P + tkb), lambda b: (b, 0)),
                  pl.BlockSpec((L, BW), lambda b: (rb + b, 0)),
                  pl.BlockSpec((L, BW), lambda b: (rb + b, OFF_DV // BW)),
                  cspec, cspec],
        out_specs=pl.BlockSpec((L, BW), lambda b: (b, 0)),
        out_shape=jax.ShapeDtypeStruct((nb * L, BW), F32),
        compiler_params=_cparams(("parallel",)),
        name="dsa_attn_sample",
    )(qn, mask, kn, z, cache_k, cache_v)


def _gla_consts(C):
    nlev = int(math.log2(C))
    t = np.arange(C)
    cum = (t[:, None] >= t[None, :]).astype(np.float32)
    sel = np.zeros((nlev * C, C), np.float32)
    for l in range(nlev):
        m = 1 << l
        mid = (t // (2 * m)) * 2 * m + m
        sel[l * C + t, mid - 1] = 1.0
    return jnp.asarray(cum, BF16), jnp.asarray(sel, BF16)


def _gla_body(gq_ref, gk_ref, gv_ref, gr_ref, ga_ref, a2_ref, ab_ref, on_ref, cum_ref, sel_ref,
              s0_ref, o_ref, sout_ref, st_ref, *, C, n_chunks):
    ci = pl.program_id(1)
    nlev = int(math.log2(C))

    @pl.when(ci == 0)
    def _():
        st_ref[...] = s0_ref[0]

    x = jnp.dot(ga_ref[...].astype(BF16), a2_ref[...], preferred_element_type=F32) + ab_ref[...]
    g = _neg_softplus(-x) / GLA_GATE_NORM
    b = _split_dot_left(cum_ref[...], g)
    c_all = _split_dot_left(sel_ref[...], b)

    row = lax.broadcasted_iota(jnp.int32, (C, C), 0)
    col = lax.broadcasted_iota(jnp.int32, (C, C), 1)
    rowl = lax.broadcasted_iota(jnp.int32, (C, LANES), 0)

    hs = range(GLA_H)
    sls = [slice(h * LANES, (h + 1) * LANES) for h in hs]
    q = [gq_ref[:, sl] * (HD ** -0.5) for sl in sls]
    k = [gk_ref[:, sl] for sl in sls]
    v16 = [gv_ref[:, sl].astype(BF16) for sl in sls]
    bh = [b[:, sl] for sl in sls]
    st = [st_ref[h] for h in hs]
    o_inter = [_dot_nt((q[h] * jnp.exp(bh[h])).astype(BF16), st[h].astype(BF16)) for h in hs]
    b_last = [bh[h][C - 1:C, :] for h in hs]
    kv = [_dot_tn(v16[h], (k[h] * jnp.exp(b_last[h] - bh[h])).astype(BF16)) for h in hs]
    a = [jnp.where(row == col, jnp.sum(q[h] * k[h], axis=1, keepdims=True), 0.0) for h in hs]
    for l in range(nlev):
        m = 1 << l
        later = (rowl & m) != 0
        same_seg = (row // (2 * m)) == (col // (2 * m))
        ch = [c_all[l * C:(l + 1) * C, sl] for sl in sls]
        qe = [jnp.where(later, q[h] * jnp.exp(jnp.where(later, bh[h] - ch[h], 0.0)), 0.0) for h in hs]
        ke = [jnp.where(later, 0.0, k[h] * jnp.exp(jnp.where(later, 0.0, ch[h] - bh[h]))) for h in hs]
        al = [_dot_nt(qe[h].astype(BF16), ke[h].astype(BF16)) for h in hs]
        a = [a[h] + jnp.where(same_seg, al[h], 0.0) for h in hs]
    o = [jnp.dot(a[h].astype(BF16), v16[h], preferred_element_type=F32) + o_inter[h] for h in hs]
    for h in hs:
        st_ref[h] = st[h] * jnp.exp(b_last[h]) + kv[h]
        on = o[h] * lax.rsqrt(jnp.mean(o[h] * o[h], axis=-1, keepdims=True) + NORM_EPS) * on_ref[...]
        gr = gr_ref[:, sls[h]]
        o_ref[:, sls[h]] = on * (gr * jax.nn.sigmoid(gr))

    @pl.when(ci == n_chunks - 1)
    def _():
        sout_ref[0] = st_ref[...]


def _gla(z, a2p, abp, onorm, s0t, row_off, n_seq, L):
    C = min(CHUNK, L)
    n_chunks = L // C
    rb = row_off // C
    cum, sel = _gla_consts(C)
    blk = lambda off: pl.BlockSpec((C, BW), lambda s, c: (rb + s * n_chunks + c, off // BW))
    full = lambda a: pl.BlockSpec(a.shape, lambda s, c: (0,) * a.ndim)
    return pl.pallas_call(
        functools.partial(_gla_body, C=C, n_chunks=n_chunks),
        grid=(n_seq, n_chunks),
        in_specs=[blk(OFF_GQ), blk(OFF_GK), blk(OFF_GV), blk(OFF_GR),
                  pl.BlockSpec((C, LANES), lambda s, c: (rb + s * n_chunks + c, OFF_GA // LANES)),
                  full(a2p), full(abp), full(onorm), full(cum), full(sel),
                  pl.BlockSpec((1, GLA_H, GLA_DV, LANES), lambda s, c: (s, 0, 0, 0))],
        out_specs=[pl.BlockSpec((C, BW), lambda s, c: (s * n_chunks + c, 0)),
                   pl.BlockSpec((1, GLA_H, GLA_DV, LANES), lambda s, c: (s, 0, 0, 0))],
        out_shape=[jax.ShapeDtypeStruct((n_seq * L, BW), F32),
                   jax.ShapeDtypeStruct((n_seq, GLA_H, GLA_DV, LANES), F32)],
        scratch_shapes=[pltpu.VMEM((GLA_H, GLA_DV, LANES), F32)],
        compiler_params=_cparams(("parallel", "arbitrary")),
        name="gla",
    )(z, z, z, z, z, a2p, abp, onorm, cum, sel, s0t)


def _rw_prep_body(z_ref, head_ref, mu_ref, w0_ref, a0_ref, kkp_ref, kap_ref, rkp_ref,
                  w2_ref, a2_ref, g2_ref, seg_ref,
                  r_ref, w_ref, k_ref, v_ref, nkk_ref, kka_ref, gate_ref, bonus_ref, *, grp):
    zz = z_ref[...]
    tm, wd = zz.shape
    heads = head_ref[...]
    head_rows = jnp.concatenate([jnp.broadcast_to(heads[g:g + 1, :], (grp, wd))
                                 for g in range(tm // grp)], axis=0)
    row = lax.broadcasted_iota(jnp.int32, (tm, wd), 0)
    prev = jnp.where(row % grp == 0, head_rows, pltpu.roll(zz, 1, axis=0))
    zm = zz + mu_ref[...] * (prev - zz)
    rr = zm[:, 0:BW]
    rk = zm[:, BW:2 * BW]
    rv = zm[:, 2 * BW:3 * BW]
    lora = zm[:, 3 * BW:3 * BW + LANES]
    gl = zm[:, 3 * BW + LANES:3 * BW + 2 * LANES]
    xw = w0_ref[...] + jnp.dot(jnp.tanh(lora).astype(BF16), w2_ref[...], preferred_element_type=F32)
    wlog = _neg_softplus(-xw) - 0.5
    a = jax.nn.sigmoid(a0_ref[...] + jnp.dot(lora.astype(BF16), a2_ref[...], preferred_element_type=F32))
    gate = jnp.dot(jax.nn.sigmoid(gl).astype(BF16), g2_ref[...], preferred_element_type=F32)
    kkf = rk * kkp_ref[...]
    nrm = jnp.sqrt(_split_dot(kkf * kkf, seg_ref[...]))
    kk = kkf / jnp.maximum(nrm, 1e-12)
    kmod = rk * (1.0 + (a - 1.0) * kap_ref[...])
    coef = _split_dot(rr * kmod * rkp_ref[...], seg_ref[...])
    r_ref[...] = rr
    w_ref[...] = -jnp.exp(wlog)
    k_ref[...] = kmod
    v_ref[...] = rv
    nkk_ref[...] = -kk
    kka_ref[...] = kk * a
    gate_ref[...] = gate
    bonus_ref[...] = coef * rv


def _rw_prep(z, heads, grp, mu, w0, a0, kkp, kap, rkp, w2p, a2p, g2, seg):
    T = z.shape[0]
    tm = SUBLANES * grp
    assert T % tm == 0, (T, tm)
    W = 4 * BW
    par = lambda a: pl.BlockSpec(a.shape, lambda i: (0,) * a.ndim)
    out = pl.BlockSpec((tm, BW), lambda i: (i, 0))
    return pl.pallas_call(
        functools.partial(_rw_prep_body, grp=grp),
        grid=(T // tm,),
        in_specs=[pl.BlockSpec((tm, W), lambda i: (i, 0)), pl.BlockSpec((SUBLANES, W), lambda i: (i, 0)),
                  par(mu), par(w0), par(a0), par(kkp), par(kap), par(rkp),
                  par(w2p), par(a2p), par(g2), par(seg)],
        out_specs=[out] * 8,
        out_shape=[jax.ShapeDtypeStruct((T, BW), F32)] * 8,
        compiler_params=_cparams(("parallel",)),
        name="rwkv_prep",
    )(z, heads, mu, w0, a0, kkp, kap, rkp, w2p, a2p, g2, seg)


def _mm3(a, b):
    ah = a.astype(BF16)
    al = (a - ah.astype(F32)).astype(BF16)
    bh = b.astype(BF16)
    bl = (b - bh.astype(F32)).astype(BF16)
    d = lambda x, y: jnp.dot(x, y, preferred_element_type=F32)
    return d(ah, bh) + (d(ah, bl) + d(al, bh))


def _rw_chunk_body(r_ref, lw_ref, k_ref, v_ref, nkk_ref, kka_ref, s0_ref, o_ref, sout_ref, s_ref,
                   *, TT, L, n_tiles):
    ti = pl.program_id(0)
    npair = NH // 2
    nblk = TT // RW_BLK
    row = lax.broadcasted_iota(jnp.int32, (TT, TT), 0)
    col = lax.broadcasted_iota(jnp.int32, (TT, TT), 1)
    same = (row // RW_BLK) == (col // RW_BLK)
    strict = same & (col < row)
    incl = same & (col <= row)
    tri = jnp.where(incl, 1.0, 0.0).astype(BF16)
    last = jnp.where(same & (col % RW_BLK == RW_BLK - 1), 1.0, 0.0).astype(BF16)
    eye = jnp.where(row == col, 1.0, 0.0)
    pair_diag = (row < HD) == (col < HD)
    lo_half = lax.broadcasted_iota(jnp.int32, (TT, LANES), 1) < HD
    lo_half_b = lax.broadcasted_iota(jnp.int32, (RW_BLK, LANES), 1) < HD

    r, lw, k, v = r_ref[...], lw_ref[...], k_ref[...], v_ref[...]
    nk, ka = nkk_ref[...], kka_ref[...]
    lc = _split_dot_left(tri, lw)
    le = _split_dot_left(last, lc)
    nkt = nk * jnp.exp(lc - lw)
    rt = r * jnp.exp(lc)
    inv = jnp.exp(-lc)
    kah = ka * inv
    kh = k * inv
    rem = jnp.exp(le - lc)
    kaw = (ka * rem).astype(BF16)
    kw = (k * rem).astype(BF16)
    wb = jnp.exp(le)

    sls = [slice(p * LANES, (p + 1) * LANES) for p in range(npair)]
    v16 = [v[:, sl].astype(BF16) for sl in sls]
    heads = [(p, e) for p in range(npair) for e in range(2)]
    grams = []
    for p, e in heads:
        hm = lo_half if e == 0 else jnp.logical_not(lo_half)
        lhs = jnp.concatenate([jnp.where(hm, nkt[:, sls[p]], 0.0), jnp.where(hm, rt[:, sls[p]], 0.0)],
                              axis=0).astype(BF16)
        rhs = jnp.concatenate([kah[:, sls[p]], kh[:, sls[p]]], axis=0).astype(BF16)
        grams.append(_dot_nt(lhs, rhs))
    n1 = [jnp.where(strict, g[:TT, :TT], 0.0) for g in grams]
    bt = [jnp.where(strict, g[:TT, TT:], 0.0).astype(BF16) for g in grams]
    cmat = [jnp.where(incl, g[TT:, :TT], 0.0).astype(BF16) for g in grams]
    et = [jnp.where(incl, g[TT:, TT:], 0.0).astype(BF16) for g in grams]
    n2 = [_mm3(a, a) for a in n1]
    n4 = [_mm3(a, a) for a in n2]
    n8 = [_mm3(a, a) for a in n4]
    tinv = [eye + a for a in n1]
    for power in (n2, n4, n8):
        tinv = [t + _mm3(t, a) for t, a in zip(tinv, power)]
    vb_h = [jnp.dot(b_, v16[p], preferred_element_type=F32) for b_, (p, e) in zip(bt, heads)]
    ev_h = [jnp.dot(e_, v16[p], preferred_element_type=F32) for e_, (p, e) in zip(et, heads)]
    vb = [jnp.where(lo_half, vb_h[2 * p], vb_h[2 * p + 1]) for p in range(npair)]
    ev = [jnp.where(lo_half, ev_h[2 * p], ev_h[2 * p + 1]) for p in range(npair)]

    if L >= TT:
        @pl.when(ti == 0)
        def _():
            s_ref[...] = s0_ref[0]
        state = {(0, p): s_ref[p] for p in range(npair)}
        rounds = [[j] for j in range(nblk)]
    else:
        bps = L // RW_BLK
        state = {(q, p): s0_ref[q, p] for q in range(TT // L) for p in range(npair)}
        rounds = [[q * bps + i for q in range(TT // L)] for i in range(bps)]
    sa_rows = [[None] * nblk for _ in range(npair)]
    os_rows = [[None] * nblk for _ in range(npair)]
    for blocks in rounds:
        items = [(j, p) for j in blocks for p in range(npair)]
        seq = lambda j: (j * RW_BLK) // L if L < TT else 0
        xs = {}
        for j, p in items:
            bs = slice(j * RW_BLK, (j + 1) * RW_BLK)
            s = state[(seq(j), p)]
            s_hi = s.astype(BF16)
            s_lo = (s - s_hi.astype(F32)).astype(BF16)
            lhs = jnp.concatenate([nkt[bs, sls[p]], rt[bs, sls[p]]], axis=0).astype(BF16)
            xs[(j, p)] = (jnp.dot(lhs, s_hi, preferred_element_type=F32)
                          + jnp.dot(lhs, s_lo, preferred_element_type=F32))
        sa = {}
        for j, p in items:
            b0 = j * RW_BLK
            bs = slice(b0, b0 + RW_BLK)
            parts = [xs[(j, p)][:RW_BLK] + vb[p][bs]]
            if b0:
                parts.insert(0, jnp.zeros((b0, LANES), F32))
            if TT - b0 - RW_BLK:
                parts.append(jnp.zeros((TT - b0 - RW_BLK, LANES), F32))
            xfull = jnp.concatenate(parts, axis=0)
            sa[(j, p)] = jnp.where(lo_half_b, _mm3(tinv[2 * p][bs], xfull),
                                   _mm3(tinv[2 * p + 1][bs], xfull))
        for j, p in items:
            b0 = j * RW_BLK
            bs = slice(b0, b0 + RW_BLK)
            u = (_dot_tn(kaw[bs, sls[p]], sa[(j, p)].astype(BF16)) + _dot_tn(kw[bs, sls[p]], v16[p][bs]))
            wcol = jnp.broadcast_to(wb[b0:b0 + 1, sls[p]], (LANES, LANES)).T
            state[(seq(j), p)] = wcol * state[(seq(j), p)] + jnp.where(pair_diag, u, 0.0)
            sa_rows[p][j] = sa[(j, p)]
            os_rows[p][j] = xs[(j, p)][RW_BLK:]
    if L < TT:
        for (q, p), s in state.items():
            sout_ref[q, p] = s

    for p in range(npair):
        sa_t = jnp.concatenate(sa_rows[p], axis=0).astype(BF16)
        o_sa = jnp.where(lo_half, jnp.dot(cmat[2 * p], sa_t, preferred_element_type=F32),
                         jnp.dot(cmat[2 * p + 1], sa_t, preferred_element_type=F32))
        o_ref[:, sls[p]] = jnp.concatenate(os_rows[p], axis=0) + o_sa + ev[p]
    if L >= TT:
        for p in range(npair):
            s_ref[p] = state[(0, p)]

        @pl.when(ti == n_tiles - 1)
        def _():
            for p in range(npair):
                sout_ref[0, p] = state[(0, p)]


def _rw_chunk(r, lw, k, v, nkk, kka, s0, row_off, n_seq, L):
    TT = LANES
    assert L % RW_BLK == 0 and (L % TT == 0 or TT % L == 0)
    n_tok = n_seq * L
    n_tiles = n_tok // TT
    rb = row_off // TT
    ns = max(1, TT // L)
    tok = pl.BlockSpec((TT, BW), lambda i: (rb + i, 0))
    if ns == 1:
        sspec = pl.BlockSpec((1, NH // 2, LANES, LANES), lambda i: (0, 0, 0, 0))
    else:
        sspec = pl.BlockSpec((ns, NH // 2, LANES, LANES), lambda i: (i, 0, 0, 0))
    return pl.pallas_call(
        functools.partial(_rw_chunk_body, TT=TT, L=L, n_tiles=n_tiles),
        grid=(n_tiles,),
        in_specs=[tok] * 6 + [sspec],
        out_specs=[pl.BlockSpec((TT, BW), lambda i: (i, 0)), sspec],
        out_shape=[jax.ShapeDtypeStruct((n_tok, BW), F32),
                   jax.ShapeDtypeStruct(s0.shape, F32)],
        scratch_shapes=[pltpu.VMEM((NH // 2, LANES, LANES), F32)],
        compiler_params=_cparams(("arbitrary",)),
        name="rwkv_chunk",
    )(r, lw, k, v, nkk, kka, s0)


def _rw_post_body(o_ref, bonus_ref, gate_ref, lnw_ref, lnb_ref, seg_ref, out_ref):
    o = o_ref[...]
    mu = _split_dot(o, seg_ref[...]) * (1.0 / HD)
    d = o - mu
    var = _split_dot(d * d, seg_ref[...]) * (1.0 / HD)
    on = d * lax.rsqrt(var + RW_LN_EPS) * lnw_ref[...] + lnb_ref[...]
    out_ref[...] = (on + bonus_ref[...]) * gate_ref[...]


def _rw_post(o, bonus, gate, lnw, lnb, seg):
    T = o.shape[0]
    tm = _pick(T, (512, 256, 128))
    blk = pl.BlockSpec((tm, BW), lambda i: (i, 0))
    par = lambda a: pl.BlockSpec(a.shape, lambda i: (0,) * a.ndim)
    return pl.pallas_call(
        _rw_post_body,
        grid=(T // tm,),
        in_specs=[blk, blk, blk, par(lnw), par(lnb), par(seg)],
        out_specs=blk,
        out_shape=jax.ShapeDtypeStruct((T, BW), F32),
        compiler_params=_cparams(("parallel",)),
        name="rwkv_post",
    )(o, bonus, gate, lnw, lnb, seg)


def _heads_alt(w):
    lead = w.shape[:-1]
    w4 = w.reshape(*lead, NH // 2, 2, HD)
    zero = jnp.zeros((*lead, NH // 2, HD), w.dtype)
    ev = jnp.concatenate([w4[..., 0, :], zero], -1)
    od = jnp.concatenate([zero, w4[..., 1, :]], -1)
    return jnp.stack([ev, od], -2).reshape(*lead, NH * LANES)


def _heads_lo(w, nh):
    lead = w.shape[:-1]
    w3 = w.reshape(*lead, nh, HD)
    return jnp.concatenate([w3, jnp.zeros_like(w3)], -1).reshape(*lead, nh * LANES)


def _padc(w, n):
    return jnp.pad(w, [(0, 0)] * (w.ndim - 1) + [(0, n - w.shape[-1])])


def _rw_reorder(a):
    o = 0
    parts = {}
    for name, wd in (("rr", BW), ("wl", RW_LORA), ("rk", BW), ("rv", BW), ("al", RW_LORA), ("gl", RW_GATE_LORA)):
        parts[name] = a[..., o:o + wd]
        o += wd
    return jnp.concatenate([parts[n] for n in ("rr", "rk", "rv", "wl", "al", "gl")], axis=-1)


def _rw_unorder(a):
    rr, rk, rv = a[..., 0:BW], a[..., BW:2 * BW], a[..., 2 * BW:3 * BW]
    wl = a[..., 3 * BW:3 * BW + RW_LORA]
    al = a[..., 3 * BW + RW_LORA:3 * BW + 2 * RW_LORA]
    gl = a[..., 3 * BW + 2 * RW_LORA:RW_COLS]
    return jnp.concatenate([rr, wl, rk, rv, al, gl], axis=-1)


def _relayout_w_in(w, d_model):
    w = w.astype(BF16)
    o = 0

    def take(n):
        nonlocal o
        s = w[..., o:o + n]
        o += n
        return s
    sq, sk, sv = take(BW), take(BW), take(BW)
    dq, dk, dv, qi, ki, wi = take(BW), take(BW), take(BW), take(BW), take(HD), take(NH)
    gq, gk = take(GLA_H * HD), take(GLA_H * HD)
    gv, gr, ga = take(BW), take(BW), take(GLA_LOWRANK)
    rw = take(RW_COLS)
    gate = take(4 * d_model)
    cols = [_padc(_rw_reorder(rw), 4 * BW), _heads_alt(sq), sk, sv, _heads_alt(dq), dk, dv,
            _heads_lo(qi, NH), _heads_lo(gq, GLA_H), _heads_lo(gk, GLA_H), gv, gr,
            _padc(ki, LANES), _padc(wi, LANES), _padc(ga, 2 * LANES), gate]
    return jnp.concatenate(cols, axis=-1)


def _relayout_ffn(w_up, w_down, tf):
    f = w_up.shape[-1] // 2
    fp = -(-f // tf) * tf
    wup = jnp.concatenate([_padc(w_up[..., :f].astype(BF16), fp), _padc(w_up[..., f:].astype(BF16), fp)],
                          axis=-1)
    wdn = jnp.pad(w_down.astype(BF16), ((0, 0), (0, fp - f), (0, 0)))
    return wup, wdn


def _gain_alt(g):
    return jnp.tile(g, 2 * NH)[None, :]


def kernel(x_prompt, x_sample, cache_sb_k, cache_sb_v, cache_dsa_k, cache_dsa_v, cache_dsa_kidx, state_gla, state_rwkv, state_rwkv_shift, ffn1_norm, ffn1_up, ffn1_down, mix_norm, w_in, sb_qnorm, sb_knorm, dsa_qnorm, dsa_knorm, gla_a2, gla_ab, gla_onorm, rwkv_mu, rwkv_w0, rwkv_w2, rwkv_a0, rwkv_a2, rwkv_g2, rwkv_kk, rwkv_ka, rwkv_rk, rwkv_lnw, rwkv_lnb, w_branch, w_out, ffn2_norm, ffn2_up, ffn2_down):
    depth = w_in.shape[0]
    bp, lp, d_model = x_prompt.shape
    nb, ls, _ = x_sample.shape
    assert bp == 1, "prompt group is one sequence"
    tp = bp * lp
    ts = nb * ls
    past = cache_sb_k.shape[2]
    to_pos_minor = lambda c: jnp.transpose(c, (0, 1, 3, 4, 2))
    ck_sb, cv_sb = to_pos_minor(cache_sb_k), to_pos_minor(cache_sb_v)
    ck_dsa, cv_dsa = to_pos_minor(cache_dsa_k), to_pos_minor(cache_dsa_v)
    cki = jnp.transpose(cache_dsa_kidx, (0, 1, 3, 2))
    tf = 512

    x = jnp.concatenate([x_prompt.reshape(tp, d_model), x_sample.reshape(ts, d_model)], axis=0)

    bd = jnp.asarray(np.kron(np.eye(2 * NH), np.full((HD, HD), 1.0 / HD)), BF16)
    seg = jnp.asarray(np.kron(np.eye(NH), np.ones((HD, HD))), BF16)
    topk_p = min(TOPK_MAX, lp // 4)
    topk_s = min(TOPK_MAX, (past + ls) // 4)

    new_p = [[] for _ in range(8)]
    new_s = [[] for _ in range(8)]
    wup1, wdn1 = _relayout_ffn(ffn1_up, ffn1_down, tf)
    wup2, wdn2 = _relayout_ffn(ffn2_up, ffn2_down, tf)
    win = _relayout_w_in(w_in, d_model)
    wbr = w_branch.astype(BF16)
    wout = w_out.astype(BF16)
    for i in range(depth):
        x = _ffn(x, ffn1_norm[:, None], wup1, wdn1, tf, i)
        z = _proj(x, mix_norm[:, None], win, i)

        sqn, skn, dqn, dkn = _qknorm(z, _gain_alt(sb_qnorm[i]), jnp.tile(sb_knorm[i], NH)[None],
                                     _gain_alt(dsa_qnorm[i]), jnp.tile(dsa_knorm[i], NH)[None], bd)
        o_sb = (_sb_prompt(sqn, skn, z, tp), _sb_sample(sqn, skn, z, ck_sb, cv_sb, i, tp, nb, ls))
        mask_p = _dsa_sel_prompt(z, tp, topk_p)
        mask_s = _dsa_sel_sample(z, cki, i, tp, nb, ls, topk_s)
        o_dsa = (_dsa_attn_prompt(dqn, dkn, z, mask_p, tp),
                 _dsa_attn_sample(dqn, dkn, z, mask_s, ck_dsa, cv_dsa, i, tp, nb, ls))

        a2p = jnp.pad(_heads_lo(gla_a2[i], GLA_H), ((0, LANES - GLA_LOWRANK), (0, 0))).astype(BF16)
        abp = _heads_lo(gla_ab[i][None], GLA_H)
        onorm = gla_onorm[i][None]

        def gla_state_in(s):
            return jnp.pad(jnp.swapaxes(s, 2, 3), ((0, 0), (0, 0), (0, 0), (0, LANES - HD)))

        def gla_state_out(s):
            return jnp.swapaxes(s[..., :HD], 2, 3)
        og_p, sg_p = _gla(z, a2p, abp, onorm, jnp.zeros((bp, GLA_H, GLA_DV, LANES), F32), 0, bp, lp)
        og_s, sg_s = _gla(z, a2p, abp, onorm, gla_state_in(state_gla[i]), tp, nb, ls)
        o_gla = (og_p, og_s)

        zrw = z[:, :RW_COLS]
        shift_s = _rw_reorder(state_rwkv_shift[i])
        heads = jnp.concatenate([jnp.zeros((1, 4 * BW), F32), z[ls - 1:tp - 1:ls, :4 * BW],
                                 _padc(shift_s.reshape(nb, RW_COLS), 4 * BW)], axis=0)
        mu = _padc(_rw_reorder(rwkv_mu[i])[None], 4 * BW)
        w2p = jnp.concatenate([rwkv_w2[i], jnp.zeros_like(rwkv_a2[i])], axis=0).astype(BF16)
        a2q = jnp.concatenate([jnp.zeros_like(rwkv_w2[i]), rwkv_a2[i]], axis=0).astype(BF16)
        r_, w_, k_, v_, nkk_, kka_, gate_, bonus_ = _rw_prep(
            z, heads, ls, mu, rwkv_w0[i][None], rwkv_a0[i][None], rwkv_kk[i][None], rwkv_ka[i][None],
            rwkv_rk[i][None], w2p, a2q, rwkv_g2[i].astype(BF16), seg)

        def rw_state_in(s):
            n = s.shape[0]
            st = jnp.swapaxes(s, 2, 3).reshape(n, NH // 2, 2, HD, HD)
            zero = jnp.zeros((n, NH // 2, HD, HD), s.dtype)
            return jnp.concatenate([jnp.concatenate([st[:, :, 0], zero], axis=-1),
                                    jnp.concatenate([zero, st[:, :, 1]], axis=-1)], axis=-2)

        def rw_state_out(s):
            n = s.shape[0]
            st = jnp.stack([s[:, :, :HD, :HD], s[:, :, HD:, HD:]], axis=2)
            return jnp.swapaxes(st, 3, 4).reshape(n, NH, HD, HD)
        o_rp, sr_p = _rw_chunk(r_, w_, k_, v_, nkk_, kka_, jnp.zeros((bp, NH // 2, LANES, LANES), F32), 0, bp, lp)
        o_rs, sr_s = _rw_chunk(r_, w_, k_, v_, nkk_, kka_, rw_state_in(state_rwkv[i]), tp, nb, ls)
        o_r = jnp.concatenate([o_rp, o_rs], axis=0)
        o_rw = _rw_post(o_r, bonus_, gate_, rwkv_lnw[i][None], rwkv_lnb[i][None], seg)

        x = _merge(x, z, (o_sb, o_dsa, o_gla, o_rw), wbr, wout, i, tp)
        x = _ffn(x, ffn2_norm[:, None], wup2, wdn2, tf, i)

        sv = z[:, OFF_SV:OFF_SV + BW]
        dv = z[:, OFF_DV:OFF_DV + BW]
        ki = z[:, OFF_KI:OFF_KI + HD]
        for dst, (lo, n, l) in ((new_p, (0, bp, lp)), (new_s, (tp, nb, ls))):
            hi = lo + n * l
            dst[0].append(skn[lo:hi].reshape(n, l, NH, HD))
            dst[1].append(sv[lo:hi].reshape(n, l, NH, HD))
            dst[2].append(dkn[lo:hi].reshape(n, l, NH, HD))
            dst[3].append(dv[lo:hi].reshape(n, l, NH, HD))
            dst[4].append(ki[lo:hi].reshape(n, l, HD))
            dst[7].append(_rw_unorder(zrw[lo:hi].reshape(n, l, RW_COLS)[:, -1:]))
        new_p[5].append(gla_state_out(sg_p))
        new_s[5].append(gla_state_out(sg_s))
        new_p[6].append(rw_state_out(sr_p))
        new_s[6].append(rw_state_out(sr_s))

    outs_p = [jnp.stack(l, axis=0) for l in new_p]
    outs_s = [jnp.stack(l, axis=0) for l in new_s]
    y_p = x[:tp].reshape(bp, lp, d_model)
    y_s = x[tp:].reshape(nb, ls, d_model)
    return (y_p, y_s, *outs_p, *outs_s)
```

```python
import functools
import math

import jax
import jax.numpy as jnp
import numpy as np
from jax import lax
from jax.experimental import pallas as pl
from jax.experimental.pallas import tpu as pltpu

F32 = jnp.float32
BF16 = jnp.bfloat16

LANES = 128
SUBLANES = 8
HD = 64
NH = 8
BW = NH * HD
GLA_H = 4
GLA_DV = 128
GLA_LOWRANK = 16
GLA_GATE_NORM = 16.0
CHUNK = 64
TOPK_MAX = 256
RW_LORA = 64
RW_GATE_LORA = 128
RW_BLK = 16
RW_COLS = 3 * BW + 2 * RW_LORA + RW_GATE_LORA
NORM_EPS = 1e-6
RW_LN_EPS = 64e-5
NEG_BIG = -1e30
LOG2E = math.log2(math.e)
SB_UNDERFLOW = -104.0
VMEM_LIMIT = 56 * 1024 * 1024

OFF_RW = 0
OFF_SQ = 2048
OFF_SK = 3072
OFF_SV = 3584
OFF_DQ = 4096
OFF_DK = 5120
OFF_DV = 5632
OFF_QI = 6144
OFF_GQ = 7168
OFF_GK = 7680
OFF_GV = 8192
OFF_GR = 8704
OFF_KI = 9216
OFF_WI = 9344
OFF_GA = 9472
OFF_GATE = 9728


def _cparams(sem):
    return pltpu.CompilerParams(dimension_semantics=sem, vmem_limit_bytes=VMEM_LIMIT)


def _pick(n, prefs):
    for p in prefs:
        if n % p == 0:
            return p
    raise ValueError(f"no tile for {n} in {prefs}")


def _split_dot(a, b):
    hi = a.astype(BF16)
    lo = (a - hi.astype(F32)).astype(BF16)
    return (jnp.dot(hi, b, preferred_element_type=F32)
            + jnp.dot(lo, b, preferred_element_type=F32))


def _split_dot_left(a, b):
    hi = b.astype(BF16)
    lo = (b - hi.astype(F32)).astype(BF16)
    return (jnp.dot(a, hi, preferred_element_type=F32)
            + jnp.dot(a, lo, preferred_element_type=F32))


def _dot_nt(a, b):
    return lax.dot_general(a, b, (((1,), (1,)), ((), ())), preferred_element_type=F32)


def _dot_tn(a, b):
    return lax.dot_general(a, b, (((0,), (0,)), ((), ())), preferred_element_type=F32)


def _neg_softplus(z):
    return -(jnp.maximum(z, 0.0) + jnp.log(1.0 + jnp.exp(-jnp.abs(z))))


def _ffn_body(x_ref, g_ref, wg_ref, wu_ref, wdn_ref, o_ref, xn_ref):
    @pl.when(pl.program_id(1) == 0)
    def _():
        x = x_ref[...]
        ms = jnp.mean(x * x, axis=-1, keepdims=True)
        xn_ref[...] = (x * lax.rsqrt(ms + NORM_EPS) * g_ref[0]).astype(BF16)
        o_ref[...] = x

    xn = xn_ref[...]
    gate = jnp.dot(xn, wg_ref[0], preferred_element_type=F32)
    up = jnp.dot(xn, wu_ref[0], preferred_element_type=F32)
    act = (gate * jax.nn.sigmoid(gate) * up).astype(BF16)
    o_ref[...] += 0.5 * jnp.dot(act, wdn_ref[0], preferred_element_type=F32)


def _ffn(x, g, wup, wdn, tf, layer):
    T, D = x.shape
    nf = wdn.shape[1] // tf
    tm = _pick(T, (512, 256, 128))
    return pl.pallas_call(
        _ffn_body,
        grid=(T // tm, nf),
        in_specs=[pl.BlockSpec((tm, D), lambda i, j: (i, 0)),
                  pl.BlockSpec((1, 1, D), lambda i, j: (layer, 0, 0)),
                  pl.BlockSpec((1, D, tf), lambda i, j: (layer, 0, j)),
                  pl.BlockSpec((1, D, tf), lambda i, j: (layer, 0, nf + j)),
                  pl.BlockSpec((1, tf, D), lambda i, j: (layer, j, 0))],
        out_specs=pl.BlockSpec((tm, D), lambda i, j: (i, 0)),
        out_shape=jax.ShapeDtypeStruct((T, D), F32),
        scratch_shapes=[pltpu.VMEM((tm, D), BF16)],
        compiler_params=_cparams(("parallel", "arbitrary")),
        name="ffn",
    )(x, g, wup, wup, wdn)


def _proj_body(x_ref, g_ref, w_ref, o_ref, xn_ref):
    @pl.when(pl.program_id(1) == 0)
    def _():
        x = x_ref[...]
        ms = jnp.mean(x * x, axis=-1, keepdims=True)
        xn_ref[...] = (x * lax.rsqrt(ms + NORM_EPS) * g_ref[0]).astype(BF16)

    o_ref[...] = jnp.dot(xn_ref[...], w_ref[0], preferred_element_type=F32)


def _proj(x, g, w, layer):
    T, D = x.shape
    n = w.shape[2]
    tm = _pick(T, (1024, 512, 256, 128))
    tn = _pick(n, (1280, 512))
    return pl.pallas_call(
        _proj_body,
        grid=(T // tm, n // tn),
        in_specs=[pl.BlockSpec((tm, D), lambda i, j: (i, 0)),
                  pl.BlockSpec((1, 1, D), lambda i, j: (layer, 0, 0)),
                  pl.BlockSpec((1, D, tn), lambda i, j: (layer, 0, j))],
        out_specs=pl.BlockSpec((tm, tn), lambda i, j: (i, j)),
        out_shape=jax.ShapeDtypeStruct((T, n), F32),
        scratch_shapes=[pltpu.VMEM((tm, D), BF16)],
        compiler_params=_cparams(("parallel", "arbitrary")),
        name="in_proj",
    )(x, g, w)


def _merge_body(*refs, split, n_prompt_tiles):
    refs = list(refs)
    on_prompt = pl.program_id(0) < n_prompt_tiles
    branches = []
    for two in split:
        if two:
            bp, bs = refs.pop(0), refs.pop(0)
            branches.append(jnp.where(on_prompt, bp[...], bs[...]))
        else:
            branches.append(refs.pop(0)[...])
    g_refs, (wb_ref, wo_ref, x_ref, o_ref) = refs[:4], refs[4:]

    @pl.when(pl.program_id(1) == 0)
    def _():
        o_ref[...] = x_ref[...]

    acc = None
    for n in range(4):
        p = jnp.dot(branches[n].astype(BF16), wb_ref[0, n], preferred_element_type=F32)
        t = jax.nn.sigmoid(g_refs[n][...]) * p
        acc = t if acc is None else acc + t
    o_ref[...] += jnp.dot(acc.astype(BF16), wo_ref[0], preferred_element_type=F32)


def _merge(x, z, branches, wb, wo, layer, n_prompt):
    T, D = x.shape
    tm = _pick(math.gcd(T, n_prompt), (512, 256, 128))
    npt = n_prompt // tm
    tc = 512
    nc = D // tc
    gate_blk = OFF_GATE // tc
    split = tuple(isinstance(b, tuple) for b in branches)
    b_args, b_specs = [], []
    for b in branches:
        if isinstance(b, tuple):
            b_args += list(b)
            b_specs += [pl.BlockSpec((tm, BW), lambda i, c: (jnp.minimum(i, npt - 1), 0)),
                        pl.BlockSpec((tm, BW), lambda i, c: (jnp.maximum(i - npt, 0), 0))]
        else:
            b_args.append(b)
            b_specs.append(pl.BlockSpec((tm, BW), lambda i, c: (i, 0)))
    gspecs = [pl.BlockSpec((tm, tc), functools.partial(lambda i, c, n: (i, gate_blk + n * nc + c), n=n))
              for n in range(4)]
    return pl.pallas_call(
        functools.partial(_merge_body, split=split, n_prompt_tiles=npt),
        grid=(T // tm, nc),
        in_specs=b_specs + gspecs + [
            pl.BlockSpec((1, 4, BW, tc), lambda i, c: (layer, 0, 0, c)),
            pl.BlockSpec((1, tc, D), lambda i, c: (layer, c, 0)),
            pl.BlockSpec((tm, D), lambda i, c: (i, 0))],
        out_specs=pl.BlockSpec((tm, D), lambda i, c: (i, 0)),
        out_shape=jax.ShapeDtypeStruct((T, D), F32),
        compiler_params=_cparams(("parallel", "arbitrary")),
        name="merge",
    )(*b_args, z, z, z, z, wb, wo, x)


def _qknorm_body(sq_ref, sk_ref, dq_ref, dk_ref, gsq_ref, gsk_ref, gdq_ref, gdk_ref, bd_ref,
                 osq_ref, osk_ref, odq_ref, odk_ref):
    def norm(x, g, n):
        ms = _split_dot(x * x, bd_ref[:n, :n])
        return x * lax.rsqrt(ms + NORM_EPS) * g

    scale = HD ** -0.5
    osq_ref[...] = (norm(sq_ref[...], gsq_ref[...], 2 * BW) * scale).astype(BF16)
    osk_ref[...] = norm(sk_ref[...], gsk_ref[...], BW)
    odq_ref[...] = (norm(dq_ref[...], gdq_ref[...], 2 * BW) * (scale * LOG2E)).astype(BF16)
    odk_ref[...] = norm(dk_ref[...], gdk_ref[...], BW)


def _qknorm(z, gsq, gsk, gdq, gdk, bd):
    T = z.shape[0]
    tm = _pick(T, (512, 256, 128))
    wq, wk = 2 * BW, BW
    row = lambda w, off: pl.BlockSpec((tm, w), lambda i: (i, off // w))
    par = lambda w: pl.BlockSpec((1, w), lambda i: (0, 0))
    out = lambda w: pl.BlockSpec((tm, w), lambda i: (i, 0))
    return pl.pallas_call(
        _qknorm_body,
        grid=(T // tm,),
        in_specs=[row(wq, OFF_SQ), row(wk, OFF_SK), row(wq, OFF_DQ), row(wk, OFF_DK),
                  par(wq), par(wk), par(wq), par(wk),
                  pl.BlockSpec((wq, wq), lambda i: (0, 0))],
        out_specs=[out(wq), out(wk), out(wq), out(wk)],
        out_shape=[jax.ShapeDtypeStruct((T, wq), BF16), jax.ShapeDtypeStruct((T, wk), F32),
                   jax.ShapeDtypeStruct((T, wq), BF16), jax.ShapeDtypeStruct((T, wk), F32)],
        compiler_params=_cparams(("parallel",)),
        name="qk_norm",
    )(z, z, z, z, gsq, gsk, gdq, gdk, bd)


def _sb_body(q_ref, kn_ref, vn_ref, o_ref, *, tq):
    qb = pl.program_id(1)
    qs = [q_ref[:, e * LANES:(e + 1) * LANES] for e in range(2)]

    def tri(n):
        r = lax.broadcasted_iota(jnp.int32, (n, n), 0)
        c = lax.broadcasted_iota(jnp.int32, (n, n), 1)
        return (r > c).astype(BF16)

    def visit(state, kblk, vblk, u, valid):
        kb16 = kblk.astype(BF16)
        vb16 = vblk.astype(BF16)
        new = []
        for e in range(2):
            carry, acc = state[e]
            z = _dot_nt(qs[e], kb16)
            lg = _neg_softplus(z)
            if valid is not None:
                lg = jnp.where(valid, lg, 0.0)
            inner = _split_dot(lg, u)
            a = jnp.exp(z + lg + inner + carry)
            if valid is not None:
                a = jnp.where(valid, a, 0.0)
            acc = acc + jnp.dot(a.astype(BF16), vb16, preferred_element_type=F32)
            carry = carry + inner[:, :1] + lg[:, :1]
            new.append((carry, acc))
        return tuple(new)

    zero = (jnp.zeros((tq, 1), F32), jnp.zeros((tq, LANES), F32))
    state = (zero, zero)

    r = lax.broadcasted_iota(jnp.int32, (tq, tq), 0)
    c = lax.broadcasted_iota(jnp.int32, (tq, tq), 1)
    u_new = tri(tq)
    row0 = pl.multiple_of(qb * tq, tq)
    state = visit(state, kn_ref[pl.ds(row0, tq), :], vn_ref[pl.ds(row0, tq), :], u_new, c < r)

    def live(c):
        i, st = c
        return (i < qb) & (jnp.max(jnp.maximum(st[0][0], st[1][0])) > SB_UNDERFLOW)

    def step(c):
        i, st = c
        r0 = pl.multiple_of((qb - 1 - i) * tq, tq)
        return i + 1, visit(st, kn_ref[pl.ds(r0, tq), :], vn_ref[pl.ds(r0, tq), :], u_new, None)
    _, state = lax.while_loop(live, step, (jnp.int32(0), state))

    lane = lax.broadcasted_iota(jnp.int32, (tq, LANES), 1)
    o_ref[...] = jnp.where(lane < HD, state[0][1], state[1][1])


def _sb_prompt(qn, kn, z, n_rows):
    tq = 256
    vblk = OFF_SV // LANES
    return pl.pallas_call(
        functools.partial(_sb_body, tq=tq),
        grid=(NH // 2, n_rows // tq),
        in_specs=[pl.BlockSpec((tq, 2 * LANES), lambda p, i: (i, p)),
                  pl.BlockSpec((n_rows, LANES), lambda p, i: (0, p)),
                  pl.BlockSpec((n_rows, LANES), lambda p, i: (0, vblk + p))],
        out_specs=pl.BlockSpec((tq, LANES), lambda p, i: (i, p)),
        out_shape=jax.ShapeDtypeStruct((n_rows, BW), F32),
        compiler_params=_cparams(("parallel", "arbitrary")),
        name="sb_prompt",
    )(qn, kn, z)


def _q_head(q_ref, h):
    lo = h * LANES + HD * (h % 2)
    return q_ref[:, lo:lo + HD]


def _sb_sample_body(q_ref, kn_ref, vn_ref, kc_ref, vc_ref, o_ref, *, L, n_cache, tkc):
    qs = [_q_head(q_ref, h) for h in range(NH)]

    def tri(n):
        r = lax.broadcasted_iota(jnp.int32, (n, n), 0)
        c = lax.broadcasted_iota(jnp.int32, (n, n), 1)
        return (r > c).astype(BF16)

    def visit(state, kget, vget, u, valid, cached):
        ks = [kget(h).astype(BF16) for h in range(NH)]
        vs = [vget(h).astype(BF16) for h in range(NH)]
        if cached:
            zs = [jnp.dot(qs[h], ks[h], preferred_element_type=F32) for h in range(NH)]
        else:
            zs = [_dot_nt(qs[h], ks[h]) for h in range(NH)]
        lgs = [_neg_softplus(z) for z in zs]
        if valid is not None:
            lgs = [jnp.where(valid, lg, 0.0) for lg in lgs]
        inners = [_split_dot(lg, u) for lg in lgs]
        ws = [jnp.exp(zs[h] + lgs[h] + inners[h] + state[h][0]) for h in range(NH)]
        if valid is not None:
            ws = [jnp.where(valid, w, 0.0) for w in ws]
        ws = [w.astype(BF16) for w in ws]
        if cached:
            pv = [_dot_nt(ws[h], vs[h]) for h in range(NH)]
        else:
            pv = [jnp.dot(ws[h], vs[h], preferred_element_type=F32) for h in range(NH)]
        return tuple((state[h][0] + inners[h][:, :1] + lgs[h][:, :1], state[h][1] + pv[h])
                     for h in range(NH))

    zero = (jnp.zeros((L, 1), F32), jnp.zeros((L, HD), F32))
    r = lax.broadcasted_iota(jnp.int32, (L, L), 0)
    c = lax.broadcasted_iota(jnp.int32, (L, L), 1)
    state = visit((zero,) * NH, lambda h: kn_ref[:, h * HD:(h + 1) * HD],
                  lambda h: vn_ref[:, h * HD:(h + 1) * HD], tri(L), c < r, False)
    u_c = tri(tkc)

    def live(cr):
        i, st = cr
        top = functools.reduce(jnp.maximum, [s[0] for s in st])
        return (i < n_cache) & (jnp.max(top) > SB_UNDERFLOW)

    def step(cr):
        i, st = cr
        r0 = pl.multiple_of((n_cache - 1 - i) * tkc, tkc)
        return i + 1, visit(st, lambda h: kc_ref[0, 0, h, :, pl.ds(r0, tkc)],
                            lambda h: vc_ref[0, 0, h, :, pl.ds(r0, tkc)], u_c, None, True)
    _, state = lax.while_loop(live, step, (jnp.int32(0), state))
    for h in range(NH):
        o_ref[:, h * HD:(h + 1) * HD] = state[h][1]


def _sb_sample(qn, kn, z, cache_k, cache_v, layer, row_off, nb, L):
    P = cache_k.shape[4]
    tkc = _pick(P, (256, 128))
    rb = row_off // L
    cspec = pl.BlockSpec((1, 1, NH, HD, P), lambda b: (layer, b, 0, 0, 0))
    return pl.pallas_call(
        functools.partial(_sb_sample_body, L=L, n_cache=P // tkc, tkc=tkc),
        grid=(nb,),
        in_specs=[pl.BlockSpec((L, NH * LANES), lambda b: (rb + b, 0)),
                  pl.BlockSpec((L, BW), lambda b: (rb + b, 0)),
                  pl.BlockSpec((L, BW), lambda b: (rb + b, OFF_SV // BW)),
                  cspec, cspec],
        out_specs=pl.BlockSpec((L, BW), lambda b: (b, 0)),
        out_shape=jax.ShapeDtypeStruct((nb * L, BW), F32),
        compiler_params=_cparams(("parallel",)),
        name="sb_sample",
    )(qn, kn, z, cache_k, cache_v)


def _sortable(x):
    b = pltpu.bitcast(x + 0.0, jnp.int32)
    return jnp.where(b < 0, b ^ jnp.int32(0x7FFFFFFF), b)


def _select_topk(key_ref, n_blk, tkb, topk, rows):
    def count(pred):
        def body(i, acc):
            c0 = pl.multiple_of(i * tkb, tkb)
            for c in range(tkb // LANES):
                blk = key_ref[:, pl.ds(c0 + c * LANES, LANES)]
                col = c0 + c * LANES + lax.broadcasted_iota(jnp.int32, (rows, LANES), 1)
                acc = acc + jnp.where(pred(blk, col), 1.0, 0.0)
            return acc
        acc = lax.fori_loop(0, n_blk, body, jnp.zeros((rows, LANES), F32))
        return jnp.sum(acc, axis=1, keepdims=True)

    def unresolved(c):
        i, _, n_ge = c
        return (i < 32) & (jnp.max(jnp.abs(n_ge - topk)) > 0)

    def bit_step(c):
        i, thr, n_ge = c
        cand = thr + (jnp.int32(1) << (31 - i))
        n_cand = count(lambda blk, col: blk >= cand)
        take = n_cand >= topk
        return i + 1, jnp.where(take, cand, thr), jnp.where(take, n_cand, n_ge)

    thr0 = jnp.full((rows, 1), jnp.iinfo(jnp.int32).min, jnp.int32)
    n_all = jnp.full((rows, 1), 1.0, F32) * jnp.asarray(n_blk * tkb).astype(F32)
    _, thr, n_ge = lax.while_loop(unresolved, bit_step, (jnp.int32(0), thr0, n_all))
    n_col_bits = max(1, int(math.ceil(math.log2(key_ref.shape[1] + 1))))

    def tie_search():
        need = topk - count(lambda blk, col: blk > thr)

        def col_step(i, j):
            cand = j + (jnp.int32(1) << (n_col_bits - 1 - i))
            n = count(lambda blk, col: (blk == thr) & (col < cand))
            return jnp.where(n < need, cand, j)
        return lax.fori_loop(0, n_col_bits, col_step, jnp.zeros((rows, 1), jnp.int32))

    any_tie = jnp.max(jnp.abs(n_ge - topk)) > 0
    j_hi = lax.cond(any_tie, tie_search,
                    lambda: jnp.full((rows, 1), jnp.iinfo(jnp.int32).max, jnp.int32))
    return thr, j_hi


def _idx_scores(qi, wi, kblk16, k_real, keys_transposed=False):
    acc = None
    for h in range(NH):
        qh = qi[:, h * LANES:h * LANES + k_real].astype(BF16)
        if keys_transposed:
            d = jnp.dot(qh, kblk16, preferred_element_type=F32)
        else:
            d = _dot_nt(qh, kblk16)
        t = ((wi[:, h:h + 1] * (NH ** -0.5)) * (HD ** -0.5)) * jnp.maximum(d, 0.0)
        acc = t if acc is None else acc + t
    return acc


def _dsa_sel_prompt_body(qi_ref, wi_ref, ki_ref, m_ref, key_ref, *, tq, tkb, topk, n_keys):
    qb = pl.program_id(0)
    n_blk = ((qb + 1) * tq + tkb - 1) // tkb
    qi = qi_ref[...]
    wi = wi_ref[...]
    q_chunk = (qb * tq + lax.broadcasted_iota(jnp.int32, (tq, tkb), 0)) // CHUNK

    def score_blk(i, _):
        c0 = pl.multiple_of(i * tkb, tkb)
        s = _idx_scores(qi, wi, ki_ref[pl.ds(c0, tkb), :].astype(BF16), LANES)
        k_chunk = (c0 + lax.broadcasted_iota(jnp.int32, (tq, tkb), 1)) // CHUNK
        s = jnp.where(k_chunk <= q_chunk, s, -jnp.inf)
        key_ref[:, pl.ds(c0, tkb)] = _sortable(s)
        return 0
    lax.fori_loop(0, n_blk, score_blk, 0)

    thr, j_hi = _select_topk(key_ref, n_blk, tkb, topk, tq)
    neg_inf_key = _sortable(jnp.full((1, 1), -jnp.inf, F32))

    def write_blk(i, _):
        c0 = pl.multiple_of(i * tkb, tkb)
        blk = key_ref[:, pl.ds(c0, tkb)]
        col = c0 + lax.broadcasted_iota(jnp.int32, (tq, tkb), 1)
        sel = ((blk > thr) | ((blk == thr) & (col <= j_hi))) & (blk > neg_inf_key)
        m_ref[:, pl.ds(c0, tkb)] = jnp.where(sel, 0.0, NEG_BIG).astype(BF16)
        return 0
    lax.fori_loop(0, n_blk, write_blk, 0)

    def zero_blk(i, _):
        m_ref[:, pl.ds(pl.multiple_of(i * tkb, tkb), tkb)] = jnp.full((tq, tkb), NEG_BIG, BF16)
        return 0
    lax.fori_loop(n_blk, n_keys // tkb, zero_blk, 0)


def _dsa_sel_prompt(z, n_rows, topk):
    tq, tkb = 128, 512
    return pl.pallas_call(
        functools.partial(_dsa_sel_prompt_body, tq=tq, tkb=tkb, topk=topk, n_keys=n_rows),
        grid=(n_rows // tq,),
        in_specs=[pl.BlockSpec((tq, NH * LANES), lambda i: (i, OFF_QI // (NH * LANES))),
                  pl.BlockSpec((tq, LANES), lambda i: (i, OFF_WI // LANES)),
                  pl.BlockSpec((n_rows, LANES), lambda i: (0, OFF_KI // LANES))],
        out_specs=pl.BlockSpec((tq, n_rows), lambda i: (i, 0)),
        out_shape=jax.ShapeDtypeStruct((n_rows, n_rows), BF16),
        scratch_shapes=[pltpu.VMEM((tq, n_rows), jnp.int32)],
        compiler_params=_cparams(("parallel",)),
        name="dsa_select_prompt",
    )(z, z, z)


def _dsa_sel_sample_body(qi_ref, wi_ref, kin_ref, kic_ref, m_ref, key_ref, *, L, P, tkb, topk):
    qi = qi_ref[...]
    wi = wi_ref[...]
    n_c = P // tkb

    def score_blk(i, _):
        c0 = pl.multiple_of(i * tkb, tkb)
        s = _idx_scores(qi, wi, kic_ref[0, 0, :, pl.ds(c0, tkb)].astype(BF16), HD, keys_transposed=True)
        key_ref[:, pl.ds(c0, tkb)] = _sortable(s)
        return 0
    lax.fori_loop(0, n_c, score_blk, 0)

    knew = jnp.concatenate([kin_ref[...], jnp.zeros((tkb - L, LANES), F32)], axis=0).astype(BF16)
    s = _idx_scores(qi, wi, knew, LANES)
    col = lax.broadcasted_iota(jnp.int32, (L, tkb), 1)
    key_ref[:, pl.ds(P, tkb)] = _sortable(jnp.where(col < L, s, -jnp.inf))

    thr, j_hi = _select_topk(key_ref, n_c + 1, tkb, topk, L)
    neg_inf_key = _sortable(jnp.full((1, 1), -jnp.inf, F32))

    def write_blk(i, _):
        c0 = pl.multiple_of(i * tkb, tkb)
        blk = key_ref[:, pl.ds(c0, tkb)]
        cc = c0 + lax.broadcasted_iota(jnp.int32, (L, tkb), 1)
        sel = ((blk > thr) | ((blk == thr) & (cc <= j_hi))) & (blk > neg_inf_key)
        m_ref[:, pl.ds(c0, tkb)] = jnp.where(sel, 0.0, NEG_BIG).astype(BF16)
        return 0
    lax.fori_loop(0, n_c + 1, write_blk, 0)


def _dsa_sel_sample(z, cache_ki, layer, row_off, nb, L, topk):
    P = cache_ki.shape[3]
    tkb = _pick(P, (512, 256, 128))
    rb = row_off // L
    return pl.pallas_call(
        functools.partial(_dsa_sel_sample_body, L=L, P=P, tkb=tkb, topk=topk),
        grid=(nb,),
        in_specs=[pl.BlockSpec((L, NH * LANES), lambda b: (rb + b, OFF_QI // (NH * LANES))),
                  pl.BlockSpec((L, LANES), lambda b: (rb + b, OFF_WI // LANES)),
                  pl.BlockSpec((L, LANES), lambda b: (rb + b, OFF_KI // LANES)),
                  pl.BlockSpec((1, 1, HD, P), lambda b: (layer, b, 0, 0))],
        out_specs=pl.BlockSpec((L, P + tkb), lambda b: (b, 0)),
        out_shape=jax.ShapeDtypeStruct((nb * L, P + tkb), BF16),
        scratch_shapes=[pltpu.VMEM((L, P + tkb), jnp.int32)],
        compiler_params=_cparams(("parallel",)),
        name="dsa_select_sample",
    )(z, z, z, cache_ki)


def _dsa_attn_body(q_ref, m_ref, kn_ref, vn_ref, o_ref, *, tq, tkb):
    qb = pl.program_id(1)
    n_blk = ((qb + 1) * tq + tkb - 1) // tkb
    qs = [q_ref[:, e * LANES:(e + 1) * LANES] for e in range(2)]

    def visit(state, kblk, vblk, mblk):
        kb16 = kblk.astype(BF16)
        vb16 = vblk.astype(BF16)
        bias = mblk.astype(F32)
        new = []
        for e in range(2):
            m, l, acc = state[e]
            s = _dot_nt(qs[e], kb16) + bias
            m_new = jnp.maximum(m, jnp.max(s, axis=1, keepdims=True))
            p = jnp.exp2(s - m_new)
            alpha = jnp.exp2(m - m_new)
            l = alpha * l + jnp.sum(p, axis=1, keepdims=True)
            acc = alpha * acc + jnp.dot(p.astype(BF16), vb16, preferred_element_type=F32)
            new.append((m_new, l, acc))
        return tuple(new)

    init = (jnp.full((tq, 1), NEG_BIG, F32), jnp.zeros((tq, 1), F32), jnp.zeros((tq, LANES), F32))
    state = (init, init)

    def step(i, st):
        c0 = pl.multiple_of(i * tkb, tkb)
        return visit(st, kn_ref[pl.ds(c0, tkb), :], vn_ref[pl.ds(c0, tkb), :], m_ref[:, pl.ds(c0, tkb)])
    state = lax.fori_loop(0, n_blk, step, state)

    lane = lax.broadcasted_iota(jnp.int32, (tq, LANES), 1)
    o0 = state[0][2] / state[0][1]
    o1 = state[1][2] / state[1][1]
    o_ref[...] = jnp.where(lane < HD, o0, o1)


def _dsa_attn_prompt(qn, kn, z, mask, n_rows):
    tq, tkb = min(512, n_rows), min(1024, n_rows)
    vblk = OFF_DV // LANES
    return pl.pallas_call(
        functools.partial(_dsa_attn_body, tq=tq, tkb=tkb),
        grid=(NH // 2, n_rows // tq),
        in_specs=[pl.BlockSpec((tq, 2 * LANES), lambda p, i: (i, p)),
                  pl.BlockSpec((tq, n_rows), lambda p, i: (i, 0)),
                  pl.BlockSpec((n_rows, LANES), lambda p, i: (0, p)),
                  pl.BlockSpec((n_rows, LANES), lambda p, i: (0, vblk + p))],
        out_specs=pl.BlockSpec((tq, LANES), lambda p, i: (i, p)),
        out_shape=jax.ShapeDtypeStruct((n_rows, BW), F32),
        compiler_params=_cparams(("parallel", "arbitrary")),
        name="dsa_attn_prompt",
    )(qn, mask, kn, z)


def _dsa_attn_sample_body(q_ref, m_ref, kn_ref, vn_ref, kc_ref, vc_ref, o_ref, *, L, tkb, n_cache):
    qs = [_q_head(q_ref, h) for h in range(NH)]

    def visit(state, kget, vget, mblk, cached):
        bias = mblk.astype(F32)
        ks = [kget(h).astype(BF16) for h in range(NH)]
        vs = [vget(h).astype(BF16) for h in range(NH)]
        if cached:
            ss = [jnp.dot(qs[h], ks[h], preferred_element_type=F32) + bias for h in range(NH)]
        else:
            ss = [_dot_nt(qs[h], ks[h]) + bias for h in range(NH)]
        ms = [jnp.maximum(state[h][0], jnp.max(ss[h], axis=1, keepdims=True)) for h in range(NH)]
        ps = [jnp.exp2(ss[h] - ms[h]) for h in range(NH)]
        al = [jnp.exp2(state[h][0] - ms[h]) for h in range(NH)]
        if cached:
            pv = [_dot_nt(ps[h].astype(BF16), vs[h]) for h in range(NH)]
        else:
            pv = [jnp.dot(ps[h].astype(BF16), vs[h], preferred_element_type=F32) for h in range(NH)]
        return tuple((ms[h], al[h] * state[h][1] + jnp.sum(ps[h], axis=1, keepdims=True),
                      al[h] * state[h][2] + pv[h]) for h in range(NH))

    init = (jnp.full((L, 1), NEG_BIG, F32), jnp.zeros((L, 1), F32), jnp.zeros((L, HD), F32))

    def step(i, st):
        c0 = pl.multiple_of(i * tkb, tkb)
        return visit(st, lambda h: kc_ref[0, 0, h, :, pl.ds(c0, tkb)],
                     lambda h: vc_ref[0, 0, h, :, pl.ds(c0, tkb)], m_ref[:, pl.ds(c0, tkb)], True)
    state = lax.fori_loop(0, n_cache, step, (init,) * NH)

    pad = jnp.zeros((tkb - L, HD), F32)
    state = visit(state, lambda h: jnp.concatenate([kn_ref[:, h * HD:(h + 1) * HD], pad], axis=0),
                  lambda h: jnp.concatenate([vn_ref[:, h * HD:(h + 1) * HD], pad], axis=0),
                  m_ref[:, pl.ds(n_cache * tkb, tkb)], False)
    for h in range(NH):
        o_ref[:, h * HD:(h + 1) * HD] = state[h][2] / state[h][1]


def _dsa_attn_sample(qn, kn, z, mask, cache_k, cache_v, layer, row_off, nb, L):
    P = cache_k.shape[4]
    tkb = mask.shape[1] - P
    rb = row_off // L
    cspec = pl.BlockSpec((1, 1, NH, HD, P), lambda b: (layer, b, 0, 0, 0))
    return pl.pallas_call(
        functools.partial(_dsa_attn_sample_body, L=L, tkb=tkb, n_cache=P // tkb),
        grid=(nb,),
        in_specs=[pl.BlockSpec((L, NH * LANES), lambda b: (rb + b, 0)),
                  pl.BlockSpec((L, P + tkb), lambda b: (b, 0)),
                  pl.BlockSpec((L, BW), lambda b: (rb + b, 0)),
                  pl.BlockSpec((L, BW), lambda b: (rb + b, OFF_DV // BW)),
                  cspec, cspec],
        out_specs=pl.BlockSpec((L, BW), lambda b: (b, 0)),
        out_shape=jax.ShapeDtypeStruct((nb * L, BW), F32),
        compiler_params=_cparams(("parallel",)),
        name="dsa_attn_sample",
    )(qn, mask, kn, z, cache_k, cache_v)


def _gla_consts(C):
    nlev = int(math.log2(C))
    t = np.arange(C)
    cum = (t[:, None] >= t[None, :]).astype(np.float32)
    sel = np.zeros((nlev * C, C), np.float32)
    for l in range(nlev):
        m = 1 << l
        mid = (t // (2 * m)) * 2 * m + m
        sel[l * C + t, mid - 1] = 1.0
    return jnp.asarray(cum, BF16), jnp.asarray(sel, BF16)


def _gla_body(gq_ref, gk_ref, gv_ref, gr_ref, ga_ref, a2_ref, ab_ref, on_ref, cum_ref, sel_ref,
              s0_ref, o_ref, sout_ref, st_ref, *, C, n_chunks):
    ci = pl.program_id(1)
    nlev = int(math.log2(C))

    @pl.when(ci == 0)
    def _():
        st_ref[...] = s0_ref[0]

    x = jnp.dot(ga_ref[...].astype(BF16), a2_ref[...], preferred_element_type=F32) + ab_ref[...]
    g = _neg_softplus(-x) / GLA_GATE_NORM
    b = _split_dot_left(cum_ref[...], g)
    c_all = _split_dot_left(sel_ref[...], b)

    row = lax.broadcasted_iota(jnp.int32, (C, C), 0)
    col = lax.broadcasted_iota(jnp.int32, (C, C), 1)
    rowl = lax.broadcasted_iota(jnp.int32, (C, LANES), 0)

    hs = range(GLA_H)
    sls = [slice(h * LANES, (h + 1) * LANES) for h in hs]
    q = [gq_ref[:, sl] * (HD ** -0.5) for sl in sls]
    k = [gk_ref[:, sl] for sl in sls]
    v16 = [gv_ref[:, sl].astype(BF16) for sl in sls]
    bh = [b[:, sl] for sl in sls]
    st = [st_ref[h] for h in hs]
    o_inter = [_dot_nt((q[h] * jnp.exp(bh[h])).astype(BF16), st[h].astype(BF16)) for h in hs]
    b_last = [bh[h][C - 1:C, :] for h in hs]
    kv = [_dot_tn(v16[h], (k[h] * jnp.exp(b_last[h] - bh[h])).astype(BF16)) for h in hs]
    a = [jnp.where(row == col, jnp.sum(q[h] * k[h], axis=1, keepdims=True), 0.0) for h in hs]
    for l in range(nlev):
        m = 1 << l
        later = (rowl & m) != 0
        same_seg = (row // (2 * m)) == (col // (2 * m))
        ch = [c_all[l * C:(l + 1) * C, sl] for sl in sls]
        qe = [jnp.where(later, q[h] * jnp.exp(jnp.where(later, bh[h] - ch[h], 0.0)), 0.0) for h in hs]
        ke = [jnp.where(later, 0.0, k[h] * jnp.exp(jnp.where(later, 0.0, ch[h] - bh[h]))) for h in hs]
        al = [_dot_nt(qe[h].astype(BF16), ke[h].astype(BF16)) for h in hs]
        a = [a[h] + jnp.where(same_seg, al[h], 0.0) for h in hs]
    o = [jnp.dot(a[h].astype(BF16), v16[h], preferred_element_type=F32) + o_inter[h] for h in hs]
    for h in hs:
        st_ref[h] = st[h] * jnp.exp(b_last[h]) + kv[h]
        on = o[h] * lax.rsqrt(jnp.mean(o[h] * o[h], axis=-1, keepdims=True) + NORM_EPS) * on_ref[...]
        gr = gr_ref[:, sls[h]]
        o_ref[:, sls[h]] = on * (gr * jax.nn.sigmoid(gr))

    @pl.when(ci == n_chunks - 1)
    def _():
        sout_ref[0] = st_ref[...]


def _gla(z, a2p, abp, onorm, s0t, row_off, n_seq, L):
    C = min(CHUNK, L)
    n_chunks = L // C
    rb = row_off // C
    cum, sel = _gla_consts(C)
    blk = lambda off: pl.BlockSpec((C, BW), lambda s, c: (rb + s * n_chunks + c, off // BW))
    full = lambda a: pl.BlockSpec(a.shape, lambda s, c: (0,) * a.ndim)
    return pl.pallas_call(
        functools.partial(_gla_body, C=C, n_chunks=n_chunks),
        grid=(n_seq, n_chunks),
        in_specs=[blk(OFF_GQ), blk(OFF_GK), blk(OFF_GV), blk(OFF_GR),
                  pl.BlockSpec((C, LANES), lambda s, c: (rb + s * n_chunks + c, OFF_GA // LANES)),
                  full(a2p), full(abp), full(onorm), full(cum), full(sel),
                  pl.BlockSpec((1, GLA_H, GLA_DV, LANES), lambda s, c: (s, 0, 0, 0))],
        out_specs=[pl.BlockSpec((C, BW), lambda s, c: (s * n_chunks + c, 0)),
                   pl.BlockSpec((1, GLA_H, GLA_DV, LANES), lambda s, c: (s, 0, 0, 0))],
        out_shape=[jax.ShapeDtypeStruct((n_seq * L, BW), F32),
                   jax.ShapeDtypeStruct((n_seq, GLA_H, GLA_DV, LANES), F32)],
        scratch_shapes=[pltpu.VMEM((GLA_H, GLA_DV, LANES), F32)],
        compiler_params=_cparams(("parallel", "arbitrary")),
        name="gla",
    )(z, z, z, z, z, a2p, abp, onorm, cum, sel, s0t)


def _rw_prep_body(z_ref, head_ref, mu_ref, w0_ref, a0_ref, kkp_ref, kap_ref, rkp_ref,
                  w2_ref, a2_ref, g2_ref, seg_ref,
                  r_ref, w_ref, k_ref, v_ref, nkk_ref, kka_ref, gate_ref, bonus_ref, *, grp):
    zz = z_ref[...]
    tm, wd = zz.shape
    heads = head_ref[...]
    head_rows = jnp.concatenate([jnp.broadcast_to(heads[g:g + 1, :], (grp, wd))
                                 for g in range(tm // grp)], axis=0)
    row = lax.broadcasted_iota(jnp.int32, (tm, wd), 0)
    prev = jnp.where(row % grp == 0, head_rows, pltpu.roll(zz, 1, axis=0))
    zm = zz + mu_ref[...] * (prev - zz)
    rr = zm[:, 0:BW]
    rk = zm[:, BW:2 * BW]
    rv = zm[:, 2 * BW:3 * BW]
    lora = zm[:, 3 * BW:3 * BW + LANES]
    gl = zm[:, 3 * BW + LANES:3 * BW + 2 * LANES]
    xw = w0_ref[...] + jnp.dot(jnp.tanh(lora).astype(BF16), w2_ref[...], preferred_element_type=F32)
    wlog = _neg_softplus(-xw) - 0.5
    a = jax.nn.sigmoid(a0_ref[...] + jnp.dot(lora.astype(BF16), a2_ref[...], preferred_element_type=F32))
    gate = jnp.dot(jax.nn.sigmoid(gl).astype(BF16), g2_ref[...], preferred_element_type=F32)
    kkf = rk * kkp_ref[...]
    nrm = jnp.sqrt(_split_dot(kkf * kkf, seg_ref[...]))
    kk = kkf / jnp.maximum(nrm, 1e-12)
    kmod = rk * (1.0 + (a - 1.0) * kap_ref[...])
    coef = _split_dot(rr * kmod * rkp_ref[...], seg_ref[...])
    r_ref[...] = rr
    w_ref[...] = -jnp.exp(wlog)
    k_ref[...] = kmod
    v_ref[...] = rv
    nkk_ref[...] = -kk
    kka_ref[...] = kk * a
    gate_ref[...] = gate
    bonus_ref[...] = coef * rv


def _rw_prep(z, heads, grp, mu, w0, a0, kkp, kap, rkp, w2p, a2p, g2, seg):
    T = z.shape[0]
    tm = SUBLANES * grp
    assert T % tm == 0, (T, tm)
    W = 4 * BW
    par = lambda a: pl.BlockSpec(a.shape, lambda i: (0,) * a.ndim)
    out = pl.BlockSpec((tm, BW), lambda i: (i, 0))
    return pl.pallas_call(
        functools.partial(_rw_prep_body, grp=grp),
        grid=(T // tm,),
        in_specs=[pl.BlockSpec((tm, W), lambda i: (i, 0)), pl.BlockSpec((SUBLANES, W), lambda i: (i, 0)),
                  par(mu), par(w0), par(a0), par(kkp), par(kap), par(rkp),
                  par(w2p), par(a2p), par(g2), par(seg)],
        out_specs=[out] * 8,
        out_shape=[jax.ShapeDtypeStruct((T, BW), F32)] * 8,
        compiler_params=_cparams(("parallel",)),
        name="rwkv_prep",
    )(z, heads, mu, w0, a0, kkp, kap, rkp, w2p, a2p, g2, seg)


def _mm3(a, b):
    ah = a.astype(BF16)
    al = (a - ah.astype(F32)).astype(BF16)
    bh = b.astype(BF16)
    bl = (b - bh.astype(F32)).astype(BF16)
    d = lambda x, y: jnp.dot(x, y, preferred_element_type=F32)
    return d(ah, bh) + (d(ah, bl) + d(al, bh))


def _rw_chunk_body(r_ref, lw_ref, k_ref, v_ref, nkk_ref, kka_ref, s0_ref, o_ref, sout_ref, s_ref,
                   *, TT, L, n_tiles):
    ti = pl.program_id(0)
    npair = NH // 2
    nblk = TT // RW_BLK
    row = lax.broadcasted_iota(jnp.int32, (TT, TT), 0)
    col = lax.broadcasted_iota(jnp.int32, (TT, TT), 1)
    same = (row // RW_BLK) == (col // RW_BLK)
    strict = same & (col < row)
    incl = same & (col <= row)
    tri = jnp.where(incl, 1.0, 0.0).astype(BF16)
    last = jnp.where(same & (col % RW_BLK == RW_BLK - 1), 1.0, 0.0).astype(BF16)
    eye = jnp.where(row == col, 1.0, 0.0)
    pair_diag = (row < HD) == (col < HD)
    lo_half = lax.broadcasted_iota(jnp.int32, (TT, LANES), 1) < HD
    lo_half_b = lax.broadcasted_iota(jnp.int32, (RW_BLK, LANES), 1) < HD

    r, lw, k, v = r_ref[...], lw_ref[...], k_ref[...], v_ref[...]
    nk, ka = nkk_ref[...], kka_ref[...]
    lc = _split_dot_left(tri, lw)
    le = _split_dot_left(last, lc)
    nkt = nk * jnp.exp(lc - lw)
    rt = r * jnp.exp(lc)
    inv = jnp.exp(-lc)
    kah = ka * inv
    kh = k * inv
    rem = jnp.exp(le - lc)
    kaw = (ka * rem).astype(BF16)
    kw = (k * rem).astype(BF16)
    wb = jnp.exp(le)

    sls = [slice(p * LANES, (p + 1) * LANES) for p in range(npair)]
    v16 = [v[:, sl].astype(BF16) for sl in sls]
    heads = [(p, e) for p in range(npair) for e in range(2)]
    grams = []
    for p, e in heads:
        hm = lo_half if e == 0 else jnp.logical_not(lo_half)
        lhs = jnp.concatenate([jnp.where(hm, nkt[:, sls[p]], 0.0), jnp.where(hm, rt[:, sls[p]], 0.0)],
                              axis=0).astype(BF16)
        rhs = jnp.concatenate([kah[:, sls[p]], kh[:, sls[p]]], axis=0).astype(BF16)
        grams.append(_dot_nt(lhs, rhs))
    n1 = [jnp.where(strict, g[:TT, :TT], 0.0) for g in grams]
    bt = [jnp.where(strict, g[:TT, TT:], 0.0).astype(BF16) for g in grams]
    cmat = [jnp.where(incl, g[TT:, :TT], 0.0).astype(BF16) for g in grams]
    et = [jnp.where(incl, g[TT:, TT:], 0.0).astype(BF16) for g in grams]
    n2 = [_mm3(a, a) for a in n1]
    n4 = [_mm3(a, a) for a in n2]
    n8 = [_mm3(a, a) for a in n4]
    tinv = [eye + a for a in n1]
    for power in (n2, n4, n8):
        tinv = [t + _mm3(t, a) for t, a in zip(tinv, power)]
    vb_h = [jnp.dot(b_, v16[p], preferred_element_type=F32) for b_, (p, e) in zip(bt, heads)]
    ev_h = [jnp.dot(e_, v16[p], preferred_element_type=F32) for e_, (p, e) in zip(et, heads)]
    vb = [jnp.where(lo_half, vb_h[2 * p], vb_h[2 * p + 1]) for p in range(npair)]
    ev = [jnp.where(lo_half, ev_h[2 * p], ev_h[2 * p + 1]) for p in range(npair)]

    if L >= TT:
        @pl.when(ti == 0)
        def _():
            s_ref[...] = s0_ref[0]
        state = {(0, p): s_ref[p] for p in range(npair)}
        rounds = [[j] for j in range(nblk)]
    else:
        bps = L // RW_BLK
        state = {(q, p): s0_ref[q, p] for q in range(TT // L) for p in range(npair)}
        rounds = [[q * bps + i for q in range(TT // L)] for i in range(bps)]
    sa_rows = [[None] * nblk for _ in range(npair)]
    os_rows = [[None] * nblk for _ in range(npair)]
    for blocks in rounds:
        items = [(j, p) for j in blocks for p in range(npair)]
        seq = lambda j: (j * RW_BLK) // L if L < TT else 0
        xs = {}
        for j, p in items:
            bs = slice(j * RW_BLK, (j + 1) * RW_BLK)
            s = state[(seq(j), p)]
            s_hi = s.astype(BF16)
            s_lo = (s - s_hi.astype(F32)).astype(BF16)
            lhs = jnp.concatenate([nkt[bs, sls[p]], rt[bs, sls[p]]], axis=0).astype(BF16)
            xs[(j, p)] = (jnp.dot(lhs, s_hi, preferred_element_type=F32)
                          + jnp.dot(lhs, s_lo, preferred_element_type=F32))
        sa = {}
        for j, p in items:
            b0 = j * RW_BLK
            bs = slice(b0, b0 + RW_BLK)
            parts = [xs[(j, p)][:RW_BLK] + vb[p][bs]]
            if b0:
                parts.insert(0, jnp.zeros((b0, LANES), F32))
            if TT - b0 - RW_BLK:
                parts.append(jnp.zeros((TT - b0 - RW_BLK, LANES), F32))
            xfull = jnp.concatenate(parts, axis=0)
            sa[(j, p)] = jnp.where(lo_half_b, _mm3(tinv[2 * p][bs], xfull),
                                   _mm3(tinv[2 * p + 1][bs], xfull))
        for j, p in items:
            b0 = j * RW_BLK
            bs = slice(b0, b0 + RW_BLK)
            u = (_dot_tn(kaw[bs, sls[p]], sa[(j, p)].astype(BF16)) + _dot_tn(kw[bs, sls[p]], v16[p][bs]))
            wcol = jnp.broadcast_to(wb[b0:b0 + 1, sls[p]], (LANES, LANES)).T
            state[(seq(j), p)] = wcol * state[(seq(j), p)] + jnp.where(pair_diag, u, 0.0)
            sa_rows[p][j] = sa[(j, p)]
            os_rows[p][j] = xs[(j, p)][RW_BLK:]
    if L < TT:
        for (q, p), s in state.items():
            sout_ref[q, p] = s

    for p in range(npair):
        sa_t = jnp.concatenate(sa_rows[p], axis=0).astype(BF16)
        o_sa = jnp.where(lo_half, jnp.dot(cmat[2 * p], sa_t, preferred_element_type=F32),
                         jnp.dot(cmat[2 * p + 1], sa_t, preferred_element_type=F32))
        o_ref[:, sls[p]] = jnp.concatenate(os_rows[p], axis=0) + o_sa + ev[p]
    if L >= TT:
        for p in range(npair):
            s_ref[p] = state[(0, p)]

        @pl.when(ti == n_tiles - 1)
        def _():
            for p in range(npair):
                sout_ref[0, p] = state[(0, p)]


def _rw_chunk(r, lw, k, v, nkk, kka, s0, row_off, n_seq, L):
    TT = LANES
    assert L % RW_BLK == 0 and (L % TT == 0 or TT % L == 0)
    n_tok = n_seq * L
    n_tiles = n_tok // TT
    rb = row_off // TT
    ns = max(1, TT // L)
    tok = pl.BlockSpec((TT, BW), lambda i: (rb + i, 0))
    if ns == 1:
        sspec = pl.BlockSpec((1, NH // 2, LANES, LANES), lambda i: (0, 0, 0, 0))
    else:
        sspec = pl.BlockSpec((ns, NH // 2, LANES, LANES), lambda i: (i, 0, 0, 0))
    return pl.pallas_call(
        functools.partial(_rw_chunk_body, TT=TT, L=L, n_tiles=n_tiles),
        grid=(n_tiles,),
        in_specs=[tok] * 6 + [sspec],
        out_specs=[pl.BlockSpec((TT, BW), lambda i: (i, 0)), sspec],
        out_shape=[jax.ShapeDtypeStruct((n_tok, BW), F32),
                   jax.ShapeDtypeStruct(s0.shape, F32)],
        scratch_shapes=[pltpu.VMEM((NH // 2, LANES, LANES), F32)],
        compiler_params=_cparams(("arbitrary",)),
        name="rwkv_chunk",
    )(r, lw, k, v, nkk, kka, s0)


def _rw_post_body(o_ref, bonus_ref, gate_ref, lnw_ref, lnb_ref, seg_ref, out_ref):
    o = o_ref[...]
    mu = _split_dot(o, seg_ref[...]) * (1.0 / HD)
    d = o - mu
    var = _split_dot(d * d, seg_ref[...]) * (1.0 / HD)
    on = d * lax.rsqrt(var + RW_LN_EPS) * lnw_ref[...] + lnb_ref[...]
    out_ref[...] = (on + bonus_ref[...]) * gate_ref[...]


def _rw_post(o, bonus, gate, lnw, lnb, seg):
    T = o.shape[0]
    tm = _pick(T, (512, 256, 128))
    blk = pl.BlockSpec((tm, BW), lambda i: (i, 0))
    par = lambda a: pl.BlockSpec(a.shape, lambda i: (0,) * a.ndim)
    return pl.pallas_call(
        _rw_post_body,
        grid=(T // tm,),
        in_specs=[blk, blk, blk, par(lnw), par(lnb), par(seg)],
        out_specs=blk,
        out_shape=jax.ShapeDtypeStruct((T, BW), F32),
        compiler_params=_cparams(("parallel",)),
        name="rwkv_post",
    )(o, bonus, gate, lnw, lnb, seg)


def _heads_alt(w):
    lead = w.shape[:-1]
    w4 = w.reshape(*lead, NH // 2, 2, HD)
    zero = jnp.zeros((*lead, NH // 2, HD), w.dtype)
    ev = jnp.concatenate([w4[..., 0, :], zero], -1)
    od = jnp.concatenate([zero, w4[..., 1, :]], -1)
    return jnp.stack([ev, od], -2).reshape(*lead, NH * LANES)


def _heads_lo(w, nh):
    lead = w.shape[:-1]
    w3 = w.reshape(*lead, nh, HD)
    return jnp.concatenate([w3, jnp.zeros_like(w3)], -1).reshape(*lead, nh * LANES)


def _padc(w, n):
    return jnp.pad(w, [(0, 0)] * (w.ndim - 1) + [(0, n - w.shape[-1])])


def _rw_reorder(a):
    o = 0
    parts = {}
    for name, wd in (("rr", BW), ("wl", RW_LORA), ("rk", BW), ("rv", BW), ("al", RW_LORA), ("gl", RW_GATE_LORA)):
        parts[name] = a[..., o:o + wd]
        o += wd
    return jnp.concatenate([parts[n] for n in ("rr", "rk", "rv", "wl", "al", "gl")], axis=-1)


def _rw_unorder(a):
    rr, rk, rv = a[..., 0:BW], a[..., BW:2 * BW], a[..., 2 * BW:3 * BW]
    wl = a[..., 3 * BW:3 * BW + RW_LORA]
    al = a[..., 3 * BW + RW_LORA:3 * BW + 2 * RW_LORA]
    gl = a[..., 3 * BW + 2 * RW_LORA:RW_COLS]
    return jnp.concatenate([rr, wl, rk, rv, al, gl], axis=-1)


def _relayout_w_in(w, d_model):
    w = w.astype(BF16)
    o = 0

    def take(n):
        nonlocal o
        s = w[..., o:o + n]
        o += n
        return s
    sq, sk, sv = take(BW), take(BW), take(BW)
    dq, dk, dv, qi, ki, wi = take(BW), take(BW), take(BW), take(BW), take(HD), take(NH)
    gq, gk = take(GLA_H * HD), take(GLA_H * HD)
    gv, gr, ga = take(BW), take(BW), take(GLA_LOWRANK)
    rw = take(RW_COLS)
    gate = take(4 * d_model)
    cols = [_padc(_rw_reorder(rw), 4 * BW), _heads_alt(sq), sk, sv, _heads_alt(dq), dk, dv,
            _heads_lo(qi, NH), _heads_lo(gq, GLA_H), _heads_lo(gk, GLA_H), gv, gr,
            _padc(ki, LANES), _padc(wi, LANES), _padc(ga, 2 * LANES), gate]
    return jnp.concatenate(cols, axis=-1)


def _relayout_ffn(w_up, w_down, tf):
    f = w_up.shape[-1] // 2
    fp = -(-f // tf) * tf
    wup = jnp.concatenate([_padc(w_up[..., :f].astype(BF16), fp), _padc(w_up[..., f:].astype(BF16), fp)],
                          axis=-1)
    wdn = jnp.pad(w_down.astype(BF16), ((0, 0), (0, fp - f), (0, 0)))
    return wup, wdn


def _gain_alt(g):
    return jnp.tile(g, 2 * NH)[None, :]


def kernel(x_prompt, x_sample, cache_sb_k, cache_sb_v, cache_dsa_k, cache_dsa_v, cache_dsa_kidx, state_gla, state_rwkv, state_rwkv_shift, ffn1_norm, ffn1_up, ffn1_down, mix_norm, w_in, sb_qnorm, sb_knorm, dsa_qnorm, dsa_knorm, gla_a2, gla_ab, gla_onorm, rwkv_mu, rwkv_w0, rwkv_w2, rwkv_a0, rwkv_a2, rwkv_g2, rwkv_kk, rwkv_ka, rwkv_rk, rwkv_lnw, rwkv_lnb, w_branch, w_out, ffn2_norm, ffn2_up, ffn2_down):
    depth = w_in.shape[0]
    bp, lp, d_model = x_prompt.shape
    nb, ls, _ = x_sample.shape
    assert bp == 1, "prompt group is one sequence"
    tp = bp * lp
    ts = nb * ls
    past = cache_sb_k.shape[2]
    to_pos_minor = lambda c: jnp.transpose(c, (0, 1, 3, 4, 2))
    ck_sb, cv_sb = to_pos_minor(cache_sb_k), to_pos_minor(cache_sb_v)
    ck_dsa, cv_dsa = to_pos_minor(cache_dsa_k), to_pos_minor(cache_dsa_v)
    cki = jnp.transpose(cache_dsa_kidx, (0, 1, 3, 2))
    tf = 512

    x = jnp.concatenate([x_prompt.reshape(tp, d_model), x_sample.reshape(ts, d_model)], axis=0)

    bd = jnp.asarray(np.kron(np.eye(2 * NH), np.full((HD, HD), 1.0 / HD)), BF16)
    seg = jnp.asarray(np.kron(np.eye(NH), np.ones((HD, HD))), BF16)
    topk_p = min(TOPK_MAX, lp // 4)
    topk_s = min(TOPK_MAX, (past + ls) // 4)

    new_p = [[] for _ in range(8)]
    new_s = [[] for _ in range(8)]
    wup1, wdn1 = _relayout_ffn(ffn1_up, ffn1_down, tf)
    wup2, wdn2 = _relayout_ffn(ffn2_up, ffn2_down, tf)
    win = _relayout_w_in(w_in, d_model)
    wbr = w_branch.astype(BF16)
    wout = w_out.astype(BF16)
    for i in range(depth):
        x = _ffn(x, ffn1_norm[:, None], wup1, wdn1, tf, i)
        z = _proj(x, mix_norm[:, None], win, i)

        sqn, skn, dqn, dkn = _qknorm(z, _gain_alt(sb_qnorm[i]), jnp.tile(sb_knorm[i], NH)[None],
                                     _gain_alt(dsa_qnorm[i]), jnp.tile(dsa_knorm[i], NH)[None], bd)
        o_sb = (_sb_prompt(sqn, skn, z, tp), _sb_sample(sqn, skn, z, ck_sb, cv_sb, i, tp, nb, ls))
        mask_p = _dsa_sel_prompt(z, tp, topk_p)
        mask_s = _dsa_sel_sample(z, cki, i, tp, nb, ls, topk_s)
        o_dsa = (_dsa_attn_prompt(dqn, dkn, z, mask_p, tp),
                 _dsa_attn_sample(dqn, dkn, z, mask_s, ck_dsa, cv_dsa, i, tp, nb, ls))

        a2p = jnp.pad(_heads_lo(gla_a2[i], GLA_H), ((0, LANES - GLA_LOWRANK), (0, 0))).astype(BF16)
        abp = _heads_lo(gla_ab[i][None], GLA_H)
        onorm = gla_onorm[i][None]

        def gla_state_in(s):
            return jnp.pad(jnp.swapaxes(s, 2, 3), ((0, 0), (0, 0), (0, 0), (0, LANES - HD)))

        def gla_state_out(s):
            return jnp.swapaxes(s[..., :HD], 2, 3)
        og_p, sg_p = _gla(z, a2p, abp, onorm, jnp.zeros((bp, GLA_H, GLA_DV, LANES), F32), 0, bp, lp)
        og_s, sg_s = _gla(z, a2p, abp, onorm, gla_state_in(state_gla[i]), tp, nb, ls)
        o_gla = (og_p, og_s)

        zrw = z[:, :RW_COLS]
        shift_s = _rw_reorder(state_rwkv_shift[i])
        heads = jnp.concatenate([jnp.zeros((1, 4 * BW), F32), z[ls - 1:tp - 1:ls, :4 * BW],
                                 _padc(shift_s.reshape(nb, RW_COLS), 4 * BW)], axis=0)
        mu = _padc(_rw_reorder(rwkv_mu[i])[None], 4 * BW)
        w2p = jnp.concatenate([rwkv_w2[i], jnp.zeros_like(rwkv_a2[i])], axis=0).astype(BF16)
        a2q = jnp.concatenate([jnp.zeros_like(rwkv_w2[i]), rwkv_a2[i]], axis=0).astype(BF16)
        r_, w_, k_, v_, nkk_, kka_, gate_, bonus_ = _rw_prep(
            z, heads, ls, mu, rwkv_w0[i][None], rwkv_a0[i][None], rwkv_kk[i][None], rwkv_ka[i][None],
            rwkv_rk[i][None], w2p, a2q, rwkv_g2[i].astype(BF16), seg)

        def rw_state_in(s):
            n = s.shape[0]
            st = jnp.swapaxes(s, 2, 3).reshape(n, NH // 2, 2, HD, HD)
            zero = jnp.zeros((n, NH // 2, HD, HD), s.dtype)
            return jnp.concatenate([jnp.concatenate([st[:, :, 0], zero], axis=-1),
                                    jnp.concatenate([zero, st[:, :, 1]], axis=-1)], axis=-2)

        def rw_state_out(s):
            n = s.shape[0]
            st = jnp.stack([s[:, :, :HD, :HD], s[:, :, HD:, HD:]], axis=2)
            return jnp.swapaxes(st, 3, 4).reshape(n, NH, HD, HD)
        o_rp, sr_p = _rw_chunk(r_, w_, k_, v_, nkk_, kka_, jnp.zeros((bp, NH // 2, LANES, LANES), F32), 0, bp, lp)
        o_rs, sr_s = _rw_chunk(r_, w_, k_, v_, nkk_, kka_, rw_state_in(state_rwkv[i]), tp, nb, ls)
        o_r = jnp.concatenate([o_rp, o_rs], axis=0)
        o_rw = _rw_post(o_r, bonus_, gate_, rwkv_lnw[i][None], rwkv_lnb[i][None], seg)

        x = _merge(x, z, (o_sb, o_dsa, o_gla, o_rw), wbr, wout, i, tp)
        x = _ffn(x, ffn2_norm[:, None], wup2, wdn2, tf, i)

        sv = z[:, OFF_SV:OFF_SV + BW]
        dv = z[:, OFF_DV:OFF_DV + BW]
        ki = z[:, OFF_KI:OFF_KI + HD]
        for dst, (lo, n, l) in ((new_p, (0, bp, lp)), (new_s, (tp, nb, ls))):
            hi = lo + n * l
            dst[0].append(skn[lo:hi].reshape(n, l, NH, HD))
            dst[1].append(sv[lo:hi].reshape(n, l, NH, HD))
            dst[2].append(dkn[lo:hi].reshape(n, l, NH, HD))
            dst[3].append(dv[lo:hi].reshape(n, l, NH, HD))
            dst[4].append(ki[lo:hi].reshape(n, l, HD))
            dst[7].append(_rw_unorder(zrw[lo:hi].reshape(n, l, RW_COLS)[:, -1:]))
        new_p[5].append(gla_state_out(sg_p))
        new_s[5].append(gla_state_out(sg_s))
        new_p[6].append(rw_state_out(sr_p))
        new_s[6].append(rw_state_out(sr_s))

    outs_p = [jnp.stack(l, axis=0) for l in new_p]
    outs_s = [jnp.stack(l, axis=0) for l in new_s]
    y_p = x[:tp].reshape(bp, lp, d_model)
    y_s = x[tp:].reshape(nb, ls, d_model)
    return (y_p, y_s, *outs_p, *outs_s)
```

```python
import functools
import math

import jax
import jax.numpy as jnp
import numpy as np
from jax import lax
from jax.experimental import pallas as pl
from jax.experimental.pallas import tpu as pltpu

F32 = jnp.float32
BF16 = jnp.bfloat16

LANES = 128
SUBLANES = 8
HD = 64
NH = 8
BW = NH * HD
GLA_H = 4
GLA_DV = 128
GLA_LOWRANK = 16
GLA_GATE_NORM = 16.0
CHUNK = 64
TOPK_MAX = 256
RW_LORA = 64
RW_GATE_LORA = 128
RW_BLK = 16
RW_COLS = 3 * BW + 2 * RW_LORA + RW_GATE_LORA
NORM_EPS = 1e-6
RW_LN_EPS = 64e-5
NEG_BIG = -1e30
LOG2E = math.log2(math.e)
SB_UNDERFLOW = -104.0
VMEM_LIMIT = 56 * 1024 * 1024

OFF_RW = 0
OFF_SQ = 2048
OFF_SK = 3072
OFF_SV = 3584
OFF_DQ = 4096
OFF_DK = 5120
OFF_DV = 5632
OFF_QI = 6144
OFF_GQ = 7168
OFF_GK = 7680
OFF_GV = 8192
OFF_GR = 8704
OFF_KI = 9216
OFF_WI = 9344
OFF_GA = 9472
OFF_GATE = 9728


def _cparams(sem):
    return pltpu.CompilerParams(dimension_semantics=sem, vmem_limit_bytes=VMEM_LIMIT)


def _pick(n, prefs):
    for p in prefs:
        if n % p == 0:
            return p
    raise ValueError(f"no tile for {n} in {prefs}")


def _split_dot(a, b):
    hi = a.astype(BF16)
    lo = (a - hi.astype(F32)).astype(BF16)
    return (jnp.dot(hi, b, preferred_element_type=F32)
            + jnp.dot(lo, b, preferred_element_type=F32))


def _split_dot_left(a, b):
    hi = b.astype(BF16)
    lo = (b - hi.astype(F32)).astype(BF16)
    return (jnp.dot(a, hi, preferred_element_type=F32)
            + jnp.dot(a, lo, preferred_element_type=F32))


def _dot_nt(a, b):
    return lax.dot_general(a, b, (((1,), (1,)), ((), ())), preferred_element_type=F32)


def _dot_tn(a, b):
    return lax.dot_general(a, b, (((0,), (0,)), ((), ())), preferred_element_type=F32)


def _neg_softplus(z):
    return -(jnp.maximum(z, 0.0) + jnp.log(1.0 + jnp.exp(-jnp.abs(z))))


def _ffn_body(x_ref, g_ref, wg_ref, wu_ref, wdn_ref, o_ref, xn_ref):
    @pl.when(pl.program_id(1) == 0)
    def _():
        x = x_ref[...]
        ms = jnp.mean(x * x, axis=-1, keepdims=True)
        xn_ref[...] = (x * lax.rsqrt(ms + NORM_EPS) * g_ref[0]).astype(BF16)
        o_ref[...] = x

    xn = xn_ref[...]
    gate = jnp.dot(xn, wg_ref[0], preferred_element_type=F32)
    up = jnp.dot(xn, wu_ref[0], preferred_element_type=F32)
    act = (gate * jax.nn.sigmoid(gate) * up).astype(BF16)
    o_ref[...] += 0.5 * jnp.dot(act, wdn_ref[0], preferred_element_type=F32)


def _ffn(x, g, wup, wdn, tf, layer):
    T, D = x.shape
    nf = wdn.shape[1] // tf
    tm = _pick(T, (512, 256, 128))
    return pl.pallas_call(
        _ffn_body,
        grid=(T // tm, nf),
        in_specs=[pl.BlockSpec((tm, D), lambda i, j: (i, 0)),
                  pl.BlockSpec((1, 1, D), lambda i, j: (layer, 0, 0)),
                  pl.BlockSpec((1, D, tf), lambda i, j: (layer, 0, j)),
                  pl.BlockSpec((1, D, tf), lambda i, j: (layer, 0, nf + j)),
                  pl.BlockSpec((1, tf, D), lambda i, j: (layer, j, 0))],
        out_specs=pl.BlockSpec((tm, D), lambda i, j: (i, 0)),
        out_shape=jax.ShapeDtypeStruct((T, D), F32),
        scratch_shapes=[pltpu.VMEM((tm, D), BF16)],
        compiler_params=_cparams(("parallel", "arbitrary")),
        name="ffn",
    )(x, g, wup, wup, wdn)


def _proj_body(x_ref, g_ref, w_ref, o_ref, xn_ref):
    @pl.when(pl.program_id(1) == 0)
    def _():
        x = x_ref[...]
        ms = jnp.mean(x * x, axis=-1, keepdims=True)
        xn_ref[...] = (x * lax.rsqrt(ms + NORM_EPS) * g_ref[0]).astype(BF16)

    o_ref[...] = jnp.dot(xn_ref[...], w_ref[0], preferred_element_type=F32)


def _proj(x, g, w, layer):
    T, D = x.shape
    n = w.shape[2]
    tm = _pick(T, (1024, 512, 256, 128))
    tn = _pick(n, (1280, 512))
    return pl.pallas_call(
        _proj_body,
        grid=(T // tm, n // tn),
        in_specs=[pl.BlockSpec((tm, D), lambda i, j: (i, 0)),
                  pl.BlockSpec((1, 1, D), lambda i, j: (layer, 0, 0)),
                  pl.BlockSpec((1, D, tn), lambda i, j: (layer, 0, j))],
        out_specs=pl.BlockSpec((tm, tn), lambda i, j: (i, j)),
        out_shape=jax.ShapeDtypeStruct((T, n), F32),
        scratch_shapes=[pltpu.VMEM((tm, D), BF16)],
        compiler_params=_cparams(("parallel", "arbitrary")),
        name="in_proj",
    )(x, g, w)


def _merge_body(*refs, split, n_prompt_tiles):
    refs = list(refs)
    on_prompt = pl.program_id(0) < n_prompt_tiles
    branches = []
    for two in split:
        if two:
            bp, bs = refs.pop(0), refs.pop(0)
            branches.append(jnp.where(on_prompt, bp[...], bs[...]))
        else:
            branches.append(refs.pop(0)[...])
    g_refs, (wb_ref, wo_ref, x_ref, o_ref) = refs[:4], refs[4:]

    @pl.when(pl.program_id(1) == 0)
    def _():
        o_ref[...] = x_ref[...]

    acc = None
    for n in range(4):
        p = jnp.dot(branches[n].astype(BF16), wb_ref[0, n], preferred_element_type=F32)
        t = jax.nn.sigmoid(g_refs[n][...]) * p
        acc = t if acc is None else acc + t
    o_ref[...] += jnp.dot(acc.astype(BF16), wo_ref[0], preferred_element_type=F32)


def _merge(x, z, branches, wb, wo, layer, n_prompt):
    T, D = x.shape
    tm = _pick(math.gcd(T, n_prompt), (512, 256, 128))
    npt = n_prompt // tm
    tc = 512
    nc = D // tc
    gate_blk = OFF_GATE // tc
    split = tuple(isinstance(b, tuple) for b in branches)
    b_args, b_specs = [], []
    for b in branches:
        if isinstance(b, tuple):
            b_args += list(b)
            b_specs += [pl.BlockSpec((tm, BW), lambda i, c: (jnp.minimum(i, npt - 1), 0)),
                        pl.BlockSpec((tm, BW), lambda i, c: (jnp.maximum(i - npt, 0), 0))]
        else:
            b_args.append(b)
            b_specs.append(pl.BlockSpec((tm, BW), lambda i, c: (i, 0)))
    gspecs = [pl.BlockSpec((tm, tc), functools.partial(lambda i, c, n: (i, gate_blk + n * nc + c), n=n))
              for n in range(4)]
    return pl.pallas_call(
        functools.partial(_merge_body, split=split, n_prompt_tiles=npt),
        grid=(T // tm, nc),
        in_specs=b_specs + gspecs + [
            pl.BlockSpec((1, 4, BW, tc), lambda i, c: (layer, 0, 0, c)),
            pl.BlockSpec((1, tc, D), lambda i, c: (layer, c, 0)),
            pl.BlockSpec((tm, D), lambda i, c: (i, 0))],
        out_specs=pl.BlockSpec((tm, D), lambda i, c: (i, 0)),
        out_shape=jax.ShapeDtypeStruct((T, D), F32),
        compiler_params=_cparams(("parallel", "arbitrary")),
        name="merge",
    )(*b_args, z, z, z, z, wb, wo, x)


def _qknorm_body(sq_ref, sk_ref, dq_ref, dk_ref, gsq_ref, gsk_ref, gdq_ref, gdk_ref, bd_ref,
                 osq_ref, osk_ref, odq_ref, odk_ref):
    def norm(x, g, n):
        ms = _split_dot(x * x, bd_ref[:n, :n])
        return x * lax.rsqrt(ms + NORM_EPS) * g

    scale = HD ** -0.5
    osq_ref[...] = (norm(sq_ref[...], gsq_ref[...], 2 * BW) * scale).astype(BF16)
    osk_ref[...] = norm(sk_ref[...], gsk_ref[...], BW)
    odq_ref[...] = (norm(dq_ref[...], gdq_ref[...], 2 * BW) * (scale * LOG2E)).astype(BF16)
    odk_ref[...] = norm(dk_ref[...], gdk_ref[...], BW)


def _qknorm(z, gsq, gsk, gdq, gdk, bd):
    T = z.shape[0]
    tm = _pick(T, (512, 256, 128))
    wq, wk = 2 * BW, BW
    row = lambda w, off: pl.BlockSpec((tm, w), lambda i: (i, off // w))
    par = lambda w: pl.BlockSpec((1, w), lambda i: (0, 0))
    out = lambda w: pl.BlockSpec((tm, w), lambda i: (i, 0))
    return pl.pallas_call(
        _qknorm_body,
        grid=(T // tm,),
        in_specs=[row(wq, OFF_SQ), row(wk, OFF_SK), row(wq, OFF_DQ), row(wk, OFF_DK),
                  par(wq), par(wk), par(wq), par(wk),
                  pl.BlockSpec((wq, wq), lambda i: (0, 0))],
        out_specs=[out(wq), out(wk), out(wq), out(wk)],
        out_shape=[jax.ShapeDtypeStruct((T, wq), BF16), jax.ShapeDtypeStruct((T, wk), F32),
                   jax.ShapeDtypeStruct((T, wq), BF16), jax.ShapeDtypeStruct((T, wk), F32)],
        compiler_params=_cparams(("parallel",)),
        name="qk_norm",
    )(z, z, z, z, gsq, gsk, gdq, gdk, bd)


def _sb_body(q_ref, kn_ref, vn_ref, o_ref, *, tq, npair):
    qb = pl.program_id(1)
    hs = range(2 * npair)
    qs = [q_ref[:, e * LANES:(e + 1) * LANES] for e in hs]

    def tri(n):
        r = lax.broadcasted_iota(jnp.int32, (n, n), 0)
        c = lax.broadcasted_iota(jnp.int32, (n, n), 1)
        return (r > c).astype(BF16)

    def visit(state, r0, u, valid):
        kb = [kn_ref[pl.ds(r0, tq), j * LANES:(j + 1) * LANES].astype(BF16) for j in range(npair)]
        vb = [vn_ref[pl.ds(r0, tq), j * LANES:(j + 1) * LANES].astype(BF16) for j in range(npair)]
        zs = [_dot_nt(qs[e], kb[e // 2]) for e in hs]
        lgs = [_neg_softplus(z) for z in zs]
        if valid is not None:
            lgs = [jnp.where(valid, lg, 0.0) for lg in lgs]
        inners = [_split_dot(lg, u) for lg in lgs]
        ws = [jnp.exp(zs[e] + lgs[e] + inners[e] + state[e][0]) for e in hs]
        if valid is not None:
            ws = [jnp.where(valid, w, 0.0) for w in ws]
        pv = [jnp.dot(ws[e].astype(BF16), vb[e // 2], preferred_element_type=F32) for e in hs]
        return tuple((state[e][0] + inners[e][:, :1] + lgs[e][:, :1], state[e][1] + pv[e]) for e in hs)

    zero = (jnp.zeros((tq, 1), F32), jnp.zeros((tq, LANES), F32))

    r = lax.broadcasted_iota(jnp.int32, (tq, tq), 0)
    c = lax.broadcasted_iota(jnp.int32, (tq, tq), 1)
    u_new = tri(tq)
    state = visit((zero,) * (2 * npair), pl.multiple_of(qb * tq, tq), u_new, c < r)

    def live(c):
        i, st = c
        top = functools.reduce(jnp.maximum, [s[0] for s in st])
        return (i < qb) & (jnp.max(top) > SB_UNDERFLOW)

    def step(c):
        i, st = c
        return i + 1, visit(st, pl.multiple_of((qb - 1 - i) * tq, tq), u_new, None)
    _, state = lax.while_loop(live, step, (jnp.int32(0), state))

    lane = lax.broadcasted_iota(jnp.int32, (tq, LANES), 1)
    for j in range(npair):
        o_ref[:, j * LANES:(j + 1) * LANES] = jnp.where(lane < HD, state[2 * j][1], state[2 * j + 1][1])


def _sb_prompt(qn, kn, z, n_rows):
    tq = 256
    npair = 2
    wp = npair * LANES
    vblk = OFF_SV // wp
    return pl.pallas_call(
        functools.partial(_sb_body, tq=tq, npair=npair),
        grid=(NH // (2 * npair), n_rows // tq),
        in_specs=[pl.BlockSpec((tq, 2 * wp), lambda p, i: (i, p)),
                  pl.BlockSpec((n_rows, wp), lambda p, i: (0, p)),
                  pl.BlockSpec((n_rows, wp), lambda p, i: (0, vblk + p))],
        out_specs=pl.BlockSpec((tq, wp), lambda p, i: (i, p)),
        out_shape=jax.ShapeDtypeStruct((n_rows, BW), F32),
        compiler_params=_cparams(("parallel", "arbitrary")),
        name="sb_prompt",
    )(qn, kn, z)


def _q_head(q_ref, h):
    lo = h * LANES + HD * (h % 2)
    return q_ref[:, lo:lo + HD]


def _sb_sample_body(q_ref, kn_ref, vn_ref, kc_ref, vc_ref, o_ref, *, L, n_cache, tkc):
    qs = [_q_head(q_ref, h) for h in range(NH)]

    def tri(n):
        r = lax.broadcasted_iota(jnp.int32, (n, n), 0)
        c = lax.broadcasted_iota(jnp.int32, (n, n), 1)
        return (r > c).astype(BF16)

    def visit(state, kget, vget, u, valid, cached):
        ks = [kget(h).astype(BF16) for h in range(NH)]
        vs = [vget(h).astype(BF16) for h in range(NH)]
        if cached:
            zs = [jnp.dot(qs[h], ks[h], preferred_element_type=F32) for h in range(NH)]
        else:
            zs = [_dot_nt(qs[h], ks[h]) for h in range(NH)]
        lgs = [_neg_softplus(z) for z in zs]
        if valid is not None:
            lgs = [jnp.where(valid, lg, 0.0) for lg in lgs]
        inners = [_split_dot(lg, u) for lg in lgs]
        ws = [jnp.exp(zs[h] + lgs[h] + inners[h] + state[h][0]) for h in range(NH)]
        if valid is not None:
            ws = [jnp.where(valid, w, 0.0) for w in ws]
        ws = [w.astype(BF16) for w in ws]
        if cached:
            pv = [_dot_nt(ws[h], vs[h]) for h in range(NH)]
        else:
            pv = [jnp.dot(ws[h], vs[h], preferred_element_type=F32) for h in range(NH)]
        return tuple((state[h][0] + inners[h][:, :1] + lgs[h][:, :1], state[h][1] + pv[h])
                     for h in range(NH))

    zero = (jnp.zeros((L, 1), F32), jnp.zeros((L, HD), F32))
    r = lax.broadcasted_iota(jnp.int32, (L, L), 0)
    c = lax.broadcasted_iota(jnp.int32, (L, L), 1)
    state = visit((zero,) * NH, lambda h: kn_ref[:, h * HD:(h + 1) * HD],
                  lambda h: vn_ref[:, h * HD:(h + 1) * HD], tri(L), c < r, False)
    u_c = tri(tkc)

    def live(cr):
        i, st = cr
        top = functools.reduce(jnp.maximum, [s[0] for s in st])
        return (i < n_cache) & (jnp.max(top) > SB_UNDERFLOW)

    def step(cr):
        i, st = cr
        r0 = pl.multiple_of((n_cache - 1 - i) * tkc, tkc)
        return i + 1, visit(st, lambda h: kc_ref[0, 0, h, :, pl.ds(r0, tkc)],
                            lambda h: vc_ref[0, 0, h, :, pl.ds(r0, tkc)], u_c, None, True)
    _, state = lax.while_loop(live, step, (jnp.int32(0), state))
    for h in range(NH):
        o_ref[:, h * HD:(h + 1) * HD] = state[h][1]


def _sb_sample(qn, kn, z, cache_k, cache_v, layer, row_off, nb, L):
    P = cache_k.shape[4]
    tkc = _pick(P, (256, 128))
    rb = row_off // L
    cspec = pl.BlockSpec((1, 1, NH, HD, P), lambda b: (layer, b, 0, 0, 0))
    return pl.pallas_call(
        functools.partial(_sb_sample_body, L=L, n_cache=P // tkc, tkc=tkc),
        grid=(nb,),
        in_specs=[pl.BlockSpec((L, NH * LANES), lambda b: (rb + b, 0)),
                  pl.BlockSpec((L, BW), lambda b: (rb + b, 0)),
                  pl.BlockSpec((L, BW), lambda b: (rb + b, OFF_SV // BW)),
                  cspec, cspec],
        out_specs=pl.BlockSpec((L, BW), lambda b: (b, 0)),
        out_shape=jax.ShapeDtypeStruct((nb * L, BW), F32),
        compiler_params=_cparams(("parallel",)),
        name="sb_sample",
    )(qn, kn, z, cache_k, cache_v)


def _sortable(x):
    b = pltpu.bitcast(x + 0.0, jnp.int32)
    return jnp.where(b < 0, b ^ jnp.int32(0x7FFFFFFF), b)


def _select_topk(key_ref, n_blk, tkb, topk, rows):
    def count(pred):
        def body(i, acc):
            c0 = pl.multiple_of(i * tkb, tkb)
            for c in range(tkb // LANES):
                blk = key_ref[:, pl.ds(c0 + c * LANES, LANES)]
                col = c0 + c * LANES + lax.broadcasted_iota(jnp.int32, (rows, LANES), 1)
                acc = acc + jnp.where(pred(blk, col), 1.0, 0.0)
            return acc
        acc = lax.fori_loop(0, n_blk, body, jnp.zeros((rows, LANES), F32))
        return jnp.sum(acc, axis=1, keepdims=True)

    def unresolved(c):
        i, _, n_ge = c
        return (i < 32) & (jnp.max(jnp.abs(n_ge - topk)) > 0)

    def bit_step(c):
        i, thr, n_ge = c
        cand = thr + (jnp.int32(1) << (31 - i))
        n_cand = count(lambda blk, col: blk >= cand)
        take = n_cand >= topk
        return i + 1, jnp.where(take, cand, thr), jnp.where(take, n_cand, n_ge)

    thr0 = jnp.full((rows, 1), jnp.iinfo(jnp.int32).min, jnp.int32)
    n_all = jnp.full((rows, 1), 1.0, F32) * jnp.asarray(n_blk * tkb).astype(F32)
    _, thr, n_ge = lax.while_loop(unresolved, bit_step, (jnp.int32(0), thr0, n_all))
    n_col_bits = max(1, int(math.ceil(math.log2(key_ref.shape[1] + 1))))

    def tie_search():
        need = topk - count(lambda blk, col: blk > thr)

        def col_step(i, j):
            cand = j + (jnp.int32(1) << (n_col_bits - 1 - i))
            n = count(lambda blk, col: (blk == thr) & (col < cand))
            return jnp.where(n < need, cand, j)
        return lax.fori_loop(0, n_col_bits, col_step, jnp.zeros((rows, 1), jnp.int32))

    any_tie = jnp.max(jnp.abs(n_ge - topk)) > 0
    j_hi = lax.cond(any_tie, tie_search,
                    lambda: jnp.full((rows, 1), jnp.iinfo(jnp.int32).max, jnp.int32))
    return thr, j_hi


def _idx_scores(qi, wi, kblk16, k_real, keys_transposed=False):
    acc = None
    for h in range(NH):
        qh = qi[:, h * LANES:h * LANES + k_real].astype(BF16)
        if keys_transposed:
            d = jnp.dot(qh, kblk16, preferred_element_type=F32)
        else:
            d = _dot_nt(qh, kblk16)
        t = ((wi[:, h:h + 1] * (NH ** -0.5)) * (HD ** -0.5)) * jnp.maximum(d, 0.0)
        acc = t if acc is None else acc + t
    return acc


def _dsa_sel_prompt_body(qi_ref, wi_ref, ki_ref, m_ref, key_ref, *, tq, tkb, topk, n_keys):
    qb = pl.program_id(0)
    n_blk = ((qb + 1) * tq + tkb - 1) // tkb
    qi = qi_ref[...]
    wi = wi_ref[...]
    q_chunk = (qb * tq + lax.broadcasted_iota(jnp.int32, (tq, tkb), 0)) // CHUNK

    def score_blk(i, _):
        c0 = pl.multiple_of(i * tkb, tkb)
        s = _idx_scores(qi, wi, ki_ref[pl.ds(c0, tkb), :].astype(BF16), LANES)
        k_chunk = (c0 + lax.broadcasted_iota(jnp.int32, (tq, tkb), 1)) // CHUNK
        s = jnp.where(k_chunk <= q_chunk, s, -jnp.inf)
        key_ref[:, pl.ds(c0, tkb)] = _sortable(s)
        return 0
    lax.fori_loop(0, n_blk, score_blk, 0)

    thr, j_hi = _select_topk(key_ref, n_blk, tkb, topk, tq)
    neg_inf_key = _sortable(jnp.full((1, 1), -jnp.inf, F32))

    def write_blk(i, _):
        c0 = pl.multiple_of(i * tkb, tkb)
        blk = key_ref[:, pl.ds(c0, tkb)]
        col = c0 + lax.broadcasted_iota(jnp.int32, (tq, tkb), 1)
        sel = ((blk > thr) | ((blk == thr) & (col <= j_hi))) & (blk > neg_inf_key)
        m_ref[:, pl.ds(c0, tkb)] = jnp.where(sel, 0.0, NEG_BIG).astype(BF16)
        return 0
    lax.fori_loop(0, n_blk, write_blk, 0)

    def zero_blk(i, _):
        m_ref[:, pl.ds(pl.multiple_of(i * tkb, tkb), tkb)] = jnp.full((tq, tkb), NEG_BIG, BF16)
        return 0
    lax.fori_loop(n_blk, n_keys // tkb, zero_blk, 0)


def _dsa_sel_prompt(z, n_rows, topk):
    tq, tkb = 128, 512
    return pl.pallas_call(
        functools.partial(_dsa_sel_prompt_body, tq=tq, tkb=tkb, topk=topk, n_keys=n_rows),
        grid=(n_rows // tq,),
        in_specs=[pl.BlockSpec((tq, NH * LANES), lambda i: (i, OFF_QI // (NH * LANES))),
                  pl.BlockSpec((tq, LANES), lambda i: (i, OFF_WI // LANES)),
                  pl.BlockSpec((n_rows, LANES), lambda i: (0, OFF_KI // LANES))],
        out_specs=pl.BlockSpec((tq, n_rows), lambda i: (i, 0)),
        out_shape=jax.ShapeDtypeStruct((n_rows, n_rows), BF16),
        scratch_shapes=[pltpu.VMEM((tq, n_rows), jnp.int32)],
        compiler_params=_cparams(("parallel",)),
        name="dsa_select_prompt",
    )(z, z, z)


def _dsa_sel_sample_body(qi_ref, wi_ref, kin_ref, kic_ref, m_ref, key_ref, *, L, P, tkb, topk):
    qi = qi_ref[...]
    wi = wi_ref[...]
    n_c = P // tkb

    def score_blk(i, _):
        c0 = pl.multiple_of(i * tkb, tkb)
        s = _idx_scores(qi, wi, kic_ref[0, 0, :, pl.ds(c0, tkb)].astype(BF16), HD, keys_transposed=True)
        key_ref[:, pl.ds(c0, tkb)] = _sortable(s)
        return 0
    lax.fori_loop(0, n_c, score_blk, 0)

    knew = jnp.concatenate([kin_ref[...], jnp.zeros((tkb - L, LANES), F32)], axis=0).astype(BF16)
    s = _idx_scores(qi, wi, knew, LANES)
    col = lax.broadcasted_iota(jnp.int32, (L, tkb), 1)
    key_ref[:, pl.ds(P, tkb)] = _sortable(jnp.where(col < L, s, -jnp.inf))

    thr, j_hi = _select_topk(key_ref, n_c + 1, tkb, topk, L)
    neg_inf_key = _sortable(jnp.full((1, 1), -jnp.inf, F32))

    def write_blk(i, _):
        c0 = pl.multiple_of(i * tkb, tkb)
        blk = key_ref[:, pl.ds(c0, tkb)]
        cc = c0 + lax.broadcasted_iota(jnp.int32, (L, tkb), 1)
        sel = ((blk > thr) | ((blk == thr) & (cc <= j_hi))) & (blk > neg_inf_key)
        m_ref[:, pl.ds(c0, tkb)] = jnp.where(sel, 0.0, NEG_BIG).astype(BF16)
        return 0
    lax.fori_loop(0, n_c + 1, write_blk, 0)


def _dsa_sel_sample(z, cache_ki, layer, row_off, nb, L, topk):
    P = cache_ki.shape[3]
    tkb = _pick(P, (512, 256, 128))
    rb = row_off // L
    return pl.pallas_call(
        functools.partial(_dsa_sel_sample_body, L=L, P=P, tkb=tkb, topk=topk),
        grid=(nb,),
        in_specs=[pl.BlockSpec((L, NH * LANES), lambda b: (rb + b, OFF_QI // (NH * LANES))),
                  pl.BlockSpec((L, LANES), lambda b: (rb + b, OFF_WI // LANES)),
                  pl.BlockSpec((L, LANES), lambda b: (rb + b, OFF_KI // LANES)),
                  pl.BlockSpec((1, 1, HD, P), lambda b: (layer, b, 0, 0))],
        out_specs=pl.BlockSpec((L, P + tkb), lambda b: (b, 0)),
        out_shape=jax.ShapeDtypeStruct((nb * L, P + tkb), BF16),
        scratch_shapes=[pltpu.VMEM((L, P + tkb), jnp.int32)],
        compiler_params=_cparams(("parallel",)),
        name="dsa_select_sample",
    )(z, z, z, cache_ki)


def _dsa_attn_body(q_ref, m_ref, kn_ref, vn_ref, o_ref, *, tq, tkb):
    qb = pl.program_id(1)
    n_blk = ((qb + 1) * tq + tkb - 1) // tkb
    qs = [q_ref[:, e * LANES:(e + 1) * LANES] for e in range(2)]

    def visit(state, kblk, vblk, mblk):
        kb16 = kblk.astype(BF16)
        vb16 = vblk.astype(BF16)
        bias = mblk.astype(F32)
        new = []
        for e in range(2):
            m, l, acc = state[e]
            s = _dot_nt(qs[e], kb16) + bias
            m_new = jnp.maximum(m, jnp.max(s, axis=1, keepdims=True))
            p = jnp.exp2(s - m_new)
            alpha = jnp.exp2(m - m_new)
            l = alpha * l + jnp.sum(p, axis=1, keepdims=True)
            acc = alpha * acc + jnp.dot(p.astype(BF16), vb16, preferred_element_type=F32)
            new.append((m_new, l, acc))
        return tuple(new)

    init = (jnp.full((tq, 1), NEG_BIG, F32), jnp.zeros((tq, 1), F32), jnp.zeros((tq, LANES), F32))
    state = (init, init)

    def step(i, st):
        c0 = pl.multiple_of(i * tkb, tkb)
        return visit(st, kn_ref[pl.ds(c0, tkb), :], vn_ref[pl.ds(c0, tkb), :], m_ref[:, pl.ds(c0, tkb)])
    state = lax.fori_loop(0, n_blk, step, state)

    lane = lax.broadcasted_iota(jnp.int32, (tq, LANES), 1)
    o0 = state[0][2] / state[0][1]
    o1 = state[1][2] / state[1][1]
    o_ref[...] = jnp.where(lane < HD, o0, o1)


def _dsa_attn_prompt(qn, kn, z, mask, n_rows):
    tq, tkb = min(512, n_rows), min(1024, n_rows)
    vblk = OFF_DV // LANES
    return pl.pallas_call(
        functools.partial(_dsa_attn_body, tq=tq, tkb=tkb),
        grid=(NH // 2, n_rows // tq),
        in_specs=[pl.BlockSpec((tq, 2 * LANES), lambda p, i: (i, p)),
                  pl.BlockSpec((tq, n_rows), lambda p, i: (i, 0)),
                  pl.BlockSpec((n_rows, LANES), lambda p, i: (0, p)),
                  pl.BlockSpec((n_rows, LANES), lambda p, i: (0, vblk + p))],
        out_specs=pl.BlockSpec((tq, LANES), lambda p, i: (i, p)),
        out_shape=jax.ShapeDtypeStruct((n_rows, BW), F32),
        compiler_params=_cparams(("parallel", "arbitrary")),
        name="dsa_attn_prompt",
    )(qn, mask, kn, z)


def _dsa_attn_sample_body(q_ref, m_ref, kn_ref, vn_ref, kc_ref, vc_ref, o_ref, *, L, tkb, n_cache):
    qs = [_q_head(q_ref, h) for h in range(NH)]

    def visit(state, kget, vget, mblk, cached):
        bias = mblk.astype(F32)
        ks = [kget(h).astype(BF16) for h in range(NH)]
        vs = [vget(h).astype(BF16) for h in range(NH)]
        if cached:
            ss = [jnp.dot(qs[h], ks[h], preferred_element_type=F32) + bias for h in range(NH)]
        else:
            ss = [_dot_nt(qs[h], ks[h]) + bias for h in range(NH)]
        ms = [jnp.maximum(state[h][0], jnp.max(ss[h], axis=1, keepdims=True)) for h in range(NH)]
        ps = [jnp.exp2(ss[h] - ms[h]) for h in range(NH)]
        al = [jnp.exp2(state[h][0] - ms[h]) for h in range(NH)]
        if cached:
            pv = [_dot_nt(ps[h].astype(BF16), vs[h]) for h in range(NH)]
        else:
            pv = [jnp.dot(ps[h].astype(BF16), vs[h], preferred_element_type=F32) for h in range(NH)]
        return tuple((ms[h], al[h] * state[h][1] + jnp.sum(ps[h], axis=1, keepdims=True),
                      al[h] * state[h][2] + pv[h]) for h in range(NH))

    init = (jnp.full((L, 1), NEG_BIG, F32), jnp.zeros((L, 1), F32), jnp.zeros((L, HD), F32))

    def step(i, st):
        c0 = pl.multiple_of(i * tkb, tkb)
        return visit(st, lambda h: kc_ref[0, 0, h, :, pl.ds(c0, tkb)],
                     lambda h: vc_ref[0, 0, h, :, pl.ds(c0, tkb)], m_ref[:, pl.ds(c0, tkb)], True)
    state = lax.fori_loop(0, n_cache, step, (init,) * NH)

    pad = jnp.zeros((tkb - L, HD), F32)
    state = visit(state, lambda h: jnp.concatenate([kn_ref[:, h * HD:(h + 1) * HD], pad], axis=0),
                  lambda h: jnp.concatenate([vn_ref[:, h * HD:(h + 1) * HD], pad], axis=0),
                  m_ref[:, pl.ds(n_cache * tkb, tkb)], False)
    for h in range(NH):
        o_ref[:, h * HD:(h + 1) * HD] = state[h][2] / state[h][1]


def _dsa_attn_sample(qn, kn, z, mask, cache_k, cache_v, layer, row_off, nb, L):
    P = cache_k.shape[4]
    tkb = mask.shape[1] - P
    rb = row_off // L
    cspec = pl.BlockSpec((1, 1, NH, HD, P), lambda b: (layer, b, 0, 0, 0))
    return pl.pallas_call(
        functools.partial(_dsa_attn_sample_body, L=L, tkb=tkb, n_cache=P // tkb),
        grid=(nb,),
        in_specs=[pl.BlockSpec((L, NH * LANES), lambda b: (rb + b, 0)),
                  pl.BlockSpec((L, P + tkb), lambda b: (b, 0)),
                  pl.BlockSpec((L, BW), lambda b: (rb + b, 0)),
                  pl.BlockSpec((L, BW), lambda b: (rb + b, OFF_DV // BW)),
                  cspec, cspec],
        out_specs=pl.BlockSpec((L, BW), lambda b: (b, 0)),
        out_shape=jax.ShapeDtypeStruct((nb * L, BW), F32),
        compiler_params=_cparams(("parallel",)),
        name="dsa_attn_sample",
    )(qn, mask, kn, z, cache_k, cache_v)


def _gla_consts(C):
    nlev = int(math.log2(C))
    t = np.arange(C)
    cum = (t[:, None] >= t[None, :]).astype(np.float32)
    sel = np.zeros((nlev * C, C), np.float32)
    for l in range(nlev):
        m = 1 << l
        mid = (t // (2 * m)) * 2 * m + m
        sel[l * C + t, mid - 1] = 1.0
    return jnp.asarray(cum, BF16), jnp.asarray(sel, BF16)


def _gla_body(gq_ref, gk_ref, gv_ref, gr_ref, ga_ref, a2_ref, ab_ref, on_ref, cum_ref, sel_ref,
              s0_ref, o_ref, sout_ref, st_ref, *, C, n_chunks):
    ci = pl.program_id(1)
    nlev = int(math.log2(C))

    @pl.when(ci == 0)
    def _():
        st_ref[...] = s0_ref[0]

    x = jnp.dot(ga_ref[...].astype(BF16), a2_ref[...], preferred_element_type=F32) + ab_ref[...]
    g = _neg_softplus(-x) / GLA_GATE_NORM
    b = _split_dot_left(cum_ref[...], g)
    c_all = _split_dot_left(sel_ref[...], b)

    row = lax.broadcasted_iota(jnp.int32, (C, C), 0)
    col = lax.broadcasted_iota(jnp.int32, (C, C), 1)
    rowl = lax.broadcasted_iota(jnp.int32, (C, LANES), 0)

    hs = range(GLA_H)
    sls = [slice(h * LANES, (h + 1) * LANES) for h in hs]
    q = [gq_ref[:, sl] * (HD ** -0.5) for sl in sls]
    k = [gk_ref[:, sl] for sl in sls]
    v16 = [gv_ref[:, sl].astype(BF16) for sl in sls]
    bh = [b[:, sl] for sl in sls]
    st = [st_ref[h] for h in hs]
    o_inter = [_dot_nt((q[h] * jnp.exp(bh[h])).astype(BF16), st[h].astype(BF16)) for h in hs]
    b_last = [bh[h][C - 1:C, :] for h in hs]
    kv = [_dot_tn(v16[h], (k[h] * jnp.exp(b_last[h] - bh[h])).astype(BF16)) for h in hs]
    a = [jnp.where(row == col, jnp.sum(q[h] * k[h], axis=1, keepdims=True), 0.0) for h in hs]
    for l in range(nlev):
        m = 1 << l
        later = (rowl & m) != 0
        same_seg = (row // (2 * m)) == (col // (2 * m))
        ch = [c_all[l * C:(l + 1) * C, sl] for sl in sls]
        qe = [jnp.where(later, q[h] * jnp.exp(jnp.where(later, bh[h] - ch[h], 0.0)), 0.0) for h in hs]
        ke = [jnp.where(later, 0.0, k[h] * jnp.exp(jnp.where(later, 0.0, ch[h] - bh[h]))) for h in hs]
        al = [_dot_nt(qe[h].astype(BF16), ke[h].astype(BF16)) for h in hs]
        a = [a[h] + jnp.where(same_seg, al[h], 0.0) for h in hs]
    o = [jnp.dot(a[h].astype(BF16), v16[h], preferred_element_type=F32) + o_inter[h] for h in hs]
    for h in hs:
        st_ref[h] = st[h] * jnp.exp(b_last[h]) + kv[h]
        on = o[h] * lax.rsqrt(jnp.mean(o[h] * o[h], axis=-1, keepdims=True) + NORM_EPS) * on_ref[...]
        gr = gr_ref[:, sls[h]]
        o_ref[:, sls[h]] = on * (gr * jax.nn.sigmoid(gr))

    @pl.when(ci == n_chunks - 1)
    def _():
        sout_ref[0] = st_ref[...]


def _gla(z, a2p, abp, onorm, s0t, row_off, n_seq, L):
    C = min(CHUNK, L)
    n_chunks = L // C
    rb = row_off // C
    cum, sel = _gla_consts(C)
    blk = lambda off: pl.BlockSpec((C, BW), lambda s, c: (rb + s * n_chunks + c, off // BW))
    full = lambda a: pl.BlockSpec(a.shape, lambda s, c: (0,) * a.ndim)
    return pl.pallas_call(
        functools.partial(_gla_body, C=C, n_chunks=n_chunks),
        grid=(n_seq, n_chunks),
        in_specs=[blk(OFF_GQ), blk(OFF_GK), blk(OFF_GV), blk(OFF_GR),
                  pl.BlockSpec((C, LANES), lambda s, c: (rb + s * n_chunks + c, OFF_GA // LANES)),
                  full(a2p), full(abp), full(onorm), full(cum), full(sel),
                  pl.BlockSpec((1, GLA_H, GLA_DV, LANES), lambda s, c: (s, 0, 0, 0))],
        out_specs=[pl.BlockSpec((C, BW), lambda s, c: (s * n_chunks + c, 0)),
                   pl.BlockSpec((1, GLA_H, GLA_DV, LANES), lambda s, c: (s, 0, 0, 0))],
        out_shape=[jax.ShapeDtypeStruct((n_seq * L, BW), F32),
                   jax.ShapeDtypeStruct((n_seq, GLA_H, GLA_DV, LANES), F32)],
        scratch_shapes=[pltpu.VMEM((GLA_H, GLA_DV, LANES), F32)],
        compiler_params=_cparams(("parallel", "arbitrary")),
        name="gla",
    )(z, z, z, z, z, a2p, abp, onorm, cum, sel, s0t)


def _rw_prep_body(z_ref, head_ref, mu_ref, w0_ref, a0_ref, kkp_ref, kap_ref, rkp_ref,
                  w2_ref, a2_ref, g2_ref, seg_ref,
                  r_ref, w_ref, k_ref, v_ref, nkk_ref, kka_ref, gate_ref, bonus_ref, *, grp):
    zz = z_ref[...]
    tm, wd = zz.shape
    heads = head_ref[...]
    head_rows = jnp.concatenate([jnp.broadcast_to(heads[g:g + 1, :], (grp, wd))
                                 for g in range(tm // grp)], axis=0)
    row = lax.broadcasted_iota(jnp.int32, (tm, wd), 0)
    prev = jnp.where(row % grp == 0, head_rows, pltpu.roll(zz, 1, axis=0))
    zm = zz + mu_ref[...] * (prev - zz)
    rr = zm[:, 0:BW]
    rk = zm[:, BW:2 * BW]
    rv = zm[:, 2 * BW:3 * BW]
    lora = zm[:, 3 * BW:3 * BW + LANES]
    gl = zm[:, 3 * BW + LANES:3 * BW + 2 * LANES]
    xw = w0_ref[...] + jnp.dot(jnp.tanh(lora).astype(BF16), w2_ref[...], preferred_element_type=F32)
    wlog = _neg_softplus(-xw) - 0.5
    a = jax.nn.sigmoid(a0_ref[...] + jnp.dot(lora.astype(BF16), a2_ref[...], preferred_element_type=F32))
    gate = jnp.dot(jax.nn.sigmoid(gl).astype(BF16), g2_ref[...], preferred_element_type=F32)
    kkf = rk * kkp_ref[...]
    nrm = jnp.sqrt(_split_dot(kkf * kkf, seg_ref[...]))
    kk = kkf / jnp.maximum(nrm, 1e-12)
    kmod = rk * (1.0 + (a - 1.0) * kap_ref[...])
    coef = _split_dot(rr * kmod * rkp_ref[...], seg_ref[...])
    r_ref[...] = rr
    w_ref[...] = -jnp.exp(wlog)
    k_ref[...] = kmod
    v_ref[...] = rv
    nkk_ref[...] = -kk
    kka_ref[...] = kk * a
    gate_ref[...] = gate
    bonus_ref[...] = coef * rv


def _rw_prep(z, heads, grp, mu, w0, a0, kkp, kap, rkp, w2p, a2p, g2, seg):
    T = z.shape[0]
    tm = SUBLANES * grp
    assert T % tm == 0, (T, tm)
    W = 4 * BW
    par = lambda a: pl.BlockSpec(a.shape, lambda i: (0,) * a.ndim)
    out = pl.BlockSpec((tm, BW), lambda i: (i, 0))
    return pl.pallas_call(
        functools.partial(_rw_prep_body, grp=grp),
        grid=(T // tm,),
        in_specs=[pl.BlockSpec((tm, W), lambda i: (i, 0)), pl.BlockSpec((SUBLANES, W), lambda i: (i, 0)),
                  par(mu), par(w0), par(a0), par(kkp), par(kap), par(rkp),
                  par(w2p), par(a2p), par(g2), par(seg)],
        out_specs=[out] * 8,
        out_shape=[jax.ShapeDtypeStruct((T, BW), F32)] * 8,
        compiler_params=_cparams(("parallel",)),
        name="rwkv_prep",
    )(z, heads, mu, w0, a0, kkp, kap, rkp, w2p, a2p, g2, seg)


def _mm3(a, b):
    ah = a.astype(BF16)
    al = (a - ah.astype(F32)).astype(BF16)
    bh = b.astype(BF16)
    bl = (b - bh.astype(F32)).astype(BF16)
    d = lambda x, y: jnp.dot(x, y, preferred_element_type=F32)
    return d(ah, bh) + (d(ah, bl) + d(al, bh))


def _rw_chunk_body(r_ref, lw_ref, k_ref, v_ref, nkk_ref, kka_ref, s0_ref, o_ref, sout_ref, s_ref,
                   *, TT, L, n_tiles):
    ti = pl.program_id(0)
    npair = NH // 2
    nblk = TT // RW_BLK
    row = lax.broadcasted_iota(jnp.int32, (TT, TT), 0)
    col = lax.broadcasted_iota(jnp.int32, (TT, TT), 1)
    same = (row // RW_BLK) == (col // RW_BLK)
    strict = same & (col < row)
    incl = same & (col <= row)
    tri = jnp.where(incl, 1.0, 0.0).astype(BF16)
    last = jnp.where(same & (col % RW_BLK == RW_BLK - 1), 1.0, 0.0).astype(BF16)
    eye = jnp.where(row == col, 1.0, 0.0)
    pair_diag = (row < HD) == (col < HD)
    lo_half = lax.broadcasted_iota(jnp.int32, (TT, LANES), 1) < HD
    lo_half_b = lax.broadcasted_iota(jnp.int32, (RW_BLK, LANES), 1) < HD

    r, lw, k, v = r_ref[...], lw_ref[...], k_ref[...], v_ref[...]
    nk, ka = nkk_ref[...], kka_ref[...]
    lc = _split_dot_left(tri, lw)
    le = _split_dot_left(last, lc)
    nkt = nk * jnp.exp(lc - lw)
    rt = r * jnp.exp(lc)
    inv = jnp.exp(-lc)
    kah = ka * inv
    kh = k * inv
    rem = jnp.exp(le - lc)
    kaw = (ka * rem).astype(BF16)
    kw = (k * rem).astype(BF16)
    wb = jnp.exp(le)

    sls = [slice(p * LANES, (p + 1) * LANES) for p in range(npair)]
    v16 = [v[:, sl].astype(BF16) for sl in sls]
    heads = [(p, e) for p in range(npair) for e in range(2)]
    grams = []
    for p, e in heads:
        hm = lo_half if e == 0 else jnp.logical_not(lo_half)
        lhs = jnp.concatenate([jnp.where(hm, nkt[:, sls[p]], 0.0), jnp.where(hm, rt[:, sls[p]], 0.0)],
                              axis=0).astype(BF16)
        rhs = jnp.concatenate([kah[:, sls[p]], kh[:, sls[p]]], axis=0).astype(BF16)
        grams.append(_dot_nt(lhs, rhs))
    n1 = [jnp.where(strict, g[:TT, :TT], 0.0) for g in grams]
    bt = [jnp.where(strict, g[:TT, TT:], 0.0).astype(BF16) for g in grams]
    cmat = [jnp.where(incl, g[TT:, :TT], 0.0).astype(BF16) for g in grams]
    et = [jnp.where(incl, g[TT:, TT:], 0.0).astype(BF16) for g in grams]
    n2 = [_mm3(a, a) for a in n1]
    n4 = [_mm3(a, a) for a in n2]
    n8 = [_mm3(a, a) for a in n4]
    tinv = [eye + a for a in n1]
    for power in (n2, n4, n8):
        tinv = [t + _mm3(t, a) for t, a in zip(tinv, power)]
    vb_h = [jnp.dot(b_, v16[p], preferred_element_type=F32) for b_, (p, e) in zip(bt, heads)]
    ev_h = [jnp.dot(e_, v16[p], preferred_element_type=F32) for e_, (p, e) in zip(et, heads)]
    vb = [jnp.where(lo_half, vb_h[2 * p], vb_h[2 * p + 1]) for p in range(npair)]
    ev = [jnp.where(lo_half, ev_h[2 * p], ev_h[2 * p + 1]) for p in range(npair)]

    if L >= TT:
        @pl.when(ti == 0)
        def _():
            s_ref[...] = s0_ref[0]
        state = {(0, p): s_ref[p] for p in range(npair)}
        rounds = [[j] for j in range(nblk)]
    else:
        bps = L // RW_BLK
        state = {(q, p): s0_ref[q, p] for q in range(TT // L) for p in range(npair)}
        rounds = [[q * bps + i for q in range(TT // L)] for i in range(bps)]
    sa_rows = [[None] * nblk for _ in range(npair)]
    os_rows = [[None] * nblk for _ in range(npair)]
    for blocks in rounds:
        items = [(j, p) for j in blocks for p in range(npair)]
        seq = lambda j: (j * RW_BLK) // L if L < TT else 0
        xs = {}
        for j, p in items:
            bs = slice(j * RW_BLK, (j + 1) * RW_BLK)
            s = state[(seq(j), p)]
            s_hi = s.astype(BF16)
            s_lo = (s - s_hi.astype(F32)).astype(BF16)
            lhs = jnp.concatenate([nkt[bs, sls[p]], rt[bs, sls[p]]], axis=0).astype(BF16)
            xs[(j, p)] = (jnp.dot(lhs, s_hi, preferred_element_type=F32)
                          + jnp.dot(lhs, s_lo, preferred_element_type=F32))
        sa = {}
        for j, p in items:
            b0 = j * RW_BLK
            bs = slice(b0, b0 + RW_BLK)
            parts = [xs[(j, p)][:RW_BLK] + vb[p][bs]]
            if b0:
                parts.insert(0, jnp.zeros((b0, LANES), F32))
            if TT - b0 - RW_BLK:
                parts.append(jnp.zeros((TT - b0 - RW_BLK, LANES), F32))
            xfull = jnp.concatenate(parts, axis=0)
            sa[(j, p)] = jnp.where(lo_half_b, _mm3(tinv[2 * p][bs], xfull),
                                   _mm3(tinv[2 * p + 1][bs], xfull))
        for j, p in items:
            b0 = j * RW_BLK
            bs = slice(b0, b0 + RW_BLK)
            u = (_dot_tn(kaw[bs, sls[p]], sa[(j, p)].astype(BF16)) + _dot_tn(kw[bs, sls[p]], v16[p][bs]))
            wcol = jnp.broadcast_to(wb[b0:b0 + 1, sls[p]], (LANES, LANES)).T
            state[(seq(j), p)] = wcol * state[(seq(j), p)] + jnp.where(pair_diag, u, 0.0)
            sa_rows[p][j] = sa[(j, p)]
            os_rows[p][j] = xs[(j, p)][RW_BLK:]
    if L < TT:
        for (q, p), s in state.items():
            sout_ref[q, p] = s

    for p in range(npair):
        sa_t = jnp.concatenate(sa_rows[p], axis=0).astype(BF16)
        o_sa = jnp.where(lo_half, jnp.dot(cmat[2 * p], sa_t, preferred_element_type=F32),
                         jnp.dot(cmat[2 * p + 1], sa_t, preferred_element_type=F32))
        o_ref[:, sls[p]] = jnp.concatenate(os_rows[p], axis=0) + o_sa + ev[p]
    if L >= TT:
        for p in range(npair):
            s_ref[p] = state[(0, p)]

        @pl.when(ti == n_tiles - 1)
        def _():
            for p in range(npair):
                sout_ref[0, p] = state[(0, p)]


def _rw_chunk(r, lw, k, v, nkk, kka, s0, row_off, n_seq, L):
    TT = LANES
    assert L % RW_BLK == 0 and (L % TT == 0 or TT % L == 0)
    n_tok = n_seq * L
    n_tiles = n_tok // TT
    rb = row_off // TT
    ns = max(1, TT // L)
    tok = pl.BlockSpec((TT, BW), lambda i: (rb + i, 0))
    if ns == 1:
        sspec = pl.BlockSpec((1, NH // 2, LANES, LANES), lambda i: (0, 0, 0, 0))
    else:
        sspec = pl.BlockSpec((ns, NH // 2, LANES, LANES), lambda i: (i, 0, 0, 0))
    return pl.pallas_call(
        functools.partial(_rw_chunk_body, TT=TT, L=L, n_tiles=n_tiles),
        grid=(n_tiles,),
        in_specs=[tok] * 6 + [sspec],
        out_specs=[pl.BlockSpec((TT, BW), lambda i: (i, 0)), sspec],
        out_shape=[jax.ShapeDtypeStruct((n_tok, BW), F32),
                   jax.ShapeDtypeStruct(s0.shape, F32)],
        scratch_shapes=[pltpu.VMEM((NH // 2, LANES, LANES), F32)],
        compiler_params=_cparams(("arbitrary",)),
        name="rwkv_chunk",
    )(r, lw, k, v, nkk, kka, s0)


def _rw_post_body(o_ref, bonus_ref, gate_ref, lnw_ref, lnb_ref, seg_ref, out_ref):
    o = o_ref[...]
    mu = _split_dot(o, seg_ref[...]) * (1.0 / HD)
    d = o - mu
    var = _split_dot(d * d, seg_ref[...]) * (1.0 / HD)
    on = d * lax.rsqrt(var + RW_LN_EPS) * lnw_ref[...] + lnb_ref[...]
    out_ref[...] = (on + bonus_ref[...]) * gate_ref[...]


def _rw_post(o, bonus, gate, lnw, lnb, seg):
    T = o.shape[0]
    tm = _pick(T, (512, 256, 128))
    blk = pl.BlockSpec((tm, BW), lambda i: (i, 0))
    par = lambda a: pl.BlockSpec(a.shape, lambda i: (0,) * a.ndim)
    return pl.pallas_call(
        _rw_post_body,
        grid=(T // tm,),
        in_specs=[blk, blk, blk, par(lnw), par(lnb), par(seg)],
        out_specs=blk,
        out_shape=jax.ShapeDtypeStruct((T, BW), F32),
        compiler_params=_cparams(("parallel",)),
        name="rwkv_post",
    )(o, bonus, gate, lnw, lnb, seg)


def _heads_alt(w):
    lead = w.shape[:-1]
    w4 = w.reshape(*lead, NH // 2, 2, HD)
    zero = jnp.zeros((*lead, NH // 2, HD), w.dtype)
    ev = jnp.concatenate([w4[..., 0, :], zero], -1)
    od = jnp.concatenate([zero, w4[..., 1, :]], -1)
    return jnp.stack([ev, od], -2).reshape(*lead, NH * LANES)


def _heads_lo(w, nh):
    lead = w.shape[:-1]
    w3 = w.reshape(*lead, nh, HD)
    return jnp.concatenate([w3, jnp.zeros_like(w3)], -1).reshape(*lead, nh * LANES)


def _padc(w, n):
    return jnp.pad(w, [(0, 0)] * (w.ndim - 1) + [(0, n - w.shape[-1])])


def _rw_reorder(a):
    o = 0
    parts = {}
    for name, wd in (("rr", BW), ("wl", RW_LORA), ("rk", BW), ("rv", BW), ("al", RW_LORA), ("gl", RW_GATE_LORA)):
        parts[name] = a[..., o:o + wd]
        o += wd
    return jnp.concatenate([parts[n] for n in ("rr", "rk", "rv", "wl", "al", "gl")], axis=-1)


def _rw_unorder(a):
    rr, rk, rv = a[..., 0:BW], a[..., BW:2 * BW], a[..., 2 * BW:3 * BW]
    wl = a[..., 3 * BW:3 * BW + RW_LORA]
    al = a[..., 3 * BW + RW_LORA:3 * BW + 2 * RW_LORA]
    gl = a[..., 3 * BW + 2 * RW_LORA:RW_COLS]
    return jnp.concatenate([rr, wl, rk, rv, al, gl], axis=-1)


def _relayout_w_in(w, d_model):
    w = w.astype(BF16)
    o = 0

    def take(n):
        nonlocal o
        s = w[..., o:o + n]
        o += n
        return s
    sq, sk, sv = take(BW), take(BW), take(BW)
    dq, dk, dv, qi, ki, wi = take(BW), take(BW), take(BW), take(BW), take(HD), take(NH)
    gq, gk = take(GLA_H * HD), take(GLA_H * HD)
    gv, gr, ga = take(BW), take(BW), take(GLA_LOWRANK)
    rw = take(RW_COLS)
    gate = take(4 * d_model)
    cols = [_padc(_rw_reorder(rw), 4 * BW), _heads_alt(sq), sk, sv, _heads_alt(dq), dk, dv,
            _heads_lo(qi, NH), _heads_lo(gq, GLA_H), _heads_lo(gk, GLA_H), gv, gr,
            _padc(ki, LANES), _padc(wi, LANES), _padc(ga, 2 * LANES), gate]
    return jnp.concatenate(cols, axis=-1)


def _relayout_ffn(w_up, w_down, tf):
    f = w_up.shape[-1] // 2
    fp = -(-f // tf) * tf
    wup = jnp.concatenate([_padc(w_up[..., :f].astype(BF16), fp), _padc(w_up[..., f:].astype(BF16), fp)],
                          axis=-1)
    wdn = jnp.pad(w_down.astype(BF16), ((0, 0), (0, fp - f), (0, 0)))
    return wup, wdn


def _gain_alt(g):
    return jnp.tile(g, 2 * NH)[None, :]


def kernel(x_prompt, x_sample, cache_sb_k, cache_sb_v, cache_dsa_k, cache_dsa_v, cache_dsa_kidx, state_gla, state_rwkv, state_rwkv_shift, ffn1_norm, ffn1_up, ffn1_down, mix_norm, w_in, sb_qnorm, sb_knorm, dsa_qnorm, dsa_knorm, gla_a2, gla_ab, gla_onorm, rwkv_mu, rwkv_w0, rwkv_w2, rwkv_a0, rwkv_a2, rwkv_g2, rwkv_kk, rwkv_ka, rwkv_rk, rwkv_lnw, rwkv_lnb, w_branch, w_out, ffn2_norm, ffn2_up, ffn2_down):
    depth = w_in.shape[0]
    bp, lp, d_model = x_prompt.shape
    nb, ls, _ = x_sample.shape
    assert bp == 1, "prompt group is one sequence"
    tp = bp * lp
    ts = nb * ls
    past = cache_sb_k.shape[2]
    to_pos_minor = lambda c: jnp.transpose(c, (0, 1, 3, 4, 2))
    ck_sb, cv_sb = to_pos_minor(cache_sb_k), to_pos_minor(cache_sb_v)
    ck_dsa, cv_dsa = to_pos_minor(cache_dsa_k), to_pos_minor(cache_dsa_v)
    cki = jnp.transpose(cache_dsa_kidx, (0, 1, 3, 2))
    tf = 512

    x = jnp.concatenate([x_prompt.reshape(tp, d_model), x_sample.reshape(ts, d_model)], axis=0)

    bd = jnp.asarray(np.kron(np.eye(2 * NH), np.full((HD, HD), 1.0 / HD)), BF16)
    seg = jnp.asarray(np.kron(np.eye(NH), np.ones((HD, HD))), BF16)
    topk_p = min(TOPK_MAX, lp // 4)
    topk_s = min(TOPK_MAX, (past + ls) // 4)

    new_p = [[] for _ in range(8)]
    new_s = [[] for _ in range(8)]
    wup1, wdn1 = _relayout_ffn(ffn1_up, ffn1_down, tf)
    wup2, wdn2 = _relayout_ffn(ffn2_up, ffn2_down, tf)
    win = _relayout_w_in(w_in, d_model)
    wbr = w_branch.astype(BF16)
    wout = w_out.astype(BF16)
    for i in range(depth):
        x = _ffn(x, ffn1_norm[:, None], wup1, wdn1, tf, i)
        z = _proj(x, mix_norm[:, None], win, i)

        sqn, skn, dqn, dkn = _qknorm(z, _gain_alt(sb_qnorm[i]), jnp.tile(sb_knorm[i], NH)[None],
                                     _gain_alt(dsa_qnorm[i]), jnp.tile(dsa_knorm[i], NH)[None], bd)
        o_sb = (_sb_prompt(sqn, skn, z, tp), _sb_sample(sqn, skn, z, ck_sb, cv_sb, i, tp, nb, ls))
        mask_p = _dsa_sel_prompt(z, tp, topk_p)
        mask_s = _dsa_sel_sample(z, cki, i, tp, nb, ls, topk_s)
        o_dsa = (_dsa_attn_prompt(dqn, dkn, z, mask_p, tp),
                 _dsa_attn_sample(dqn, dkn, z, mask_s, ck_dsa, cv_dsa, i, tp, nb, ls))

        a2p = jnp.pad(_heads_lo(gla_a2[i], GLA_H), ((0, LANES - GLA_LOWRANK), (0, 0))).astype(BF16)
        abp = _heads_lo(gla_ab[i][None], GLA_H)
        onorm = gla_onorm[i][None]

        def gla_state_in(s):
            return jnp.pad(jnp.swapaxes(s, 2, 3), ((0, 0), (0, 0), (0, 0), (0, LANES - HD)))

        def gla_state_out(s):
            return jnp.swapaxes(s[..., :HD], 2, 3)
        og_p, sg_p = _gla(z, a2p, abp, onorm, jnp.zeros((bp, GLA_H, GLA_DV, LANES), F32), 0, bp, lp)
        og_s, sg_s = _gla(z, a2p, abp, onorm, gla_state_in(state_gla[i]), tp, nb, ls)
        o_gla = (og_p, og_s)

        zrw = z[:, :RW_COLS]
        shift_s = _rw_reorder(state_rwkv_shift[i])
        heads = jnp.concatenate([jnp.zeros((1, 4 * BW), F32), z[ls - 1:tp - 1:ls, :4 * BW],
                                 _padc(shift_s.reshape(nb, RW_COLS), 4 * BW)], axis=0)
        mu = _padc(_rw_reorder(rwkv_mu[i])[None], 4 * BW)
        w2p = jnp.concatenate([rwkv_w2[i], jnp.zeros_like(rwkv_a2[i])], axis=0).astype(BF16)
        a2q = jnp.concatenate([jnp.zeros_like(rwkv_w2[i]), rwkv_a2[i]], axis=0).astype(BF16)
        r_, w_, k_, v_, nkk_, kka_, gate_, bonus_ = _rw_prep(
            z, heads, ls, mu, rwkv_w0[i][None], rwkv_a0[i][None], rwkv_kk[i][None], rwkv_ka[i][None],
            rwkv_rk[i][None], w2p, a2q, rwkv_g2[i].astype(BF16), seg)

        def rw_state_in(s):
            n = s.shape[0]
            st = jnp.swapaxes(s, 2, 3).reshape(n, NH // 2, 2, HD, HD)
            zero = jnp.zeros((n, NH // 2, HD, HD), s.dtype)
            return jnp.concatenate([jnp.concatenate([st[:, :, 0], zero], axis=-1),
                                    jnp.concatenate([zero, st[:, :, 1]], axis=-1)], axis=-2)

        def rw_state_out(s):
            n = s.shape[0]
            st = jnp.stack([s[:, :, :HD, :HD], s[:, :, HD:, HD:]], axis=2)
            return jnp.swapaxes(st, 3, 4).reshape(n, NH, HD, HD)
        o_rp, sr_p = _rw_chunk(r_, w_, k_, v_, nkk_, kka_, jnp.zeros((bp, NH // 2, LANES, LANES), F32), 0, bp, lp)
        o_rs, sr_s = _rw_chunk(r_, w_, k_, v_, nkk_, kka_, rw_state_in(state_rwkv[i]), tp, nb, ls)
        o_r = jnp.concatenate([o_rp, o_rs], axis=0)
        o_rw = _rw_post(o_r, bonus_, gate_, rwkv_lnw[i][None], rwkv_lnb[i][None], seg)

        x = _merge(x, z, (o_sb, o_dsa, o_gla, o_rw), wbr, wout, i, tp)
        x = _ffn(x, ffn2_norm[:, None], wup2, wdn2, tf, i)

        sv = z[:, OFF_SV:OFF_SV + BW]
        dv = z[:, OFF_DV:OFF_DV + BW]
        ki = z[:, OFF_KI:OFF_KI + HD]
        for dst, (lo, n, l) in ((new_p, (0, bp, lp)), (new_s, (tp, nb, ls))):
            hi = lo + n * l
            dst[0].append(skn[lo:hi].reshape(n, l, NH, HD))
            dst[1].append(sv[lo:hi].reshape(n, l, NH, HD))
            dst[2].append(dkn[lo:hi].reshape(n, l, NH, HD))
            dst[3].append(dv[lo:hi].reshape(n, l, NH, HD))
            dst[4].append(ki[lo:hi].reshape(n, l, HD))
            dst[7].append(_rw_unorder(zrw[lo:hi].reshape(n, l, RW_COLS)[:, -1:]))
        new_p[5].append(gla_state_out(sg_p))
        new_s[5].append(gla_state_out(sg_s))
        new_p[6].append(rw_state_out(sr_p))
        new_s[6].append(rw_state_out(sr_s))

    outs_p = [jnp.stack(l, axis=0) for l in new_p]
    outs_s = [jnp.stack(l, axis=0) for l in new_s]
    y_p = x[:tp].reshape(bp, lp, d_model)
    y_s = x[tp:].reshape(nb, ls, d_model)
    return (y_p, y_s, *outs_p, *outs_s)
```

```python
import functools
import math

import jax
import jax.numpy as jnp
import numpy as np
from jax import lax
from jax.experimental import pallas as pl
from jax.experimental.pallas import tpu as pltpu

F32 = jnp.float32
BF16 = jnp.bfloat16

LANES = 128
SUBLANES = 8
HD = 64
NH = 8
BW = NH * HD
GLA_H = 4
GLA_DV = 128
GLA_LOWRANK = 16
GLA_GATE_NORM = 16.0
CHUNK = 64
TOPK_MAX = 256
RW_LORA = 64
RW_GATE_LORA = 128
RW_BLK = 16
RW_COLS = 3 * BW + 2 * RW_LORA + RW_GATE_LORA
NORM_EPS = 1e-6
RW_LN_EPS = 64e-5
NEG_BIG = -1e30
LOG2E = math.log2(math.e)
SB_UNDERFLOW = -104.0
VMEM_LIMIT = 56 * 1024 * 1024

OFF_RW = 0
OFF_SQ = 2048
OFF_SK = 3072
OFF_SV = 3584
OFF_DQ = 4096
OFF_DK = 5120
OFF_DV = 5632
OFF_QI = 6144
OFF_GQ = 7168
OFF_GK = 7680
OFF_GV = 8192
OFF_GR = 8704
OFF_KI = 9216
OFF_WI = 9344
OFF_GA = 9472
OFF_GATE = 9728


def _cparams(sem):
    return pltpu.CompilerParams(dimension_semantics=sem, vmem_limit_bytes=VMEM_LIMIT)


def _pick(n, prefs):
    for p in prefs:
        if n % p == 0:
            return p
    raise ValueError(f"no tile for {n} in {prefs}")


def _split_dot(a, b):
    hi = a.astype(BF16)
    lo = (a - hi.astype(F32)).astype(BF16)
    return (jnp.dot(hi, b, preferred_element_type=F32)
            + jnp.dot(lo, b, preferred_element_type=F32))


def _split_dot_left(a, b):
    hi = b.astype(BF16)
    lo = (b - hi.astype(F32)).astype(BF16)
    return (jnp.dot(a, hi, preferred_element_type=F32)
            + jnp.dot(a, lo, preferred_element_type=F32))


def _dot_nt(a, b):
    return lax.dot_general(a, b, (((1,), (1,)), ((), ())), preferred_element_type=F32)


def _dot_tn(a, b):
    return lax.dot_general(a, b, (((0,), (0,)), ((), ())), preferred_element_type=F32)


def _neg_softplus(z):
    return -(jnp.maximum(z, 0.0) + jnp.log(1.0 + jnp.exp(-jnp.abs(z))))


def _ffn_body(x_ref, g_ref, wg_ref, wu_ref, wdn_ref, o_ref, xn_ref):
    @pl.when(pl.program_id(1) == 0)
    def _():
        x = x_ref[...]
        ms = jnp.mean(x * x, axis=-1, keepdims=True)
        xn_ref[...] = (x * lax.rsqrt(ms + NORM_EPS) * g_ref[0]).astype(BF16)
        o_ref[...] = x

    xn = xn_ref[...]
    gate = jnp.dot(xn, wg_ref[0], preferred_element_type=F32)
    up = jnp.dot(xn, wu_ref[0], preferred_element_type=F32)
    act = (gate * jax.nn.sigmoid(gate) * up).astype(BF16)
    o_ref[...] += 0.5 * jnp.dot(act, wdn_ref[0], preferred_element_type=F32)


def _ffn(x, g, wup, wdn, tf, layer):
    T, D = x.shape
    nf = wdn.shape[1] // tf
    tm = _pick(T, (512, 256, 128))
    return pl.pallas_call(
        _ffn_body,
        grid=(T // tm, nf),
        in_specs=[pl.BlockSpec((tm, D), lambda i, j: (i, 0)),
                  pl.BlockSpec((1, 1, D), lambda i, j: (layer, 0, 0)),
                  pl.BlockSpec((1, D, tf), lambda i, j: (layer, 0, j)),
                  pl.BlockSpec((1, D, tf), lambda i, j: (layer, 0, nf + j)),
                  pl.BlockSpec((1, tf, D), lambda i, j: (layer, j, 0))],
        out_specs=pl.BlockSpec((tm, D), lambda i, j: (i, 0)),
        out_shape=jax.ShapeDtypeStruct((T, D), F32),
        scratch_shapes=[pltpu.VMEM((tm, D), BF16)],
        compiler_params=_cparams(("parallel", "arbitrary")),
        name="ffn",
    )(x, g, wup, wup, wdn)


def _proj_body(x_ref, g_ref, w_ref, o_ref, xn_ref):
    @pl.when(pl.program_id(1) == 0)
    def _():
        x = x_ref[...]
        ms = jnp.mean(x * x, axis=-1, keepdims=True)
        xn_ref[...] = (x * lax.rsqrt(ms + NORM_EPS) * g_ref[0]).astype(BF16)

    o_ref[...] = jnp.dot(xn_ref[...], w_ref[0], preferred_element_type=F32)


def _proj(x, g, w, layer):
    T, D = x.shape
    n = w.shape[2]
    tm = _pick(T, (1024, 512, 256, 128))
    tn = _pick(n, (1280, 512))
    return pl.pallas_call(
        _proj_body,
        grid=(T // tm, n // tn),
        in_specs=[pl.BlockSpec((tm, D), lambda i, j: (i, 0)),
                  pl.BlockSpec((1, 1, D), lambda i, j: (layer, 0, 0)),
                  pl.BlockSpec((1, D, tn), lambda i, j: (layer, 0, j))],
        out_specs=pl.BlockSpec((tm, tn), lambda i, j: (i, j)),
        out_shape=jax.ShapeDtypeStruct((T, n), F32),
        scratch_shapes=[pltpu.VMEM((tm, D), BF16)],
        compiler_params=_cparams(("parallel", "arbitrary")),
        name="in_proj",
    )(x, g, w)


def _merge_body(*refs, split, n_prompt_tiles):
    refs = list(refs)
    on_prompt = pl.program_id(0) < n_prompt_tiles
    branches = []
    for two in split:
        if two:
            bp, bs = refs.pop(0), refs.pop(0)
            branches.append(jnp.where(on_prompt, bp[...], bs[...]))
        else:
            branches.append(refs.pop(0)[...])
    g_refs, (wb_ref, wo_ref, x_ref, o_ref) = refs[:4], refs[4:]

    @pl.when(pl.program_id(1) == 0)
    def _():
        o_ref[...] = x_ref[...]

    acc = None
    for n in range(4):
        p = jnp.dot(branches[n].astype(BF16), wb_ref[0, n], preferred_element_type=F32)
        t = jax.nn.sigmoid(g_refs[n][...]) * p
        acc = t if acc is None else acc + t
    o_ref[...] += jnp.dot(acc.astype(BF16), wo_ref[0], preferred_element_type=F32)


def _merge(x, z, branches, wb, wo, layer, n_prompt):
    T, D = x.shape
    tm = _pick(math.gcd(T, n_prompt), (512, 256, 128))
    npt = n_prompt // tm
    tc = 512
    nc = D // tc
    gate_blk = OFF_GATE // tc
    split = tuple(isinstance(b, tuple) for b in branches)
    b_args, b_specs = [], []
    for b in branches:
        if isinstance(b, tuple):
            b_args += list(b)
            b_specs += [pl.BlockSpec((tm, BW), lambda i, c: (jnp.minimum(i, npt - 1), 0)),
                        pl.BlockSpec((tm, BW), lambda i, c: (jnp.maximum(i - npt, 0), 0))]
        else:
            b_args.append(b)
            b_specs.append(pl.BlockSpec((tm, BW), lambda i, c: (i, 0)))
    gspecs = [pl.BlockSpec((tm, tc), functools.partial(lambda i, c, n: (i, gate_blk + n * nc + c), n=n))
              for n in range(4)]
    return pl.pallas_call(
        functools.partial(_merge_body, split=split, n_prompt_tiles=npt),
        grid=(T // tm, nc),
        in_specs=b_specs + gspecs + [
            pl.BlockSpec((1, 4, BW, tc), lambda i, c: (layer, 0, 0, c)),
            pl.BlockSpec((1, tc, D), lambda i, c: (layer, c, 0)),
            pl.BlockSpec((tm, D), lambda i, c: (i, 0))],
        out_specs=pl.BlockSpec((tm, D), lambda i, c: (i, 0)),
        out_shape=jax.ShapeDtypeStruct((T, D), F32),
        compiler_params=_cparams(("parallel", "arbitrary")),
        name="merge",
    )(*b_args, z, z, z, z, wb, wo, x)


def _qknorm_body(sq_ref, sk_ref, dq_ref, dk_ref, gsq_ref, gsk_ref, gdq_ref, gdk_ref, bd_ref,
                 osq_ref, osk_ref, odq_ref, odk_ref):
    def norm(x, g, n):
        ms = _split_dot(x * x, bd_ref[:n, :n])
        return x * lax.rsqrt(ms + NORM_EPS) * g

    scale = HD ** -0.5
    osq_ref[...] = (norm(sq_ref[...], gsq_ref[...], 2 * BW) * scale).astype(BF16)
    osk_ref[...] = norm(sk_ref[...], gsk_ref[...], BW)
    odq_ref[...] = (norm(dq_ref[...], gdq_ref[...], 2 * BW) * (scale * LOG2E)).astype(BF16)
    odk_ref[...] = norm(dk_ref[...], gdk_ref[...], BW)


def _qknorm(z, gsq, gsk, gdq, gdk, bd):
    T = z.shape[0]
    tm = _pick(T, (512, 256, 128))
    wq, wk = 2 * BW, BW
    row = lambda w, off: pl.BlockSpec((tm, w), lambda i: (i, off // w))
    par = lambda w: pl.BlockSpec((1, w), lambda i: (0, 0))
    out = lambda w: pl.BlockSpec((tm, w), lambda i: (i, 0))
    return pl.pallas_call(
        _qknorm_body,
        grid=(T // tm,),
        in_specs=[row(wq, OFF_SQ), row(wk, OFF_SK), row(wq, OFF_DQ), row(wk, OFF_DK),
                  par(wq), par(wk), par(wq), par(wk),
                  pl.BlockSpec((wq, wq), lambda i: (0, 0))],
        out_specs=[out(wq), out(wk), out(wq), out(wk)],
        out_shape=[jax.ShapeDtypeStruct((T, wq), BF16), jax.ShapeDtypeStruct((T, wk), F32),
                   jax.ShapeDtypeStruct((T, wq), BF16), jax.ShapeDtypeStruct((T, wk), F32)],
        compiler_params=_cparams(("parallel",)),
        name="qk_norm",
    )(z, z, z, z, gsq, gsk, gdq, gdk, bd)


def _sb_body(q_ref, kn_ref, vn_ref, o_ref, *, tq, npair):
    qb = pl.program_id(1)
    hs = range(2 * npair)
    qs = [q_ref[:, e * LANES:(e + 1) * LANES] for e in hs]

    def tri(n):
        r = lax.broadcasted_iota(jnp.int32, (n, n), 0)
        c = lax.broadcasted_iota(jnp.int32, (n, n), 1)
        return (r > c).astype(BF16)

    def visit(state, r0, u, valid):
        kb = [kn_ref[pl.ds(r0, tq), j * LANES:(j + 1) * LANES].astype(BF16) for j in range(npair)]
        vb = [vn_ref[pl.ds(r0, tq), j * LANES:(j + 1) * LANES].astype(BF16) for j in range(npair)]
        zs = [_dot_nt(qs[e], kb[e // 2]) for e in hs]
        lgs = [_neg_softplus(z) for z in zs]
        if valid is not None:
            lgs = [jnp.where(valid, lg, 0.0) for lg in lgs]
        inners = [_split_dot(lg, u) for lg in lgs]
        ws = [jnp.exp(zs[e] + lgs[e] + inners[e] + state[e][0]) for e in hs]
        if valid is not None:
            ws = [jnp.where(valid, w, 0.0) for w in ws]
        pv = [jnp.dot(ws[e].astype(BF16), vb[e // 2], preferred_element_type=F32) for e in hs]
        return tuple((state[e][0] + inners[e][:, :1] + lgs[e][:, :1], state[e][1] + pv[e]) for e in hs)

    zero = (jnp.zeros((tq, 1), F32), jnp.zeros((tq, LANES), F32))

    r = lax.broadcasted_iota(jnp.int32, (tq, tq), 0)
    c = lax.broadcasted_iota(jnp.int32, (tq, tq), 1)
    u_new = tri(tq)
    state = visit((zero,) * (2 * npair), pl.multiple_of(qb * tq, tq), u_new, c < r)

    def live(c):
        i, st = c
        top = functools.reduce(jnp.maximum, [s[0] for s in st])
        return (i < qb) & (jnp.max(top) > SB_UNDERFLOW)

    def step(c):
        i, st = c
        return i + 1, visit(st, pl.multiple_of((qb - 1 - i) * tq, tq), u_new, None)
    _, state = lax.while_loop(live, step, (jnp.int32(0), state))

    lane = lax.broadcasted_iota(jnp.int32, (tq, LANES), 1)
    for j in range(npair):
        o_ref[:, j * LANES:(j + 1) * LANES] = jnp.where(lane < HD, state[2 * j][1], state[2 * j + 1][1])


def _sb_prompt(qn, kn, z, n_rows):
    tq = 256
    npair = 2
    wp = npair * LANES
    vblk = OFF_SV // wp
    return pl.pallas_call(
        functools.partial(_sb_body, tq=tq, npair=npair),
        grid=(NH // (2 * npair), n_rows // tq),
        in_specs=[pl.BlockSpec((tq, 2 * wp), lambda p, i: (i, p)),
                  pl.BlockSpec((n_rows, wp), lambda p, i: (0, p)),
                  pl.BlockSpec((n_rows, wp), lambda p, i: (0, vblk + p))],
        out_specs=pl.BlockSpec((tq, wp), lambda p, i: (i, p)),
        out_shape=jax.ShapeDtypeStruct((n_rows, BW), F32),
        compiler_params=_cparams(("parallel", "arbitrary")),
        name="sb_prompt",
    )(qn, kn, z)


def _q_head(q_ref, h):
    lo = h * LANES + HD * (h % 2)
    return q_ref[:, lo:lo + HD]


def _sb_sample_body(q_ref, kn_ref, vn_ref, kc_ref, vc_ref, o_ref, *, L, n_cache, tkc):
    qs = [_q_head(q_ref, h) for h in range(NH)]

    def tri(n):
        r = lax.broadcasted_iota(jnp.int32, (n, n), 0)
        c = lax.broadcasted_iota(jnp.int32, (n, n), 1)
        return (r > c).astype(BF16)

    def visit(state, kget, vget, u, valid, cached):
        ks = [kget(h).astype(BF16) for h in range(NH)]
        vs = [vget(h).astype(BF16) for h in range(NH)]
        if cached:
            zs = [jnp.dot(qs[h], ks[h], preferred_element_type=F32) for h in range(NH)]
        else:
            zs = [_dot_nt(qs[h], ks[h]) for h in range(NH)]
        lgs = [_neg_softplus(z) for z in zs]
        if valid is not None:
            lgs = [jnp.where(valid, lg, 0.0) for lg in lgs]
        inners = [_split_dot(lg, u) for lg in lgs]
        ws = [jnp.exp(zs[h] + lgs[h] + inners[h] + state[h][0]) for h in range(NH)]
        if valid is not None:
            ws = [jnp.where(valid, w, 0.0) for w in ws]
        ws = [w.astype(BF16) for w in ws]
        if cached:
            pv = [_dot_nt(ws[h], vs[h]) for h in range(NH)]
        else:
            pv = [jnp.dot(ws[h], vs[h], preferred_element_type=F32) for h in range(NH)]
        return tuple((state[h][0] + inners[h][:, :1] + lgs[h][:, :1], state[h][1] + pv[h])
                     for h in range(NH))

    zero = (jnp.zeros((L, 1), F32), jnp.zeros((L, HD), F32))
    r = lax.broadcasted_iota(jnp.int32, (L, L), 0)
    c = lax.broadcasted_iota(jnp.int32, (L, L), 1)
    state = visit((zero,) * NH, lambda h: kn_ref[:, h * HD:(h + 1) * HD],
                  lambda h: vn_ref[:, h * HD:(h + 1) * HD], tri(L), c < r, False)
    u_c = tri(tkc)

    def live(cr):
        i, st = cr
        top = functools.reduce(jnp.maximum, [s[0] for s in st])
        return (i < n_cache) & (jnp.max(top) > SB_UNDERFLOW)

    def step(cr):
        i, st = cr
        r0 = pl.multiple_of((n_cache - 1 - i) * tkc, tkc)
        return i + 1, visit(st, lambda h: kc_ref[0, 0, h, :, pl.ds(r0, tkc)],
                            lambda h: vc_ref[0, 0, h, :, pl.ds(r0, tkc)], u_c, None, True)
    _, state = lax.while_loop(live, step, (jnp.int32(0), state))
    for h in range(NH):
        o_ref[:, h * HD:(h + 1) * HD] = state[h][1]


def _sb_sample(qn, kn, z, cache_k, cache_v, layer, row_off, nb, L):
    P = cache_k.shape[4]
    tkc = _pick(P, (256, 128))
    rb = row_off // L
    cspec = pl.BlockSpec((1, 1, NH, HD, P), lambda b: (layer, b, 0, 0, 0))
    return pl.pallas_call(
        functools.partial(_sb_sample_body, L=L, n_cache=P // tkc, tkc=tkc),
        grid=(nb,),
        in_specs=[pl.BlockSpec((L, NH * LANES), lambda b: (rb + b, 0)),
                  pl.BlockSpec((L, BW), lambda b: (rb + b, 0)),
                  pl.BlockSpec((L, BW), lambda b: (rb + b, OFF_SV // BW)),
                  cspec, cspec],
        out_specs=pl.BlockSpec((L, BW), lambda b: (b, 0)),
        out_shape=jax.ShapeDtypeStruct((nb * L, BW), F32),
        compiler_params=_cparams(("parallel",)),
        name="sb_sample",
    )(qn, kn, z, cache_k, cache_v)


def _sortable(x):
    b = pltpu.bitcast(x + 0.0, jnp.int32)
    return jnp.where(b < 0, b ^ jnp.int32(0x7FFFFFFF), b)


def _select_topk(key_ref, n_blk, tkb, topk, rows):
    def count(pred):
        def body(i, acc):
            c0 = pl.multiple_of(i * tkb, tkb)
            for c in range(tkb // LANES):
                blk = key_ref[:, pl.ds(c0 + c * LANES, LANES)]
                col = c0 + c * LANES + lax.broadcasted_iota(jnp.int32, (rows, LANES), 1)
                acc = acc + jnp.where(pred(blk, col), 1.0, 0.0)
            return acc
        acc = lax.fori_loop(0, n_blk, body, jnp.zeros((rows, LANES), F32))
        return jnp.sum(acc, axis=1, keepdims=True)

    def unresolved(c):
        i, _, n_ge = c
        return (i < 32) & (jnp.max(jnp.abs(n_ge - topk)) > 0)

    def bit_step(c):
        i, thr, n_ge = c
        cand = thr + (jnp.int32(1) << (31 - i))
        n_cand = count(lambda blk, col: blk >= cand)
        take = n_cand >= topk
        return i + 1, jnp.where(take, cand, thr), jnp.where(take, n_cand, n_ge)

    thr0 = jnp.full((rows, 1), jnp.iinfo(jnp.int32).min, jnp.int32)
    n_all = jnp.full((rows, 1), 1.0, F32) * jnp.asarray(n_blk * tkb).astype(F32)
    _, thr, n_ge = lax.while_loop(unresolved, bit_step, (jnp.int32(0), thr0, n_all))
    n_col_bits = max(1, int(math.ceil(math.log2(key_ref.shape[1] + 1))))

    def tie_search():
        need = topk - count(lambda blk, col: blk > thr)

        def col_step(i, j):
            cand = j + (jnp.int32(1) << (n_col_bits - 1 - i))
            n = count(lambda blk, col: (blk == thr) & (col < cand))
            return jnp.where(n < need, cand, j)
        return lax.fori_loop(0, n_col_bits, col_step, jnp.zeros((rows, 1), jnp.int32))

    any_tie = jnp.max(jnp.abs(n_ge - topk)) > 0
    j_hi = lax.cond(any_tie, tie_search,
                    lambda: jnp.full((rows, 1), jnp.iinfo(jnp.int32).max, jnp.int32))
    return thr, j_hi


def _idx_scores(qi, wi, kblk16, k_real, keys_transposed=False):
    acc = None
    for h in range(NH):
        qh = qi[:, h * LANES:h * LANES + k_real].astype(BF16)
        if keys_transposed:
            d = jnp.dot(qh, kblk16, preferred_element_type=F32)
        else:
            d = _dot_nt(qh, kblk16)
        t = ((wi[:, h:h + 1] * (NH ** -0.5)) * (HD ** -0.5)) * jnp.maximum(d, 0.0)
        acc = t if acc is None else acc + t
    return acc


def _dsa_sel_prompt_body(qi_ref, wi_ref, ki_ref, m_ref, key_ref, *, tq, tkb, topk, n_keys):
    qb = pl.program_id(0)
    n_blk = ((qb + 1) * tq + tkb - 1) // tkb
    qi = qi_ref[...]
    wi = wi_ref[...]
    q_chunk = (qb * tq + lax.broadcasted_iota(jnp.int32, (tq, tkb), 0)) // CHUNK

    def score_blk(i, _):
        c0 = pl.multiple_of(i * tkb, tkb)
        s = _idx_scores(qi, wi, ki_ref[pl.ds(c0, tkb), :].astype(BF16), LANES)
        k_chunk = (c0 + lax.broadcasted_iota(jnp.int32, (tq, tkb), 1)) // CHUNK
        s = jnp.where(k_chunk <= q_chunk, s, -jnp.inf)
        key_ref[:, pl.ds(c0, tkb)] = _sortable(s)
        return 0
    lax.fori_loop(0, n_blk, score_blk, 0)

    thr, j_hi = _select_topk(key_ref, n_blk, tkb, topk, tq)
    neg_inf_key = _sortable(jnp.full((1, 1), -jnp.inf, F32))

    def write_blk(i, _):
        c0 = pl.multiple_of(i * tkb, tkb)
        blk = key_ref[:, pl.ds(c0, tkb)]
        col = c0 + lax.broadcasted_iota(jnp.int32, (tq, tkb), 1)
        sel = ((blk > thr) | ((blk == thr) & (col <= j_hi))) & (blk > neg_inf_key)
        m_ref[:, pl.ds(c0, tkb)] = jnp.where(sel, 0.0, NEG_BIG).astype(BF16)
        return 0
    lax.fori_loop(0, n_blk, write_blk, 0)

    def zero_blk(i, _):
        m_ref[:, pl.ds(pl.multiple_of(i * tkb, tkb), tkb)] = jnp.full((tq, tkb), NEG_BIG, BF16)
        return 0
    lax.fori_loop(n_blk, n_keys // tkb, zero_blk, 0)


def _dsa_sel_prompt(z, n_rows, topk):
    tq, tkb = 128, 512
    return pl.pallas_call(
        functools.partial(_dsa_sel_prompt_body, tq=tq, tkb=tkb, topk=topk, n_keys=n_rows),
        grid=(n_rows // tq,),
        in_specs=[pl.BlockSpec((tq, NH * LANES), lambda i: (i, OFF_QI // (NH * LANES))),
                  pl.BlockSpec((tq, LANES), lambda i: (i, OFF_WI // LANES)),
                  pl.BlockSpec((n_rows, LANES), lambda i: (0, OFF_KI // LANES))],
        out_specs=pl.BlockSpec((tq, n_rows), lambda i: (i, 0)),
        out_shape=jax.ShapeDtypeStruct((n_rows, n_rows), BF16),
        scratch_shapes=[pltpu.VMEM((tq, n_rows), jnp.int32)],
        compiler_params=_cparams(("parallel",)),
        name="dsa_select_prompt",
    )(z, z, z)


def _dsa_sel_sample_body(qi_ref, wi_ref, kin_ref, kic_ref, m_ref, key_ref, *, L, G, P, tkb, topk):
    n_c = P // tkb
    rows = G * L
    for g in range(G):
        rs = slice(g * L, (g + 1) * L)
        qi = qi_ref[rs, :]
        wi = wi_ref[rs, :]

        def score_blk(i, _, g=g, rs=rs, qi=qi, wi=wi):
            c0 = pl.multiple_of(i * tkb, tkb)
            s = _idx_scores(qi, wi, kic_ref[0, g, :, pl.ds(c0, tkb)].astype(BF16), HD, keys_transposed=True)
            key_ref[rs, pl.ds(c0, tkb)] = _sortable(s)
            return 0
        lax.fori_loop(0, n_c, score_blk, 0)

        knew = jnp.concatenate([kin_ref[rs, :], jnp.zeros((tkb - L, LANES), F32)], axis=0).astype(BF16)
        s = _idx_scores(qi, wi, knew, LANES)
        col = lax.broadcasted_iota(jnp.int32, (L, tkb), 1)
        key_ref[rs, pl.ds(P, tkb)] = _sortable(jnp.where(col < L, s, -jnp.inf))

    thr, j_hi = _select_topk(key_ref, n_c + 1, tkb, topk, rows)
    neg_inf_key = _sortable(jnp.full((1, 1), -jnp.inf, F32))

    def write_blk(i, _):
        c0 = pl.multiple_of(i * tkb, tkb)
        blk = key_ref[:, pl.ds(c0, tkb)]
        cc = c0 + lax.broadcasted_iota(jnp.int32, (rows, tkb), 1)
        sel = ((blk > thr) | ((blk == thr) & (cc <= j_hi))) & (blk > neg_inf_key)
        m_ref[:, pl.ds(c0, tkb)] = jnp.where(sel, 0.0, NEG_BIG).astype(BF16)
        return 0
    lax.fori_loop(0, n_c + 1, write_blk, 0)


def _dsa_sel_sample(z, cache_ki, layer, row_off, nb, L, topk):
    P = cache_ki.shape[3]
    tkb = _pick(P, (512, 256, 128))
    G = _pick(nb, (4, 2, 1))
    rows = G * L
    rb = row_off // rows
    return pl.pallas_call(
        functools.partial(_dsa_sel_sample_body, L=L, G=G, P=P, tkb=tkb, topk=topk),
        grid=(nb // G,),
        in_specs=[pl.BlockSpec((rows, NH * LANES), lambda b: (rb + b, OFF_QI // (NH * LANES))),
                  pl.BlockSpec((rows, LANES), lambda b: (rb + b, OFF_WI // LANES)),
                  pl.BlockSpec((rows, LANES), lambda b: (rb + b, OFF_KI // LANES)),
                  pl.BlockSpec((1, G, HD, P), lambda b: (layer, b, 0, 0))],
        out_specs=pl.BlockSpec((rows, P + tkb), lambda b: (b, 0)),
        out_shape=jax.ShapeDtypeStruct((nb * L, P + tkb), BF16),
        scratch_shapes=[pltpu.VMEM((rows, P + tkb), jnp.int32)],
        compiler_params=_cparams(("parallel",)),
        name="dsa_select_sample",
    )(z, z, z, cache_ki)


def _dsa_attn_body(q_ref, m_ref, kn_ref, vn_ref, o_ref, *, tq, tkb):
    qb = pl.program_id(1)
    n_blk = ((qb + 1) * tq + tkb - 1) // tkb
    qs = [q_ref[:, e * LANES:(e + 1) * LANES] for e in range(2)]

    def visit(state, kblk, vblk, mblk):
        kb16 = kblk.astype(BF16)
        vb16 = vblk.astype(BF16)
        bias = mblk.astype(F32)
        hs = range(2)
        ss = [_dot_nt(qs[e], kb16) + bias for e in hs]
        ms = [jnp.maximum(state[e][0], jnp.max(ss[e], axis=1, keepdims=True)) for e in hs]
        ps = [jnp.exp2(ss[e] - ms[e]) for e in hs]
        al = [jnp.exp2(state[e][0] - ms[e]) for e in hs]
        pv = [jnp.dot(ps[e].astype(BF16), vb16, preferred_element_type=F32) for e in hs]
        return tuple((ms[e], al[e] * state[e][1] + jnp.sum(ps[e], axis=1, keepdims=True),
                      al[e] * state[e][2] + pv[e]) for e in hs)

    init = (jnp.full((tq, 1), NEG_BIG, F32), jnp.zeros((tq, 1), F32), jnp.zeros((tq, LANES), F32))
    state = (init, init)

    def step(i, st):
        c0 = pl.multiple_of(i * tkb, tkb)
        return visit(st, kn_ref[pl.ds(c0, tkb), :], vn_ref[pl.ds(c0, tkb), :], m_ref[:, pl.ds(c0, tkb)])
    state = lax.fori_loop(0, n_blk, step, state)

    lane = lax.broadcasted_iota(jnp.int32, (tq, LANES), 1)
    o0 = state[0][2] / state[0][1]
    o1 = state[1][2] / state[1][1]
    o_ref[...] = jnp.where(lane < HD, o0, o1)


def _dsa_attn_prompt(qn, kn, z, mask, n_rows):
    tq, tkb = min(512, n_rows), min(1024, n_rows)
    vblk = OFF_DV // LANES
    return pl.pallas_call(
        functools.partial(_dsa_attn_body, tq=tq, tkb=tkb),
        grid=(NH // 2, n_rows // tq),
        in_specs=[pl.BlockSpec((tq, 2 * LANES), lambda p, i: (i, p)),
                  pl.BlockSpec((tq, n_rows), lambda p, i: (i, 0)),
                  pl.BlockSpec((n_rows, LANES), lambda p, i: (0, p)),
                  pl.BlockSpec((n_rows, LANES), lambda p, i: (0, vblk + p))],
        out_specs=pl.BlockSpec((tq, LANES), lambda p, i: (i, p)),
        out_shape=jax.ShapeDtypeStruct((n_rows, BW), F32),
        compiler_params=_cparams(("parallel", "arbitrary")),
        name="dsa_attn_prompt",
    )(qn, mask, kn, z)


def _dsa_attn_sample_body(q_ref, m_ref, kn_ref, vn_ref, kc_ref, vc_ref, o_ref, *, L, tkb, n_cache):
    qs = [_q_head(q_ref, h) for h in range(NH)]

    def visit(state, kget, vget, mblk, cached):
        bias = mblk.astype(F32)
        ks = [kget(h).astype(BF16) for h in range(NH)]
        vs = [vget(h).astype(BF16) for h in range(NH)]
        if cached:
            ss = [jnp.dot(qs[h], ks[h], preferred_element_type=F32) + bias for h in range(NH)]
        else:
            ss = [_dot_nt(qs[h], ks[h]) + bias for h in range(NH)]
        ms = [jnp.maximum(state[h][0], jnp.max(ss[h], axis=1, keepdims=True)) for h in range(NH)]
        ps = [jnp.exp2(ss[h] - ms[h]) for h in range(NH)]
        al = [jnp.exp2(state[h][0] - ms[h]) for h in range(NH)]
        if cached:
            pv = [_dot_nt(ps[h].astype(BF16), vs[h]) for h in range(NH)]
        else:
            pv = [jnp.dot(ps[h].astype(BF16), vs[h], preferred_element_type=F32) for h in range(NH)]
        return tuple((ms[h], al[h] * state[h][1] + jnp.sum(ps[h], axis=1, keepdims=True),
                      al[h] * state[h][2] + pv[h]) for h in range(NH))

    init = (jnp.full((L, 1), NEG_BIG, F32), jnp.zeros((L, 1), F32), jnp.zeros((L, HD), F32))

    def step(i, st):
        c0 = pl.multiple_of(i * tkb, tkb)
        return visit(st, lambda h: kc_ref[0, 0, h, :, pl.ds(c0, tkb)],
                     lambda h: vc_ref[0, 0, h, :, pl.ds(c0, tkb)], m_ref[:, pl.ds(c0, tkb)], True)
    state = lax.fori_loop(0, n_cache, step, (init,) * NH)

    pad = jnp.zeros((tkb - L, HD), F32)
    state = visit(state, lambda h: jnp.concatenate([kn_ref[:, h * HD:(h + 1) * HD], pad], axis=0),
                  lambda h: jnp.concatenate([vn_ref[:, h * HD:(h + 1) * HD], pad], axis=0),
                  m_ref[:, pl.ds(n_cache * tkb, tkb)], False)
    for h in range(NH):
        o_ref[:, h * HD:(h + 1) * HD] = state[h][2] / state[h][1]


def _dsa_attn_sample(qn, kn, z, mask, cache_k, cache_v, layer, row_off, nb, L):
    P = cache_k.shape[4]
    tkb = mask.shape[1] - P
    rb = row_off // L
    cspec = pl.BlockSpec((1, 1, NH, HD, P), lambda b: (layer, b, 0, 0, 0))
    return pl.pallas_call(
        functools.partial(_dsa_attn_sample_body, L=L, tkb=tkb, n_cache=P // tkb),
        grid=(nb,),
        in_specs=[pl.BlockSpec((L, NH * LANES), lambda b: (rb + b, 0)),
                  pl.BlockSpec((L, P + tkb), lambda b: (b, 0)),
                  pl.BlockSpec((L, BW), lambda b: (rb + b, 0)),
                  pl.BlockSpec((L, BW), lambda b: (rb + b, OFF_DV // BW)),
                  cspec, cspec],
        out_specs=pl.BlockSpec((L, BW), lambda b: (b, 0)),
        out_shape=jax.ShapeDtypeStruct((nb * L, BW), F32),
        compiler_params=_cparams(("parallel",)),
        name="dsa_attn_sample",
    )(qn, mask, kn, z, cache_k, cache_v)


def _gla_consts(C):
    nlev = int(math.log2(C))
    t = np.arange(C)
    cum = (t[:, None] >= t[None, :]).astype(np.float32)
    sel = np.zeros((nlev * C, C), np.float32)
    for l in range(nlev):
        m = 1 << l
        mid = (t // (2 * m)) * 2 * m + m
        sel[l * C + t, mid - 1] = 1.0
    return jnp.asarray(cum, BF16), jnp.asarray(sel, BF16)


def _gla_body(gq_ref, gk_ref, gv_ref, gr_ref, ga_ref, a2_ref, ab_ref, on_ref, cum_ref, sel_ref,
              s0_ref, o_ref, sout_ref, st_ref, *, C, n_chunks):
    ci = pl.program_id(1)
    nlev = int(math.log2(C))

    @pl.when(ci == 0)
    def _():
        st_ref[...] = s0_ref[0]

    x = jnp.dot(ga_ref[...].astype(BF16), a2_ref[...], preferred_element_type=F32) + ab_ref[...]
    g = _neg_softplus(-x) / GLA_GATE_NORM
    b = _split_dot_left(cum_ref[...], g)
    c_all = _split_dot_left(sel_ref[...], b)

    row = lax.broadcasted_iota(jnp.int32, (C, C), 0)
    col = lax.broadcasted_iota(jnp.int32, (C, C), 1)
    rowl = lax.broadcasted_iota(jnp.int32, (C, LANES), 0)

    hs = range(GLA_H)
    sls = [slice(h * LANES, (h + 1) * LANES) for h in hs]
    q = [gq_ref[:, sl] * (HD ** -0.5) for sl in sls]
    k = [gk_ref[:, sl] for sl in sls]
    v16 = [gv_ref[:, sl].astype(BF16) for sl in sls]
    bh = [b[:, sl] for sl in sls]
    st = [st_ref[h] for h in hs]
    o_inter = [_dot_nt((q[h] * jnp.exp(bh[h])).astype(BF16), st[h].astype(BF16)) for h in hs]
    b_last = [bh[h][C - 1:C, :] for h in hs]
    kv = [_dot_tn(v16[h], (k[h] * jnp.exp(b_last[h] - bh[h])).astype(BF16)) for h in hs]
    a = [jnp.where(row == col, jnp.sum(q[h] * k[h], axis=1, keepdims=True), 0.0) for h in hs]
    for l in range(nlev):
        m = 1 << l
        later = (rowl & m) != 0
        same_seg = (row // (2 * m)) == (col // (2 * m))
        ch = [c_all[l * C:(l + 1) * C, sl] for sl in sls]
        qe = [jnp.where(later, q[h] * jnp.exp(jnp.where(later, bh[h] - ch[h], 0.0)), 0.0) for h in hs]
        ke = [jnp.where(later, 0.0, k[h] * jnp.exp(jnp.where(later, 0.0, ch[h] - bh[h]))) for h in hs]
        al = [_dot_nt(qe[h].astype(BF16), ke[h].astype(BF16)) for h in hs]
        a = [a[h] + jnp.where(same_seg, al[h], 0.0) for h in hs]
    o = [jnp.dot(a[h].astype(BF16), v16[h], preferred_element_type=F32) + o_inter[h] for h in hs]
    for h in hs:
        st_ref[h] = st[h] * jnp.exp(b_last[h]) + kv[h]
        on = o[h] * lax.rsqrt(jnp.mean(o[h] * o[h], axis=-1, keepdims=True) + NORM_EPS) * on_ref[...]
        gr = gr_ref[:, sls[h]]
        o_ref[:, sls[h]] = on * (gr * jax.nn.sigmoid(gr))

    @pl.when(ci == n_chunks - 1)
    def _():
        sout_ref[0] = st_ref[...]


def _gla(z, a2p, abp, onorm, s0t, row_off, n_seq, L):
    C = min(CHUNK, L)
    n_chunks = L // C
    rb = row_off // C
    cum, sel = _gla_consts(C)
    blk = lambda off: pl.BlockSpec((C, BW), lambda s, c: (rb + s * n_chunks + c, off // BW))
    full = lambda a: pl.BlockSpec(a.shape, lambda s, c: (0,) * a.ndim)
    return pl.pallas_call(
        functools.partial(_gla_body, C=C, n_chunks=n_chunks),
        grid=(n_seq, n_chunks),
        in_specs=[blk(OFF_GQ), blk(OFF_GK), blk(OFF_GV), blk(OFF_GR),
                  pl.BlockSpec((C, LANES), lambda s, c: (rb + s * n_chunks + c, OFF_GA // LANES)),
                  full(a2p), full(abp), full(onorm), full(cum), full(sel),
                  pl.BlockSpec((1, GLA_H, GLA_DV, LANES), lambda s, c: (s, 0, 0, 0))],
        out_specs=[pl.BlockSpec((C, BW), lambda s, c: (s * n_chunks + c, 0)),
                   pl.BlockSpec((1, GLA_H, GLA_DV, LANES), lambda s, c: (s, 0, 0, 0))],
        out_shape=[jax.ShapeDtypeStruct((n_seq * L, BW), F32),
                   jax.ShapeDtypeStruct((n_seq, GLA_H, GLA_DV, LANES), F32)],
        scratch_shapes=[pltpu.VMEM((GLA_H, GLA_DV, LANES), F32)],
        compiler_params=_cparams(("parallel", "arbitrary")),
        name="gla",
    )(z, z, z, z, z, a2p, abp, onorm, cum, sel, s0t)


def _rw_prep_body(z_ref, head_ref, mu_ref, w0_ref, a0_ref, kkp_ref, kap_ref, rkp_ref,
                  w2_ref, a2_ref, g2_ref, seg_ref,
                  r_ref, w_ref, k_ref, v_ref, nkk_ref, kka_ref, gate_ref, bonus_ref, *, grp):
    zz = z_ref[...]
    tm, wd = zz.shape
    heads = head_ref[...]
    head_rows = jnp.concatenate([jnp.broadcast_to(heads[g:g + 1, :], (grp, wd))
                                 for g in range(tm // grp)], axis=0)
    row = lax.broadcasted_iota(jnp.int32, (tm, wd), 0)
    prev = jnp.where(row % grp == 0, head_rows, pltpu.roll(zz, 1, axis=0))
    zm = zz + mu_ref[...] * (prev - zz)
    rr = zm[:, 0:BW]
    rk = zm[:, BW:2 * BW]
    rv = zm[:, 2 * BW:3 * BW]
    lora = zm[:, 3 * BW:3 * BW + LANES]
    gl = zm[:, 3 * BW + LANES:3 * BW + 2 * LANES]
    xw = w0_ref[...] + jnp.dot(jnp.tanh(lora).astype(BF16), w2_ref[...], preferred_element_type=F32)
    wlog = _neg_softplus(-xw) - 0.5
    a = jax.nn.sigmoid(a0_ref[...] + jnp.dot(lora.astype(BF16), a2_ref[...], preferred_element_type=F32))
    gate = jnp.dot(jax.nn.sigmoid(gl).astype(BF16), g2_ref[...], preferred_element_type=F32)
    kkf = rk * kkp_ref[...]
    nrm = jnp.sqrt(_split_dot(kkf * kkf, seg_ref[...]))
    kk = kkf / jnp.maximum(nrm, 1e-12)
    kmod = rk * (1.0 + (a - 1.0) * kap_ref[...])
    coef = _split_dot(rr * kmod * rkp_ref[...], seg_ref[...])
    r_ref[...] = rr
    w_ref[...] = -jnp.exp(wlog)
    k_ref[...] = kmod
    v_ref[...] = rv
    nkk_ref[...] = -kk
    kka_ref[...] = kk * a
    gate_ref[...] = gate
    bonus_ref[...] = coef * rv


def _rw_prep(z, heads, grp, mu, w0, a0, kkp, kap, rkp, w2p, a2p, g2, seg):
    T = z.shape[0]
    tm = SUBLANES * grp
    assert T % tm == 0, (T, tm)
    W = 4 * BW
    par = lambda a: pl.BlockSpec(a.shape, lambda i: (0,) * a.ndim)
    out = pl.BlockSpec((tm, BW), lambda i: (i, 0))
    return pl.pallas_call(
        functools.partial(_rw_prep_body, grp=grp),
        grid=(T // tm,),
        in_specs=[pl.BlockSpec((tm, W), lambda i: (i, 0)), pl.BlockSpec((SUBLANES, W), lambda i: (i, 0)),
                  par(mu), par(w0), par(a0), par(kkp), par(kap), par(rkp),
                  par(w2p), par(a2p), par(g2), par(seg)],
        out_specs=[out] * 8,
        out_shape=[jax.ShapeDtypeStruct((T, BW), F32)] * 8,
        compiler_params=_cparams(("parallel",)),
        name="rwkv_prep",
    )(z, heads, mu, w0, a0, kkp, kap, rkp, w2p, a2p, g2, seg)


def _mm3(a, b):
    ah = a.astype(BF16)
    al = (a - ah.astype(F32)).astype(BF16)
    bh = b.astype(BF16)
    bl = (b - bh.astype(F32)).astype(BF16)
    d = lambda x, y: jnp.dot(x, y, preferred_element_type=F32)
    return d(ah, bh) + (d(ah, bl) + d(al, bh))


def _rw_chunk_body(r_ref, lw_ref, k_ref, v_ref, nkk_ref, kka_ref, s0_ref, o_ref, sout_ref, s_ref,
                   *, TT, L, n_tiles):
    ti = pl.program_id(0)
    npair = NH // 2
    nblk = TT // RW_BLK
    row = lax.broadcasted_iota(jnp.int32, (TT, TT), 0)
    col = lax.broadcasted_iota(jnp.int32, (TT, TT), 1)
    same = (row // RW_BLK) == (col // RW_BLK)
    strict = same & (col < row)
    incl = same & (col <= row)
    tri = jnp.where(incl, 1.0, 0.0).astype(BF16)
    last = jnp.where(same & (col % RW_BLK == RW_BLK - 1), 1.0, 0.0).astype(BF16)
    eye = jnp.where(row == col, 1.0, 0.0)
    pair_diag = (row < HD) == (col < HD)
    lo_half = lax.broadcasted_iota(jnp.int32, (TT, LANES), 1) < HD
    lo_half_b = lax.broadcasted_iota(jnp.int32, (RW_BLK, LANES), 1) < HD

    r, lw, k, v = r_ref[...], lw_ref[...], k_ref[...], v_ref[...]
    nk, ka = nkk_ref[...], kka_ref[...]
    lc = _split_dot_left(tri, lw)
    le = _split_dot_left(last, lc)
    nkt = nk * jnp.exp(lc - lw)
    rt = r * jnp.exp(lc)
    inv = jnp.exp(-lc)
    kah = ka * inv
    kh = k * inv
    rem = jnp.exp(le - lc)
    kaw = (ka * rem).astype(BF16)
    kw = (k * rem).astype(BF16)
    wb = jnp.exp(le)

    sls = [slice(p * LANES, (p + 1) * LANES) for p in range(npair)]
    v16 = [v[:, sl].astype(BF16) for sl in sls]
    heads = [(p, e) for p in range(npair) for e in range(2)]
    grams = []
    for p, e in heads:
        hm = lo_half if e == 0 else jnp.logical_not(lo_half)
        lhs = jnp.concatenate([jnp.where(hm, nkt[:, sls[p]], 0.0), jnp.where(hm, rt[:, sls[p]], 0.0)],
                              axis=0).astype(BF16)
        rhs = jnp.concatenate([kah[:, sls[p]], kh[:, sls[p]]], axis=0).astype(BF16)
        grams.append(_dot_nt(lhs, rhs))
    n1 = [jnp.where(strict, g[:TT, :TT], 0.0) for g in grams]
    bt = [jnp.where(strict, g[:TT, TT:], 0.0).astype(BF16) for g in grams]
    cmat = [jnp.where(incl, g[TT:, :TT], 0.0).astype(BF16) for g in grams]
    et = [jnp.where(incl, g[TT:, TT:], 0.0).astype(BF16) for g in grams]
    n2 = [_mm3(a, a) for a in n1]
    n4 = [_mm3(a, a) for a in n2]
    n8 = [_mm3(a, a) for a in n4]
    tinv = [eye + a for a in n1]
    for power in (n2, n4, n8):
        tinv = [t + _mm3(t, a) for t, a in zip(tinv, power)]
    vb_h = [jnp.dot(b_, v16[p], preferred_element_type=F32) for b_, (p, e) in zip(bt, heads)]
    ev_h = [jnp.dot(e_, v16[p], preferred_element_type=F32) for e_, (p, e) in zip(et, heads)]
    vb = [jnp.where(lo_half, vb_h[2 * p], vb_h[2 * p + 1]) for p in range(npair)]
    ev = [jnp.where(lo_half, ev_h[2 * p], ev_h[2 * p + 1]) for p in range(npair)]

    if L >= TT:
        @pl.when(ti == 0)
        def _():
            s_ref[...] = s0_ref[0]
        state = {(0, p): s_ref[p] for p in range(npair)}
        rounds = [[j] for j in range(nblk)]
    else:
        bps = L // RW_BLK
        state = {(q, p): s0_ref[q, p] for q in range(TT // L) for p in range(npair)}
        rounds = [[q * bps + i for q in range(TT // L)] for i in range(bps)]
    sa_rows = [[None] * nblk for _ in range(npair)]
    os_rows = [[None] * nblk for _ in range(npair)]
    for blocks in rounds:
        items = [(j, p) for j in blocks for p in range(npair)]
        seq = lambda j: (j * RW_BLK) // L if L < TT else 0
        xs = {}
        for j, p in items:
            bs = slice(j * RW_BLK, (j + 1) * RW_BLK)
            s = state[(seq(j), p)]
            s_hi = s.astype(BF16)
            s_lo = (s - s_hi.astype(F32)).astype(BF16)
            lhs = jnp.concatenate([nkt[bs, sls[p]], rt[bs, sls[p]]], axis=0).astype(BF16)
            xs[(j, p)] = (jnp.dot(lhs, s_hi, preferred_element_type=F32)
                          + jnp.dot(lhs, s_lo, preferred_element_type=F32))
        sa = {}
        for j, p in items:
            b0 = j * RW_BLK
            bs = slice(b0, b0 + RW_BLK)
            parts = [xs[(j, p)][:RW_BLK] + vb[p][bs]]
            if b0:
                parts.insert(0, jnp.zeros((b0, LANES), F32))
            if TT - b0 - RW_BLK:
                parts.append(jnp.zeros((TT - b0 - RW_BLK, LANES), F32))
            xfull = jnp.concatenate(parts, axis=0)
            sa[(j, p)] = jnp.where(lo_half_b, _mm3(tinv[2 * p][bs], xfull),
                                   _mm3(tinv[2 * p + 1][bs], xfull))
        for j, p in items:
            b0 = j * RW_BLK
            bs = slice(b0, b0 + RW_BLK)
            u = (_dot_tn(kaw[bs, sls[p]], sa[(j, p)].astype(BF16)) + _dot_tn(kw[bs, sls[p]], v16[p][bs]))
            wcol = jnp.broadcast_to(wb[b0:b0 + 1, sls[p]], (LANES, LANES)).T
            state[(seq(j), p)] = wcol * state[(seq(j), p)] + jnp.where(pair_diag, u, 0.0)
            sa_rows[p][j] = sa[(j, p)]
            os_rows[p][j] = xs[(j, p)][RW_BLK:]
    if L < TT:
        for (q, p), s in state.items():
            sout_ref[q, p] = s

    for p in range(npair):
        sa_t = jnp.concatenate(sa_rows[p], axis=0).astype(BF16)
        o_sa = jnp.where(lo_half, jnp.dot(cmat[2 * p], sa_t, preferred_element_type=F32),
                         jnp.dot(cmat[2 * p + 1], sa_t, preferred_element_type=F32))
        o_ref[:, sls[p]] = jnp.concatenate(os_rows[p], axis=0) + o_sa + ev[p]
    if L >= TT:
        for p in range(npair):
            s_ref[p] = state[(0, p)]

        @pl.when(ti == n_tiles - 1)
        def _():
            for p in range(npair):
                sout_ref[0, p] = state[(0, p)]


def _rw_chunk(r, lw, k, v, nkk, kka, s0, row_off, n_seq, L):
    TT = LANES
    assert L % RW_BLK == 0 and (L % TT == 0 or TT % L == 0)
    n_tok = n_seq * L
    n_tiles = n_tok // TT
    rb = row_off // TT
    ns = max(1, TT // L)
    tok = pl.BlockSpec((TT, BW), lambda i: (rb + i, 0))
    if ns == 1:
        sspec = pl.BlockSpec((1, NH // 2, LANES, LANES), lambda i: (0, 0, 0, 0))
    else:
        sspec = pl.BlockSpec((ns, NH // 2, LANES, LANES), lambda i: (i, 0, 0, 0))
    return pl.pallas_call(
        functools.partial(_rw_chunk_body, TT=TT, L=L, n_tiles=n_tiles),
        grid=(n_tiles,),
        in_specs=[tok] * 6 + [sspec],
        out_specs=[pl.BlockSpec((TT, BW), lambda i: (i, 0)), sspec],
        out_shape=[jax.ShapeDtypeStruct((n_tok, BW), F32),
                   jax.ShapeDtypeStruct(s0.shape, F32)],
        scratch_shapes=[pltpu.VMEM((NH // 2, LANES, LANES), F32)],
        compiler_params=_cparams(("arbitrary",)),
        name="rwkv_chunk",
    )(r, lw, k, v, nkk, kka, s0)


def _rw_post_body(o_ref, bonus_ref, gate_ref, lnw_ref, lnb_ref, seg_ref, out_ref):
    o = o_ref[...]
    mu = _split_dot(o, seg_ref[...]) * (1.0 / HD)
    d = o - mu
    var = _split_dot(d * d, seg_ref[...]) * (1.0 / HD)
    on = d * lax.rsqrt(var + RW_LN_EPS) * lnw_ref[...] + lnb_ref[...]
    out_ref[...] = (on + bonus_ref[...]) * gate_ref[...]


def _rw_post(o, bonus, gate, lnw, lnb, seg):
    T = o.shape[0]
    tm = _pick(T, (512, 256, 128))
    blk = pl.BlockSpec((tm, BW), lambda i: (i, 0))
    par = lambda a: pl.BlockSpec(a.shape, lambda i: (0,) * a.ndim)
    return pl.pallas_call(
        _rw_post_body,
        grid=(T // tm,),
        in_specs=[blk, blk, blk, par(lnw), par(lnb), par(seg)],
        out_specs=blk,
        out_shape=jax.ShapeDtypeStruct((T, BW), F32),
        compiler_params=_cparams(("parallel",)),
        name="rwkv_post",
    )(o, bonus, gate, lnw, lnb, seg)


def _heads_alt(w):
    lead = w.shape[:-1]
    w4 = w.reshape(*lead, NH // 2, 2, HD)
    zero = jnp.zeros((*lead, NH // 2, HD), w.dtype)
    ev = jnp.concatenate([w4[..., 0, :], zero], -1)
    od = jnp.concatenate([zero, w4[..., 1, :]], -1)
    return jnp.stack([ev, od], -2).reshape(*lead, NH * LANES)


def _heads_lo(w, nh):
    lead = w.shape[:-1]
    w3 = w.reshape(*lead, nh, HD)
    return jnp.concatenate([w3, jnp.zeros_like(w3)], -1).reshape(*lead, nh * LANES)


def _padc(w, n):
    return jnp.pad(w, [(0, 0)] * (w.ndim - 1) + [(0, n - w.shape[-1])])


def _rw_reorder(a):
    o = 0
    parts = {}
    for name, wd in (("rr", BW), ("wl", RW_LORA), ("rk", BW), ("rv", BW), ("al", RW_LORA), ("gl", RW_GATE_LORA)):
        parts[name] = a[..., o:o + wd]
        o += wd
    return jnp.concatenate([parts[n] for n in ("rr", "rk", "rv", "wl", "al", "gl")], axis=-1)


def _rw_unorder(a):
    rr, rk, rv = a[..., 0:BW], a[..., BW:2 * BW], a[..., 2 * BW:3 * BW]
    wl = a[..., 3 * BW:3 * BW + RW_LORA]
    al = a[..., 3 * BW + RW_LORA:3 * BW + 2 * RW_LORA]
    gl = a[..., 3 * BW + 2 * RW_LORA:RW_COLS]
    return jnp.concatenate([rr, wl, rk, rv, al, gl], axis=-1)


def _relayout_w_in(w, d_model):
    w = w.astype(BF16)
    o = 0

    def take(n):
        nonlocal o
        s = w[..., o:o + n]
        o += n
        return s
    sq, sk, sv = take(BW), take(BW), take(BW)
    dq, dk, dv, qi, ki, wi = take(BW), take(BW), take(BW), take(BW), take(HD), take(NH)
    gq, gk = take(GLA_H * HD), take(GLA_H * HD)
    gv, gr, ga = take(BW), take(BW), take(GLA_LOWRANK)
    rw = take(RW_COLS)
    gate = take(4 * d_model)
    cols = [_padc(_rw_reorder(rw), 4 * BW), _heads_alt(sq), sk, sv, _heads_alt(dq), dk, dv,
            _heads_lo(qi, NH), _heads_lo(gq, GLA_H), _heads_lo(gk, GLA_H), gv, gr,
            _padc(ki, LANES), _padc(wi, LANES), _padc(ga, 2 * LANES), gate]
    return jnp.concatenate(cols, axis=-1)


def _relayout_ffn(w_up, w_down, tf):
    f = w_up.shape[-1] // 2
    fp = -(-f // tf) * tf
    wup = jnp.concatenate([_padc(w_up[..., :f].astype(BF16), fp), _padc(w_up[..., f:].astype(BF16), fp)],
                          axis=-1)
    wdn = jnp.pad(w_down.astype(BF16), ((0, 0), (0, fp - f), (0, 0)))
    return wup, wdn


def _gain_alt(g):
    return jnp.tile(g, 2 * NH)[None, :]


def kernel(x_prompt, x_sample, cache_sb_k, cache_sb_v, cache_dsa_k, cache_dsa_v, cache_dsa_kidx, state_gla, state_rwkv, state_rwkv_shift, ffn1_norm, ffn1_up, ffn1_down, mix_norm, w_in, sb_qnorm, sb_knorm, dsa_qnorm, dsa_knorm, gla_a2, gla_ab, gla_onorm, rwkv_mu, rwkv_w0, rwkv_w2, rwkv_a0, rwkv_a2, rwkv_g2, rwkv_kk, rwkv_ka, rwkv_rk, rwkv_lnw, rwkv_lnb, w_branch, w_out, ffn2_norm, ffn2_up, ffn2_down):
    depth = w_in.shape[0]
    bp, lp, d_model = x_prompt.shape
    nb, ls, _ = x_sample.shape
    assert bp == 1, "prompt group is one sequence"
    tp = bp * lp
    ts = nb * ls
    past = cache_sb_k.shape[2]
    to_pos_minor = lambda c: jnp.transpose(c, (0, 1, 3, 4, 2))
    ck_sb, cv_sb = to_pos_minor(cache_sb_k), to_pos_minor(cache_sb_v)
    ck_dsa, cv_dsa = to_pos_minor(cache_dsa_k), to_pos_minor(cache_dsa_v)
    cki = jnp.transpose(cache_dsa_kidx, (0, 1, 3, 2))
    tf = 512

    x = jnp.concatenate([x_prompt.reshape(tp, d_model), x_sample.reshape(ts, d_model)], axis=0)

    bd = jnp.asarray(np.kron(np.eye(2 * NH), np.full((HD, HD), 1.0 / HD)), BF16)
    seg = jnp.asarray(np.kron(np.eye(NH), np.ones((HD, HD))), BF16)
    topk_p = min(TOPK_MAX, lp // 4)
    topk_s = min(TOPK_MAX, (past + ls) // 4)

    new_p = [[] for _ in range(8)]
    new_s = [[] for _ in range(8)]
    wup1, wdn1 = _relayout_ffn(ffn1_up, ffn1_down, tf)
    wup2, wdn2 = _relayout_ffn(ffn2_up, ffn2_down, tf)
    win = _relayout_w_in(w_in, d_model)
    wbr = w_branch.astype(BF16)
    wout = w_out.astype(BF16)
    for i in range(depth):
        x = _ffn(x, ffn1_norm[:, None], wup1, wdn1, tf, i)
        z = _proj(x, mix_norm[:, None], win, i)

        sqn, skn, dqn, dkn = _qknorm(z, _gain_alt(sb_qnorm[i]), jnp.tile(sb_knorm[i], NH)[None],
                                     _gain_alt(dsa_qnorm[i]), jnp.tile(dsa_knorm[i], NH)[None], bd)
        o_sb = (_sb_prompt(sqn, skn, z, tp), _sb_sample(sqn, skn, z, ck_sb, cv_sb, i, tp, nb, ls))
        mask_p = _dsa_sel_prompt(z, tp, topk_p)
        mask_s = _dsa_sel_sample(z, cki, i, tp, nb, ls, topk_s)
        o_dsa = (_dsa_attn_prompt(dqn, dkn, z, mask_p, tp),
                 _dsa_attn_sample(dqn, dkn, z, mask_s, ck_dsa, cv_dsa, i, tp, nb, ls))

        a2p = jnp.pad(_heads_lo(gla_a2[i], GLA_H), ((0, LANES - GLA_LOWRANK), (0, 0))).astype(BF16)
        abp = _heads_lo(gla_ab[i][None], GLA_H)
        onorm = gla_onorm[i][None]

        def gla_state_in(s):
            return jnp.pad(jnp.swapaxes(s, 2, 3), ((0, 0), (0, 0), (0, 0), (0, LANES - HD)))

        def gla_state_out(s):
            return jnp.swapaxes(s[..., :HD], 2, 3)
        og_p, sg_p = _gla(z, a2p, abp, onorm, jnp.zeros((bp, GLA_H, GLA_DV, LANES), F32), 0, bp, lp)
        og_s, sg_s = _gla(z, a2p, abp, onorm, gla_state_in(state_gla[i]), tp, nb, ls)
        o_gla = (og_p, og_s)

        zrw = z[:, :RW_COLS]
        shift_s = _rw_reorder(state_rwkv_shift[i])
        heads = jnp.concatenate([jnp.zeros((1, 4 * BW), F32), z[ls - 1:tp - 1:ls, :4 * BW],
                                 _padc(shift_s.reshape(nb, RW_COLS), 4 * BW)], axis=0)
        mu = _padc(_rw_reorder(rwkv_mu[i])[None], 4 * BW)
        w2p = jnp.concatenate([rwkv_w2[i], jnp.zeros_like(rwkv_a2[i])], axis=0).astype(BF16)
        a2q = jnp.concatenate([jnp.zeros_like(rwkv_w2[i]), rwkv_a2[i]], axis=0).astype(BF16)
        r_, w_, k_, v_, nkk_, kka_, gate_, bonus_ = _rw_prep(
            z, heads, ls, mu, rwkv_w0[i][None], rwkv_a0[i][None], rwkv_kk[i][None], rwkv_ka[i][None],
            rwkv_rk[i][None], w2p, a2q, rwkv_g2[i].astype(BF16), seg)

        def rw_state_in(s):
            n = s.shape[0]
            st = jnp.swapaxes(s, 2, 3).reshape(n, NH // 2, 2, HD, HD)
            zero = jnp.zeros((n, NH // 2, HD, HD), s.dtype)
            return jnp.concatenate([jnp.concatenate([st[:, :, 0], zero], axis=-1),
                                    jnp.concatenate([zero, st[:, :, 1]], axis=-1)], axis=-2)

        def rw_state_out(s):
            n = s.shape[0]
            st = jnp.stack([s[:, :, :HD, :HD], s[:, :, HD:, HD:]], axis=2)
            return jnp.swapaxes(st, 3, 4).reshape(n, NH, HD, HD)
        o_rp, sr_p = _rw_chunk(r_, w_, k_, v_, nkk_, kka_, jnp.zeros((bp, NH // 2, LANES, LANES), F32), 0, bp, lp)
        o_rs, sr_s = _rw_chunk(r_, w_, k_, v_, nkk_, kka_, rw_state_in(state_rwkv[i]), tp, nb, ls)
        o_r = jnp.concatenate([o_rp, o_rs], axis=0)
        o_rw = _rw_post(o_r, bonus_, gate_, rwkv_lnw[i][None], rwkv_lnb[i][None], seg)

        x = _merge(x, z, (o_sb, o_dsa, o_gla, o_rw), wbr, wout, i, tp)
        x = _ffn(x, ffn2_norm[:, None], wup2, wdn2, tf, i)

        sv = z[:, OFF_SV:OFF_SV + BW]
        dv = z[:, OFF_DV:OFF_DV + BW]
        ki = z[:, OFF_KI:OFF_KI + HD]
        for dst, (lo, n, l) in ((new_p, (0, bp, lp)), (new_s, (tp, nb, ls))):
            hi = lo + n * l
            dst[0].append(skn[lo:hi].reshape(n, l, NH, HD))
            dst[1].append(sv[lo:hi].reshape(n, l, NH, HD))
            dst[2].append(dkn[lo:hi].reshape(n, l, NH, HD))
            dst[3].append(dv[lo:hi].reshape(n, l, NH, HD))
            dst[4].append(ki[lo:hi].reshape(n, l, HD))
            dst[7].append(_rw_unorder(zrw[lo:hi].reshape(n, l, RW_COLS)[:, -1:]))
        new_p[5].append(gla_state_out(sg_p))
        new_s[5].append(gla_state_out(sg_s))
        new_p[6].append(rw_state_out(sr_p))
        new_s[6].append(rw_state_out(sr_s))

    outs_p = [jnp.stack(l, axis=0) for l in new_p]
    outs_s = [jnp.stack(l, axis=0) for l in new_s]
    y_p = x[:tp].reshape(bp, lp, d_model)
    y_s = x[tp:].reshape(nb, ls, d_model)
    return (y_p, y_s, *outs_p, *outs_s)
```

```python
import functools
import math

import jax
import jax.numpy as jnp
import numpy as np
from jax import lax
from jax.experimental import pallas as pl
from jax.experimental.pallas import tpu as pltpu

F32 = jnp.float32
BF16 = jnp.bfloat16

LANES = 128
SUBLANES = 8
HD = 64
NH = 8
BW = NH * HD
GLA_H = 4
GLA_DV = 128
GLA_LOWRANK = 16
GLA_GATE_NORM = 16.0
CHUNK = 64
TOPK_MAX = 256
RW_LORA = 64
RW_GATE_LORA = 128
RW_BLK = 16
RW_COLS = 3 * BW + 2 * RW_LORA + RW_GATE_LORA
NORM_EPS = 1e-6
RW_LN_EPS = 64e-5
NEG_BIG = -1e30
LOG2E = math.log2(math.e)
SB_UNDERFLOW = -104.0
VMEM_LIMIT = 56 * 1024 * 1024
OFF_RW = 0
OFF_SQ = 2048
OFF_SK = 3072
OFF_SV = 3584
OFF_DQ = 4096
OFF_DK = 5120
OFF_DV = 5632
OFF_QI = 6144
OFF_GQ = 7168
OFF_GK = 7680
OFF_GV = 8192
OFF_GR = 8704
OFF_KI = 9216
OFF_WI = 9344
OFF_GA = 9472
OFF_GATE = 9728


def _cparams(sem):
    return pltpu.CompilerParams(dimension_semantics=sem, vmem_limit_bytes=VMEM_LIMIT)


def _pick(n, prefs):
    for p in prefs:
        if n % p == 0:
            return p
    raise ValueError(f"no tile for {n} in {prefs}")


def _split_dot(a, b):
    hi = a.astype(BF16)
    lo = (a - hi.astype(F32)).astype(BF16)
    return (jnp.dot(hi, b, preferred_element_type=F32)
            + jnp.dot(lo, b, preferred_element_type=F32))


def _split_dot_left(a, b):
    hi = b.astype(BF16)
    lo = (b - hi.astype(F32)).astype(BF16)
    return (jnp.dot(a, hi, preferred_element_type=F32)
            + jnp.dot(a, lo, preferred_element_type=F32))


def _dot_nt(a, b):
    return lax.dot_general(a, b, (((1,), (1,)), ((), ())), preferred_element_type=F32)


def _dot_tn(a, b):
    return lax.dot_general(a, b, (((0,), (0,)), ((), ())), preferred_element_type=F32)


def _neg_softplus(z):
    return -(jnp.maximum(z, 0.0) + jnp.log(1.0 + jnp.exp(-jnp.abs(z))))


def _ffn_body(x_ref, g_ref, wg_ref, wu_ref, wdn_ref, o_ref, xn_ref):
    @pl.when(pl.program_id(1) == 0)
    def _():
        x = x_ref[...]
        ms = jnp.mean(x * x, axis=-1, keepdims=True)
        xn_ref[...] = (x * lax.rsqrt(ms + NORM_EPS) * g_ref[0]).astype(BF16)
        o_ref[...] = x

    xn = xn_ref[...]
    gate = jnp.dot(xn, wg_ref[0], preferred_element_type=F32)
    up = jnp.dot(xn, wu_ref[0], preferred_element_type=F32)
    act = (gate * jax.nn.sigmoid(gate) * up).astype(BF16)
    o_ref[...] += 0.5 * jnp.dot(act, wdn_ref[0], preferred_element_type=F32)


def _ffn(x, g, wup, wdn, tf, layer):
    T, D = x.shape
    nf = wdn.shape[1] // tf
    tm = _pick(T, (512, 256, 128))
    return pl.pallas_call(
        _ffn_body,
        grid=(T // tm, nf),
        in_specs=[pl.BlockSpec((tm, D), lambda i, j: (i, 0)),
                  pl.BlockSpec((1, 1, D), lambda i, j: (layer, 0, 0)),
                  pl.BlockSpec((1, D, tf), lambda i, j: (layer, 0, j)),
                  pl.BlockSpec((1, D, tf), lambda i, j: (layer, 0, nf + j)),
                  pl.BlockSpec((1, tf, D), lambda i, j: (layer, j, 0))],
        out_specs=pl.BlockSpec((tm, D), lambda i, j: (i, 0)),
        out_shape=jax.ShapeDtypeStruct((T, D), F32),
        scratch_shapes=[pltpu.VMEM((tm, D), BF16)],
        compiler_params=_cparams(("parallel", "arbitrary")),
        name="ffn",
    )(x, g, wup, wup, wdn)


def _proj_body(x_ref, g_ref, w_ref, o_ref, xn_ref):
    @pl.when(pl.program_id(1) == 0)
    def _():
        x = x_ref[...]
        ms = jnp.mean(x * x, axis=-1, keepdims=True)
        xn_ref[...] = (x * lax.rsqrt(ms + NORM_EPS) * g_ref[0]).astype(BF16)

    o_ref[...] = jnp.dot(xn_ref[...], w_ref[0], preferred_element_type=F32)


def _proj(x, g, w, layer):
    T, D = x.shape
    n = w.shape[2]
    tm = _pick(T, (1024, 512, 256, 128))
    tn = _pick(n, (1280, 512))
    return pl.pallas_call(
        _proj_body,
        grid=(T // tm, n // tn),
        in_specs=[pl.BlockSpec((tm, D), lambda i, j: (i, 0)),
                  pl.BlockSpec((1, 1, D), lambda i, j: (layer, 0, 0)),
                  pl.BlockSpec((1, D, tn), lambda i, j: (layer, 0, j))],
        out_specs=pl.BlockSpec((tm, tn), lambda i, j: (i, j)),
        out_shape=jax.ShapeDtypeStruct((T, n), F32),
        scratch_shapes=[pltpu.VMEM((tm, D), BF16)],
        compiler_params=_cparams(("parallel", "arbitrary")),
        name="in_proj",
    )(x, g, w)


def _merge_body(*refs, split, n_prompt_tiles):
    refs = list(refs)
    on_prompt = pl.program_id(0) < n_prompt_tiles
    branches = []
    for two in split:
        if two:
            bp, bs = refs.pop(0), refs.pop(0)
            branches.append(jnp.where(on_prompt, bp[...], bs[...]))
        else:
            branches.append(refs.pop(0)[...])
    nc = (len(refs) - 4) // 4
    g_refs, (wb_ref, wo_ref, x_ref, o_ref) = refs[:4 * nc], refs[4 * nc:]
    b16 = [b.astype(BF16) for b in branches]
    tc = g_refs[0].shape[1]
    cols = []
    for c in range(nc):
        acc = None
        for n in range(4):
            p = jnp.dot(b16[n], wb_ref[0, n, :, c * tc:(c + 1) * tc], preferred_element_type=F32)
            t = jax.nn.sigmoid(g_refs[n * nc + c][...]) * p
            acc = t if acc is None else acc + t
        cols.append(acc.astype(BF16))
    o_ref[...] = x_ref[...] + jnp.dot(jnp.concatenate(cols, axis=1), wo_ref[0], preferred_element_type=F32)


def _merge(x, z, branches, wb, wo, layer, n_prompt):
    T, D = x.shape
    tm = _pick(math.gcd(T, n_prompt), (256, 128))
    npt = n_prompt // tm
    tc = 512
    nc = D // tc
    gate_blk = OFF_GATE // tc
    split = tuple(isinstance(b, tuple) for b in branches)
    b_args, b_specs = [], []
    for b in branches:
        if isinstance(b, tuple):
            b_args += list(b)
            b_specs += [pl.BlockSpec((tm, BW), lambda i: (jnp.minimum(i, npt - 1), 0)),
                        pl.BlockSpec((tm, BW), lambda i: (jnp.maximum(i - npt, 0), 0))]
        else:
            b_args.append(b)
            b_specs.append(pl.BlockSpec((tm, BW), lambda i: (i, 0)))
    gspecs = [pl.BlockSpec((tm, tc), functools.partial(lambda i, k: (i, gate_blk + k), k=k))
              for k in range(4 * nc)]
    resident = pl.Buffered(1)
    return pl.pallas_call(
        functools.partial(_merge_body, split=split, n_prompt_tiles=npt),
        grid=(T // tm,),
        in_specs=b_specs + gspecs + [
            pl.BlockSpec((1, 4, BW, D), lambda i: (layer, 0, 0, 0), pipeline_mode=resident),
            pl.BlockSpec((1, D, D), lambda i: (layer, 0, 0), pipeline_mode=resident),
            pl.BlockSpec((tm, D), lambda i: (i, 0))],
        out_specs=pl.BlockSpec((tm, D), lambda i: (i, 0)),
        out_shape=jax.ShapeDtypeStruct((T, D), F32),
        compiler_params=_cparams(("parallel",)),
        name="merge",
    )(*b_args, *([z] * (4 * nc)), wb, wo, x)


def _qknorm_body(sq_ref, sk_ref, dq_ref, dk_ref, gsq_ref, gsk_ref, gdq_ref, gdk_ref, bd_ref,
                 osq_ref, osk_ref, odq_ref, odk_ref):
    bd = bd_ref[:LANES, :LANES]

    def norm(x, g, n):
        x2 = x * x
        ms = jnp.concatenate([_split_dot(x2[:, c:c + LANES], bd) for c in range(0, n, LANES)], axis=1)
        return x * lax.rsqrt(ms + NORM_EPS) * g

    scale = HD ** -0.5
    osq_ref[...] = (norm(sq_ref[...], gsq_ref[...], 2 * BW) * scale).astype(BF16)
    osk_ref[...] = norm(sk_ref[...], gsk_ref[...], BW)
    odq_ref[...] = (norm(dq_ref[...], gdq_ref[...], 2 * BW) * (scale * LOG2E)).astype(BF16)
    odk_ref[...] = norm(dk_ref[...], gdk_ref[...], BW)


def _qknorm(z, gsq, gsk, gdq, gdk, bd):
    T = z.shape[0]
    tm = _pick(T, (512, 256, 128))
    wq, wk = 2 * BW, BW
    row = lambda w, off: pl.BlockSpec((tm, w), lambda i: (i, off // w))
    par = lambda w: pl.BlockSpec((1, w), lambda i: (0, 0))
    out = lambda w: pl.BlockSpec((tm, w), lambda i: (i, 0))
    return pl.pallas_call(
        _qknorm_body,
        grid=(T // tm,),
        in_specs=[row(wq, OFF_SQ), row(wk, OFF_SK), row(wq, OFF_DQ), row(wk, OFF_DK),
                  par(wq), par(wk), par(wq), par(wk),
                  pl.BlockSpec(bd.shape, lambda i: (0, 0))],
        out_specs=[out(wq), out(wk), out(wq), out(wk)],
        out_shape=[jax.ShapeDtypeStruct((T, wq), BF16), jax.ShapeDtypeStruct((T, wk), F32),
                   jax.ShapeDtypeStruct((T, wq), BF16), jax.ShapeDtypeStruct((T, wk), F32)],
        compiler_params=_cparams(("parallel",)),
        name="qk_norm",
    )(z, z, z, z, gsq, gsk, gdq, gdk, bd)


def _sb_body(q_ref, kn_ref, vn_ref, o_ref, *, tq, npair):
    qb = pl.program_id(1)
    hs = range(2 * npair)
    qs = [q_ref[:, e * LANES:(e + 1) * LANES] for e in hs]

    def tri(n):
        r = lax.broadcasted_iota(jnp.int32, (n, n), 0)
        c = lax.broadcasted_iota(jnp.int32, (n, n), 1)
        return (r > c).astype(BF16)

    def visit(state, r0, u, valid):
        kb = [kn_ref[pl.ds(r0, tq), j * LANES:(j + 1) * LANES].astype(BF16) for j in range(npair)]
        vb = [vn_ref[pl.ds(r0, tq), j * LANES:(j + 1) * LANES].astype(BF16) for j in range(npair)]
        zs = [_dot_nt(qs[e], kb[e // 2]) for e in hs]
        lgs = [_neg_softplus(z) for z in zs]
        if valid is not None:
            lgs = [jnp.where(valid, lg, 0.0) for lg in lgs]
        inners = [_split_dot(lg, u) for lg in lgs]
        ws = [jnp.exp(zs[e] + lgs[e] + inners[e] + state[e][0]) for e in hs]
        if valid is not None:
            ws = [jnp.where(valid, w, 0.0) for w in ws]
        pv = [jnp.dot(ws[e].astype(BF16), vb[e // 2], preferred_element_type=F32) for e in hs]
        return tuple((state[e][0] + inners[e][:, :1] + lgs[e][:, :1], state[e][1] + pv[e]) for e in hs)

    zero = (jnp.zeros((tq, 1), F32), jnp.zeros((tq, LANES), F32))

    r = lax.broadcasted_iota(jnp.int32, (tq, tq), 0)
    c = lax.broadcasted_iota(jnp.int32, (tq, tq), 1)
    u_new = tri(tq)
    state = visit((zero,) * (2 * npair), pl.multiple_of(qb * tq, tq), u_new, c < r)

    def live(c):
        i, st = c
        top = functools.reduce(jnp.maximum, [s[0] for s in st])
        return (i < qb) & (jnp.max(top) > SB_UNDERFLOW)

    def step(c):
        i, st = c
        return i + 1, visit(st, pl.multiple_of((qb - 1 - i) * tq, tq), u_new, None)
    _, state = lax.while_loop(live, step, (jnp.int32(0), state))

    lane = lax.broadcasted_iota(jnp.int32, (tq, LANES), 1)
    for j in range(npair):
        o_ref[:, j * LANES:(j + 1) * LANES] = jnp.where(lane < HD, state[2 * j][1], state[2 * j + 1][1])


def _sb_prompt(qn, kn, z, n_rows):
    tq = 256
    npair = 2
    wp = npair * LANES
    vblk = OFF_SV // wp
    return pl.pallas_call(
        functools.partial(_sb_body, tq=tq, npair=npair),
        grid=(NH // (2 * npair), n_rows // tq),
        in_specs=[pl.BlockSpec((tq, 2 * wp), lambda p, i: (i, p)),
                  pl.BlockSpec((n_rows, wp), lambda p, i: (0, p)),
                  pl.BlockSpec((n_rows, wp), lambda p, i: (0, vblk + p))],
        out_specs=pl.BlockSpec((tq, wp), lambda p, i: (i, p)),
        out_shape=jax.ShapeDtypeStruct((n_rows, BW), F32),
        compiler_params=_cparams(("parallel", "arbitrary")),
        name="sb_prompt",
    )(qn, kn, z)


def _q_head(q_ref, h):
    lo = h * LANES + HD * (h % 2)
    return q_ref[:, lo:lo + HD]


def _sb_sample_body(q_ref, kn_ref, vn_ref, kc_ref, vc_ref, o_ref, *, L, n_cache, tkc):
    qs = [_q_head(q_ref, h) for h in range(NH)]

    def tri(n):
        r = lax.broadcasted_iota(jnp.int32, (n, n), 0)
        c = lax.broadcasted_iota(jnp.int32, (n, n), 1)
        return (r > c).astype(BF16)

    def visit(state, kget, vget, u, valid, cached):
        ks = [kget(h).astype(BF16) for h in range(NH)]
        vs = [vget(h).astype(BF16) for h in range(NH)]
        if cached:
            zs = [jnp.dot(qs[h], ks[h], preferred_element_type=F32) for h in range(NH)]
        else:
            zs = [_dot_nt(qs[h], ks[h]) for h in range(NH)]
        lgs = [_neg_softplus(z) for z in zs]
        if valid is not None:
            lgs = [jnp.where(valid, lg, 0.0) for lg in lgs]
        inners = [_split_dot(lg, u) for lg in lgs]
        ws = [jnp.exp(zs[h] + lgs[h] + inners[h] + state[h][0]) for h in range(NH)]
        if valid is not None:
            ws = [jnp.where(valid, w, 0.0) for w in ws]
        ws = [w.astype(BF16) for w in ws]
        if cached:
            pv = [_dot_nt(ws[h], vs[h]) for h in range(NH)]
        else:
            pv = [jnp.dot(ws[h], vs[h], preferred_element_type=F32) for h in range(NH)]
        return tuple((state[h][0] + inners[h][:, :1] + lgs[h][:, :1], state[h][1] + pv[h])
                     for h in range(NH))

    zero = (jnp.zeros((L, 1), F32), jnp.zeros((L, HD), F32))
    r = lax.broadcasted_iota(jnp.int32, (L, L), 0)
    c = lax.broadcasted_iota(jnp.int32, (L, L), 1)
    state = visit((zero,) * NH, lambda h: kn_ref[:, h * HD:(h + 1) * HD],
                  lambda h: vn_ref[:, h * HD:(h + 1) * HD], tri(L), c < r, False)
    u_c = tri(tkc)

    def live(cr):
        i, st = cr
        top = functools.reduce(jnp.maximum, [s[0] for s in st])
        return (i < n_cache) & (jnp.max(top) > SB_UNDERFLOW)

    def step(cr):
        i, st = cr
        r0 = pl.multiple_of((n_cache - 1 - i) * tkc, tkc)
        return i + 1, visit(st, lambda h: kc_ref[0, 0, h, :, pl.ds(r0, tkc)],
                            lambda h: vc_ref[0, 0, h, :, pl.ds(r0, tkc)], u_c, None, True)
    _, state = lax.while_loop(live, step, (jnp.int32(0), state))
    for h in range(NH):
        o_ref[:, h * HD:(h + 1) * HD] = state[h][1]


def _sb_sample(qn, kn, z, cache_k, cache_v, layer, row_off, nb, L):
    P = cache_k.shape[4]
    tkc = _pick(P, (256, 128))
    rb = row_off // L
    cspec = pl.BlockSpec((1, 1, NH, HD, P), lambda b: (layer, b, 0, 0, 0))
    return pl.pallas_call(
        functools.partial(_sb_sample_body, L=L, n_cache=P // tkc, tkc=tkc),
        grid=(nb,),
        in_specs=[pl.BlockSpec((L, NH * LANES), lambda b: (rb + b, 0)),
                  pl.BlockSpec((L, BW), lambda b: (rb + b, 0)),
                  pl.BlockSpec((L, BW), lambda b: (rb + b, OFF_SV // BW)),
                  cspec, cspec],
        out_specs=pl.BlockSpec((L, BW), lambda b: (b, 0)),
        out_shape=jax.ShapeDtypeStruct((nb * L, BW), F32),
        compiler_params=_cparams(("parallel",)),
        name="sb_sample",
    )(qn, kn, z, cache_k, cache_v)


def _sortable(x):
    b = pltpu.bitcast(x + 0.0, jnp.int32)
    return jnp.where(b < 0, b ^ jnp.int32(0x7FFFFFFF), b)


def _select_topk(key_ref, n_blk, tkb, topk, rows):
    def count(pred):
        def body(i, acc):
            c0 = pl.multiple_of(i * tkb, tkb)
            for c in range(tkb // LANES):
                blk = key_ref[:, pl.ds(c0 + c * LANES, LANES)]
                col = c0 + c * LANES + lax.broadcasted_iota(jnp.int32, (rows, LANES), 1)
                acc = acc + jnp.where(pred(blk, col), 1.0, 0.0)
            return acc
        acc = lax.fori_loop(0, n_blk, body, jnp.zeros((rows, LANES), F32))
        return jnp.sum(acc, axis=1, keepdims=True)

    def unresolved(c):
        i, _, n_ge = c
        return (i < 32) & (jnp.max(jnp.abs(n_ge - topk)) > 0)

    def bit_step(c):
        i, thr, n_ge = c
        cand = thr + (jnp.int32(1) << (31 - i))
        n_cand = count(lambda blk, col: blk >= cand)
        take = n_cand >= topk
        return i + 1, jnp.where(take, cand, thr), jnp.where(take, n_cand, n_ge)

    thr0 = jnp.full((rows, 1), jnp.iinfo(jnp.int32).min, jnp.int32)
    n_all = jnp.full((rows, 1), 1.0, F32) * jnp.asarray(n_blk * tkb).astype(F32)
    _, thr, n_ge = lax.while_loop(unresolved, bit_step, (jnp.int32(0), thr0, n_all))
    n_col_bits = max(1, int(math.ceil(math.log2(key_ref.shape[1] + 1))))

    def tie_search():
        need = topk - count(lambda blk, col: blk > thr)

        def col_step(i, j):
            cand = j + (jnp.int32(1) << (n_col_bits - 1 - i))
            n = count(lambda blk, col: (blk == thr) & (col < cand))
            return jnp.where(n < need, cand, j)
        return lax.fori_loop(0, n_col_bits, col_step, jnp.zeros((rows, 1), jnp.int32))

    any_tie = jnp.max(jnp.abs(n_ge - topk)) > 0
    j_hi = lax.cond(any_tie, tie_search,
                    lambda: jnp.full((rows, 1), jnp.iinfo(jnp.int32).max, jnp.int32))
    return thr, j_hi


def _idx_scores(qi, wi, kblk16, k_real, keys_transposed=False):
    acc = None
    for h in range(NH):
        qh = qi[:, h * LANES:h * LANES + k_real].astype(BF16)
        if keys_transposed:
            d = jnp.dot(qh, kblk16, preferred_element_type=F32)
        else:
            d = _dot_nt(qh, kblk16)
        t = ((wi[:, h:h + 1] * (NH ** -0.5)) * (HD ** -0.5)) * jnp.maximum(d, 0.0)
        acc = t if acc is None else acc + t
    return acc


def _dsa_sel_prompt_body(qi_ref, wi_ref, ki_ref, m_ref, key_ref, *, tq, tkb, topk, n_keys):
    qb = pl.program_id(0)
    n_blk = ((qb + 1) * tq + tkb - 1) // tkb
    qi = qi_ref[...]
    wi = wi_ref[...]
    q_chunk = (qb * tq + lax.broadcasted_iota(jnp.int32, (tq, tkb), 0)) // CHUNK

    def score_blk(i, _):
        c0 = pl.multiple_of(i * tkb, tkb)
        s = _idx_scores(qi, wi, ki_ref[pl.ds(c0, tkb), :].astype(BF16), LANES)
        k_chunk = (c0 + lax.broadcasted_iota(jnp.int32, (tq, tkb), 1)) // CHUNK
        s = jnp.where(k_chunk <= q_chunk, s, -jnp.inf)
        key_ref[:, pl.ds(c0, tkb)] = _sortable(s)
        return 0
    lax.fori_loop(0, n_blk, score_blk, 0)

    thr, j_hi = _select_topk(key_ref, n_blk, tkb, topk, tq)
    neg_inf_key = _sortable(jnp.full((1, 1), -jnp.inf, F32))

    def write_blk(i, _):
        c0 = pl.multiple_of(i * tkb, tkb)
        blk = key_ref[:, pl.ds(c0, tkb)]
        col = c0 + lax.broadcasted_iota(jnp.int32, (tq, tkb), 1)
        sel = ((blk > thr) | ((blk == thr) & (col <= j_hi))) & (blk > neg_inf_key)
        m_ref[:, pl.ds(c0, tkb)] = jnp.where(sel, 0.0, NEG_BIG).astype(BF16)
        return 0
    lax.fori_loop(0, n_blk, write_blk, 0)

    def zero_blk(i, _):
        m_ref[:, pl.ds(pl.multiple_of(i * tkb, tkb), tkb)] = jnp.full((tq, tkb), NEG_BIG, BF16)
        return 0
    lax.fori_loop(n_blk, n_keys // tkb, zero_blk, 0)


def _dsa_sel_prompt(z, n_rows, topk):
    tq, tkb = 128, 512
    return pl.pallas_call(
        functools.partial(_dsa_sel_prompt_body, tq=tq, tkb=tkb, topk=topk, n_keys=n_rows),
        grid=(n_rows // tq,),
        in_specs=[pl.BlockSpec((tq, NH * LANES), lambda i: (i, OFF_QI // (NH * LANES))),
                  pl.BlockSpec((tq, LANES), lambda i: (i, OFF_WI // LANES)),
                  pl.BlockSpec((n_rows, LANES), lambda i: (0, OFF_KI // LANES))],
        out_specs=pl.BlockSpec((tq, n_rows), lambda i: (i, 0)),
        out_shape=jax.ShapeDtypeStruct((n_rows, n_rows), BF16),
        scratch_shapes=[pltpu.VMEM((tq, n_rows), jnp.int32)],
        compiler_params=_cparams(("parallel",)),
        name="dsa_select_prompt",
    )(z, z, z)


def _dsa_sel_sample_body(qi_ref, wi_ref, kin_ref, kic_ref, m_ref, key_ref, *, L, G, P, tkb, topk):
    n_c = P // tkb
    rows = G * L
    for g in range(G):
        rs = slice(g * L, (g + 1) * L)
        qi = qi_ref[rs, :]
        wi = wi_ref[rs, :]

        def score_blk(i, _, g=g, rs=rs, qi=qi, wi=wi):
            c0 = pl.multiple_of(i * tkb, tkb)
            s = _idx_scores(qi, wi, kic_ref[0, g, :, pl.ds(c0, tkb)].astype(BF16), HD, keys_transposed=True)
            key_ref[rs, pl.ds(c0, tkb)] = _sortable(s)
            return 0
        lax.fori_loop(0, n_c, score_blk, 0)

        knew = jnp.concatenate([kin_ref[rs, :], jnp.zeros((tkb - L, LANES), F32)], axis=0).astype(BF16)
        s = _idx_scores(qi, wi, knew, LANES)
        col = lax.broadcasted_iota(jnp.int32, (L, tkb), 1)
        key_ref[rs, pl.ds(P, tkb)] = _sortable(jnp.where(col < L, s, -jnp.inf))

    thr, j_hi = _select_topk(key_ref, n_c + 1, tkb, topk, rows)
    neg_inf_key = _sortable(jnp.full((1, 1), -jnp.inf, F32))

    def write_blk(i, _):
        c0 = pl.multiple_of(i * tkb, tkb)
        blk = key_ref[:, pl.ds(c0, tkb)]
        cc = c0 + lax.broadcasted_iota(jnp.int32, (rows, tkb), 1)
        sel = ((blk > thr) | ((blk == thr) & (cc <= j_hi))) & (blk > neg_inf_key)
        m_ref[:, pl.ds(c0, tkb)] = jnp.where(sel, 0.0, NEG_BIG).astype(BF16)
        return 0
    lax.fori_loop(0, n_c + 1, write_blk, 0)


def _dsa_sel_sample(z, cache_ki, layer, row_off, nb, L, topk):
    P = cache_ki.shape[3]
    tkb = _pick(P, (512, 256, 128))
    G = _pick(nb, (4, 2, 1))
    rows = G * L
    rb = row_off // rows
    return pl.pallas_call(
        functools.partial(_dsa_sel_sample_body, L=L, G=G, P=P, tkb=tkb, topk=topk),
        grid=(nb // G,),
        in_specs=[pl.BlockSpec((rows, NH * LANES), lambda b: (rb + b, OFF_QI // (NH * LANES))),
                  pl.BlockSpec((rows, LANES), lambda b: (rb + b, OFF_WI // LANES)),
                  pl.BlockSpec((rows, LANES), lambda b: (rb + b, OFF_KI // LANES)),
                  pl.BlockSpec((1, G, HD, P), lambda b: (layer, b, 0, 0))],
        out_specs=pl.BlockSpec((rows, P + tkb), lambda b: (b, 0)),
        out_shape=jax.ShapeDtypeStruct((nb * L, P + tkb), BF16),
        scratch_shapes=[pltpu.VMEM((rows, P + tkb), jnp.int32)],
        compiler_params=_cparams(("parallel",)),
        name="dsa_select_sample",
    )(z, z, z, cache_ki)


def _dsa_attn_body(q_ref, m_ref, kn_ref, vn_ref, o_ref, *, tq, tkb):
    qb = pl.program_id(1)
    n_blk = ((qb + 1) * tq + tkb - 1) // tkb
    qs = [q_ref[:, e * LANES:(e + 1) * LANES] for e in range(2)]

    def visit(state, kblk, vblk, mblk):
        kb16 = kblk.astype(BF16)
        vb16 = vblk.astype(BF16)
        bias = mblk.astype(F32)
        hs = range(2)
        ss = [_dot_nt(qs[e], kb16) + bias for e in hs]
        ms = [jnp.maximum(state[e][0], jnp.max(ss[e], axis=1, keepdims=True)) for e in hs]
        ps = [jnp.exp2(ss[e] - ms[e]) for e in hs]
        al = [jnp.exp2(state[e][0] - ms[e]) for e in hs]
        pv = [jnp.dot(ps[e].astype(BF16), vb16, preferred_element_type=F32) for e in hs]
        return tuple((ms[e], al[e] * state[e][1] + jnp.sum(ps[e], axis=1, keepdims=True),
                      al[e] * state[e][2] + pv[e]) for e in hs)

    init = (jnp.full((tq, 1), NEG_BIG, F32), jnp.zeros((tq, 1), F32), jnp.zeros((tq, LANES), F32))
    state = (init, init)

    def step(i, st):
        c0 = pl.multiple_of(i * tkb, tkb)
        return visit(st, kn_ref[pl.ds(c0, tkb), :], vn_ref[pl.ds(c0, tkb), :], m_ref[:, pl.ds(c0, tkb)])
    state = lax.fori_loop(0, n_blk, step, state)

    lane = lax.broadcasted_iota(jnp.int32, (tq, LANES), 1)
    o0 = state[0][2] / state[0][1]
    o1 = state[1][2] / state[1][1]
    o_ref[...] = jnp.where(lane < HD, o0, o1)


def _dsa_attn_prompt(qn, kn, z, mask, n_rows):
    tq, tkb = min(512, n_rows), min(1024, n_rows)
    vblk = OFF_DV // LANES
    return pl.pallas_call(
        functools.partial(_dsa_attn_body, tq=tq, tkb=tkb),
        grid=(NH // 2, n_rows // tq),
        in_specs=[pl.BlockSpec((tq, 2 * LANES), lambda p, i: (i, p)),
                  pl.BlockSpec((tq, n_rows), lambda p, i: (i, 0)),
                  pl.BlockSpec((n_rows, LANES), lambda p, i: (0, p)),
                  pl.BlockSpec((n_rows, LANES), lambda p, i: (0, vblk + p))],
        out_specs=pl.BlockSpec((tq, LANES), lambda p, i: (i, p)),
        out_shape=jax.ShapeDtypeStruct((n_rows, BW), F32),
        compiler_params=_cparams(("parallel", "arbitrary")),
        name="dsa_attn_prompt",
    )(qn, mask, kn, z)


def _dsa_attn_sample_body(q_ref, m_ref, kn_ref, vn_ref, kc_ref, vc_ref, o_ref, *, L, tkb, n_cache):
    qs = [_q_head(q_ref, h) for h in range(NH)]

    def visit(state, kget, vget, mblk, cached):
        bias = mblk.astype(F32)
        ks = [kget(h).astype(BF16) for h in range(NH)]
        vs = [vget(h).astype(BF16) for h in range(NH)]
        if cached:
            ss = [jnp.dot(qs[h], ks[h], preferred_element_type=F32) + bias for h in range(NH)]
        else:
            ss = [_dot_nt(qs[h], ks[h]) + bias for h in range(NH)]
        ms = [jnp.maximum(state[h][0], jnp.max(ss[h], axis=1, keepdims=True)) for h in range(NH)]
        ps = [jnp.exp2(ss[h] - ms[h]) for h in range(NH)]
        al = [jnp.exp2(state[h][0] - ms[h]) for h in range(NH)]
        if cached:
            pv = [_dot_nt(ps[h].astype(BF16), vs[h]) for h in range(NH)]
        else:
            pv = [jnp.dot(ps[h].astype(BF16), vs[h], preferred_element_type=F32) for h in range(NH)]
        return tuple((ms[h], al[h] * state[h][1] + jnp.sum(ps[h], axis=1, keepdims=True),
                      al[h] * state[h][2] + pv[h]) for h in range(NH))

    init = (jnp.full((L, 1), NEG_BIG, F32), jnp.zeros((L, 1), F32), jnp.zeros((L, HD), F32))

    def step(i, st):
        c0 = pl.multiple_of(i * tkb, tkb)
        return visit(st, lambda h: kc_ref[0, 0, h, :, pl.ds(c0, tkb)],
                     lambda h: vc_ref[0, 0, h, :, pl.ds(c0, tkb)], m_ref[:, pl.ds(c0, tkb)], True)
    state = lax.fori_loop(0, n_cache, step, (init,) * NH)

    pad = jnp.zeros((tkb - L, HD), F32)
    state = visit(state, lambda h: jnp.concatenate([kn_ref[:, h * HD:(h + 1) * HD], pad], axis=0),
                  lambda h: jnp.concatenate([vn_ref[:, h * HD:(h + 1) * HD], pad], axis=0),
                  m_ref[:, pl.ds(n_cache * tkb, tkb)], False)
    for h in range(NH):
        o_ref[:, h * HD:(h + 1) * HD] = state[h][2] / state[h][1]


def _dsa_attn_sample(qn, kn, z, mask, cache_k, cache_v, layer, row_off, nb, L):
    P = cache_k.shape[4]
    tkb = mask.shape[1] - P
    rb = row_off // L
    cspec = pl.BlockSpec((1, 1, NH, HD, P), lambda b: (layer, b, 0, 0, 0))
    return pl.pallas_call(
        functools.partial(_dsa_attn_sample_body, L=L, tkb=tkb, n_cache=P // tkb),
        grid=(nb,),
        in_specs=[pl.BlockSpec((L, NH * LANES), lambda b: (rb + b, 0)),
                  pl.BlockSpec((L, P + tkb), lambda b: (b, 0)),
                  pl.BlockSpec((L, BW), lambda b: (rb + b, 0)),
                  pl.BlockSpec((L, BW), lambda b: (rb + b, OFF_DV // BW)),
                  cspec, cspec],
        out_specs=pl.BlockSpec((L, BW), lambda b: (b, 0)),
        out_shape=jax.ShapeDtypeStruct((nb * L, BW), F32),
        compiler_params=_cparams(("parallel",)),
        name="dsa_attn_sample",
    )(qn, mask, kn, z, cache_k, cache_v)


def _gla_consts(C):
    nlev = int(math.log2(C))
    t = np.arange(C)
    cum = (t[:, None] >= t[None, :]).astype(np.float32)
    sel = np.zeros((nlev * C, C), np.float32)
    for l in range(nlev):
        m = 1 << l
        mid = (t // (2 * m)) * 2 * m + m
        sel[l * C + t, mid - 1] = 1.0
    return jnp.asarray(cum, BF16), jnp.asarray(sel, BF16)


def _gla_body(gq_ref, gk_ref, gv_ref, gr_ref, ga_ref, a2_ref, ab_ref, on_ref, cum_ref, sel_ref,
              s0_ref, o_ref, sout_ref, st_ref, *, C, NS, n_steps):
    ci = pl.program_id(1)
    nlev = int(math.log2(C))

    @pl.when(ci == 0)
    def _():
        st_ref[...] = s0_ref[0]

    row = lax.broadcasted_iota(jnp.int32, (C, C), 0)
    col = lax.broadcasted_iota(jnp.int32, (C, C), 1)
    rowl = lax.broadcasted_iota(jnp.int32, (C, LANES), 0)
    hs = range(GLA_H)
    sls = [slice(h * LANES, (h + 1) * LANES) for h in hs]
    st = [st_ref[h] for h in hs]
    for sub in range(NS):
        rs = slice(sub * C, (sub + 1) * C)
        x = jnp.dot(ga_ref[rs, :].astype(BF16), a2_ref[...], preferred_element_type=F32) + ab_ref[...]
        g = _neg_softplus(-x) / GLA_GATE_NORM
        b = _split_dot_left(cum_ref[...], g)
        c_all = _split_dot_left(sel_ref[...], b)
        q = [gq_ref[rs, sl] * (HD ** -0.5) for sl in sls]
        k = [gk_ref[rs, sl] for sl in sls]
        v16 = [gv_ref[rs, sl].astype(BF16) for sl in sls]
        bh = [b[:, sl] for sl in sls]
        o_inter = [_dot_nt((q[h] * jnp.exp(bh[h])).astype(BF16), st[h].astype(BF16)) for h in hs]
        b_last = [bh[h][C - 1:C, :] for h in hs]
        kv = [_dot_tn(v16[h], (k[h] * jnp.exp(b_last[h] - bh[h])).astype(BF16)) for h in hs]
        a = [jnp.where(row == col, jnp.sum(q[h] * k[h], axis=1, keepdims=True), 0.0) for h in hs]
        for l in range(nlev):
            m = 1 << l
            later = (rowl & m) != 0
            same_seg = (row // (2 * m)) == (col // (2 * m))
            ch = [c_all[l * C:(l + 1) * C, sl] for sl in sls]
            qe = [jnp.where(later, q[h] * jnp.exp(jnp.where(later, bh[h] - ch[h], 0.0)), 0.0) for h in hs]
            ke = [jnp.where(later, 0.0, k[h] * jnp.exp(jnp.where(later, 0.0, ch[h] - bh[h]))) for h in hs]
            al = [_dot_nt(qe[h].astype(BF16), ke[h].astype(BF16)) for h in hs]
            a = [a[h] + jnp.where(same_seg, al[h], 0.0) for h in hs]
        o = [jnp.dot(a[h].astype(BF16), v16[h], preferred_element_type=F32) + o_inter[h] for h in hs]
        st = [st[h] * jnp.exp(b_last[h]) + kv[h] for h in hs]
        for h in hs:
            on = o[h] * lax.rsqrt(jnp.mean(o[h] * o[h], axis=-1, keepdims=True) + NORM_EPS) * on_ref[...]
            gr = gr_ref[rs, sls[h]]
            o_ref[rs, sls[h]] = on * (gr * jax.nn.sigmoid(gr))
    for h in hs:
        st_ref[h] = st[h]

    @pl.when(ci == n_steps - 1)
    def _():
        sout_ref[0] = st_ref[...]


def _gla(z, a2p, abp, onorm, s0t, row_off, n_seq, L):
    C = min(CHUNK, L)
    NS = 2 if (L // C) % 2 == 0 else 1
    R = NS * C
    n_steps = L // R
    rb = row_off // R
    cum, sel = _gla_consts(C)
    blk = lambda off: pl.BlockSpec((R, BW), lambda s, c: (rb + s * n_steps + c, off // BW))
    full = lambda a: pl.BlockSpec(a.shape, lambda s, c: (0,) * a.ndim)
    return pl.pallas_call(
        functools.partial(_gla_body, C=C, NS=NS, n_steps=n_steps),
        grid=(n_seq, n_steps),
        in_specs=[blk(OFF_GQ), blk(OFF_GK), blk(OFF_GV), blk(OFF_GR),
                  pl.BlockSpec((R, LANES), lambda s, c: (rb + s * n_steps + c, OFF_GA // LANES)),
                  full(a2p), full(abp), full(onorm), full(cum), full(sel),
                  pl.BlockSpec((1, GLA_H, GLA_DV, LANES), lambda s, c: (s, 0, 0, 0))],
        out_specs=[pl.BlockSpec((R, BW), lambda s, c: (s * n_steps + c, 0)),
                   pl.BlockSpec((1, GLA_H, GLA_DV, LANES), lambda s, c: (s, 0, 0, 0))],
        out_shape=[jax.ShapeDtypeStruct((n_seq * L, BW), F32),
                   jax.ShapeDtypeStruct((n_seq, GLA_H, GLA_DV, LANES), F32)],
        scratch_shapes=[pltpu.VMEM((GLA_H, GLA_DV, LANES), F32)],
        compiler_params=_cparams(("parallel", "arbitrary")),
        name="gla",
    )(z, z, z, z, z, a2p, abp, onorm, cum, sel, s0t)


def _rw_prep_body(z_ref, head_ref, mu_ref, w0_ref, a0_ref, kkp_ref, kap_ref, rkp_ref,
                  w2_ref, a2_ref, g2_ref, seg_ref,
                  r_ref, w_ref, k_ref, v_ref, nkk_ref, kka_ref, gate_ref, bonus_ref, *, grp):
    zz = z_ref[...]
    tm, wd = zz.shape
    heads = head_ref[...]
    head_rows = jnp.concatenate([jnp.broadcast_to(heads[g:g + 1, :], (grp, wd))
                                 for g in range(tm // grp)], axis=0)
    row = lax.broadcasted_iota(jnp.int32, (tm, wd), 0)
    prev = jnp.where(row % grp == 0, head_rows, pltpu.roll(zz, 1, axis=0))
    zm = zz + mu_ref[...] * (prev - zz)
    rr = zm[:, 0:BW]
    rk = zm[:, BW:2 * BW]
    rv = zm[:, 2 * BW:3 * BW]
    lora = zm[:, 3 * BW:3 * BW + LANES]
    gl = zm[:, 3 * BW + LANES:3 * BW + 2 * LANES]
    xw = w0_ref[...] + jnp.dot(jnp.tanh(lora).astype(BF16), w2_ref[...], preferred_element_type=F32)
    wlog = _neg_softplus(-xw) - 0.5
    a = jax.nn.sigmoid(a0_ref[...] + jnp.dot(lora.astype(BF16), a2_ref[...], preferred_element_type=F32))
    gate = jnp.dot(jax.nn.sigmoid(gl).astype(BF16), g2_ref[...], preferred_element_type=F32)
    kkf = rk * kkp_ref[...]
    nrm = jnp.sqrt(_split_dot(kkf * kkf, seg_ref[...]))
    kk = kkf / jnp.maximum(nrm, 1e-12)
    kmod = rk * (1.0 + (a - 1.0) * kap_ref[...])
    coef = _split_dot(rr * kmod * rkp_ref[...], seg_ref[...])
    r_ref[...] = rr
    w_ref[...] = -jnp.exp(wlog)
    k_ref[...] = kmod
    v_ref[...] = rv
    nkk_ref[...] = -kk
    kka_ref[...] = kk * a
    gate_ref[...] = gate
    bonus_ref[...] = coef * rv


def _rw_prep(z, heads, grp, mu, w0, a0, kkp, kap, rkp, w2p, a2p, g2, seg):
    T = z.shape[0]
    tm = SUBLANES * grp
    assert T % tm == 0, (T, tm)
    W = 4 * BW
    par = lambda a: pl.BlockSpec(a.shape, lambda i: (0,) * a.ndim)
    out = pl.BlockSpec((tm, BW), lambda i: (i, 0))
    return pl.pallas_call(
        functools.partial(_rw_prep_body, grp=grp),
        grid=(T // tm,),
        in_specs=[pl.BlockSpec((tm, W), lambda i: (i, 0)), pl.BlockSpec((SUBLANES, W), lambda i: (i, 0)),
                  par(mu), par(w0), par(a0), par(kkp), par(kap), par(rkp),
                  par(w2p), par(a2p), par(g2), par(seg)],
        out_specs=[out] * 8,
        out_shape=[jax.ShapeDtypeStruct((T, BW), F32)] * 8,
        compiler_params=_cparams(("parallel",)),
        name="rwkv_prep",
    )(z, heads, mu, w0, a0, kkp, kap, rkp, w2p, a2p, g2, seg)


def _mm3(a, b):
    ah = a.astype(BF16)
    al = (a - ah.astype(F32)).astype(BF16)
    bh = b.astype(BF16)
    bl = (b - bh.astype(F32)).astype(BF16)
    d = lambda x, y: jnp.dot(x, y, preferred_element_type=F32)
    return d(ah, bh) + (d(ah, bl) + d(al, bh))


def _rw_chunk_body(r_ref, lw_ref, k_ref, v_ref, nkk_ref, kka_ref, s0_ref, o_ref, sout_ref, s_ref,
                   *, TT, L, n_tiles):
    ti = pl.program_id(0)
    npair = NH // 2
    nblk = TT // RW_BLK
    row = lax.broadcasted_iota(jnp.int32, (TT, TT), 0)
    col = lax.broadcasted_iota(jnp.int32, (TT, TT), 1)
    same = (row // RW_BLK) == (col // RW_BLK)
    strict = same & (col < row)
    incl = same & (col <= row)
    tri = jnp.where(incl, 1.0, 0.0).astype(BF16)
    last = jnp.where(same & (col % RW_BLK == RW_BLK - 1), 1.0, 0.0).astype(BF16)
    eye = jnp.where(row == col, 1.0, 0.0)
    pair_diag = (row < HD) == (col < HD)
    lo_half = lax.broadcasted_iota(jnp.int32, (TT, LANES), 1) < HD
    lo_half_b = lax.broadcasted_iota(jnp.int32, (RW_BLK, LANES), 1) < HD

    r, lw, k, v = r_ref[...], lw_ref[...], k_ref[...], v_ref[...]
    nk, ka = nkk_ref[...], kka_ref[...]
    lc = _split_dot_left(tri, lw)
    le = _split_dot_left(last, lc)
    nkt = nk * jnp.exp(lc - lw)
    rt = r * jnp.exp(lc)
    inv = jnp.exp(-lc)
    kah = ka * inv
    kh = k * inv
    rem = jnp.exp(le - lc)
    kaw = (ka * rem).astype(BF16)
    kw = (k * rem).astype(BF16)
    wb = jnp.exp(le)

    sls = [slice(p * LANES, (p + 1) * LANES) for p in range(npair)]
    v16 = [v[:, sl].astype(BF16) for sl in sls]
    heads = [(p, e) for p in range(npair) for e in range(2)]
    grams = []
    for p, e in heads:
        hm = lo_half if e == 0 else jnp.logical_not(lo_half)
        lhs = jnp.concatenate([jnp.where(hm, nkt[:, sls[p]], 0.0), jnp.where(hm, rt[:, sls[p]], 0.0)],
                              axis=0).astype(BF16)
        rhs = jnp.concatenate([kah[:, sls[p]], kh[:, sls[p]]], axis=0).astype(BF16)
        grams.append(_dot_nt(lhs, rhs))
    n1 = [jnp.where(strict, g[:TT, :TT], 0.0) for g in grams]
    bt = [jnp.where(strict, g[:TT, TT:], 0.0).astype(BF16) for g in grams]
    cmat = [jnp.where(incl, g[TT:, :TT], 0.0).astype(BF16) for g in grams]
    et = [jnp.where(incl, g[TT:, TT:], 0.0).astype(BF16) for g in grams]
    n2 = [_mm3(a, a) for a in n1]
    n4 = [_mm3(a, a) for a in n2]
    n8 = [_mm3(a, a) for a in n4]
    tinv = [eye + a for a in n1]
    for power in (n2, n4, n8):
        tinv = [t + _mm3(t, a) for t, a in zip(tinv, power)]
    vb_h = [jnp.dot(b_, v16[p], preferred_element_type=F32) for b_, (p, e) in zip(bt, heads)]
    ev_h = [jnp.dot(e_, v16[p], preferred_element_type=F32) for e_, (p, e) in zip(et, heads)]
    vb = [jnp.where(lo_half, vb_h[2 * p], vb_h[2 * p + 1]) for p in range(npair)]
    ev = [jnp.where(lo_half, ev_h[2 * p], ev_h[2 * p + 1]) for p in range(npair)]

    if L >= TT:
        @pl.when(ti == 0)
        def _():
            s_ref[...] = s0_ref[0]
        state = {(0, p): s_ref[p] for p in range(npair)}
        rounds = [[j] for j in range(nblk)]
    else:
        bps = L // RW_BLK
        state = {(q, p): s0_ref[q, p] for q in range(TT // L) for p in range(npair)}
        rounds = [[q * bps + i for q in range(TT // L)] for i in range(bps)]
    sa_rows = [[None] * nblk for _ in range(npair)]
    os_rows = [[None] * nblk for _ in range(npair)]
    for blocks in rounds:
        items = [(j, p) for j in blocks for p in range(npair)]
        seq = lambda j: (j * RW_BLK) // L if L < TT else 0
        xs = {}
        for j, p in items:
            bs = slice(j * RW_BLK, (j + 1) * RW_BLK)
            s = state[(seq(j), p)]
            s_hi = s.astype(BF16)
            s_lo = (s - s_hi.astype(F32)).astype(BF16)
            lhs = jnp.concatenate([nkt[bs, sls[p]], rt[bs, sls[p]]], axis=0).astype(BF16)
            xs[(j, p)] = (jnp.dot(lhs, s_hi, preferred_element_type=F32)
                          + jnp.dot(lhs, s_lo, preferred_element_type=F32))
        sa = {}
        for j, p in items:
            b0 = j * RW_BLK
            bs = slice(b0, b0 + RW_BLK)
            parts = [xs[(j, p)][:RW_BLK] + vb[p][bs]]
            if b0:
                parts.insert(0, jnp.zeros((b0, LANES), F32))
            if TT - b0 - RW_BLK:
                parts.append(jnp.zeros((TT - b0 - RW_BLK, LANES), F32))
            xfull = jnp.concatenate(parts, axis=0)
            sa[(j, p)] = jnp.where(lo_half_b, _mm3(tinv[2 * p][bs], xfull),
                                   _mm3(tinv[2 * p + 1][bs], xfull))
        for j, p in items:
            b0 = j * RW_BLK
            bs = slice(b0, b0 + RW_BLK)
            u = (_dot_tn(kaw[bs, sls[p]], sa[(j, p)].astype(BF16)) + _dot_tn(kw[bs, sls[p]], v16[p][bs]))
            wcol = jnp.broadcast_to(wb[b0:b0 + 1, sls[p]], (LANES, LANES)).T
            state[(seq(j), p)] = wcol * state[(seq(j), p)] + jnp.where(pair_diag, u, 0.0)
            sa_rows[p][j] = sa[(j, p)]
            os_rows[p][j] = xs[(j, p)][RW_BLK:]
    if L < TT:
        for (q, p), s in state.items():
            sout_ref[q, p] = s

    for p in range(npair):
        sa_t = jnp.concatenate(sa_rows[p], axis=0).astype(BF16)
        o_sa = jnp.where(lo_half, jnp.dot(cmat[2 * p], sa_t, preferred_element_type=F32),
                         jnp.dot(cmat[2 * p + 1], sa_t, preferred_element_type=F32))
        o_ref[:, sls[p]] = jnp.concatenate(os_rows[p], axis=0) + o_sa + ev[p]
    if L >= TT:
        for p in range(npair):
            s_ref[p] = state[(0, p)]

        @pl.when(ti == n_tiles - 1)
        def _():
            for p in range(npair):
                sout_ref[0, p] = state[(0, p)]


def _rw_chunk(r, lw, k, v, nkk, kka, s0, row_off, n_seq, L):
    TT = LANES
    assert L % RW_BLK == 0 and (L % TT == 0 or TT % L == 0)
    n_tok = n_seq * L
    n_tiles = n_tok // TT
    rb = row_off // TT
    ns = max(1, TT // L)
    tok = pl.BlockSpec((TT, BW), lambda i: (rb + i, 0))
    if ns == 1:
        sspec = pl.BlockSpec((1, NH // 2, LANES, LANES), lambda i: (0, 0, 0, 0))
    else:
        sspec = pl.BlockSpec((ns, NH // 2, LANES, LANES), lambda i: (i, 0, 0, 0))
    return pl.pallas_call(
        functools.partial(_rw_chunk_body, TT=TT, L=L, n_tiles=n_tiles),
        grid=(n_tiles,),
        in_specs=[tok] * 6 + [sspec],
        out_specs=[pl.BlockSpec((TT, BW), lambda i: (i, 0)), sspec],
        out_shape=[jax.ShapeDtypeStruct((n_tok, BW), F32),
                   jax.ShapeDtypeStruct(s0.shape, F32)],
        scratch_shapes=[pltpu.VMEM((NH // 2, LANES, LANES), F32)],
        compiler_params=_cparams(("arbitrary",)),
        name="rwkv_chunk",
    )(r, lw, k, v, nkk, kka, s0)


def _rw_post_body(o_ref, bonus_ref, gate_ref, lnw_ref, lnb_ref, seg_ref, out_ref):
    o = o_ref[...]
    mu = _split_dot(o, seg_ref[...]) * (1.0 / HD)
    d = o - mu
    var = _split_dot(d * d, seg_ref[...]) * (1.0 / HD)
    on = d * lax.rsqrt(var + RW_LN_EPS) * lnw_ref[...] + lnb_ref[...]
    out_ref[...] = (on + bonus_ref[...]) * gate_ref[...]


def _rw_post(o, bonus, gate, lnw, lnb, seg):
    T = o.shape[0]
    tm = _pick(T, (512, 256, 128))
    blk = pl.BlockSpec((tm, BW), lambda i: (i, 0))
    par = lambda a: pl.BlockSpec(a.shape, lambda i: (0,) * a.ndim)
    return pl.pallas_call(
        _rw_post_body,
        grid=(T // tm,),
        in_specs=[blk, blk, blk, par(lnw), par(lnb), par(seg)],
        out_specs=blk,
        out_shape=jax.ShapeDtypeStruct((T, BW), F32),
        compiler_params=_cparams(("parallel",)),
        name="rwkv_post",
    )(o, bonus, gate, lnw, lnb, seg)


def _heads_alt(w):
    lead = w.shape[:-1]
    w4 = w.reshape(*lead, NH // 2, 2, HD)
    zero = jnp.zeros((*lead, NH // 2, HD), w.dtype)
    ev = jnp.concatenate([w4[..., 0, :], zero], -1)
    od = jnp.concatenate([zero, w4[..., 1, :]], -1)
    return jnp.stack([ev, od], -2).reshape(*lead, NH * LANES)


def _heads_lo(w, nh):
    lead = w.shape[:-1]
    w3 = w.reshape(*lead, nh, HD)
    return jnp.concatenate([w3, jnp.zeros_like(w3)], -1).reshape(*lead, nh * LANES)


def _padc(w, n):
    return jnp.pad(w, [(0, 0)] * (w.ndim - 1) + [(0, n - w.shape[-1])])


def _rw_reorder(a):
    o = 0
    parts = {}
    for name, wd in (("rr", BW), ("wl", RW_LORA), ("rk", BW), ("rv", BW), ("al", RW_LORA), ("gl", RW_GATE_LORA)):
        parts[name] = a[..., o:o + wd]
        o += wd
    return jnp.concatenate([parts[n] for n in ("rr", "rk", "rv", "wl", "al", "gl")], axis=-1)


def _rw_unorder(a):
    rr, rk, rv = a[..., 0:BW], a[..., BW:2 * BW], a[..., 2 * BW:3 * BW]
    wl = a[..., 3 * BW:3 * BW + RW_LORA]
    al = a[..., 3 * BW + RW_LORA:3 * BW + 2 * RW_LORA]
    gl = a[..., 3 * BW + 2 * RW_LORA:RW_COLS]
    return jnp.concatenate([rr, wl, rk, rv, al, gl], axis=-1)


def _relayout_w_in(w, d_model):
    w = w.astype(BF16)
    o = 0

    def take(n):
        nonlocal o
        s = w[..., o:o + n]
        o += n
        return s
    sq, sk, sv = take(BW), take(BW), take(BW)
    dq, dk, dv, qi, ki, wi = take(BW), take(BW), take(BW), take(BW), take(HD), take(NH)
    gq, gk = take(GLA_H * HD), take(GLA_H * HD)
    gv, gr, ga = take(BW), take(BW), take(GLA_LOWRANK)
    rw = take(RW_COLS)
    gate = take(4 * d_model)
    cols = [_padc(_rw_reorder(rw), 4 * BW), _heads_alt(sq), sk, sv, _heads_alt(dq), dk, dv,
            _heads_lo(qi, NH), _heads_lo(gq, GLA_H), _heads_lo(gk, GLA_H), gv, gr,
            _padc(ki, LANES), _padc(wi, LANES), _padc(ga, 2 * LANES), gate]
    return jnp.concatenate(cols, axis=-1)


def _relayout_ffn(w_up, w_down, tf):
    f = w_up.shape[-1] // 2
    fp = -(-f // tf) * tf
    wup = jnp.concatenate([_padc(w_up[..., :f].astype(BF16), fp), _padc(w_up[..., f:].astype(BF16), fp)],
                          axis=-1)
    wdn = jnp.pad(w_down.astype(BF16), ((0, 0), (0, fp - f), (0, 0)))
    return wup, wdn


def _gain_alt(g):
    return jnp.tile(g, 2 * NH)[None, :]


def kernel(x_prompt, x_sample, cache_sb_k, cache_sb_v, cache_dsa_k, cache_dsa_v, cache_dsa_kidx, state_gla, state_rwkv, state_rwkv_shift, ffn1_norm, ffn1_up, ffn1_down, mix_norm, w_in, sb_qnorm, sb_knorm, dsa_qnorm, dsa_knorm, gla_a2, gla_ab, gla_onorm, rwkv_mu, rwkv_w0, rwkv_w2, rwkv_a0, rwkv_a2, rwkv_g2, rwkv_kk, rwkv_ka, rwkv_rk, rwkv_lnw, rwkv_lnb, w_branch, w_out, ffn2_norm, ffn2_up, ffn2_down):
    depth = w_in.shape[0]
    bp, lp, d_model = x_prompt.shape
    nb, ls, _ = x_sample.shape
    assert bp == 1, "prompt group is one sequence"
    tp = bp * lp
    ts = nb * ls
    past = cache_sb_k.shape[2]
    to_pos_minor = lambda c: jnp.transpose(c, (0, 1, 3, 4, 2))
    ck_sb, cv_sb = to_pos_minor(cache_sb_k), to_pos_minor(cache_sb_v)
    ck_dsa, cv_dsa = to_pos_minor(cache_dsa_k), to_pos_minor(cache_dsa_v)
    cki = jnp.transpose(cache_dsa_kidx, (0, 1, 3, 2))
    tf = 512

    x = jnp.concatenate([x_prompt.reshape(tp, d_model), x_sample.reshape(ts, d_model)], axis=0)

    bd = jnp.asarray(np.kron(np.eye(2), np.full((HD, HD), 1.0 / HD)), BF16)
    seg = jnp.asarray(np.kron(np.eye(NH), np.ones((HD, HD))), BF16)
    topk_p = min(TOPK_MAX, lp // 4)
    topk_s = min(TOPK_MAX, (past + ls) // 4)

    new_p = [[] for _ in range(8)]
    new_s = [[] for _ in range(8)]
    wup1, wdn1 = _relayout_ffn(ffn1_up, ffn1_down, tf)
    wup2, wdn2 = _relayout_ffn(ffn2_up, ffn2_down, tf)
    win = _relayout_w_in(w_in, d_model)
    wbr = w_branch.astype(BF16)
    wout = w_out.astype(BF16)
    for i in range(depth):
        x = _ffn(x, ffn1_norm[:, None], wup1, wdn1, tf, i)
        z = _proj(x, mix_norm[:, None], win, i)

        sqn, skn, dqn, dkn = _qknorm(z, _gain_alt(sb_qnorm[i]), jnp.tile(sb_knorm[i], NH)[None],
                                     _gain_alt(dsa_qnorm[i]), jnp.tile(dsa_knorm[i], NH)[None], bd)
        o_sb = (_sb_prompt(sqn, skn, z, tp), _sb_sample(sqn, skn, z, ck_sb, cv_sb, i, tp, nb, ls))
        mask_p = _dsa_sel_prompt(z, tp, topk_p)
        mask_s = _dsa_sel_sample(z, cki, i, tp, nb, ls, topk_s)
        o_dsa = (_dsa_attn_prompt(dqn, dkn, z, mask_p, tp),
                 _dsa_attn_sample(dqn, dkn, z, mask_s, ck_dsa, cv_dsa, i, tp, nb, ls))

        a2p = jnp.pad(_heads_lo(gla_a2[i], GLA_H), ((0, LANES - GLA_LOWRANK), (0, 0))).astype(BF16)
        abp = _heads_lo(gla_ab[i][None], GLA_H)
        onorm = gla_onorm[i][None]

        def gla_state_in(s):
            return jnp.pad(jnp.swapaxes(s, 2, 3), ((0, 0), (0, 0), (0, 0), (0, LANES - HD)))

        def gla_state_out(s):
            return jnp.swapaxes(s[..., :HD], 2, 3)
        og_p, sg_p = _gla(z, a2p, abp, onorm, jnp.zeros((bp, GLA_H, GLA_DV, LANES), F32), 0, bp, lp)
        og_s, sg_s = _gla(z, a2p, abp, onorm, gla_state_in(state_gla[i]), tp, nb, ls)
        o_gla = (og_p, og_s)

        zrw = z[:, :RW_COLS]
        shift_s = _rw_reorder(state_rwkv_shift[i])
        heads = jnp.concatenate([jnp.zeros((1, 4 * BW), F32), z[ls - 1:tp - 1:ls, :4 * BW],
                                 _padc(shift_s.reshape(nb, RW_COLS), 4 * BW)], axis=0)
        mu = _padc(_rw_reorder(rwkv_mu[i])[None], 4 * BW)
        w2p = jnp.concatenate([rwkv_w2[i], jnp.zeros_like(rwkv_a2[i])], axis=0).astype(BF16)
        a2q = jnp.concatenate([jnp.zeros_like(rwkv_w2[i]), rwkv_a2[i]], axis=0).astype(BF16)
        r_, w_, k_, v_, nkk_, kka_, gate_, bonus_ = _rw_prep(
            z, heads, ls, mu, rwkv_w0[i][None], rwkv_a0[i][None], rwkv_kk[i][None], rwkv_ka[i][None],
            rwkv_rk[i][None], w2p, a2q, rwkv_g2[i].astype(BF16), seg)

        def rw_state_in(s):
            n = s.shape[0]
            st = jnp.swapaxes(s, 2, 3).reshape(n, NH // 2, 2, HD, HD)
            zero = jnp.zeros((n, NH // 2, HD, HD), s.dtype)
            return jnp.concatenate([jnp.concatenate([st[:, :, 0], zero], axis=-1),
                                    jnp.concatenate([zero, st[:, :, 1]], axis=-1)], axis=-2)

        def rw_state_out(s):
            n = s.shape[0]
            st = jnp.stack([s[:, :, :HD, :HD], s[:, :, HD:, HD:]], axis=2)
            return jnp.swapaxes(st, 3, 4).reshape(n, NH, HD, HD)
        o_rp, sr_p = _rw_chunk(r_, w_, k_, v_, nkk_, kka_, jnp.zeros((bp, NH // 2, LANES, LANES), F32), 0, bp, lp)
        o_rs, sr_s = _rw_chunk(r_, w_, k_, v_, nkk_, kka_, rw_state_in(state_rwkv[i]), tp, nb, ls)
        o_r = jnp.concatenate([o_rp, o_rs], axis=0)
        o_rw = _rw_post(o_r, bonus_, gate_, rwkv_lnw[i][None], rwkv_lnb[i][None], seg)

        x = _merge(x, z, (o_sb, o_dsa, o_gla, o_rw), wbr, wout, i, tp)
        x = _ffn(x, ffn2_norm[:, None], wup2, wdn2, tf, i)

        sv = z[:, OFF_SV:OFF_SV + BW]
        dv = z[:, OFF_DV:OFF_DV + BW]
        ki = z[:, OFF_KI:OFF_KI + HD]
        for dst, (lo, n, l) in ((new_p, (0, bp, lp)), (new_s, (tp, nb, ls))):
            hi = lo + n * l
            dst[0].append(skn[lo:hi].reshape(n, l, NH, HD))
            dst[1].append(sv[lo:hi].reshape(n, l, NH, HD))
            dst[2].append(dkn[lo:hi].reshape(n, l, NH, HD))
            dst[3].append(dv[lo:hi].reshape(n, l, NH, HD))
            dst[4].append(ki[lo:hi].reshape(n, l, HD))
            dst[7].append(_rw_unorder(zrw[lo:hi].reshape(n, l, RW_COLS)[:, -1:]))
        new_p[5].append(gla_state_out(sg_p))
        new_s[5].append(gla_state_out(sg_s))
        new_p[6].append(rw_state_out(sr_p))
        new_s[6].append(rw_state_out(sr_s))

    outs_p = [jnp.stack(l, axis=0) for l in new_p]
    outs_s = [jnp.stack(l, axis=0) for l in new_s]
    y_p = x[:tp].reshape(bp, lp, d_model)
    y_s = x[tp:].reshape(nb, ls, d_model)
    return (y_p, y_s, *outs_p, *outs_s)
```

```python
import functools
import math

import jax
import jax.numpy as jnp
import numpy as np
from jax import lax
from jax.experimental import pallas as pl
from jax.experimental.pallas import tpu as pltpu

F32 = jnp.float32
BF16 = jnp.bfloat16

LANES = 128
SUBLANES = 8
HD = 64
NH = 8
BW = NH * HD
GLA_H = 4
GLA_DV = 128
GLA_LOWRANK = 16
GLA_GATE_NORM = 16.0
CHUNK = 64
TOPK_MAX = 256
RW_LORA = 64
RW_GATE_LORA = 128
RW_BLK = 16
RW_COLS = 3 * BW + 2 * RW_LORA + RW_GATE_LORA
NORM_EPS = 1e-6
RW_LN_EPS = 64e-5
NEG_BIG = -1e30
LOG2E = math.log2(math.e)
SB_UNDERFLOW = -104.0
VMEM_LIMIT = 56 * 1024 * 1024
OFF_RW = 0
OFF_SQ = 2048
OFF_SK = 3072
OFF_SV = 3584
OFF_DQ = 4096
OFF_DK = 5120
OFF_DV = 5632
OFF_QI = 6144
OFF_GQ = 7168
OFF_GK = 7680
OFF_GV = 8192
OFF_GR = 8704
OFF_KI = 9216
OFF_WI = 9344
OFF_GA = 9472
OFF_GATE = 9728


def _cparams(sem):
    return pltpu.CompilerParams(dimension_semantics=sem, vmem_limit_bytes=VMEM_LIMIT)


def _pick(n, prefs):
    for p in prefs:
        if n % p == 0:
            return p
    raise ValueError(f"no tile for {n} in {prefs}")


def _split_dot(a, b):
    hi = a.astype(BF16)
    lo = (a - hi.astype(F32)).astype(BF16)
    return (jnp.dot(hi, b, preferred_element_type=F32)
            + jnp.dot(lo, b, preferred_element_type=F32))


def _split_dot_left(a, b):
    hi = b.astype(BF16)
    lo = (b - hi.astype(F32)).astype(BF16)
    return (jnp.dot(a, hi, preferred_element_type=F32)
            + jnp.dot(a, lo, preferred_element_type=F32))


def _dot_nt(a, b):
    return lax.dot_general(a, b, (((1,), (1,)), ((), ())), preferred_element_type=F32)


def _dot_tn(a, b):
    return lax.dot_general(a, b, (((0,), (0,)), ((), ())), preferred_element_type=F32)


def _neg_softplus(z):
    return -(jnp.maximum(z, 0.0) + jnp.log(1.0 + jnp.exp(-jnp.abs(z))))


def _ffn_body(x_ref, g_ref, wg_ref, wu_ref, wdn_ref, o_ref, xn_ref):
    @pl.when(pl.program_id(1) == 0)
    def _():
        x = x_ref[...]
        ms = jnp.mean(x * x, axis=-1, keepdims=True)
        xn_ref[...] = (x * lax.rsqrt(ms + NORM_EPS) * g_ref[0]).astype(BF16)
        o_ref[...] = x

    xn = xn_ref[...]
    gate = jnp.dot(xn, wg_ref[0], preferred_element_type=F32)
    up = jnp.dot(xn, wu_ref[0], preferred_element_type=F32)
    act = (gate * jax.nn.sigmoid(gate) * up).astype(BF16)
    o_ref[...] += 0.5 * jnp.dot(act, wdn_ref[0], preferred_element_type=F32)


def _ffn(x, g, wup, wdn, tf, layer):
    T, D = x.shape
    nf = wdn.shape[1] // tf
    tm = _pick(T, (1024, 512, 256, 128))
    return pl.pallas_call(
        _ffn_body,
        grid=(T // tm, nf),
        in_specs=[pl.BlockSpec((tm, D), lambda i, j: (i, 0), pipeline_mode=pl.Buffered(1)),
                  pl.BlockSpec((1, 1, D), lambda i, j: (layer, 0, 0)),
                  pl.BlockSpec((1, D, tf), lambda i, j: (layer, 0, j)),
                  pl.BlockSpec((1, D, tf), lambda i, j: (layer, 0, nf + j)),
                  pl.BlockSpec((1, tf, D), lambda i, j: (layer, j, 0))],
        out_specs=pl.BlockSpec((tm, D), lambda i, j: (i, 0)),
        out_shape=jax.ShapeDtypeStruct((T, D), F32),
        scratch_shapes=[pltpu.VMEM((tm, D), BF16)],
        compiler_params=_cparams(("parallel", "arbitrary")),
        name="ffn",
    )(x, g, wup, wup, wdn)


def _proj_body(x_ref, g_ref, w_ref, o_ref, xn_ref):
    @pl.when(pl.program_id(1) == 0)
    def _():
        x = x_ref[...]
        ms = jnp.mean(x * x, axis=-1, keepdims=True)
        xn_ref[...] = (x * lax.rsqrt(ms + NORM_EPS) * g_ref[0]).astype(BF16)

    o_ref[...] = jnp.dot(xn_ref[...], w_ref[0], preferred_element_type=F32)


def _proj(x, g, w, layer):
    T, D = x.shape
    n = w.shape[2]
    tm = _pick(T, (1024, 512, 256, 128))
    tn = _pick(n, (1280, 512))
    return pl.pallas_call(
        _proj_body,
        grid=(T // tm, n // tn),
        in_specs=[pl.BlockSpec((tm, D), lambda i, j: (i, 0)),
                  pl.BlockSpec((1, 1, D), lambda i, j: (layer, 0, 0)),
                  pl.BlockSpec((1, D, tn), lambda i, j: (layer, 0, j))],
        out_specs=pl.BlockSpec((tm, tn), lambda i, j: (i, j)),
        out_shape=jax.ShapeDtypeStruct((T, n), F32),
        scratch_shapes=[pltpu.VMEM((tm, D), BF16)],
        compiler_params=_cparams(("parallel", "arbitrary")),
        name="in_proj",
    )(x, g, w)


def _merge_body(*refs, split, n_prompt_tiles):
    refs = list(refs)
    on_prompt = pl.program_id(0) < n_prompt_tiles
    branches = []
    for two in split:
        if two:
            bp, bs = refs.pop(0), refs.pop(0)
            branches.append(jnp.where(on_prompt, bp[...], bs[...]))
        else:
            branches.append(refs.pop(0)[...])
    nc = (len(refs) - 4) // 4
    g_refs, (wb_ref, wo_ref, x_ref, o_ref) = refs[:4 * nc], refs[4 * nc:]
    b16 = [b.astype(BF16) for b in branches]
    tc = g_refs[0].shape[1]
    cols = []
    for c in range(nc):
        acc = None
        for n in range(4):
            p = jnp.dot(b16[n], wb_ref[0, n, :, c * tc:(c + 1) * tc], preferred_element_type=F32)
            t = jax.nn.sigmoid(g_refs[n * nc + c][...]) * p
            acc = t if acc is None else acc + t
        cols.append(acc.astype(BF16))
    o_ref[...] = x_ref[...] + jnp.dot(jnp.concatenate(cols, axis=1), wo_ref[0], preferred_element_type=F32)


def _merge(x, z, branches, wb, wo, layer, n_prompt):
    T, D = x.shape
    tm = _pick(math.gcd(T, n_prompt), (256, 128))
    npt = n_prompt // tm
    tc = 512
    nc = D // tc
    gate_blk = OFF_GATE // tc
    split = tuple(isinstance(b, tuple) for b in branches)
    b_args, b_specs = [], []
    for b in branches:
        if isinstance(b, tuple):
            b_args += list(b)
            b_specs += [pl.BlockSpec((tm, BW), lambda i: (jnp.minimum(i, npt - 1), 0)),
                        pl.BlockSpec((tm, BW), lambda i: (jnp.maximum(i - npt, 0), 0))]
        else:
            b_args.append(b)
            b_specs.append(pl.BlockSpec((tm, BW), lambda i: (i, 0)))
    gspecs = [pl.BlockSpec((tm, tc), functools.partial(lambda i, k: (i, gate_blk + k), k=k))
              for k in range(4 * nc)]
    resident = pl.Buffered(1)
    return pl.pallas_call(
        functools.partial(_merge_body, split=split, n_prompt_tiles=npt),
        grid=(T // tm,),
        in_specs=b_specs + gspecs + [
            pl.BlockSpec((1, 4, BW, D), lambda i: (layer, 0, 0, 0), pipeline_mode=resident),
            pl.BlockSpec((1, D, D), lambda i: (layer, 0, 0), pipeline_mode=resident),
            pl.BlockSpec((tm, D), lambda i: (i, 0))],
        out_specs=pl.BlockSpec((tm, D), lambda i: (i, 0)),
        out_shape=jax.ShapeDtypeStruct((T, D), F32),
        compiler_params=_cparams(("parallel",)),
        name="merge",
    )(*b_args, *([z] * (4 * nc)), wb, wo, x)


def _qknorm_body(sq_ref, sk_ref, dq_ref, dk_ref, gsq_ref, gsk_ref, gdq_ref, gdk_ref, bd_ref,
                 osq_ref, osk_ref, odq_ref, odk_ref):
    bd = bd_ref[:LANES, :LANES]

    def norm(x, g, n):
        x2 = x * x
        ms = jnp.concatenate([_split_dot(x2[:, c:c + LANES], bd) for c in range(0, n, LANES)], axis=1)
        return x * lax.rsqrt(ms + NORM_EPS) * g

    scale = HD ** -0.5
    osq_ref[...] = (norm(sq_ref[...], gsq_ref[...], 2 * BW) * scale).astype(BF16)
    osk_ref[...] = norm(sk_ref[...], gsk_ref[...], BW)
    odq_ref[...] = (norm(dq_ref[...], gdq_ref[...], 2 * BW) * (scale * LOG2E)).astype(BF16)
    odk_ref[...] = norm(dk_ref[...], gdk_ref[...], BW)


def _qknorm(z, gsq, gsk, gdq, gdk, bd):
    T = z.shape[0]
    tm = _pick(T, (512, 256, 128))
    wq, wk = 2 * BW, BW
    row = lambda w, off: pl.BlockSpec((tm, w), lambda i: (i, off // w))
    par = lambda w: pl.BlockSpec((1, w), lambda i: (0, 0))
    out = lambda w: pl.BlockSpec((tm, w), lambda i: (i, 0))
    return pl.pallas_call(
        _qknorm_body,
        grid=(T // tm,),
        in_specs=[row(wq, OFF_SQ), row(wk, OFF_SK), row(wq, OFF_DQ), row(wk, OFF_DK),
                  par(wq), par(wk), par(wq), par(wk),
                  pl.BlockSpec(bd.shape, lambda i: (0, 0))],
        out_specs=[out(wq), out(wk), out(wq), out(wk)],
        out_shape=[jax.ShapeDtypeStruct((T, wq), BF16), jax.ShapeDtypeStruct((T, wk), F32),
                   jax.ShapeDtypeStruct((T, wq), BF16), jax.ShapeDtypeStruct((T, wk), F32)],
        compiler_params=_cparams(("parallel",)),
        name="qk_norm",
    )(z, z, z, z, gsq, gsk, gdq, gdk, bd)


def _sb_body(q_ref, kn_ref, vn_ref, o_ref, *, tq, npair):
    qb = pl.program_id(1)
    hs = range(2 * npair)
    qs = [q_ref[:, e * LANES:(e + 1) * LANES] for e in hs]

    def tri(n):
        r = lax.broadcasted_iota(jnp.int32, (n, n), 0)
        c = lax.broadcasted_iota(jnp.int32, (n, n), 1)
        return (r > c).astype(BF16)

    def visit(state, r0, u, valid):
        kb = [kn_ref[pl.ds(r0, tq), j * LANES:(j + 1) * LANES].astype(BF16) for j in range(npair)]
        vb = [vn_ref[pl.ds(r0, tq), j * LANES:(j + 1) * LANES].astype(BF16) for j in range(npair)]
        zs = [_dot_nt(qs[e], kb[e // 2]) for e in hs]
        lgs = [_neg_softplus(z) for z in zs]
        if valid is not None:
            lgs = [jnp.where(valid, lg, 0.0) for lg in lgs]
        inners = [_split_dot(lg, u) for lg in lgs]
        ws = [jnp.exp(zs[e] + lgs[e] + inners[e] + state[e][0]) for e in hs]
        if valid is not None:
            ws = [jnp.where(valid, w, 0.0) for w in ws]
        pv = [jnp.dot(ws[e].astype(BF16), vb[e // 2], preferred_element_type=F32) for e in hs]
        return tuple((state[e][0] + inners[e][:, :1] + lgs[e][:, :1], state[e][1] + pv[e]) for e in hs)

    zero = (jnp.zeros((tq, 1), F32), jnp.zeros((tq, LANES), F32))

    r = lax.broadcasted_iota(jnp.int32, (tq, tq), 0)
    c = lax.broadcasted_iota(jnp.int32, (tq, tq), 1)
    u_new = tri(tq)
    state = visit((zero,) * (2 * npair), pl.multiple_of(qb * tq, tq), u_new, c < r)

    def live(c):
        i, st = c
        top = functools.reduce(jnp.maximum, [s[0] for s in st])
        return (i < qb) & (jnp.max(top) > SB_UNDERFLOW)

    def step(c):
        i, st = c
        return i + 1, visit(st, pl.multiple_of((qb - 1 - i) * tq, tq), u_new, None)
    _, state = lax.while_loop(live, step, (jnp.int32(0), state))

    lane = lax.broadcasted_iota(jnp.int32, (tq, LANES), 1)
    for j in range(npair):
        o_ref[:, j * LANES:(j + 1) * LANES] = jnp.where(lane < HD, state[2 * j][1], state[2 * j + 1][1])


def _sb_prompt(qn, kn, z, n_rows):
    tq = 256
    npair = 2
    wp = npair * LANES
    vblk = OFF_SV // wp
    return pl.pallas_call(
        functools.partial(_sb_body, tq=tq, npair=npair),
        grid=(NH // (2 * npair), n_rows // tq),
        in_specs=[pl.BlockSpec((tq, 2 * wp), lambda p, i: (i, p)),
                  pl.BlockSpec((n_rows, wp), lambda p, i: (0, p)),
                  pl.BlockSpec((n_rows, wp), lambda p, i: (0, vblk + p))],
        out_specs=pl.BlockSpec((tq, wp), lambda p, i: (i, p)),
        out_shape=jax.ShapeDtypeStruct((n_rows, BW), F32),
        compiler_params=_cparams(("parallel", "arbitrary")),
        name="sb_prompt",
    )(qn, kn, z)


def _q_head(q_ref, h):
    lo = h * LANES + HD * (h % 2)
    return q_ref[:, lo:lo + HD]


def _sb_sample_body(q_ref, kn_ref, vn_ref, kc_ref, vc_ref, o_ref, *, L, n_cache, tkc):
    qs = [_q_head(q_ref, h) for h in range(NH)]

    def tri(n):
        r = lax.broadcasted_iota(jnp.int32, (n, n), 0)
        c = lax.broadcasted_iota(jnp.int32, (n, n), 1)
        return (r > c).astype(BF16)

    def visit(state, kget, vget, u, valid, cached):
        ks = [kget(h).astype(BF16) for h in range(NH)]
        vs = [vget(h).astype(BF16) for h in range(NH)]
        if cached:
            zs = [jnp.dot(qs[h], ks[h], preferred_element_type=F32) for h in range(NH)]
        else:
            zs = [_dot_nt(qs[h], ks[h]) for h in range(NH)]
        lgs = [_neg_softplus(z) for z in zs]
        if valid is not None:
            lgs = [jnp.where(valid, lg, 0.0) for lg in lgs]
        inners = [_split_dot(lg, u) for lg in lgs]
        ws = [jnp.exp(zs[h] + lgs[h] + inners[h] + state[h][0]) for h in range(NH)]
        if valid is not None:
            ws = [jnp.where(valid, w, 0.0) for w in ws]
        ws = [w.astype(BF16) for w in ws]
        if cached:
            pv = [_dot_nt(ws[h], vs[h]) for h in range(NH)]
        else:
            pv = [jnp.dot(ws[h], vs[h], preferred_element_type=F32) for h in range(NH)]
        return tuple((state[h][0] + inners[h][:, :1] + lgs[h][:, :1], state[h][1] + pv[h])
                     for h in range(NH))

    zero = (jnp.zeros((L, 1), F32), jnp.zeros((L, HD), F32))
    r = lax.broadcasted_iota(jnp.int32, (L, L), 0)
    c = lax.broadcasted_iota(jnp.int32, (L, L), 1)
    state = visit((zero,) * NH, lambda h: kn_ref[:, h * HD:(h + 1) * HD],
                  lambda h: vn_ref[:, h * HD:(h + 1) * HD], tri(L), c < r, False)
    u_c = tri(tkc)

    def live(cr):
        i, st = cr
        top = functools.reduce(jnp.maximum, [s[0] for s in st])
        return (i < n_cache) & (jnp.max(top) > SB_UNDERFLOW)

    def step(cr):
        i, st = cr
        r0 = pl.multiple_of((n_cache - 1 - i) * tkc, tkc)
        return i + 1, visit(st, lambda h: kc_ref[0, 0, h, :, pl.ds(r0, tkc)],
                            lambda h: vc_ref[0, 0, h, :, pl.ds(r0, tkc)], u_c, None, True)
    _, state = lax.while_loop(live, step, (jnp.int32(0), state))
    for h in range(NH):
        o_ref[:, h * HD:(h + 1) * HD] = state[h][1]


def _sb_sample(qn, kn, z, cache_k, cache_v, layer, row_off, nb, L):
    P = cache_k.shape[4]
    tkc = _pick(P, (256, 128))
    rb = row_off // L
    cspec = pl.BlockSpec((1, 1, NH, HD, P), lambda b: (layer, b, 0, 0, 0))
    return pl.pallas_call(
        functools.partial(_sb_sample_body, L=L, n_cache=P // tkc, tkc=tkc),
        grid=(nb,),
        in_specs=[pl.BlockSpec((L, NH * LANES), lambda b: (rb + b, 0)),
                  pl.BlockSpec((L, BW), lambda b: (rb + b, 0)),
                  pl.BlockSpec((L, BW), lambda b: (rb + b, OFF_SV // BW)),
                  cspec, cspec],
        out_specs=pl.BlockSpec((L, BW), lambda b: (b, 0)),
        out_shape=jax.ShapeDtypeStruct((nb * L, BW), F32),
        compiler_params=_cparams(("parallel",)),
        name="sb_sample",
    )(qn, kn, z, cache_k, cache_v)


def _sortable(x):
    b = pltpu.bitcast(x + 0.0, jnp.int32)
    return jnp.where(b < 0, b ^ jnp.int32(0x7FFFFFFF), b)


def _select_topk(key_ref, n_blk, tkb, topk, rows):
    def count(pred):
        def body(i, acc):
            c0 = pl.multiple_of(i * tkb, tkb)
            for c in range(tkb // LANES):
                blk = key_ref[:, pl.ds(c0 + c * LANES, LANES)]
                col = c0 + c * LANES + lax.broadcasted_iota(jnp.int32, (rows, LANES), 1)
                acc = acc + jnp.where(pred(blk, col), 1.0, 0.0)
            return acc
        acc = lax.fori_loop(0, n_blk, body, jnp.zeros((rows, LANES), F32))
        return jnp.sum(acc, axis=1, keepdims=True)

    def unresolved(c):
        i, _, n_ge = c
        return (i < 32) & (jnp.max(jnp.abs(n_ge - topk)) > 0)

    def bit_step(c):
        i, thr, n_ge = c
        cand = thr + (jnp.int32(1) << (31 - i))
        n_cand = count(lambda blk, col: blk >= cand)
        take = n_cand >= topk
        return i + 1, jnp.where(take, cand, thr), jnp.where(take, n_cand, n_ge)

    thr0 = jnp.full((rows, 1), jnp.iinfo(jnp.int32).min, jnp.int32)
    n_all = jnp.full((rows, 1), 1.0, F32) * jnp.asarray(n_blk * tkb).astype(F32)
    _, thr, n_ge = lax.while_loop(unresolved, bit_step, (jnp.int32(0), thr0, n_all))
    n_col_bits = max(1, int(math.ceil(math.log2(key_ref.shape[1] + 1))))

    def tie_search():
        need = topk - count(lambda blk, col: blk > thr)

        def col_step(i, j):
            cand = j + (jnp.int32(1) << (n_col_bits - 1 - i))
            n = count(lambda blk, col: (blk == thr) & (col < cand))
            return jnp.where(n < need, cand, j)
        return lax.fori_loop(0, n_col_bits, col_step, jnp.zeros((rows, 1), jnp.int32))

    any_tie = jnp.max(jnp.abs(n_ge - topk)) > 0
    j_hi = lax.cond(any_tie, tie_search,
                    lambda: jnp.full((rows, 1), jnp.iinfo(jnp.int32).max, jnp.int32))
    return thr, j_hi


def _idx_scores(qi, wi, kblk16, k_real, keys_transposed=False):
    acc = None
    for h in range(NH):
        qh = qi[:, h * LANES:h * LANES + k_real].astype(BF16)
        if keys_transposed:
            d = jnp.dot(qh, kblk16, preferred_element_type=F32)
        else:
            d = _dot_nt(qh, kblk16)
        t = ((wi[:, h:h + 1] * (NH ** -0.5)) * (HD ** -0.5)) * jnp.maximum(d, 0.0)
        acc = t if acc is None else acc + t
    return acc


def _dsa_sel_prompt_body(qi_ref, wi_ref, ki_ref, m_ref, key_ref, *, tq, tkb, topk, n_keys):
    qb = pl.program_id(0)
    n_blk = ((qb + 1) * tq + tkb - 1) // tkb
    qi = qi_ref[...]
    wi = wi_ref[...]
    q_chunk = (qb * tq + lax.broadcasted_iota(jnp.int32, (tq, tkb), 0)) // CHUNK

    def score_blk(i, _):
        c0 = pl.multiple_of(i * tkb, tkb)
        s = _idx_scores(qi, wi, ki_ref[pl.ds(c0, tkb), :].astype(BF16), LANES)
        k_chunk = (c0 + lax.broadcasted_iota(jnp.int32, (tq, tkb), 1)) // CHUNK
        s = jnp.where(k_chunk <= q_chunk, s, -jnp.inf)
        key_ref[:, pl.ds(c0, tkb)] = _sortable(s)
        return 0
    lax.fori_loop(0, n_blk, score_blk, 0)

    thr, j_hi = _select_topk(key_ref, n_blk, tkb, topk, tq)
    neg_inf_key = _sortable(jnp.full((1, 1), -jnp.inf, F32))

    def write_blk(i, _):
        c0 = pl.multiple_of(i * tkb, tkb)
        blk = key_ref[:, pl.ds(c0, tkb)]
        col = c0 + lax.broadcasted_iota(jnp.int32, (tq, tkb), 1)
        sel = ((blk > thr) | ((blk == thr) & (col <= j_hi))) & (blk > neg_inf_key)
        m_ref[:, pl.ds(c0, tkb)] = jnp.where(sel, 0.0, NEG_BIG).astype(BF16)
        return 0
    lax.fori_loop(0, n_blk, write_blk, 0)

    def zero_blk(i, _):
        m_ref[:, pl.ds(pl.multiple_of(i * tkb, tkb), tkb)] = jnp.full((tq, tkb), NEG_BIG, BF16)
        return 0
    lax.fori_loop(n_blk, n_keys // tkb, zero_blk, 0)


def _dsa_sel_prompt(z, n_rows, topk):
    tq, tkb = 128, 512
    return pl.pallas_call(
        functools.partial(_dsa_sel_prompt_body, tq=tq, tkb=tkb, topk=topk, n_keys=n_rows),
        grid=(n_rows // tq,),
        in_specs=[pl.BlockSpec((tq, NH * LANES), lambda i: (i, OFF_QI // (NH * LANES))),
                  pl.BlockSpec((tq, LANES), lambda i: (i, OFF_WI // LANES)),
                  pl.BlockSpec((n_rows, LANES), lambda i: (0, OFF_KI // LANES))],
        out_specs=pl.BlockSpec((tq, n_rows), lambda i: (i, 0)),
        out_shape=jax.ShapeDtypeStruct((n_rows, n_rows), BF16),
        scratch_shapes=[pltpu.VMEM((tq, n_rows), jnp.int32)],
        compiler_params=_cparams(("parallel",)),
        name="dsa_select_prompt",
    )(z, z, z)


def _dsa_sel_sample_body(qi_ref, wi_ref, kin_ref, kic_ref, m_ref, key_ref, *, L, G, P, tkb, topk):
    n_c = P // tkb
    rows = G * L
    for g in range(G):
        rs = slice(g * L, (g + 1) * L)
        qi = qi_ref[rs, :]
        wi = wi_ref[rs, :]

        def score_blk(i, _, g=g, rs=rs, qi=qi, wi=wi):
            c0 = pl.multiple_of(i * tkb, tkb)
            s = _idx_scores(qi, wi, kic_ref[0, g, :, pl.ds(c0, tkb)].astype(BF16), HD, keys_transposed=True)
            key_ref[rs, pl.ds(c0, tkb)] = _sortable(s)
            return 0
        lax.fori_loop(0, n_c, score_blk, 0)

        knew = jnp.concatenate([kin_ref[rs, :], jnp.zeros((tkb - L, LANES), F32)], axis=0).astype(BF16)
        s = _idx_scores(qi, wi, knew, LANES)
        col = lax.broadcasted_iota(jnp.int32, (L, tkb), 1)
        key_ref[rs, pl.ds(P, tkb)] = _sortable(jnp.where(col < L, s, -jnp.inf))

    thr, j_hi = _select_topk(key_ref, n_c + 1, tkb, topk, rows)
    neg_inf_key = _sortable(jnp.full((1, 1), -jnp.inf, F32))

    def write_blk(i, _):
        c0 = pl.multiple_of(i * tkb, tkb)
        blk = key_ref[:, pl.ds(c0, tkb)]
        cc = c0 + lax.broadcasted_iota(jnp.int32, (rows, tkb), 1)
        sel = ((blk > thr) | ((blk == thr) & (cc <= j_hi))) & (blk > neg_inf_key)
        m_ref[:, pl.ds(c0, tkb)] = jnp.where(sel, 0.0, NEG_BIG).astype(BF16)
        return 0
    lax.fori_loop(0, n_c + 1, write_blk, 0)


def _dsa_sel_sample(z, cache_ki, layer, row_off, nb, L, topk):
    P = cache_ki.shape[3]
    tkb = _pick(P, (512, 256, 128))
    G = _pick(nb, (4, 2, 1))
    rows = G * L
    rb = row_off // rows
    return pl.pallas_call(
        functools.partial(_dsa_sel_sample_body, L=L, G=G, P=P, tkb=tkb, topk=topk),
        grid=(nb // G,),
        in_specs=[pl.BlockSpec((rows, NH * LANES), lambda b: (rb + b, OFF_QI // (NH * LANES))),
                  pl.BlockSpec((rows, LANES), lambda b: (rb + b, OFF_WI // LANES)),
                  pl.BlockSpec((rows, LANES), lambda b: (rb + b, OFF_KI // LANES)),
                  pl.BlockSpec((1, G, HD, P), lambda b: (layer, b, 0, 0))],
        out_specs=pl.BlockSpec((rows, P + tkb), lambda b: (b, 0)),
        out_shape=jax.ShapeDtypeStruct((nb * L, P + tkb), BF16),
        scratch_shapes=[pltpu.VMEM((rows, P + tkb), jnp.int32)],
        compiler_params=_cparams(("parallel",)),
        name="dsa_select_sample",
    )(z, z, z, cache_ki)


def _dsa_attn_body(q_ref, m_ref, kn_ref, vn_ref, o_ref, *, tq, tkb):
    qb = pl.program_id(1)
    n_blk = ((qb + 1) * tq + tkb - 1) // tkb
    qs = [q_ref[:, e * LANES:(e + 1) * LANES] for e in range(2)]

    def visit(state, kblk, vblk, mblk):
        kb16 = kblk.astype(BF16)
        vb16 = vblk.astype(BF16)
        bias = mblk.astype(F32)
        hs = range(2)
        ss = [_dot_nt(qs[e], kb16) + bias for e in hs]
        ms = [jnp.maximum(state[e][0], jnp.max(ss[e], axis=1, keepdims=True)) for e in hs]
        ps = [jnp.exp2(ss[e] - ms[e]) for e in hs]
        al = [jnp.exp2(state[e][0] - ms[e]) for e in hs]
        pv = [jnp.dot(ps[e].astype(BF16), vb16, preferred_element_type=F32) for e in hs]
        return tuple((ms[e], al[e] * state[e][1] + jnp.sum(ps[e], axis=1, keepdims=True),
                      al[e] * state[e][2] + pv[e]) for e in hs)

    init = (jnp.full((tq, 1), NEG_BIG, F32), jnp.zeros((tq, 1), F32), jnp.zeros((tq, LANES), F32))
    state = (init, init)

    def step(i, st):
        c0 = pl.multiple_of(i * tkb, tkb)
        return visit(st, kn_ref[pl.ds(c0, tkb), :], vn_ref[pl.ds(c0, tkb), :], m_ref[:, pl.ds(c0, tkb)])
    state = lax.fori_loop(0, n_blk, step, state)

    lane = lax.broadcasted_iota(jnp.int32, (tq, LANES), 1)
    o0 = state[0][2] / state[0][1]
    o1 = state[1][2] / state[1][1]
    o_ref[...] = jnp.where(lane < HD, o0, o1)


def _dsa_attn_prompt(qn, kn, z, mask, n_rows):
    tq, tkb = min(512, n_rows), min(1024, n_rows)
    vblk = OFF_DV // LANES
    return pl.pallas_call(
        functools.partial(_dsa_attn_body, tq=tq, tkb=tkb),
        grid=(NH // 2, n_rows // tq),
        in_specs=[pl.BlockSpec((tq, 2 * LANES), lambda p, i: (i, p)),
                  pl.BlockSpec((tq, n_rows), lambda p, i: (i, 0)),
                  pl.BlockSpec((n_rows, LANES), lambda p, i: (0, p)),
                  pl.BlockSpec((n_rows, LANES), lambda p, i: (0, vblk + p))],
        out_specs=pl.BlockSpec((tq, LANES), lambda p, i: (i, p)),
        out_shape=jax.ShapeDtypeStruct((n_rows, BW), F32),
        compiler_params=_cparams(("parallel", "arbitrary")),
        name="dsa_attn_prompt",
    )(qn, mask, kn, z)


def _dsa_attn_sample_body(q_ref, m_ref, kn_ref, vn_ref, kc_ref, vc_ref, o_ref, *, L, tkb, n_cache):
    qs = [_q_head(q_ref, h) for h in range(NH)]

    def visit(state, kget, vget, mblk, cached):
        bias = mblk.astype(F32)
        ks = [kget(h).astype(BF16) for h in range(NH)]
        vs = [vget(h).astype(BF16) for h in range(NH)]
        if cached:
            ss = [jnp.dot(qs[h], ks[h], preferred_element_type=F32) + bias for h in range(NH)]
        else:
            ss = [_dot_nt(qs[h], ks[h]) + bias for h in range(NH)]
        ms = [jnp.maximum(state[h][0], jnp.max(ss[h], axis=1, keepdims=True)) for h in range(NH)]
        ps = [jnp.exp2(ss[h] - ms[h]) for h in range(NH)]
        al = [jnp.exp2(state[h][0] - ms[h]) for h in range(NH)]
        if cached:
            pv = [_dot_nt(ps[h].astype(BF16), vs[h]) for h in range(NH)]
        else:
            pv = [jnp.dot(ps[h].astype(BF16), vs[h], preferred_element_type=F32) for h in range(NH)]
        return tuple((ms[h], al[h] * state[h][1] + jnp.sum(ps[h], axis=1, keepdims=True),
                      al[h] * state[h][2] + pv[h]) for h in range(NH))

    init = (jnp.full((L, 1), NEG_BIG, F32), jnp.zeros((L, 1), F32), jnp.zeros((L, HD), F32))

    def step(i, st):
        c0 = pl.multiple_of(i * tkb, tkb)
        return visit(st, lambda h: kc_ref[0, 0, h, :, pl.ds(c0, tkb)],
                     lambda h: vc_ref[0, 0, h, :, pl.ds(c0, tkb)], m_ref[:, pl.ds(c0, tkb)], True)
    state = lax.fori_loop(0, n_cache, step, (init,) * NH)

    pad = jnp.zeros((tkb - L, HD), F32)
    state = visit(state, lambda h: jnp.concatenate([kn_ref[:, h * HD:(h + 1) * HD], pad], axis=0),
                  lambda h: jnp.concatenate([vn_ref[:, h * HD:(h + 1) * HD], pad], axis=0),
                  m_ref[:, pl.ds(n_cache * tkb, tkb)], False)
    for h in range(NH):
        o_ref[:, h * HD:(h + 1) * HD] = state[h][2] / state[h][1]


def _dsa_attn_sample(qn, kn, z, mask, cache_k, cache_v, layer, row_off, nb, L):
    P = cache_k.shape[4]
    tkb = mask.shape[1] - P
    rb = row_off // L
    cspec = pl.BlockSpec((1, 1, NH, HD, P), lambda b: (layer, b, 0, 0, 0))
    return pl.pallas_call(
        functools.partial(_dsa_attn_sample_body, L=L, tkb=tkb, n_cache=P // tkb),
        grid=(nb,),
        in_specs=[pl.BlockSpec((L, NH * LANES), lambda b: (rb + b, 0)),
                  pl.BlockSpec((L, P + tkb), lambda b: (b, 0)),
                  pl.BlockSpec((L, BW), lambda b: (rb + b, 0)),
                  pl.BlockSpec((L, BW), lambda b: (rb + b, OFF_DV // BW)),
                  cspec, cspec],
        out_specs=pl.BlockSpec((L, BW), lambda b: (b, 0)),
        out_shape=jax.ShapeDtypeStruct((nb * L, BW), F32),
        compiler_params=_cparams(("parallel",)),
        name="dsa_attn_sample",
    )(qn, mask, kn, z, cache_k, cache_v)


def _gla_consts(C):
    nlev = int(math.log2(C))
    t = np.arange(C)
    cum = (t[:, None] >= t[None, :]).astype(np.float32)
    sel = np.zeros((nlev * C, C), np.float32)
    for l in range(nlev):
        m = 1 << l
        mid = (t // (2 * m)) * 2 * m + m
        sel[l * C + t, mid - 1] = 1.0
    return jnp.asarray(cum, BF16), jnp.asarray(sel, BF16)


def _gla_body(gq_ref, gk_ref, gv_ref, gr_ref, ga_ref, a2_ref, ab_ref, on_ref, cum_ref, sel_ref,
              s0_ref, o_ref, sout_ref, st_ref, *, C, NS, n_steps):
    ci = pl.program_id(1)
    nlev = int(math.log2(C))

    @pl.when(ci == 0)
    def _():
        st_ref[...] = s0_ref[0]

    row = lax.broadcasted_iota(jnp.int32, (C, C), 0)
    col = lax.broadcasted_iota(jnp.int32, (C, C), 1)
    rowl = lax.broadcasted_iota(jnp.int32, (C, LANES), 0)
    hs = range(GLA_H)
    sls = [slice(h * LANES, (h + 1) * LANES) for h in hs]
    st = [st_ref[h] for h in hs]
    for sub in range(NS):
        rs = slice(sub * C, (sub + 1) * C)
        x = jnp.dot(ga_ref[rs, :].astype(BF16), a2_ref[...], preferred_element_type=F32) + ab_ref[...]
        g = _neg_softplus(-x) / GLA_GATE_NORM
        b = _split_dot_left(cum_ref[...], g)
        c_all = _split_dot_left(sel_ref[...], b)
        q = [gq_ref[rs, sl] * (HD ** -0.5) for sl in sls]
        k = [gk_ref[rs, sl] for sl in sls]
        v16 = [gv_ref[rs, sl].astype(BF16) for sl in sls]
        bh = [b[:, sl] for sl in sls]
        o_inter = [_dot_nt((q[h] * jnp.exp(bh[h])).astype(BF16), st[h].astype(BF16)) for h in hs]
        b_last = [bh[h][C - 1:C, :] for h in hs]
        kv = [_dot_tn(v16[h], (k[h] * jnp.exp(b_last[h] - bh[h])).astype(BF16)) for h in hs]
        a = [jnp.where(row == col, jnp.sum(q[h] * k[h], axis=1, keepdims=True), 0.0) for h in hs]
        for l in range(nlev):
            m = 1 << l
            later = (rowl & m) != 0
            same_seg = (row // (2 * m)) == (col // (2 * m))
            ch = [c_all[l * C:(l + 1) * C, sl] for sl in sls]
            qe = [jnp.where(later, q[h] * jnp.exp(jnp.where(later, bh[h] - ch[h], 0.0)), 0.0) for h in hs]
            ke = [jnp.where(later, 0.0, k[h] * jnp.exp(jnp.where(later, 0.0, ch[h] - bh[h]))) for h in hs]
            al = [_dot_nt(qe[h].astype(BF16), ke[h].astype(BF16)) for h in hs]
            a = [a[h] + jnp.where(same_seg, al[h], 0.0) for h in hs]
        o = [jnp.dot(a[h].astype(BF16), v16[h], preferred_element_type=F32) + o_inter[h] for h in hs]
        st = [st[h] * jnp.exp(b_last[h]) + kv[h] for h in hs]
        for h in hs:
            on = o[h] * lax.rsqrt(jnp.mean(o[h] * o[h], axis=-1, keepdims=True) + NORM_EPS) * on_ref[...]
            gr = gr_ref[rs, sls[h]]
            o_ref[rs, sls[h]] = on * (gr * jax.nn.sigmoid(gr))
    for h in hs:
        st_ref[h] = st[h]

    @pl.when(ci == n_steps - 1)
    def _():
        sout_ref[0] = st_ref[...]


def _gla(z, a2p, abp, onorm, s0t, row_off, n_seq, L):
    C = min(CHUNK, L)
    NS = 2 if (L // C) % 2 == 0 else 1
    R = NS * C
    n_steps = L // R
    rb = row_off // R
    cum, sel = _gla_consts(C)
    blk = lambda off: pl.BlockSpec((R, BW), lambda s, c: (rb + s * n_steps + c, off // BW))
    full = lambda a: pl.BlockSpec(a.shape, lambda s, c: (0,) * a.ndim)
    return pl.pallas_call(
        functools.partial(_gla_body, C=C, NS=NS, n_steps=n_steps),
        grid=(n_seq, n_steps),
        in_specs=[blk(OFF_GQ), blk(OFF_GK), blk(OFF_GV), blk(OFF_GR),
                  pl.BlockSpec((R, LANES), lambda s, c: (rb + s * n_steps + c, OFF_GA // LANES)),
                  full(a2p), full(abp), full(onorm), full(cum), full(sel),
                  pl.BlockSpec((1, GLA_H, GLA_DV, LANES), lambda s, c: (s, 0, 0, 0))],
        out_specs=[pl.BlockSpec((R, BW), lambda s, c: (s * n_steps + c, 0)),
                   pl.BlockSpec((1, GLA_H, GLA_DV, LANES), lambda s, c: (s, 0, 0, 0))],
        out_shape=[jax.ShapeDtypeStruct((n_seq * L, BW), F32),
                   jax.ShapeDtypeStruct((n_seq, GLA_H, GLA_DV, LANES), F32)],
        scratch_shapes=[pltpu.VMEM((GLA_H, GLA_DV, LANES), F32)],
        compiler_params=_cparams(("parallel", "arbitrary")),
        name="gla",
    )(z, z, z, z, z, a2p, abp, onorm, cum, sel, s0t)


def _rw_prep_body(z_ref, head_ref, mu_ref, w0_ref, a0_ref, kkp_ref, kap_ref, rkp_ref,
                  w2_ref, a2_ref, g2_ref, seg_ref,
                  r_ref, w_ref, k_ref, v_ref, nkk_ref, kka_ref, gate_ref, bonus_ref, *, grp):
    zz = z_ref[...]
    tm, wd = zz.shape
    heads = head_ref[...]
    head_rows = jnp.concatenate([jnp.broadcast_to(heads[g:g + 1, :], (grp, wd))
                                 for g in range(tm // grp)], axis=0)
    row = lax.broadcasted_iota(jnp.int32, (tm, wd), 0)
    prev = jnp.where(row % grp == 0, head_rows, pltpu.roll(zz, 1, axis=0))
    zm = zz + mu_ref[...] * (prev - zz)
    rr = zm[:, 0:BW]
    rk = zm[:, BW:2 * BW]
    rv = zm[:, 2 * BW:3 * BW]
    lora = zm[:, 3 * BW:3 * BW + LANES]
    gl = zm[:, 3 * BW + LANES:3 * BW + 2 * LANES]
    xw = w0_ref[...] + jnp.dot(jnp.tanh(lora).astype(BF16), w2_ref[...], preferred_element_type=F32)
    wlog = _neg_softplus(-xw) - 0.5
    a = jax.nn.sigmoid(a0_ref[...] + jnp.dot(lora.astype(BF16), a2_ref[...], preferred_element_type=F32))
    gate = jnp.dot(jax.nn.sigmoid(gl).astype(BF16), g2_ref[...], preferred_element_type=F32)
    kkf = rk * kkp_ref[...]
    nrm = jnp.sqrt(_split_dot(kkf * kkf, seg_ref[...]))
    kk = kkf / jnp.maximum(nrm, 1e-12)
    kmod = rk * (1.0 + (a - 1.0) * kap_ref[...])
    coef = _split_dot(rr * kmod * rkp_ref[...], seg_ref[...])
    r_ref[...] = rr
    w_ref[...] = -jnp.exp(wlog)
    k_ref[...] = kmod
    v_ref[...] = rv
    nkk_ref[...] = -kk
    kka_ref[...] = kk * a
    gate_ref[...] = gate
    bonus_ref[...] = coef * rv


def _rw_prep(z, heads, grp, mu, w0, a0, kkp, kap, rkp, w2p, a2p, g2, seg):
    T = z.shape[0]
    tm = SUBLANES * grp
    assert T % tm == 0, (T, tm)
    W = 4 * BW
    par = lambda a: pl.BlockSpec(a.shape, lambda i: (0,) * a.ndim)
    out = pl.BlockSpec((tm, BW), lambda i: (i, 0))
    return pl.pallas_call(
        functools.partial(_rw_prep_body, grp=grp),
        grid=(T // tm,),
        in_specs=[pl.BlockSpec((tm, W), lambda i: (i, 0)), pl.BlockSpec((SUBLANES, W), lambda i: (i, 0)),
                  par(mu), par(w0), par(a0), par(kkp), par(kap), par(rkp),
                  par(w2p), par(a2p), par(g2), par(seg)],
        out_specs=[out] * 8,
        out_shape=[jax.ShapeDtypeStruct((T, BW), F32)] * 8,
        compiler_params=_cparams(("parallel",)),
        name="rwkv_prep",
    )(z, heads, mu, w0, a0, kkp, kap, rkp, w2p, a2p, g2, seg)


def _mm3(a, b):
    ah = a.astype(BF16)
    al = (a - ah.astype(F32)).astype(BF16)
    bh = b.astype(BF16)
    bl = (b - bh.astype(F32)).astype(BF16)
    d = lambda x, y: jnp.dot(x, y, preferred_element_type=F32)
    return d(ah, bh) + (d(ah, bl) + d(al, bh))


def _rw_chunk_body(r_ref, lw_ref, k_ref, v_ref, nkk_ref, kka_ref, s0_ref, o_ref, sout_ref, s_ref,
                   *, TT, L, n_tiles):
    ti = pl.program_id(0)
    npair = NH // 2
    nblk = TT // RW_BLK
    row = lax.broadcasted_iota(jnp.int32, (TT, TT), 0)
    col = lax.broadcasted_iota(jnp.int32, (TT, TT), 1)
    same = (row // RW_BLK) == (col // RW_BLK)
    strict = same & (col < row)
    incl = same & (col <= row)
    tri = jnp.where(incl, 1.0, 0.0).astype(BF16)
    last = jnp.where(same & (col % RW_BLK == RW_BLK - 1), 1.0, 0.0).astype(BF16)
    eye = jnp.where(row == col, 1.0, 0.0)
    pair_diag = (row < HD) == (col < HD)
    lo_half = lax.broadcasted_iota(jnp.int32, (TT, LANES), 1) < HD
    lo_half_b = lax.broadcasted_iota(jnp.int32, (RW_BLK, LANES), 1) < HD

    r, lw, k, v = r_ref[...], lw_ref[...], k_ref[...], v_ref[...]
    nk, ka = nkk_ref[...], kka_ref[...]
    lc = _split_dot_left(tri, lw)
    le = _split_dot_left(last, lc)
    nkt = nk * jnp.exp(lc - lw)
    rt = r * jnp.exp(lc)
    inv = jnp.exp(-lc)
    kah = ka * inv
    kh = k * inv
    rem = jnp.exp(le - lc)
    kaw = (ka * rem).astype(BF16)
    kw = (k * rem).astype(BF16)
    wb = jnp.exp(le)

    sls = [slice(p * LANES, (p + 1) * LANES) for p in range(npair)]
    v16 = [v[:, sl].astype(BF16) for sl in sls]
    heads = [(p, e) for p in range(npair) for e in range(2)]
    grams = []
    for p, e in heads:
        hm = lo_half if e == 0 else jnp.logical_not(lo_half)
        lhs = jnp.concatenate([jnp.where(hm, nkt[:, sls[p]], 0.0), jnp.where(hm, rt[:, sls[p]], 0.0)],
                              axis=0).astype(BF16)
        rhs = jnp.concatenate([kah[:, sls[p]], kh[:, sls[p]]], axis=0).astype(BF16)
        grams.append(_dot_nt(lhs, rhs))
    n1 = [jnp.where(strict, g[:TT, :TT], 0.0) for g in grams]
    bt = [jnp.where(strict, g[:TT, TT:], 0.0).astype(BF16) for g in grams]
    cmat = [jnp.where(incl, g[TT:, :TT], 0.0).astype(BF16) for g in grams]
    et = [jnp.where(incl, g[TT:, TT:], 0.0).astype(BF16) for g in grams]
    n2 = [_mm3(a, a) for a in n1]
    n4 = [_mm3(a, a) for a in n2]
    n8 = [_mm3(a, a) for a in n4]
    tinv = [eye + a for a in n1]
    for power in (n2, n4, n8):
        tinv = [t + _mm3(t, a) for t, a in zip(tinv, power)]
    vb_h = [jnp.dot(b_, v16[p], preferred_element_type=F32) for b_, (p, e) in zip(bt, heads)]
    ev_h = [jnp.dot(e_, v16[p], preferred_element_type=F32) for e_, (p, e) in zip(et, heads)]
    vb = [jnp.where(lo_half, vb_h[2 * p], vb_h[2 * p + 1]) for p in range(npair)]
    ev = [jnp.where(lo_half, ev_h[2 * p], ev_h[2 * p + 1]) for p in range(npair)]

    if L >= TT:
        @pl.when(ti == 0)
        def _():
            s_ref[...] = s0_ref[0]
        state = {(0, p): s_ref[p] for p in range(npair)}
        rounds = [[j] for j in range(nblk)]
    else:
        bps = L // RW_BLK
        state = {(q, p): s0_ref[q, p] for q in range(TT // L) for p in range(npair)}
        rounds = [[q * bps + i for q in range(TT // L)] for i in range(bps)]
    sa_rows = [[None] * nblk for _ in range(npair)]
    os_rows = [[None] * nblk for _ in range(npair)]
    for blocks in rounds:
        items = [(j, p) for j in blocks for p in range(npair)]
        seq = lambda j: (j * RW_BLK) // L if L < TT else 0
        xs = {}
        for j, p in items:
            bs = slice(j * RW_BLK, (j + 1) * RW_BLK)
            s = state[(seq(j), p)]
            s_hi = s.astype(BF16)
            s_lo = (s - s_hi.astype(F32)).astype(BF16)
            lhs = jnp.concatenate([nkt[bs, sls[p]], rt[bs, sls[p]]], axis=0).astype(BF16)
            xs[(j, p)] = (jnp.dot(lhs, s_hi, preferred_element_type=F32)
                          + jnp.dot(lhs, s_lo, preferred_element_type=F32))
        sa = {}
        for j, p in items:
            b0 = j * RW_BLK
            bs = slice(b0, b0 + RW_BLK)
            parts = [xs[(j, p)][:RW_BLK] + vb[p][bs]]
            if b0:
                parts.insert(0, jnp.zeros((b0, LANES), F32))
            if TT - b0 - RW_BLK:
                parts.append(jnp.zeros((TT - b0 - RW_BLK, LANES), F32))
            xfull = jnp.concatenate(parts, axis=0)
            sa[(j, p)] = jnp.where(lo_half_b, _mm3(tinv[2 * p][bs], xfull),
                                   _mm3(tinv[2 * p + 1][bs], xfull))
        for j, p in items:
            b0 = j * RW_BLK
            bs = slice(b0, b0 + RW_BLK)
            u = (_dot_tn(kaw[bs, sls[p]], sa[(j, p)].astype(BF16)) + _dot_tn(kw[bs, sls[p]], v16[p][bs]))
            wcol = jnp.broadcast_to(wb[b0:b0 + 1, sls[p]], (LANES, LANES)).T
            state[(seq(j), p)] = wcol * state[(seq(j), p)] + jnp.where(pair_diag, u, 0.0)
            sa_rows[p][j] = sa[(j, p)]
            os_rows[p][j] = xs[(j, p)][RW_BLK:]
    if L < TT:
        for (q, p), s in state.items():
            sout_ref[q, p] = s

    for p in range(npair):
        sa_t = jnp.concatenate(sa_rows[p], axis=0).astype(BF16)
        o_sa = jnp.where(lo_half, jnp.dot(cmat[2 * p], sa_t, preferred_element_type=F32),
                         jnp.dot(cmat[2 * p + 1], sa_t, preferred_element_type=F32))
        o_ref[:, sls[p]] = jnp.concatenate(os_rows[p], axis=0) + o_sa + ev[p]
    if L >= TT:
        for p in range(npair):
            s_ref[p] = state[(0, p)]

        @pl.when(ti == n_tiles - 1)
        def _():
            for p in range(npair):
                sout_ref[0, p] = state[(0, p)]


def _rw_chunk(r, lw, k, v, nkk, kka, s0, row_off, n_seq, L):
    TT = LANES
    assert L % RW_BLK == 0 and (L % TT == 0 or TT % L == 0)
    n_tok = n_seq * L
    n_tiles = n_tok // TT
    rb = row_off // TT
    ns = max(1, TT // L)
    tok = pl.BlockSpec((TT, BW), lambda i: (rb + i, 0))
    if ns == 1:
        sspec = pl.BlockSpec((1, NH // 2, LANES, LANES), lambda i: (0, 0, 0, 0))
    else:
        sspec = pl.BlockSpec((ns, NH // 2, LANES, LANES), lambda i: (i, 0, 0, 0))
    return pl.pallas_call(
        functools.partial(_rw_chunk_body, TT=TT, L=L, n_tiles=n_tiles),
        grid=(n_tiles,),
        in_specs=[tok] * 6 + [sspec],
        out_specs=[pl.BlockSpec((TT, BW), lambda i: (i, 0)), sspec],
        out_shape=[jax.ShapeDtypeStruct((n_tok, BW), F32),
                   jax.ShapeDtypeStruct(s0.shape, F32)],
        scratch_shapes=[pltpu.VMEM((NH // 2, LANES, LANES), F32)],
        compiler_params=_cparams(("arbitrary",)),
        name="rwkv_chunk",
    )(r, lw, k, v, nkk, kka, s0)


def _rw_post_body(o_ref, bonus_ref, gate_ref, lnw_ref, lnb_ref, seg_ref, out_ref):
    o = o_ref[...]
    mu = _split_dot(o, seg_ref[...]) * (1.0 / HD)
    d = o - mu
    var = _split_dot(d * d, seg_ref[...]) * (1.0 / HD)
    on = d * lax.rsqrt(var + RW_LN_EPS) * lnw_ref[...] + lnb_ref[...]
    out_ref[...] = (on + bonus_ref[...]) * gate_ref[...]


def _rw_post(o, bonus, gate, lnw, lnb, seg):
    T = o.shape[0]
    tm = _pick(T, (512, 256, 128))
    blk = pl.BlockSpec((tm, BW), lambda i: (i, 0))
    par = lambda a: pl.BlockSpec(a.shape, lambda i: (0,) * a.ndim)
    return pl.pallas_call(
        _rw_post_body,
        grid=(T // tm,),
        in_specs=[blk, blk, blk, par(lnw), par(lnb), par(seg)],
        out_specs=blk,
        out_shape=jax.ShapeDtypeStruct((T, BW), F32),
        compiler_params=_cparams(("parallel",)),
        name="rwkv_post",
    )(o, bonus, gate, lnw, lnb, seg)


def _heads_alt(w):
    lead = w.shape[:-1]
    w4 = w.reshape(*lead, NH // 2, 2, HD)
    zero = jnp.zeros((*lead, NH // 2, HD), w.dtype)
    ev = jnp.concatenate([w4[..., 0, :], zero], -1)
    od = jnp.concatenate([zero, w4[..., 1, :]], -1)
    return jnp.stack([ev, od], -2).reshape(*lead, NH * LANES)


def _heads_lo(w, nh):
    lead = w.shape[:-1]
    w3 = w.reshape(*lead, nh, HD)
    return jnp.concatenate([w3, jnp.zeros_like(w3)], -1).reshape(*lead, nh * LANES)


def _padc(w, n):
    return jnp.pad(w, [(0, 0)] * (w.ndim - 1) + [(0, n - w.shape[-1])])


def _rw_reorder(a):
    o = 0
    parts = {}
    for name, wd in (("rr", BW), ("wl", RW_LORA), ("rk", BW), ("rv", BW), ("al", RW_LORA), ("gl", RW_GATE_LORA)):
        parts[name] = a[..., o:o + wd]
        o += wd
    return jnp.concatenate([parts[n] for n in ("rr", "rk", "rv", "wl", "al", "gl")], axis=-1)


def _rw_unorder(a):
    rr, rk, rv = a[..., 0:BW], a[..., BW:2 * BW], a[..., 2 * BW:3 * BW]
    wl = a[..., 3 * BW:3 * BW + RW_LORA]
    al = a[..., 3 * BW + RW_LORA:3 * BW + 2 * RW_LORA]
    gl = a[..., 3 * BW + 2 * RW_LORA:RW_COLS]
    return jnp.concatenate([rr, wl, rk, rv, al, gl], axis=-1)


def _relayout_w_in(w, d_model):
    w = w.astype(BF16)
    o = 0

    def take(n):
        nonlocal o
        s = w[..., o:o + n]
        o += n
        return s
    sq, sk, sv = take(BW), take(BW), take(BW)
    dq, dk, dv, qi, ki, wi = take(BW), take(BW), take(BW), take(BW), take(HD), take(NH)
    gq, gk = take(GLA_H * HD), take(GLA_H * HD)
    gv, gr, ga = take(BW), take(BW), take(GLA_LOWRANK)
    rw = take(RW_COLS)
    gate = take(4 * d_model)
    cols = [_padc(_rw_reorder(rw), 4 * BW), _heads_alt(sq), sk, sv, _heads_alt(dq), dk, dv,
            _heads_lo(qi, NH), _heads_lo(gq, GLA_H), _heads_lo(gk, GLA_H), gv, gr,
            _padc(ki, LANES), _padc(wi, LANES), _padc(ga, 2 * LANES), gate]
    return jnp.concatenate(cols, axis=-1)


def _relayout_ffn(w_up, w_down, tf):
    f = w_up.shape[-1] // 2
    fp = -(-f // tf) * tf
    wup = jnp.concatenate([_padc(w_up[..., :f].astype(BF16), fp), _padc(w_up[..., f:].astype(BF16), fp)],
                          axis=-1)
    wdn = jnp.pad(w_down.astype(BF16), ((0, 0), (0, fp - f), (0, 0)))
    return wup, wdn


def _gain_alt(g):
    return jnp.tile(g, 2 * NH)[None, :]


def kernel(x_prompt, x_sample, cache_sb_k, cache_sb_v, cache_dsa_k, cache_dsa_v, cache_dsa_kidx, state_gla, state_rwkv, state_rwkv_shift, ffn1_norm, ffn1_up, ffn1_down, mix_norm, w_in, sb_qnorm, sb_knorm, dsa_qnorm, dsa_knorm, gla_a2, gla_ab, gla_onorm, rwkv_mu, rwkv_w0, rwkv_w2, rwkv_a0, rwkv_a2, rwkv_g2, rwkv_kk, rwkv_ka, rwkv_rk, rwkv_lnw, rwkv_lnb, w_branch, w_out, ffn2_norm, ffn2_up, ffn2_down):
    depth = w_in.shape[0]
    bp, lp, d_model = x_prompt.shape
    nb, ls, _ = x_sample.shape
    assert bp == 1, "prompt group is one sequence"
    tp = bp * lp
    ts = nb * ls
    past = cache_sb_k.shape[2]
    to_pos_minor = lambda c: jnp.transpose(c, (0, 1, 3, 4, 2))
    ck_sb, cv_sb = to_pos_minor(cache_sb_k), to_pos_minor(cache_sb_v)
    ck_dsa, cv_dsa = to_pos_minor(cache_dsa_k), to_pos_minor(cache_dsa_v)
    cki = jnp.transpose(cache_dsa_kidx, (0, 1, 3, 2))
    tf = 512

    x = jnp.concatenate([x_prompt.reshape(tp, d_model), x_sample.reshape(ts, d_model)], axis=0)

    bd = jnp.asarray(np.kron(np.eye(2), np.full((HD, HD), 1.0 / HD)), BF16)
    seg = jnp.asarray(np.kron(np.eye(NH), np.ones((HD, HD))), BF16)
    topk_p = min(TOPK_MAX, lp // 4)
    topk_s = min(TOPK_MAX, (past + ls) // 4)

    new_p = [[] for _ in range(8)]
    new_s = [[] for _ in range(8)]
    wup1, wdn1 = _relayout_ffn(ffn1_up, ffn1_down, tf)
    wup2, wdn2 = _relayout_ffn(ffn2_up, ffn2_down, tf)
    win = _relayout_w_in(w_in, d_model)
    wbr = w_branch.astype(BF16)
    wout = w_out.astype(BF16)
    for i in range(depth):
        x = _ffn(x, ffn1_norm[:, None], wup1, wdn1, tf, i)
        z = _proj(x, mix_norm[:, None], win, i)

        sqn, skn, dqn, dkn = _qknorm(z, _gain_alt(sb_qnorm[i]), jnp.tile(sb_knorm[i], NH)[None],
                                     _gain_alt(dsa_qnorm[i]), jnp.tile(dsa_knorm[i], NH)[None], bd)
        o_sb = (_sb_prompt(sqn, skn, z, tp), _sb_sample(sqn, skn, z, ck_sb, cv_sb, i, tp, nb, ls))
        mask_p = _dsa_sel_prompt(z, tp, topk_p)
        mask_s = _dsa_sel_sample(z, cki, i, tp, nb, ls, topk_s)
        o_dsa = (_dsa_attn_prompt(dqn, dkn, z, mask_p, tp),
                 _dsa_attn_sample(dqn, dkn, z, mask_s, ck_dsa, cv_dsa, i, tp, nb, ls))

        a2p = jnp.pad(_heads_lo(gla_a2[i], GLA_H), ((0, LANES - GLA_LOWRANK), (0, 0))).astype(BF16)
        abp = _heads_lo(gla_ab[i][None], GLA_H)
        onorm = gla_onorm[i][None]

        def gla_state_in(s):
            return jnp.pad(jnp.swapaxes(s, 2, 3), ((0, 0), (0, 0), (0, 0), (0, LANES - HD)))

        def gla_state_out(s):
            return jnp.swapaxes(s[..., :HD], 2, 3)
        og_p, sg_p = _gla(z, a2p, abp, onorm, jnp.zeros((bp, GLA_H, GLA_DV, LANES), F32), 0, bp, lp)
        og_s, sg_s = _gla(z, a2p, abp, onorm, gla_state_in(state_gla[i]), tp, nb, ls)
        o_gla = (og_p, og_s)

        zrw = z[:, :RW_COLS]
        shift_s = _rw_reorder(state_rwkv_shift[i])
        heads = jnp.concatenate([jnp.zeros((1, 4 * BW), F32), z[ls - 1:tp - 1:ls, :4 * BW],
                                 _padc(shift_s.reshape(nb, RW_COLS), 4 * BW)], axis=0)
        mu = _padc(_rw_reorder(rwkv_mu[i])[None], 4 * BW)
        w2p = jnp.concatenate([rwkv_w2[i], jnp.zeros_like(rwkv_a2[i])], axis=0).astype(BF16)
        a2q = jnp.concatenate([jnp.zeros_like(rwkv_w2[i]), rwkv_a2[i]], axis=0).astype(BF16)
        r_, w_, k_, v_, nkk_, kka_, gate_, bonus_ = _rw_prep(
            z, heads, ls, mu, rwkv_w0[i][None], rwkv_a0[i][None], rwkv_kk[i][None], rwkv_ka[i][None],
            rwkv_rk[i][None], w2p, a2q, rwkv_g2[i].astype(BF16), seg)

        def rw_state_in(s):
            n = s.shape[0]
            st = jnp.swapaxes(s, 2, 3).reshape(n, NH // 2, 2, HD, HD)
            zero = jnp.zeros((n, NH // 2, HD, HD), s.dtype)
            return jnp.concatenate([jnp.concatenate([st[:, :, 0], zero], axis=-1),
                                    jnp.concatenate([zero, st[:, :, 1]], axis=-1)], axis=-2)

        def rw_state_out(s):
            n = s.shape[0]
            st = jnp.stack([s[:, :, :HD, :HD], s[:, :, HD:, HD:]], axis=2)
            return jnp.swapaxes(st, 3, 4).reshape(n, NH, HD, HD)
        o_rp, sr_p = _rw_chunk(r_, w_, k_, v_, nkk_, kka_, jnp.zeros((bp, NH // 2, LANES, LANES), F32), 0, bp, lp)
        o_rs, sr_s = _rw_chunk(r_, w_, k_, v_, nkk_, kka_, rw_state_in(state_rwkv[i]), tp, nb, ls)
        o_r = jnp.concatenate([o_rp, o_rs], axis=0)
        o_rw = _rw_post(o_r, bonus_, gate_, rwkv_lnw[i][None], rwkv_lnb[i][None], seg)

        x = _merge(x, z, (o_sb, o_dsa, o_gla, o_rw), wbr, wout, i, tp)
        x = _ffn(x, ffn2_norm[:, None], wup2, wdn2, tf, i)

        sv = z[:, OFF_SV:OFF_SV + BW]
        dv = z[:, OFF_DV:OFF_DV + BW]
        ki = z[:, OFF_KI:OFF_KI + HD]
        for dst, (lo, n, l) in ((new_p, (0, bp, lp)), (new_s, (tp, nb, ls))):
            hi = lo + n * l
            dst[0].append(skn[lo:hi].reshape(n, l, NH, HD))
            dst[1].append(sv[lo:hi].reshape(n, l, NH, HD))
            dst[2].append(dkn[lo:hi].reshape(n, l, NH, HD))
            dst[3].append(dv[lo:hi].reshape(n, l, NH, HD))
            dst[4].append(ki[lo:hi].reshape(n, l, HD))
            dst[7].append(_rw_unorder(zrw[lo:hi].reshape(n, l, RW_COLS)[:, -1:]))
        new_p[5].append(gla_state_out(sg_p))
        new_s[5].append(gla_state_out(sg_s))
        new_p[6].append(rw_state_out(sr_p))
        new_s[6].append(rw_state_out(sr_s))

    outs_p = [jnp.stack(l, axis=0) for l in new_p]
    outs_s = [jnp.stack(l, axis=0) for l in new_s]
    y_p = x[:tp].reshape(bp, lp, d_model)
    y_s = x[tp:].reshape(nb, ls, d_model)
    return (y_p, y_s, *outs_p, *outs_s)
```
